```python
import jax, jax.numpy as jnp
from jax import lax
import numpy as np

D_MODEL = 1024
BATCH = 4
SEQ = 8192
DEPTH = 4

GRID_W = 64
CTX_LEN = 256
EPS = 1e-6
ROPE_THETA = 10000.0
Q_BLOCK = 128
N_MOD = 6

CHUNK = 128
A_GROUPS = 4
A_WIDTH = 512
A_HEAD = A_WIDTH // A_GROUPS
B_WIDTH = 512
POOL_WINDOWS = (2, 4, 8, 16)
B_GROUP = B_WIDTH // len(POOL_WINDOWS)
AB_IN = 2 * A_WIDTH + B_WIDTH
AB_OUT = A_WIDTH + B_WIDTH

HEAD_DIM = 64
C_HEADS = 8
C_KV_HEADS = 2
C_Q = C_HEADS * HEAD_DIM
C_KV = C_KV_HEADS * HEAD_DIM
D_HEADS = 8
D_NOPE = 64
D_ROPE = 32
D_V = 64
Q_RANK = 384
KV_RANK = 256
OFF_KC = C_Q
OFF_VC = OFF_KC + C_KV
OFF_CQ = OFF_VC + C_KV
OFF_CKV = OFF_CQ + Q_RANK
OFF_KR = OFF_CKV + KV_RANK
CD_IN = OFF_KR + D_ROPE
CD_OUT = C_HEADS * HEAD_DIM + D_HEADS * D_V

D_FF = 2816
N_EXPERTS = 8
TOP_K = 2
D_FF_EXPERT = 2816

kernel_name = "hybrid_diffusion_prefix_trunk"


def rmsnorm(x, gain=None):
    xf = x.astype(jnp.float32)
    y = xf * lax.rsqrt(jnp.mean(xf * xf, axis=-1, keepdims=True) + EPS)
    if gain is not None:
        y = y * gain.astype(jnp.float32)
    return y.astype(x.dtype)


def modulate(x, shift, scale):
    return rmsnorm(x) * (1 + scale) + shift


def adaln(cvec, w, b):
    m = jax.nn.silu(cvec) @ w + b
    return tuple(jnp.split(m[:, None, :], N_MOD, axis=-1))


def axial_rope(length, rot_dim):
    rows = length // GRID_W
    row = jnp.repeat(jnp.arange(rows), GRID_W).astype(jnp.float32)
    col = jnp.tile(jnp.arange(GRID_W), rows).astype(jnp.float32)
    axis_dim = rot_dim // 2
    inv = ROPE_THETA ** (-jnp.arange(0, axis_dim, 2, dtype=jnp.float32) / axis_dim)
    ang = jnp.concatenate([row[:, None] * inv, col[:, None] * inv], axis=-1)
    return jnp.cos(ang), jnp.sin(ang)


def apply_rope(x, cos, sin):
    half = x.shape[-1] // 2
    shape = (1, x.shape[1]) + (1,) * (x.ndim - 3) + (half,)
    c = cos.reshape(shape).astype(x.dtype)
    s = sin.reshape(shape).astype(x.dtype)
    x1, x2 = x[..., :half], x[..., half:]
    return jnp.concatenate([x1 * c - x2 * s, x1 * s + x2 * c], axis=-1)


def block_attention(q, k, v, scale):
    b, lq, hkv, g, dk = q.shape
    nb = lq // Q_BLOCK
    qb = q.reshape(b, nb, Q_BLOCK, hkv, g, dk).transpose(1, 0, 2, 3, 4, 5)

    def one_block(qblk):
        s = jnp.einsum('bqhgd,bkhd->bhgqk', qblk, k, preferred_element_type=jnp.float32) * scale
        p = jax.nn.softmax(s, axis=-1).astype(v.dtype)
        return jnp.einsum('bhgqk,bkhd->bqhgd', p, v)

    o = lax.map(one_block, qb)
    return o.transpose(1, 0, 2, 3, 4, 5).reshape(b, lq, -1)


def chunk_gmlp(a, v_gain, w_s, b_s):
    u, v = a[..., :A_WIDTH], a[..., A_WIDTH:]
    v = rmsnorm(v, v_gain)
    b, l, _ = v.shape
    vc = v.reshape(b, l // CHUNK, CHUNK, A_GROUPS, A_HEAD)
    s = jnp.einsum('gpq,bcqgd->bcpgd', w_s, vc) + b_s.T[:, :, None]
    return u * s.reshape(b, l, A_WIDTH)


def multiscale_pool(z, w_pool, scale):
    b, l, _ = z.shape
    zf = z.astype(jnp.float32)
    cs = jnp.concatenate([jnp.zeros((b, 1, B_WIDTH), jnp.float32), jnp.cumsum(zf, axis=1)], axis=1)
    pos = jnp.arange(l)
    outs = []
    for g, w in enumerate(POOL_WINDOWS):
        lo = jnp.clip(pos - w // 2, 0, l)
        hi = jnp.clip(pos - w // 2 + w, 0, l)
        sl = slice(g * B_GROUP, (g + 1) * B_GROUP)
        csg = cs[..., sl]
        mean = (jnp.take(csg, hi, axis=1) - jnp.take(csg, lo, axis=1)) / (hi - lo).astype(jnp.float32)[None, :, None]
        outs.append((mean - zf[..., sl]).astype(z.dtype))
    pooled = jnp.stack(outs, axis=2)
    y = jnp.einsum('blgi,gio->blgo', pooled, w_pool).reshape(b, l, B_WIDTH)
    return y * scale


def ab_mixer(h, w_in, v_gain, w_s, b_s, w_pool, p_scale, w_out):
    z = h @ w_in
    ya = chunk_gmlp(jax.nn.gelu(z[..., :2 * A_WIDTH], approximate=False), v_gain, w_s, b_s)
    yb = multiscale_pool(z[..., 2 * A_WIDTH:], w_pool, p_scale)
    return jnp.concatenate([ya, yb], axis=-1) @ w_out


def cd_queries(z, q_gain, cq_gain, w_uq, rope_c, rope_d):
    b, l, _ = z.shape
    qc = rmsnorm(z[..., :OFF_KC].reshape(b, l, C_HEADS, HEAD_DIM), q_gain)
    qd = (rmsnorm(z[..., OFF_CQ:OFF_CKV], cq_gain) @ w_uq).reshape(b, l, D_HEADS, D_NOPE + D_ROPE)
    q_nope, q_rope = qd[..., :D_NOPE], qd[..., D_NOPE:]
    if rope_c is not None:
        qc = apply_rope(qc, *rope_c)
        q_rope = apply_rope(q_rope, *rope_d)
    qc = qc.reshape(b, l, C_KV_HEADS, C_HEADS // C_KV_HEADS, HEAD_DIM)
    qd = jnp.concatenate([q_nope, q_rope], axis=-1)[:, :, :, None, :]
    return qc, qd


def cd_keys(z, k_gain, ckv_gain, w_ukv, rope_c, rope_d):
    b, l, _ = z.shape
    kc = rmsnorm(z[..., OFF_KC:OFF_VC].reshape(b, l, C_KV_HEADS, HEAD_DIM), k_gain)
    vc = z[..., OFF_VC:OFF_CQ].reshape(b, l, C_KV_HEADS, HEAD_DIM)
    kv = (rmsnorm(z[..., OFF_CKV:OFF_KR], ckv_gain) @ w_ukv).reshape(b, l, D_HEADS, D_NOPE + D_V)
    kr = z[..., OFF_KR:][:, :, None, :]
    if rope_c is not None:
        kc = apply_rope(kc, *rope_c)
        kr = apply_rope(kr, *rope_d)
    kd = jnp.concatenate([kv[..., :D_NOPE], jnp.broadcast_to(kr, (b, l, D_HEADS, D_ROPE))], axis=-1)
    vd = kv[..., D_NOPE:]
    return kc, vc, kd, vd


def cd_attend(qc, qd, kc, vc, kd, vd, w_out):
    oc = block_attention(qc, kc, vc, HEAD_DIM ** -0.5)
    od = block_attention(qd, kd, vd, (D_NOPE + D_ROPE) ** -0.5)
    return jnp.concatenate([oc, od], axis=-1) @ w_out


def swiglu(h, w_gate, w_up, w_down):
    return (jax.nn.silu(h @ w_gate) * (h @ w_up)) @ w_down


def moe_swiglu(h, router, w_gate, w_up, w_down):
    b, l, d = h.shape
    hf = h.reshape(-1, d)
    logits = (hf @ router).astype(jnp.float32)
    top_v, top_i = lax.top_k(logits, TOP_K)
    top_w = jax.nn.softmax(top_v, axis=-1)
    gates = jnp.sum(jax.nn.one_hot(top_i, N_EXPERTS, dtype=jnp.float32) * top_w[..., None], axis=1).astype(h.dtype)
    y = jnp.zeros_like(hf)
    for e in range(N_EXPERTS):
        y = y + gates[:, e:e + 1] * swiglu(hf, w_gate[e], w_up[e], w_down[e])
    return y.reshape(b, l, d)


def setup_inputs(seed: int = 0) -> dict:
    key = jax.random.key(seed)
    ks = iter(jax.random.split(key, 32))
    n_even = (DEPTH + 1) // 2
    n_odd = DEPTH // 2
    D = D_MODEL

    def nrm(shape, scale):
        return scale * jax.random.normal(next(ks), shape, jnp.float32)

    def gain(shape):
        return 1.0 + 0.1 * jax.random.normal(next(ks), shape, jnp.float32)

    return {
        "x": nrm((BATCH, SEQ, D), 1.0),
        "c": nrm((BATCH, D), 1.0),
        "ctx": nrm((BATCH, CTX_LEN, D), 1.0),
        "c_ctx": nrm((D,), 1.0),
        "ada_w": nrm((DEPTH, D, N_MOD * D), D ** -0.5),
        "ada_b": nrm((DEPTH, N_MOD * D), 0.02),
        "ab_w_in": nrm((n_even, D, AB_IN), D ** -0.5),
        "a_v_gain": gain((n_even, A_WIDTH)),
        "a_w_s": nrm((n_even, A_GROUPS, CHUNK, CHUNK), CHUNK ** -0.5),
        "a_b_s": gain((n_even, A_GROUPS, CHUNK)),
        "b_w_pool": nrm((n_even, len(POOL_WINDOWS), B_GROUP, B_GROUP), B_GROUP ** -0.5),
        "b_scale": gain((n_even, B_WIDTH)),
        "ab_w_out": nrm((n_even, AB_OUT, D), AB_OUT ** -0.5),
        "ffn_w_gate": nrm((n_even, D, D_FF), D ** -0.5),
        "ffn_w_up": nrm((n_even, D, D_FF), D ** -0.5),
        "ffn_w_down": nrm((n_even, D_FF, D), D_FF ** -0.5),
        "cd_w_in": nrm((n_odd, D, CD_IN), D ** -0.5),
        "c_q_gain": gain((n_odd, HEAD_DIM)),
        "c_k_gain": gain((n_odd, HEAD_DIM)),
        "d_cq_gain": gain((n_odd, Q_RANK)),
        "d_ckv_gain": gain((n_odd, KV_RANK)),
        "d_w_uq": nrm((n_odd, Q_RANK, D_HEADS * (D_NOPE + D_ROPE)), Q_RANK ** -0.5),
        "d_w_ukv": nrm((n_odd, KV_RANK, D_HEADS * (D_NOPE + D_V)), KV_RANK ** -0.5),
        "cd_w_out": nrm((n_odd, CD_OUT, D), CD_OUT ** -0.5),
        "moe_router": nrm((n_odd, D, N_EXPERTS), D ** -0.5),
        "moe_w_gate": nrm((n_odd, N_EXPERTS, D, D_FF_EXPERT), D ** -0.5),
        "moe_w_up": nrm((n_odd, N_EXPERTS, D, D_FF_EXPERT), D ** -0.5),
        "moe_w_down": nrm((n_odd, N_EXPERTS, D_FF_EXPERT, D), D_FF_EXPERT ** -0.5),
        "final_gain": gain((D,)),
    }


def reference(x, c, ctx, c_ctx, ada_w, ada_b, ab_w_in, a_v_gain, a_w_s, a_b_s, b_w_pool, b_scale, ab_w_out,
              ffn_w_gate, ffn_w_up, ffn_w_down, cd_w_in, c_q_gain, c_k_gain, d_cq_gain, d_ckv_gain, d_w_uq,
              d_w_ukv, cd_w_out, moe_router, moe_w_gate, moe_w_up, moe_w_down, final_gain):
    seq = x.shape[1]
    rope_c = axial_rope(seq, HEAD_DIM)
    rope_d = axial_rope(seq, D_ROPE)
    t = ctx
    ctx_len = ctx.shape[1]

    def cat(a, b):
        return jnp.concatenate([a, b], axis=1)

    for layer in range(DEPTH):
        last = layer == DEPTH - 1
        odd = layer % 2 == 1
        i = layer // 2
        sh1, sc1, g1, sh2, sc2, g2 = adaln(c, ada_w[layer], ada_b[layer])
        if odd or not last:
            tsh1, tsc1, tg1, tsh2, tsc2, tg2 = adaln(c_ctx[None, :], ada_w[layer], ada_b[layer])
            ht = modulate(t, tsh1, tsc1)
        hx = modulate(x, sh1, sc1)

        if not odd:
            ab = (ab_w_in[i], a_v_gain[i], a_w_s[i], a_b_s[i], b_w_pool[i], b_scale[i], ab_w_out[i])
            mx = ab_mixer(hx, *ab)
            if not last:
                mt = ab_mixer(ht, *ab)
        else:
            zx = hx @ cd_w_in[i]
            zt = ht @ cd_w_in[i]
            qcx, qdx = cd_queries(zx, c_q_gain[i], d_cq_gain[i], d_w_uq[i], rope_c, rope_d)
            kcx, vcx, kdx, vdx = cd_keys(zx, c_k_gain[i], d_ckv_gain[i], d_w_ukv[i], rope_c, rope_d)
            kct, vct, kdt, vdt = cd_keys(zt, c_k_gain[i], d_ckv_gain[i], d_w_ukv[i], None, None)
            mx = cd_attend(qcx, qdx, cat(kct, kcx), cat(vct, vcx), cat(kdt, kdx), cat(vdt, vdx), cd_w_out[i])
            if not last:
                qct, qdt = cd_queries(zt, c_q_gain[i], d_cq_gain[i], d_w_uq[i], None, None)
                mt = cd_attend(qct, qdt, kct, vct, kdt, vdt, cd_w_out[i])
        x = x + g1 * mx
        if not last:
            t = t + tg1 * mt

        hx2 = modulate(x, sh2, sc2)
        h2 = hx2 if last else cat(modulate(t, tsh2, tsc2), hx2)
        if odd:
            f = moe_swiglu(h2, moe_router[i], moe_w_gate[i], moe_w_up[i], moe_w_down[i])
        else:
            f = swiglu(h2, ffn_w_gate[i], ffn_w_up[i], ffn_w_down[i])
        if last:
            x = x + g2 * f
        else:
            t = t + tg2 * f[:, :ctx_len]
            x = x + g2 * f[:, ctx_len:]

    return rmsnorm(x, final_gain)
```

```python
import functools

import numpy as np
import jax
import jax.numpy as jnp
from jax import lax
from jax.experimental import pallas as pl
from jax.experimental.pallas import tpu as pltpu

F32 = jnp.float32
BF16 = jnp.bfloat16
EPS = 1e-6
ROPE_THETA = 10000.0
GRID_W = 64
LANES = 128
HALO = 8
POOL_WINDOWS = (2, 4, 8, 16)
NEG = -1e30
VMEM_LIMIT = 56 * 1024 * 1024


def _cparams(sem):
    return pltpu.CompilerParams(dimension_semantics=sem, vmem_limit_bytes=VMEM_LIMIT)


def _rms(x):
    return x * lax.rsqrt(jnp.mean(x * x, axis=-1, keepdims=True) + EPS)


def _is_ctx_rows(tm, ctx_len):
    pos = pl.program_id(1) * tm + lax.broadcasted_iota(jnp.int32, (tm, 1), 0)
    return pos < ctx_len


def _mod(mx_ref, mt_ref, idx, is_ctx):
    return jnp.where(is_ctx, mt_ref[0, idx:idx + 1, :], mx_ref[0, idx:idx + 1, :])


def _dot(a, b):
    return jnp.dot(a, b, preferred_element_type=F32)


def _dot_nt(a, b):
    return lax.dot_general(a, b, (((1,), (1,)), ((), ())), preferred_element_type=F32)


def _silu(x):
    return x * jax.nn.sigmoid(x)


def _ada_kernel(c_ref, w_ref, b_ref, o_ref):
    c = c_ref[...]
    o_ref[0] = jnp.dot(_silu(c), w_ref[0], preferred_element_type=F32,
                       precision=lax.Precision.HIGHEST) + b_ref[0]


def _ada_all(cvec, ada_w, ada_b):
    depth, d, n = ada_w.shape
    rows = cvec.shape[0]
    tn = 1536
    return pl.pallas_call(
        _ada_kernel,
        grid=(depth, n // tn),
        in_specs=[
            pl.BlockSpec((rows, d), lambda l, j: (0, 0)),
            pl.BlockSpec((1, d, tn), lambda l, j: (l, 0, j)),
            pl.BlockSpec((1, 1, tn), lambda l, j: (l, 0, j)),
        ],
        out_specs=pl.BlockSpec((1, rows, tn), lambda l, j: (l, 0, j)),
        out_shape=jax.ShapeDtypeStruct((depth, rows, n), F32),
        compiler_params=_cparams(("arbitrary", "arbitrary")),
        name="adaln",
    )(cvec, ada_w, ada_b.reshape(depth, 1, n))


def _row_spec(tm, width):
    return pl.BlockSpec((1, tm, width), lambda b, j, *_: (b, j, 0))


def _mod_specs(nb, d):
    return [pl.BlockSpec((1, 6, d), lambda b, j, *_: (b, 0, 0)),
            pl.BlockSpec((1, 6, d), lambda b, j, *_: (nb, 0, 0))]


def _full_spec(shape):
    nd = len(shape)
    return pl.BlockSpec(shape, lambda *_: (0,) * nd)


def _inproj_kernel(x_ref, mx_ref, mt_ref, w_ref, z_ref, *, tm, ctx_len):
    is_ctx = _is_ctx_rows(tm, ctx_len)
    h = _rms(x_ref[0]) * (1.0 + _mod(mx_ref, mt_ref, 1, is_ctx)) + _mod(mx_ref, mt_ref, 0, is_ctx)
    z_ref[0] = _dot(h.astype(BF16), w_ref[...])


def _inproj(s, mods, w, *, ctx_len, tm):
    nb, lt, d = s.shape
    n = w.shape[1]
    return pl.pallas_call(
        functools.partial(_inproj_kernel, tm=tm, ctx_len=ctx_len),
        grid=(nb, lt // tm),
        in_specs=[_row_spec(tm, d)] + _mod_specs(nb, d) + [_full_spec((d, n))],
        out_specs=_row_spec(tm, n),
        out_shape=jax.ShapeDtypeStruct((nb, lt, n), F32),
        compiler_params=_cparams(("parallel", "parallel")),
        name="ab_inproj",
    )(s, mods, mods, w)


def _mixer_kernel(z_ref, hp_ref, hn_ref, x_ref, mx_ref, mt_ref, vg_ref, ws_ref, bs_ref, wp_ref,
                  psc_ref, wo_ref, xo_ref, h2_ref, *, tm, ctx_len, lt, chunk):
    is_ctx = _is_ctx_rows(tm, ctx_len)
    pos0 = pl.program_id(1) * tm
    z = z_ref[0]
    aw = 4 * LANES
    za = z[:, :2 * aw]
    gl = 0.5 * za * (1.0 + lax.erf(za * (2.0 ** -0.5)))
    u = gl[:, :aw]
    v = gl[:, aw:]
    vn = (_rms(v) * vg_ref[...]).astype(BF16)
    ya_rows = []
    for c in range(tm // chunk):
        rs = slice(c * chunk, (c + 1) * chunk)
        cols = []
        for g in range(4):
            cs = slice(g * LANES, (g + 1) * LANES)
            sg = _dot(ws_ref[g], vn[rs, cs]) + bs_ref[:, cs]
            cols.append(u[rs, cs] * sg)
        ya_rows.append(jnp.concatenate(cols, axis=1))
    ya = jnp.concatenate(ya_rows, axis=0)

    in_ctx = pos0 < ctx_len
    seq_start = jnp.where(in_ctx, 0, ctx_len)
    seq_len = jnp.where(in_ctx, ctx_len, lt - ctx_len)
    p_in = pos0 - seq_start + lax.broadcasted_iota(jnp.int32, (tm, 1), 0)
    has_prev = jnp.logical_and(pos0 != 0, pos0 != ctx_len).astype(F32)
    has_next = jnp.logical_and(pos0 + tm != ctx_len, pos0 + tm != lt).astype(F32)
    zp = z[:, 2 * aw:]
    ext = jnp.concatenate([hp_ref[0] * has_prev, zp, hn_ref[0] * has_next], axis=0)
    n = tm + 2 * HALO
    yb_cols = []
    for g, w in enumerate(POOL_WINDOWS):
        cs = slice(g * LANES, (g + 1) * LANES)
        a = ext[:, cs]
        step = 1
        while step < w:
            a = a + pltpu.roll(a, n - step, 0)
            step *= 2
        shift = HALO - w // 2
        if shift:
            a = pltpu.roll(a, n - shift, 0)
        wsum = a[:tm]
        lo = jnp.clip(p_in - w // 2, 0, seq_len)
        hi = jnp.clip(p_in - w // 2 + w, 0, seq_len)
        pooled = wsum / (hi - lo).astype(F32) - zp[:, cs]
        yb_cols.append(_dot(pooled.astype(BF16), wp_ref[g]) * psc_ref[:, cs])
    y = jnp.concatenate([ya] + yb_cols, axis=1).astype(BF16)
    xn = x_ref[0] + _mod(mx_ref, mt_ref, 2, is_ctx) * _dot(y, wo_ref[...])
    xo_ref[0] = xn
    h2 = _rms(xn) * (1.0 + _mod(mx_ref, mt_ref, 4, is_ctx)) + _mod(mx_ref, mt_ref, 3, is_ctx)
    h2_ref[0] = h2.astype(BF16)


def _mixer(z, s, mods, v_gain, w_s, bs_full, w_pool, p_scale, w_out, *, ctx_len, tm, chunk):
    nb, lt, d = s.shape
    aw = 4 * LANES
    nh = lt // HALO
    th = tm // HALO
    pw = 4 * LANES
    halo_prev = pl.BlockSpec((1, HALO, pw), lambda b, j: (b, jnp.maximum(j * th - 1, 0), 2))
    halo_next = pl.BlockSpec((1, HALO, pw), lambda b, j: (b, jnp.minimum((j + 1) * th, nh - 1), 2))
    return pl.pallas_call(
        functools.partial(_mixer_kernel, tm=tm, ctx_len=ctx_len, lt=lt, chunk=chunk),
        grid=(nb, lt // tm),
        in_specs=[_row_spec(tm, 2 * aw + pw), halo_prev, halo_next, _row_spec(tm, d)] + _mod_specs(nb, d) + [
            _full_spec((1, aw)), _full_spec(w_s.shape), _full_spec(bs_full.shape),
            _full_spec(w_pool.shape), _full_spec((1, pw)), _full_spec(w_out.shape)],
        out_specs=[_row_spec(tm, d), _row_spec(tm, d)],
        out_shape=[jax.ShapeDtypeStruct((nb, lt, d), F32), jax.ShapeDtypeStruct((nb, lt, d), BF16)],
        compiler_params=_cparams(("parallel", "parallel")),
        name="ab_mixer",
    )(z, z, z, s, mods, mods, v_gain, w_s, bs_full, w_pool, p_scale, w_out)


def _swiglu_kernel(h_ref, x_ref, mx_ref, mt_ref, g_ref, wg_ref, wu_ref, wd_ref, o_ref, acc_ref,
                   *, tm, ctx_len, n_exp, n_k, gated):
    e = pl.program_id(2)
    k = pl.program_id(3)
    h = h_ref[0]
    a = _silu(_dot(h, wg_ref[0])) * _dot(h, wu_ref[0])
    if gated:
        lane = lax.broadcasted_iota(jnp.int32, (tm, LANES), 1)
        a = a * jnp.sum(jnp.where(lane == e, g_ref[0], 0.0), axis=-1, keepdims=True)
    part = _dot(a.astype(BF16), wd_ref[0])
    first = jnp.logical_and(e == 0, k == 0)

    @pl.when(first)
    def _():
        acc_ref[...] = part

    @pl.when(jnp.logical_not(first))
    def _():
        acc_ref[...] += part

    @pl.when(jnp.logical_and(e == n_exp - 1, k == n_k - 1))
    def _():
        is_ctx = _is_ctx_rows(tm, ctx_len)
        o_ref[0] = x_ref[0] + _mod(mx_ref, mt_ref, 5, is_ctx) * acc_ref[...]


def _swiglu(h2, s, mods, gates, w_gate, w_up, w_down, *, ctx_len, tm, tf, gated):
    nb, lt, d = s.shape
    n_exp, _, f = w_gate.shape
    n_k = f // tf
    return pl.pallas_call(
        functools.partial(_swiglu_kernel, tm=tm, ctx_len=ctx_len, n_exp=n_exp, n_k=n_k, gated=gated),
        grid=(nb, lt // tm, n_exp, n_k),
        in_specs=[_row_spec(tm, d), _row_spec(tm, d)] + _mod_specs(nb, d) + [
            _row_spec(tm, LANES),
            pl.BlockSpec((1, d, tf), lambda b, j, e, k: (e, 0, k)),
            pl.BlockSpec((1, d, tf), lambda b, j, e, k: (e, 0, k)),
            pl.BlockSpec((1, tf, d), lambda b, j, e, k: (e, k, 0))],
        out_specs=_row_spec(tm, d),
        out_shape=jax.ShapeDtypeStruct((nb, lt, d), F32),
        scratch_shapes=[pltpu.VMEM((tm, d), F32)],
        compiler_params=_cparams(("parallel", "parallel", "arbitrary", "arbitrary")),
        name="swiglu_gated" if gated else "swiglu",
    )(h2, s, mods, mods, gates, w_gate, w_up, w_down)


def _seg_sum(x2, b_ref):
    hi = x2.astype(BF16)
    lo = (x2 - hi.astype(F32)).astype(BF16)
    return _dot(hi, b_ref[...]) + _dot(lo, b_ref[...])


def _rope_slabs(x, cos, sin, half):
    lane = lax.broadcasted_iota(jnp.int32, (x.shape[0], LANES), 1)
    first = (lane % (2 * half)) < half
    out = []
    for s in range(x.shape[1] // LANES):
        xs = x[:, s * LANES:(s + 1) * LANES]
        swapped = jnp.where(first, pltpu.roll(xs, LANES - half, 1), pltpu.roll(xs, half, 1))
        out.append(xs * cos + swapped * sin)
    return out[0] if len(out) == 1 else jnp.concatenate(out, axis=1)


def _qkv_kernel(x_ref, mx_ref, mt_ref, win_ref, cc_ref, sc_ref, cd_ref, sd_ref,
                qg_ref, kg_ref, cqg_ref, ckvg_ref, b512_ref, b128_ref,
                pct_ref, wqnt_ref, wqr_ref, prt_ref, wkn_ref, pkr_ref, wvt_ref, pkc_ref, eye_ref,
                qtc_ref, qtd_ref, kc_ref, kd_ref, vtc_ref, vtd_ref,
                *, tm, ctx_len, head_dim, d_qk):
    is_ctx = _is_ctx_rows(tm, ctx_len)
    h = _rms(x_ref[0]) * (1.0 + _mod(mx_ref, mt_ref, 1, is_ctx)) + _mod(mx_ref, mt_ref, 0, is_ctx)
    z = _dot(h.astype(BF16), win_ref[...])
    cos_c, sin_c, cos_d, sin_d = cc_ref[...], sc_ref[...], cd_ref[...], sd_ref[...]
    o_kc, o_vc, o_cq, o_ckv, o_kr = 512, 640, 768, 1152, 1408

    qc = z[:, :o_kc]
    qc = qc * lax.rsqrt(_seg_sum(qc * qc, b512_ref) * (1.0 / head_dim) + EPS) * qg_ref[...]
    qc = _rope_slabs(qc, cos_c, sin_c, head_dim // 2) * (head_dim ** -0.5)
    qtc_ref[0] = _dot_nt(pct_ref[...], qc.astype(BF16)).astype(BF16)

    kc = z[:, o_kc:o_vc]
    kc = kc * lax.rsqrt(_seg_sum(kc * kc, b128_ref) * (1.0 / head_dim) + EPS) * kg_ref[...]
    kc = _rope_slabs(kc, cos_c, sin_c, head_dim // 2)
    kc_ref[0] = _dot(kc.astype(BF16), pkc_ref[...]).astype(BF16)
    vtc_ref[0, 0] = _dot_nt(eye_ref[...], z[:, o_vc:o_cq].astype(BF16)).astype(BF16)

    cq = (_rms(z[:, o_cq:o_ckv]) * cqg_ref[...]) * (d_qk ** -0.5)
    cqb = cq.astype(BF16)
    q_rope = _rope_slabs(_dot(cqb, wqr_ref[...]), cos_d, sin_d, 16)
    qtd = _dot_nt(wqnt_ref[...], cqb) + _dot_nt(prt_ref[...], q_rope.astype(BF16))
    qtd_ref[0] = qtd.astype(BF16)

    ckv = (_rms(z[:, o_ckv:o_kr]) * ckvg_ref[...]).astype(BF16)
    kr = _rope_slabs(z[:, o_kr:], cos_d, sin_d, 16)
    kd_ref[0] = (_dot(ckv, wkn_ref[...]) + _dot(kr.astype(BF16), pkr_ref[...])).astype(BF16)
    vtd_ref[0, 0] = _dot_nt(wvt_ref[...], ckv).astype(BF16)


def _qkv(s, mods, w_in, tabs, consts, *, ctx_len, tm):
    nb, lt, d = s.shape
    nch = lt // tm
    tab_spec = pl.BlockSpec((tm, LANES), lambda b, j: (j, 0))
    out_shapes = [
        jax.ShapeDtypeStruct((nb, 8 * LANES, lt), BF16),
        jax.ShapeDtypeStruct((nb, 8 * LANES, lt), BF16),
        jax.ShapeDtypeStruct((nb, lt, 2 * LANES), BF16),
        jax.ShapeDtypeStruct((nb, lt, 8 * LANES), BF16),
        jax.ShapeDtypeStruct((nb, nch, 2 * 64, tm), BF16),
        jax.ShapeDtypeStruct((nb, nch, 8 * 64, tm), BF16),
    ]
    out_specs = [
        pl.BlockSpec((1, 8 * LANES, tm), lambda b, j: (b, 0, j)),
        pl.BlockSpec((1, 8 * LANES, tm), lambda b, j: (b, 0, j)),
        _row_spec(tm, 2 * LANES),
        _row_spec(tm, 8 * LANES),
        pl.BlockSpec((1, 1, 2 * 64, tm), lambda b, j: (b, j, 0, 0)),
        pl.BlockSpec((1, 1, 8 * 64, tm), lambda b, j: (b, j, 0, 0)),
    ]
    return pl.pallas_call(
        functools.partial(_qkv_kernel, tm=tm, ctx_len=ctx_len, head_dim=64, d_qk=96),
        grid=(nb, nch),
        in_specs=[_row_spec(tm, d)] + _mod_specs(nb, d) + [_full_spec(w_in.shape)] + [tab_spec] * 4
        + [_full_spec(c.shape) for c in consts],
        out_specs=out_specs,
        out_shape=out_shapes,
        compiler_params=_cparams(("parallel", "parallel")),
        name="cd_qkv",
    )(s, mods, mods, w_in, *tabs, *consts)


def _attn_kernel(qt_ref, k_ref, vt_ref, o_ref, *, heads, shared, tq, tk, ctx_len, lt, dv):
    qi = pl.program_id(2)
    n_kv = jnp.where(qi * tq < ctx_len, ctx_len // tk, lt // tk)
    qts = [qt_ref[0, g * LANES:(g + 1) * LANES, :] for g in range(heads)]

    def body(j, carry):
        off = pl.multiple_of(j * tk, tk)
        new = []
        for g in range(heads):
            m, l, acc = carry[g]
            kg = 0 if shared else g
            k = k_ref[0, pl.ds(off, tk), kg * LANES:(kg + 1) * LANES]
            st = _dot(k, qts[g])
            m_new = jnp.maximum(m, jnp.max(st, axis=0, keepdims=True))
            alpha = jnp.exp(m - m_new)
            p = jnp.exp(st - m_new)
            l = alpha * l + jnp.sum(p, axis=0, keepdims=True)
            vt = vt_ref[0, j, kg * dv:(kg + 1) * dv, :]
            acc = alpha * acc + _dot(vt, p.astype(BF16))
            new.append((m_new, l, acc))
        return tuple(new)

    init = tuple((jnp.full((1, tq), NEG, F32), jnp.zeros((1, tq), F32), jnp.zeros((dv, tq), F32))
                 for _ in range(heads))
    res = lax.fori_loop(0, n_kv, body, init)
    ot = jnp.concatenate([acc / l for (_, l, acc) in res], axis=0)
    o_ref[0] = ot.T.astype(o_ref.dtype)


def _attention(qt, k, vt, *, heads, shared, ctx_len, tq):
    nb, hq, lt = qt.shape
    n_heads = hq // LANES
    nch, tk = vt.shape[1], vt.shape[3]
    dv = 64
    kvw = 1 if shared else heads
    per_kv = (n_heads // (k.shape[2] // LANES)) // heads if shared else 1
    kv_idx = (lambda p: p // per_kv) if shared else (lambda p: p)
    return pl.pallas_call(
        functools.partial(_attn_kernel, heads=heads, shared=shared, tq=tq, tk=tk, ctx_len=ctx_len,
                          lt=lt, dv=dv),
        grid=(nb, n_heads // heads, lt // tq),
        in_specs=[
            pl.BlockSpec((1, heads * LANES, tq), lambda b, p, q: (b, p, q)),
            pl.BlockSpec((1, lt, kvw * LANES), lambda b, p, q: (b, 0, kv_idx(p))),
            pl.BlockSpec((1, nch, kvw * dv, tk), lambda b, p, q: (b, 0, kv_idx(p), 0)),
        ],
        out_specs=pl.BlockSpec((1, tq, heads * dv), lambda b, p, q: (b, q, p)),
        out_shape=jax.ShapeDtypeStruct((nb, lt, n_heads * dv), BF16),
        compiler_params=_cparams(("parallel", "parallel", "arbitrary")),
        name="attn_gqa" if shared else "attn_mla",
    )(qt, k, vt)


def _oproj_kernel(oc_ref, od_ref, x_ref, mx_ref, mt_ref, w_ref, r_ref, xo_ref, h2_ref, g_ref,
                  *, tm, ctx_len, n_exp):
    is_ctx = _is_ctx_rows(tm, ctx_len)
    o = jnp.concatenate([oc_ref[0], od_ref[0]], axis=-1)
    xn = x_ref[0] + _mod(mx_ref, mt_ref, 2, is_ctx) * _dot(o, w_ref[...])
    xo_ref[0] = xn
    h2 = _rms(xn) * (1.0 + _mod(mx_ref, mt_ref, 4, is_ctx)) + _mod(mx_ref, mt_ref, 3, is_ctx)
    h2_ref[0] = h2.astype(BF16)
    logits = jnp.dot(h2, r_ref[...], preferred_element_type=F32, precision=lax.Precision.HIGHEST)
    lane = lax.broadcasted_iota(jnp.int32, (tm, LANES), 1).astype(F32)
    lg = jnp.where(lane < n_exp, logits, NEG)
    m1 = jnp.max(lg, axis=-1, keepdims=True)
    i1 = jnp.min(jnp.where(lg == m1, lane, float(LANES)), axis=-1, keepdims=True)
    lg2 = jnp.where(lane == i1, NEG, lg)
    m2 = jnp.max(lg2, axis=-1, keepdims=True)
    i2 = jnp.min(jnp.where(lg2 == m2, lane, float(LANES)), axis=-1, keepdims=True)
    e2 = jnp.exp(m2 - m1)
    den = 1.0 + e2
    g_ref[0] = jnp.where(lane == i1, 1.0 / den, 0.0) + jnp.where(lane == i2, e2 / den, 0.0)


def _oproj(oc, od, s, mods, w_out, router, *, ctx_len, tm, n_exp):
    nb, lt, d = s.shape
    return pl.pallas_call(
        functools.partial(_oproj_kernel, tm=tm, ctx_len=ctx_len, n_exp=n_exp),
        grid=(nb, lt // tm),
        in_specs=[_row_spec(tm, oc.shape[2]), _row_spec(tm, od.shape[2]), _row_spec(tm, d)]
        + _mod_specs(nb, d) + [_full_spec(w_out.shape), _full_spec(router.shape)],
        out_specs=[_row_spec(tm, d), _row_spec(tm, d), _row_spec(tm, LANES)],
        out_shape=[jax.ShapeDtypeStruct((nb, lt, d), F32), jax.ShapeDtypeStruct((nb, lt, d), BF16),
                   jax.ShapeDtypeStruct((nb, lt, LANES), F32)],
        compiler_params=_cparams(("parallel", "parallel")),
        name="cd_oproj_router",
    )(oc, od, s, mods, mods, w_out, router)


def _final_kernel(x_ref, g_ref, o_ref):
    o_ref[0] = _rms(x_ref[0]) * g_ref[...]


def _final(s, gain, *, ctx_len, tm):
    nb, lt, d = s.shape
    seq = lt - ctx_len
    off = ctx_len // tm
    return pl.pallas_call(
        _final_kernel,
        grid=(nb, seq // tm),
        in_specs=[pl.BlockSpec((1, tm, d), lambda b, j: (b, j + off, 0)), _full_spec((1, d))],
        out_specs=_row_spec(tm, d),
        out_shape=jax.ShapeDtypeStruct((nb, seq, d), F32),
        compiler_params=_cparams(("parallel", "parallel")),
        name="final_norm",
    )(s, gain.reshape(1, d))


def _rope_tables(seq, ctx_len):
    rows = seq // GRID_W
    row = jnp.repeat(jnp.arange(rows), GRID_W).astype(F32)
    col = jnp.tile(jnp.arange(GRID_W), rows).astype(F32)

    def table(rot_dim, reps):
        axis_dim = rot_dim // 2
        inv = ROPE_THETA ** (-jnp.arange(0, axis_dim, 2, dtype=F32) / axis_dim)
        ang = jnp.concatenate([row[:, None] * inv, col[:, None] * inv], axis=-1)
        cos = jnp.concatenate([jnp.ones((ctx_len, axis_dim), F32), jnp.cos(ang)], axis=0)
        sin = jnp.concatenate([jnp.zeros((ctx_len, axis_dim), F32), jnp.sin(ang)], axis=0)
        return (jnp.tile(jnp.concatenate([cos, cos], axis=-1), (1, reps)),
                jnp.tile(jnp.concatenate([-sin, sin], axis=-1), (1, reps)))

    cos_c, sin_c = table(64, 2)
    cos_d, sin_d = table(32, 4)
    return cos_c, sin_c, cos_d, sin_d


def _placement(n_rows, n_cols, pairs):
    m = np.zeros((n_rows, n_cols), np.float32)
    for r, c in pairs:
        m[r, c] = 1.0
    return jnp.asarray(m, BF16)


def _cd_consts(q_gain, k_gain, cq_gain, ckv_gain, w_uq, w_ukv):
    hd, nh, nkv, d_nope, d_rope, d_v = 64, 8, 2, 64, 32, 64
    d_qk = d_nope + d_rope
    heads = np.arange(nh)
    b512 = jnp.asarray(np.kron(np.eye(nh), np.ones((hd, hd))), BF16)
    b128 = jnp.asarray(np.kron(np.eye(nkv), np.ones((hd, hd))), BF16)
    pct = _placement(nh * LANES, nh * hd, [(h * LANES + i, h * hd + i) for h in heads for i in range(hd)])
    nope_cols = (heads[:, None] * d_qk + np.arange(d_nope)[None, :]).reshape(-1)
    rope_cols = (heads[:, None] * d_qk + d_nope + np.arange(d_rope)[None, :]).reshape(-1)
    nope_rows = (heads[:, None] * LANES + np.arange(d_nope)[None, :]).reshape(-1)
    wqnt = jnp.zeros((nh * LANES, w_uq.shape[0]), F32).at[nope_rows].set(w_uq[:, nope_cols].T).astype(BF16)
    wqr = w_uq[:, rope_cols].astype(BF16)
    prt = _placement(nh * LANES, nh * d_rope,
                     [(h * LANES + d_nope + i, h * d_rope + i) for h in heads for i in range(d_rope)])
    kn_cols = (heads[:, None] * (d_nope + d_v) + np.arange(d_nope)[None, :]).reshape(-1)
    v_cols = (heads[:, None] * (d_nope + d_v) + d_nope + np.arange(d_v)[None, :]).reshape(-1)
    wkn = jnp.zeros((w_ukv.shape[0], nh * LANES), F32).at[:, nope_rows].set(w_ukv[:, kn_cols]).astype(BF16)
    pkr = _placement(LANES, nh * LANES, [(i, h * LANES + d_nope + i) for h in heads for i in range(d_rope)])
    wvt = w_ukv[:, v_cols].T.astype(BF16)
    pkc = _placement(nkv * hd, nkv * LANES, [(h * hd + i, h * LANES + i) for h in range(nkv) for i in range(hd)])
    eye = jnp.asarray(np.eye(nkv * hd), BF16)
    return [jnp.tile(q_gain, nh).reshape(1, -1), jnp.tile(k_gain, nkv).reshape(1, -1),
            cq_gain.reshape(1, -1), ckv_gain.reshape(1, -1), b512, b128,
            pct, wqnt, wqr, prt, wkn, pkr, wvt, pkc, eye]


def _largest_tile(n, candidates):
    for t in candidates:
        if n % t == 0:
            return t
    raise ValueError(f"no tile in {candidates} divides {n}")


def kernel(x, c, ctx, c_ctx, ada_w, ada_b, ab_w_in, a_v_gain, a_w_s, a_b_s, b_w_pool, b_scale, ab_w_out,
           ffn_w_gate, ffn_w_up, ffn_w_down, cd_w_in, c_q_gain, c_k_gain, d_cq_gain, d_ckv_gain, d_w_uq,
           d_w_ukv, cd_w_out, moe_router, moe_w_gate, moe_w_up, moe_w_down, final_gain):
    nb, seq, d = x.shape
    ctx_len = ctx.shape[1]
    lt = ctx_len + seq
    depth = ada_w.shape[0]
    n_exp = moe_router.shape[-1]
    chunk = a_w_s.shape[-1]
    tr = 256
    assert ctx_len % tr == 0 and seq % tr == 0 and seq % GRID_W == 0
    tm_ff = _largest_tile(lt, (384, 256))
    tf = ffn_w_gate.shape[-1] // 2

    s = jnp.concatenate([ctx, x], axis=1)

    pad = -(nb + 1) % 8
    cvec = jnp.concatenate([c, c_ctx[None, :], jnp.zeros((pad, d), F32)], axis=0)
    mods_all = _ada_all(cvec, ada_w, ada_b).reshape(depth, nb + 1 + pad, 6, d)
    tabs = _rope_tables(seq, ctx_len)
    no_gates = jnp.zeros((nb, lt, LANES), F32)

    for layer in range(depth):
        i = layer // 2
        mods = mods_all[layer]
        if layer % 2 == 0:
            z = _inproj(s, mods, ab_w_in[i].astype(BF16), ctx_len=ctx_len, tm=tr)
            bs_full = jnp.repeat(a_b_s[i].T, LANES, axis=1)
            s, h2 = _mixer(z, s, mods, a_v_gain[i].reshape(1, -1), a_w_s[i].astype(BF16), bs_full,
                           b_w_pool[i].astype(BF16), b_scale[i].reshape(1, -1), ab_w_out[i].astype(BF16),
                           ctx_len=ctx_len, tm=tr, chunk=chunk)
            s = _swiglu(h2, s, mods, no_gates, ffn_w_gate[i][None].astype(BF16),
                        ffn_w_up[i][None].astype(BF16), ffn_w_down[i][None].astype(BF16),
                        ctx_len=ctx_len, tm=tm_ff, tf=tf, gated=False)
        else:
            w_in = jnp.pad(cd_w_in[i], ((0, 0), (0, 12 * LANES - cd_w_in.shape[-1]))).astype(BF16)
            consts = _cd_consts(c_q_gain[i], c_k_gain[i], d_cq_gain[i], d_ckv_gain[i], d_w_uq[i], d_w_ukv[i])
            qtc, qtd, kc, kd, vtc, vtd = _qkv(s, mods, w_in, tabs, consts, ctx_len=ctx_len, tm=tr)
            oc = _attention(qtc, kc, vtc, heads=2, shared=True, ctx_len=ctx_len, tq=tr)
            od = _attention(qtd, kd, vtd, heads=2, shared=False, ctx_len=ctx_len, tq=tr)
            router = jnp.pad(moe_router[i], ((0, 0), (0, LANES - n_exp)))
            s, h2, gates = _oproj(oc, od, s, mods, cd_w_out[i].astype(BF16), router,
                                  ctx_len=ctx_len, tm=tr, n_exp=n_exp)
            s = _swiglu(h2, s, mods, gates, moe_w_gate[i].astype(BF16), moe_w_up[i].astype(BF16),
                        moe_w_down[i].astype(BF16), ctx_len=ctx_len, tm=tm_ff, tf=tf, gated=True)

    return _final(s, final_gain, ctx_len=ctx_len, tm=tr)
```

```python
import functools

import numpy as np
import jax
import jax.numpy as jnp
from jax import lax
from jax.experimental import pallas as pl
from jax.experimental.pallas import tpu as pltpu

F32 = jnp.float32
BF16 = jnp.bfloat16
EPS = 1e-6
ROPE_THETA = 10000.0
GRID_W = 64
LANES = 128
HALO = 8
POOL_WINDOWS = (2, 4, 8, 16)
NEG = -1e30
LOG2E = 1.4426950408889634
VMEM_LIMIT = 56 * 1024 * 1024


def _cparams(sem):
    return pltpu.CompilerParams(dimension_semantics=sem, vmem_limit_bytes=VMEM_LIMIT)


def _rms(x):
    return x * lax.rsqrt(jnp.mean(x * x, axis=-1, keepdims=True) + EPS)


def _is_ctx_rows(tm, ctx_len):
    pos = pl.program_id(1) * tm + lax.broadcasted_iota(jnp.int32, (tm, 1), 0)
    return pos < ctx_len


def _mod(mx_ref, mt_ref, idx, is_ctx):
    return jnp.where(is_ctx, mt_ref[0, idx:idx + 1, :], mx_ref[0, idx:idx + 1, :])


def _dot(a, b):
    return jnp.dot(a, b, preferred_element_type=F32)


def _dot_nt(a, b):
    return lax.dot_general(a, b, (((1,), (1,)), ((), ())), preferred_element_type=F32)


def _silu(x):
    return x * jax.nn.sigmoid(x)


def _ada_kernel(c_ref, w_ref, b_ref, o_ref):
    c = c_ref[...]
    o_ref[0] = jnp.dot(_silu(c), w_ref[0], preferred_element_type=F32,
                       precision=lax.Precision.HIGHEST) + b_ref[0]


def _ada_all(cvec, ada_w, ada_b):
    depth, d, n = ada_w.shape
    rows = cvec.shape[0]
    tn = 1536
    return pl.pallas_call(
        _ada_kernel,
        grid=(depth, n // tn),
        in_specs=[
            pl.BlockSpec((rows, d), lambda l, j: (0, 0)),
            pl.BlockSpec((1, d, tn), lambda l, j: (l, 0, j)),
            pl.BlockSpec((1, 1, tn), lambda l, j: (l, 0, j)),
        ],
        out_specs=pl.BlockSpec((1, rows, tn), lambda l, j: (l, 0, j)),
        out_shape=jax.ShapeDtypeStruct((depth, rows, n), F32),
        compiler_params=_cparams(("arbitrary", "arbitrary")),
        name="adaln",
    )(cvec, ada_w, ada_b.reshape(depth, 1, n))


def _row_spec(tm, width):
    return pl.BlockSpec((1, tm, width), lambda b, j, *_: (b, j, 0))


def _mod_specs(nb, d):
    return [pl.BlockSpec((1, 6, d), lambda b, j, *_: (b, 0, 0)),
            pl.BlockSpec((1, 6, d), lambda b, j, *_: (nb, 0, 0))]


def _full_spec(shape):
    nd = len(shape)
    return pl.BlockSpec(shape, lambda *_: (0,) * nd)


def _inproj_kernel(x_ref, mx_ref, mt_ref, w_ref, z_ref, *, tm, ctx_len):
    is_ctx = _is_ctx_rows(tm, ctx_len)
    h = _rms(x_ref[0]) * (1.0 + _mod(mx_ref, mt_ref, 1, is_ctx)) + _mod(mx_ref, mt_ref, 0, is_ctx)
    z_ref[0] = _dot(h.astype(BF16), w_ref[...])


def _inproj(s, mods, w, *, ctx_len, tm):
    nb, lt, d = s.shape
    n = w.shape[1]
    return pl.pallas_call(
        functools.partial(_inproj_kernel, tm=tm, ctx_len=ctx_len),
        grid=(nb, lt // tm),
        in_specs=[_row_spec(tm, d)] + _mod_specs(nb, d) + [_full_spec((d, n))],
        out_specs=_row_spec(tm, n),
        out_shape=jax.ShapeDtypeStruct((nb, lt, n), F32),
        compiler_params=_cparams(("parallel", "parallel")),
        name="ab_inproj",
    )(s, mods, mods, w)


def _mixer_kernel(z_ref, hp_ref, hn_ref, x_ref, mx_ref, mt_ref, vg_ref, ws_ref, bs_ref, wp_ref,
                  psc_ref, wo_ref, xo_ref, h2_ref, *, tm, ctx_len, lt, chunk):
    is_ctx = _is_ctx_rows(tm, ctx_len)
    pos0 = pl.program_id(1) * tm
    z = z_ref[0]
    aw = 4 * LANES
    za = z[:, :2 * aw]
    gl = 0.5 * za * (1.0 + lax.erf(za * (2.0 ** -0.5)))
    u = gl[:, :aw]
    v = gl[:, aw:]
    vn = (_rms(v) * vg_ref[...]).astype(BF16)
    ya_rows = []
    for c in range(tm // chunk):
        rs = slice(c * chunk, (c + 1) * chunk)
        cols = []
        for g in range(4):
            cs = slice(g * LANES, (g + 1) * LANES)
            sg = _dot(ws_ref[g], vn[rs, cs]) + bs_ref[:, cs]
            cols.append(u[rs, cs] * sg)
        ya_rows.append(jnp.concatenate(cols, axis=1))
    ya = jnp.concatenate(ya_rows, axis=0)

    in_ctx = pos0 < ctx_len
    seq_start = jnp.where(in_ctx, 0, ctx_len)
    seq_len = jnp.where(in_ctx, ctx_len, lt - ctx_len)
    p_in = pos0 - seq_start + lax.broadcasted_iota(jnp.int32, (tm, 1), 0)
    has_prev = jnp.logical_and(pos0 != 0, pos0 != ctx_len).astype(F32)
    has_next = jnp.logical_and(pos0 + tm != ctx_len, pos0 + tm != lt).astype(F32)
    zp = z[:, 2 * aw:]
    ext = jnp.concatenate([hp_ref[0] * has_prev, zp, hn_ref[0] * has_next], axis=0)
    n = tm + 2 * HALO
    yb_cols = []
    for g, w in enumerate(POOL_WINDOWS):
        cs = slice(g * LANES, (g + 1) * LANES)
        a = ext[:, cs]
        step = 1
        while step < w:
            a = a + pltpu.roll(a, n - step, 0)
            step *= 2
        shift = HALO - w // 2
        if shift:
            a = pltpu.roll(a, n - shift, 0)
        wsum = a[:tm]
        lo = jnp.clip(p_in - w // 2, 0, seq_len)
        hi = jnp.clip(p_in - w // 2 + w, 0, seq_len)
        pooled = wsum / (hi - lo).astype(F32) - zp[:, cs]
        yb_cols.append(_dot(pooled.astype(BF16), wp_ref[g]) * psc_ref[:, cs])
    y = jnp.concatenate([ya] + yb_cols, axis=1).astype(BF16)
    xn = x_ref[0] + _mod(mx_ref, mt_ref, 2, is_ctx) * _dot(y, wo_ref[...])
    xo_ref[0] = xn
    h2 = _rms(xn) * (1.0 + _mod(mx_ref, mt_ref, 4, is_ctx)) + _mod(mx_ref, mt_ref, 3, is_ctx)
    h2_ref[0] = h2.astype(BF16)


def _mixer(z, s, mods, v_gain, w_s, bs_full, w_pool, p_scale, w_out, *, ctx_len, tm, chunk):
    nb, lt, d = s.shape
    aw = 4 * LANES
    nh = lt // HALO
    th = tm // HALO
    pw = 4 * LANES
    halo_prev = pl.BlockSpec((1, HALO, pw), lambda b, j: (b, jnp.maximum(j * th - 1, 0), 2))
    halo_next = pl.BlockSpec((1, HALO, pw), lambda b, j: (b, jnp.minimum((j + 1) * th, nh - 1), 2))
    return pl.pallas_call(
        functools.partial(_mixer_kernel, tm=tm, ctx_len=ctx_len, lt=lt, chunk=chunk),
        grid=(nb, lt // tm),
        in_specs=[_row_spec(tm, 2 * aw + pw), halo_prev, halo_next, _row_spec(tm, d)] + _mod_specs(nb, d) + [
            _full_spec((1, aw)), _full_spec(w_s.shape), _full_spec(bs_full.shape),
            _full_spec(w_pool.shape), _full_spec((1, pw)), _full_spec(w_out.shape)],
        out_specs=[_row_spec(tm, d), _row_spec(tm, d)],
        out_shape=[jax.ShapeDtypeStruct((nb, lt, d), F32), jax.ShapeDtypeStruct((nb, lt, d), BF16)],
        compiler_params=_cparams(("parallel", "parallel")),
        name="ab_mixer",
    )(z, z, z, s, mods, mods, v_gain, w_s, bs_full, w_pool, p_scale, w_out)


def _swiglu_kernel(h_ref, x_ref, mx_ref, mt_ref, g_ref, wg_ref, wu_ref, wd_ref, o_ref, acc_ref,
                   *, tm, ctx_len, n_exp, n_k, gated):
    e = pl.program_id(2)
    k = pl.program_id(3)
    h = h_ref[0]
    a = _silu(_dot(h, wg_ref[0])) * _dot(h, wu_ref[0])
    if gated:
        lane = lax.broadcasted_iota(jnp.int32, (tm, LANES), 1)
        a = a * jnp.sum(jnp.where(lane == e, g_ref[0], 0.0), axis=-1, keepdims=True)
    part = _dot(a.astype(BF16), wd_ref[0])
    first = jnp.logical_and(e == 0, k == 0)

    @pl.when(first)
    def _():
        acc_ref[...] = part

    @pl.when(jnp.logical_not(first))
    def _():
        acc_ref[...] += part

    @pl.when(jnp.logical_and(e == n_exp - 1, k == n_k - 1))
    def _():
        is_ctx = _is_ctx_rows(tm, ctx_len)
        o_ref[0] = x_ref[0] + _mod(mx_ref, mt_ref, 5, is_ctx) * acc_ref[...]


def _swiglu(h2, s, mods, gates, w_gate, w_up, w_down, *, ctx_len, tm, tf, gated):
    nb, lt, d = s.shape
    n_exp, _, f = w_gate.shape
    n_k = f // tf
    return pl.pallas_call(
        functools.partial(_swiglu_kernel, tm=tm, ctx_len=ctx_len, n_exp=n_exp, n_k=n_k, gated=gated),
        grid=(nb, lt // tm, n_exp, n_k),
        in_specs=[_row_spec(tm, d), _row_spec(tm, d)] + _mod_specs(nb, d) + [
            _row_spec(tm, LANES),
            pl.BlockSpec((1, d, tf), lambda b, j, e, k: (e, 0, k)),
            pl.BlockSpec((1, d, tf), lambda b, j, e, k: (e, 0, k)),
            pl.BlockSpec((1, tf, d), lambda b, j, e, k: (e, k, 0))],
        out_specs=_row_spec(tm, d),
        out_shape=jax.ShapeDtypeStruct((nb, lt, d), F32),
        scratch_shapes=[pltpu.VMEM((tm, d), F32)],
        compiler_params=_cparams(("parallel", "parallel", "arbitrary", "arbitrary")),
        name="swiglu_gated" if gated else "swiglu",
    )(h2, s, mods, mods, gates, w_gate, w_up, w_down)


def _seg_sum(x2, b_ref):
    hi = x2.astype(BF16)
    lo = (x2 - hi.astype(F32)).astype(BF16)
    return _dot(hi, b_ref[...]) + _dot(lo, b_ref[...])


def _rope_slabs(x, cos, sin, half):
    lane = lax.broadcasted_iota(jnp.int32, (x.shape[0], LANES), 1)
    first = (lane % (2 * half)) < half
    out = []
    for s in range(x.shape[1] // LANES):
        xs = x[:, s * LANES:(s + 1) * LANES]
        swapped = jnp.where(first, pltpu.roll(xs, LANES - half, 1), pltpu.roll(xs, half, 1))
        out.append(xs * cos + swapped * sin)
    return out[0] if len(out) == 1 else jnp.concatenate(out, axis=1)


def _qkv_kernel(x_ref, mx_ref, mt_ref, win_ref, cc_ref, sc_ref, cd_ref, sd_ref,
                qg_ref, kg_ref, cqg_ref, ckvg_ref, b512_ref, b128_ref,
                pct_ref, wqnt_ref, wqr_ref, prt_ref, wkn_ref, pkr_ref, wvt_ref, pkc_ref, eye_ref,
                qtc_ref, qtd_ref, kc_ref, kd_ref, vtc_ref, vtd_ref,
                *, tm, ctx_len, head_dim, d_qk):
    is_ctx = _is_ctx_rows(tm, ctx_len)
    h = _rms(x_ref[0]) * (1.0 + _mod(mx_ref, mt_ref, 1, is_ctx)) + _mod(mx_ref, mt_ref, 0, is_ctx)
    z = _dot(h.astype(BF16), win_ref[...])
    cos_c, sin_c, cos_d, sin_d = cc_ref[...], sc_ref[...], cd_ref[...], sd_ref[...]
    o_kc, o_vc, o_cq, o_ckv, o_kr = 512, 640, 768, 1152, 1408

    qc = z[:, :o_kc]
    qc = qc * lax.rsqrt(_seg_sum(qc * qc, b512_ref) * (1.0 / head_dim) + EPS) * qg_ref[...]
    qc = _rope_slabs(qc, cos_c, sin_c, head_dim // 2) * (head_dim ** -0.5 * LOG2E)
    qtc_ref[0] = _dot_nt(pct_ref[...], qc.astype(BF16)).astype(BF16)

    kc = z[:, o_kc:o_vc]
    kc = kc * lax.rsqrt(_seg_sum(kc * kc, b128_ref) * (1.0 / head_dim) + EPS) * kg_ref[...]
    kc = _rope_slabs(kc, cos_c, sin_c, head_dim // 2)
    kc_ref[0, 0] = _dot(kc.astype(BF16), pkc_ref[...]).astype(BF16)
    vtc_ref[0, 0] = _dot_nt(eye_ref[...], z[:, o_vc:o_cq].astype(BF16)).astype(BF16)

    cq = (_rms(z[:, o_cq:o_ckv]) * cqg_ref[...]) * (d_qk ** -0.5 * LOG2E)
    cqb = cq.astype(BF16)
    q_rope = _rope_slabs(_dot(cqb, wqr_ref[...]), cos_d, sin_d, 16)
    qtd = _dot_nt(wqnt_ref[...], cqb) + _dot_nt(prt_ref[...], q_rope.astype(BF16))
    qtd_ref[0] = qtd.astype(BF16)

    ckv = (_rms(z[:, o_ckv:o_kr]) * ckvg_ref[...]).astype(BF16)
    kr = _rope_slabs(z[:, o_kr:], cos_d, sin_d, 16)
    kd_ref[0, 0] = (_dot(ckv, wkn_ref[...]) + _dot(kr.astype(BF16), pkr_ref[...])).astype(BF16)
    vtd_ref[0, 0] = _dot_nt(wvt_ref[...], ckv).astype(BF16)


def _qkv(s, mods, w_in, tabs, consts, *, ctx_len, tm):
    nb, lt, d = s.shape
    nch = lt // tm
    tab_spec = pl.BlockSpec((tm, LANES), lambda b, j: (j, 0))
    out_shapes = [
        jax.ShapeDtypeStruct((nb, 8 * LANES, lt), BF16),
        jax.ShapeDtypeStruct((nb, 8 * LANES, lt), BF16),
        jax.ShapeDtypeStruct((nb, nch, tm, 2 * LANES), BF16),
        jax.ShapeDtypeStruct((nb, nch, tm, 8 * LANES), BF16),
        jax.ShapeDtypeStruct((nb, nch, 2 * 64, tm), BF16),
        jax.ShapeDtypeStruct((nb, nch, 8 * 64, tm), BF16),
    ]
    out_specs = [
        pl.BlockSpec((1, 8 * LANES, tm), lambda b, j: (b, 0, j)),
        pl.BlockSpec((1, 8 * LANES, tm), lambda b, j: (b, 0, j)),
        pl.BlockSpec((1, 1, tm, 2 * LANES), lambda b, j: (b, j, 0, 0)),
        pl.BlockSpec((1, 1, tm, 8 * LANES), lambda b, j: (b, j, 0, 0)),
        pl.BlockSpec((1, 1, 2 * 64, tm), lambda b, j: (b, j, 0, 0)),
        pl.BlockSpec((1, 1, 8 * 64, tm), lambda b, j: (b, j, 0, 0)),
    ]
    return pl.pallas_call(
        functools.partial(_qkv_kernel, tm=tm, ctx_len=ctx_len, head_dim=64, d_qk=96),
        grid=(nb, nch),
        in_specs=[_row_spec(tm, d)] + _mod_specs(nb, d) + [_full_spec(w_in.shape)] + [tab_spec] * 4
        + [_full_spec(c.shape) for c in consts],
        out_specs=out_specs,
        out_shape=out_shapes,
        compiler_params=_cparams(("parallel", "parallel")),
        name="cd_qkv",
    )(s, mods, mods, w_in, *tabs, *consts)


def _attn_kernel(qt_ref, k_ref, vt_ref, o_ref, s0_ref, s1_ref, m_ref, l_ref, acc_ref,
                 *, kv_slots, tq, n_ctx, n_all, dv):
    heads = len(kv_slots)

    def scores(j, dst):
        for g, kg in enumerate(kv_slots):
            dst[g] = _dot(k_ref[0, j, :, kg * LANES:(kg + 1) * LANES], qt_ref[0, g * LANES:(g + 1) * LANES, :])

    def consume(j, src, g, kg):
        st = src[g]
        m_old = m_ref[g]
        m_new = jnp.maximum(m_old, jnp.max(st, axis=0, keepdims=True))
        alpha = jnp.exp2(m_old - m_new)
        p = jnp.exp2(st - m_new)
        l_ref[g] = alpha * l_ref[g] + jnp.sum(p, axis=0, keepdims=True)
        m_ref[g] = m_new
        acc_ref[g] = alpha * acc_ref[g] + _dot(vt_ref[0, j, kg * dv:(kg + 1) * dv, :], p.astype(BF16))

    def step(j, j_next, cur, nxt):
        for g, kg in enumerate(kv_slots):
            if j_next is not None:
                nxt[g] = _dot(k_ref[0, j_next, :, kg * LANES:(kg + 1) * LANES],
                              qt_ref[0, g * LANES:(g + 1) * LANES, :])
            consume(j, cur, g, kg)

    def run(n):
        m_ref[...] = jnp.full(m_ref.shape, NEG, F32)
        l_ref[...] = jnp.zeros(l_ref.shape, F32)
        acc_ref[...] = jnp.zeros(acc_ref.shape, F32)
        scores(0, s0_ref)

        def pair(i, c):
            j = 2 * i
            step(j, j + 1, s0_ref, s1_ref)
            step(j + 1, jnp.minimum(j + 2, n - 1), s1_ref, s0_ref)
            return c

        lax.fori_loop(0, n // 2, pair, 0)
        if n % 2:
            step(n - 1, None, s0_ref, None)
        ot = jnp.concatenate([acc_ref[g] / l_ref[g] for g in range(heads)], axis=0)
        o_ref[0] = ot.T.astype(o_ref.dtype)

    is_ctx_tile = pl.program_id(1) * tq < n_ctx * k_ref.shape[2]

    @pl.when(is_ctx_tile)
    def _():
        run(n_ctx)

    @pl.when(jnp.logical_not(is_ctx_tile))
    def _():
        run(n_all)


def _attention(qt, k, vt, *, kv_slots, ctx_len, tq, name):
    nb, hq, lt = qt.shape
    heads = len(kv_slots)
    assert hq == heads * LANES
    nch, tk = k.shape[1], k.shape[2]
    dv = vt.shape[2] // (k.shape[3] // LANES)
    once = pl.Buffered(1)
    return pl.pallas_call(
        functools.partial(_attn_kernel, kv_slots=kv_slots, tq=tq, n_ctx=ctx_len // tk, n_all=nch, dv=dv),
        grid=(nb, lt // tq),
        in_specs=[
            pl.BlockSpec((1, hq, tq), lambda b, q: (b, 0, q)),
            pl.BlockSpec((1, nch, tk, k.shape[3]), lambda b, q: (b, 0, 0, 0), pipeline_mode=once),
            pl.BlockSpec((1, nch, vt.shape[2], tk), lambda b, q: (b, 0, 0, 0), pipeline_mode=once),
        ],
        out_specs=pl.BlockSpec((1, tq, heads * dv), lambda b, q: (b, q, 0)),
        out_shape=jax.ShapeDtypeStruct((nb, lt, heads * dv), BF16),
        scratch_shapes=[pltpu.VMEM((heads, tk, tq), F32), pltpu.VMEM((heads, tk, tq), F32),
                        pltpu.VMEM((heads, 1, tq), F32), pltpu.VMEM((heads, 1, tq), F32),
                        pltpu.VMEM((heads, dv, tq), F32)],
        compiler_params=_cparams(("parallel", "arbitrary")),
        name=name,
    )(qt, k, vt)


def _oproj_kernel(oc_ref, od_ref, x_ref, mx_ref, mt_ref, w_ref, r_ref, xo_ref, h2_ref, g_ref,
                  *, tm, ctx_len, n_exp):
    is_ctx = _is_ctx_rows(tm, ctx_len)
    o = jnp.concatenate([oc_ref[0], od_ref[0]], axis=-1)
    xn = x_ref[0] + _mod(mx_ref, mt_ref, 2, is_ctx) * _dot(o, w_ref[...])
    xo_ref[0] = xn
    h2 = _rms(xn) * (1.0 + _mod(mx_ref, mt_ref, 4, is_ctx)) + _mod(mx_ref, mt_ref, 3, is_ctx)
    h2_ref[0] = h2.astype(BF16)
    logits = jnp.dot(h2, r_ref[...], preferred_element_type=F32, precision=lax.Precision.HIGHEST)
    lane = lax.broadcasted_iota(jnp.int32, (tm, LANES), 1).astype(F32)
    lg = jnp.where(lane < n_exp, logits, NEG)
    m1 = jnp.max(lg, axis=-1, keepdims=True)
    i1 = jnp.min(jnp.where(lg == m1, lane, float(LANES)), axis=-1, keepdims=True)
    lg2 = jnp.where(lane == i1, NEG, lg)
    m2 = jnp.max(lg2, axis=-1, keepdims=True)
    i2 = jnp.min(jnp.where(lg2 == m2, lane, float(LANES)), axis=-1, keepdims=True)
    e2 = jnp.exp(m2 - m1)
    den = 1.0 + e2
    g_ref[0] = jnp.where(lane == i1, 1.0 / den, 0.0) + jnp.where(lane == i2, e2 / den, 0.0)


def _oproj(oc, od, s, mods, w_out, router, *, ctx_len, tm, n_exp):
    nb, lt, d = s.shape
    return pl.pallas_call(
        functools.partial(_oproj_kernel, tm=tm, ctx_len=ctx_len, n_exp=n_exp),
        grid=(nb, lt // tm),
        in_specs=[_row_spec(tm, oc.shape[2]), _row_spec(tm, od.shape[2]), _row_spec(tm, d)]
        + _mod_specs(nb, d) + [_full_spec(w_out.shape), _full_spec(router.shape)],
        out_specs=[_row_spec(tm, d), _row_spec(tm, d), _row_spec(tm, LANES)],
        out_shape=[jax.ShapeDtypeStruct((nb, lt, d), F32), jax.ShapeDtypeStruct((nb, lt, d), BF16),
                   jax.ShapeDtypeStruct((nb, lt, LANES), F32)],
        compiler_params=_cparams(("parallel", "parallel")),
        name="cd_oproj_router",
    )(oc, od, s, mods, mods, w_out, router)


def _final_kernel(x_ref, g_ref, o_ref):
    o_ref[0] = _rms(x_ref[0]) * g_ref[...]


def _final(s, gain, *, ctx_len, tm):
    nb, lt, d = s.shape
    seq = lt - ctx_len
    off = ctx_len // tm
    return pl.pallas_call(
        _final_kernel,
        grid=(nb, seq // tm),
        in_specs=[pl.BlockSpec((1, tm, d), lambda b, j: (b, j + off, 0)), _full_spec((1, d))],
        out_specs=_row_spec(tm, d),
        out_shape=jax.ShapeDtypeStruct((nb, seq, d), F32),
        compiler_params=_cparams(("parallel", "parallel")),
        name="final_norm",
    )(s, gain.reshape(1, d))


def _rope_tables(seq, ctx_len):
    rows = seq // GRID_W
    row = jnp.repeat(jnp.arange(rows), GRID_W).astype(F32)
    col = jnp.tile(jnp.arange(GRID_W), rows).astype(F32)

    def table(rot_dim, reps):
        axis_dim = rot_dim // 2
        inv = ROPE_THETA ** (-jnp.arange(0, axis_dim, 2, dtype=F32) / axis_dim)
        ang = jnp.concatenate([row[:, None] * inv, col[:, None] * inv], axis=-1)
        cos = jnp.concatenate([jnp.ones((ctx_len, axis_dim), F32), jnp.cos(ang)], axis=0)
        sin = jnp.concatenate([jnp.zeros((ctx_len, axis_dim), F32), jnp.sin(ang)], axis=0)
        return (jnp.tile(jnp.concatenate([cos, cos], axis=-1), (1, reps)),
                jnp.tile(jnp.concatenate([-sin, sin], axis=-1), (1, reps)))

    cos_c, sin_c = table(64, 2)
    cos_d, sin_d = table(32, 4)
    return cos_c, sin_c, cos_d, sin_d


def _placement(n_rows, n_cols, pairs):
    m = np.zeros((n_rows, n_cols), np.float32)
    for r, c in pairs:
        m[r, c] = 1.0
    return jnp.asarray(m, BF16)


def _cd_consts(q_gain, k_gain, cq_gain, ckv_gain, w_uq, w_ukv):
    hd, nh, nkv, d_nope, d_rope, d_v = 64, 8, 2, 64, 32, 64
    d_qk = d_nope + d_rope
    heads = np.arange(nh)
    b512 = jnp.asarray(np.kron(np.eye(nh), np.ones((hd, hd))), BF16)
    b128 = jnp.asarray(np.kron(np.eye(nkv), np.ones((hd, hd))), BF16)
    pct = _placement(nh * LANES, nh * hd, [(h * LANES + i, h * hd + i) for h in heads for i in range(hd)])
    nope_cols = (heads[:, None] * d_qk + np.arange(d_nope)[None, :]).reshape(-1)
    rope_cols = (heads[:, None] * d_qk + d_nope + np.arange(d_rope)[None, :]).reshape(-1)
    nope_rows = (heads[:, None] * LANES + np.arange(d_nope)[None, :]).reshape(-1)
    wqnt = jnp.zeros((nh * LANES, w_uq.shape[0]), F32).at[nope_rows].set(w_uq[:, nope_cols].T).astype(BF16)
    wqr = w_uq[:, rope_cols].astype(BF16)
    prt = _placement(nh * LANES, nh * d_rope,
                     [(h * LANES + d_nope + i, h * d_rope + i) for h in heads for i in range(d_rope)])
    kn_cols = (heads[:, None] * (d_nope + d_v) + np.arange(d_nope)[None, :]).reshape(-1)
    v_cols = (heads[:, None] * (d_nope + d_v) + d_nope + np.arange(d_v)[None, :]).reshape(-1)
    wkn = jnp.zeros((w_ukv.shape[0], nh * LANES), F32).at[:, nope_rows].set(w_ukv[:, kn_cols]).astype(BF16)
    pkr = _placement(LANES, nh * LANES, [(i, h * LANES + d_nope + i) for h in heads for i in range(d_rope)])
    wvt = w_ukv[:, v_cols].T.astype(BF16)
    pkc = _placement(nkv * hd, nkv * LANES, [(h * hd + i, h * LANES + i) for h in range(nkv) for i in range(hd)])
    eye = jnp.asarray(np.eye(nkv * hd), BF16)
    return [jnp.tile(q_gain, nh).reshape(1, -1), jnp.tile(k_gain, nkv).reshape(1, -1),
            cq_gain.reshape(1, -1), ckv_gain.reshape(1, -1), b512, b128,
            pct, wqnt, wqr, prt, wkn, pkr, wvt, pkc, eye]


def _largest_tile(n, candidates):
    for t in candidates:
        if n % t == 0:
            return t
    raise ValueError(f"no tile in {candidates} divides {n}")


def kernel(x, c, ctx, c_ctx, ada_w, ada_b, ab_w_in, a_v_gain, a_w_s, a_b_s, b_w_pool, b_scale, ab_w_out,
           ffn_w_gate, ffn_w_up, ffn_w_down, cd_w_in, c_q_gain, c_k_gain, d_cq_gain, d_ckv_gain, d_w_uq,
           d_w_ukv, cd_w_out, moe_router, moe_w_gate, moe_w_up, moe_w_down, final_gain):
    nb, seq, d = x.shape
    ctx_len = ctx.shape[1]
    lt = ctx_len + seq
    depth = ada_w.shape[0]
    n_exp = moe_router.shape[-1]
    chunk = a_w_s.shape[-1]
    tr = 256
    assert ctx_len % tr == 0 and seq % tr == 0 and seq % GRID_W == 0
    tm_ff = _largest_tile(lt, (384, 256))
    tf = ffn_w_gate.shape[-1] // 2

    s = jnp.concatenate([ctx, x], axis=1)

    pad = -(nb + 1) % 8
    cvec = jnp.concatenate([c, c_ctx[None, :], jnp.zeros((pad, d), F32)], axis=0)
    mods_all = _ada_all(cvec, ada_w, ada_b).reshape(depth, nb + 1 + pad, 6, d)
    tabs = _rope_tables(seq, ctx_len)
    no_gates = jnp.zeros((nb, lt, LANES), F32)

    for layer in range(depth):
        i = layer // 2
        mods = mods_all[layer]
        if layer % 2 == 0:
            z = _inproj(s, mods, ab_w_in[i].astype(BF16), ctx_len=ctx_len, tm=tr)
            bs_full = jnp.repeat(a_b_s[i].T, LANES, axis=1)
            s, h2 = _mixer(z, s, mods, a_v_gain[i].reshape(1, -1), a_w_s[i].astype(BF16), bs_full,
                           b_w_pool[i].astype(BF16), b_scale[i].reshape(1, -1), ab_w_out[i].astype(BF16),
                           ctx_len=ctx_len, tm=tr, chunk=chunk)
            s = _swiglu(h2, s, mods, no_gates, ffn_w_gate[i][None].astype(BF16),
                        ffn_w_up[i][None].astype(BF16), ffn_w_down[i][None].astype(BF16),
                        ctx_len=ctx_len, tm=tm_ff, tf=tf, gated=False)
        else:
            w_in = jnp.pad(cd_w_in[i], ((0, 0), (0, 12 * LANES - cd_w_in.shape[-1]))).astype(BF16)
            consts = _cd_consts(c_q_gain[i], c_k_gain[i], d_cq_gain[i], d_ckv_gain[i], d_w_uq[i], d_w_ukv[i])
            qtc, qtd, kc, kd, vtc, vtd = _qkv(s, mods, w_in, tabs, consts, ctx_len=ctx_len, tm=tr)
            oc = _attention(qtc, kc, vtc, kv_slots=(0, 0, 0, 0, 1, 1, 1, 1), ctx_len=ctx_len, tq=tr,
                            name="attn_gqa")
            od = _attention(qtd, kd, vtd, kv_slots=tuple(range(8)), ctx_len=ctx_len, tq=tr, name="attn_mla")
            router = jnp.pad(moe_router[i], ((0, 0), (0, LANES - n_exp)))
            s, h2, gates = _oproj(oc, od, s, mods, cd_w_out[i].astype(BF16), router,
                                  ctx_len=ctx_len, tm=tr, n_exp=n_exp)
            s = _swiglu(h2, s, mods, gates, moe_w_gate[i].astype(BF16), moe_w_up[i].astype(BF16),
                        moe_w_down[i].astype(BF16), ctx_len=ctx_len, tm=tm_ff, tf=tf, gated=True)

    return _final(s, final_gain, ctx_len=ctx_len, tm=tr)
```

```python
import functools

import numpy as np
import jax
import jax.numpy as jnp
from jax import lax
from jax.experimental import pallas as pl
from jax.experimental.pallas import tpu as pltpu
from jax.experimental.pallas import tpu_sc as plsc

F32 = jnp.float32
BF16 = jnp.bfloat16
EPS = 1e-6
ROPE_THETA = 10000.0
GRID_W = 64
LANES = 128
HALO = 8
POOL_WINDOWS = (2, 4, 8, 16)
NEG = -1e30
LOG2E = 1.4426950408889634
SC_CHUNK = 32
VMEM_LIMIT = 56 * 1024 * 1024


def _cparams(sem):
    return pltpu.CompilerParams(dimension_semantics=sem, vmem_limit_bytes=VMEM_LIMIT)


def _rms(x):
    return x * lax.rsqrt(jnp.mean(x * x, axis=-1, keepdims=True) + EPS)


def _is_ctx_rows(tm, ctx_len):
    pos = pl.program_id(1) * tm + lax.broadcasted_iota(jnp.int32, (tm, 1), 0)
    return pos < ctx_len


def _mod(mx_ref, mt_ref, idx, is_ctx):
    return jnp.where(is_ctx, mt_ref[0, idx:idx + 1, :], mx_ref[0, idx:idx + 1, :])


def _dot(a, b):
    return jnp.dot(a, b, preferred_element_type=F32)


def _dot_nt(a, b):
    return lax.dot_general(a, b, (((1,), (1,)), ((), ())), preferred_element_type=F32)


def _silu(x):
    return x * jax.nn.sigmoid(x)


def _ada_kernel(c_ref, w_ref, b_ref, o_ref):
    c = c_ref[...]
    o_ref[0] = jnp.dot(_silu(c), w_ref[0], preferred_element_type=F32,
                       precision=lax.Precision.HIGHEST) + b_ref[0]


def _ada_all(cvec, ada_w, ada_b):
    depth, d, n = ada_w.shape
    rows = cvec.shape[0]
    tn = 1536
    return pl.pallas_call(
        _ada_kernel,
        grid=(depth, n // tn),
        in_specs=[
            pl.BlockSpec((rows, d), lambda l, j: (0, 0)),
            pl.BlockSpec((1, d, tn), lambda l, j: (l, 0, j)),
            pl.BlockSpec((1, 1, tn), lambda l, j: (l, 0, j)),
        ],
        out_specs=pl.BlockSpec((1, rows, tn), lambda l, j: (l, 0, j)),
        out_shape=jax.ShapeDtypeStruct((depth, rows, n), F32),
        compiler_params=_cparams(("arbitrary", "arbitrary")),
        name="adaln",
    )(cvec, ada_w, ada_b.reshape(depth, 1, n))


def _row_spec(tm, width):
    return pl.BlockSpec((1, tm, width), lambda b, j, *_: (b, j, 0))


def _mod_specs(nb, d):
    return [pl.BlockSpec((1, 6, d), lambda b, j, *_: (b, 0, 0)),
            pl.BlockSpec((1, 6, d), lambda b, j, *_: (nb, 0, 0))]


def _full_spec(shape):
    nd = len(shape)
    return pl.BlockSpec(shape, lambda *_: (0,) * nd)


def _inproj_kernel(x_ref, mx_ref, mt_ref, w_ref, z_ref, *, tm, ctx_len):
    is_ctx = _is_ctx_rows(tm, ctx_len)
    h = _rms(x_ref[0]) * (1.0 + _mod(mx_ref, mt_ref, 1, is_ctx)) + _mod(mx_ref, mt_ref, 0, is_ctx)
    z_ref[0] = _dot(h.astype(BF16), w_ref[...])


def _inproj(s, mods, w, *, ctx_len, tm):
    nb, lt, d = s.shape
    n = w.shape[1]
    return pl.pallas_call(
        functools.partial(_inproj_kernel, tm=tm, ctx_len=ctx_len),
        grid=(nb, lt // tm),
        in_specs=[_row_spec(tm, d)] + _mod_specs(nb, d) + [_full_spec((d, n))],
        out_specs=_row_spec(tm, n),
        out_shape=jax.ShapeDtypeStruct((nb, lt, n), F32),
        compiler_params=_cparams(("parallel", "parallel")),
        name="ab_inproj",
    )(s, mods, mods, w)


def _mixer_kernel(z_ref, hp_ref, hn_ref, x_ref, mx_ref, mt_ref, vg_ref, ws_ref, bs_ref, wp_ref,
                  psc_ref, wo_ref, xo_ref, h2_ref, *, tm, ctx_len, lt, chunk):
    is_ctx = _is_ctx_rows(tm, ctx_len)
    pos0 = pl.program_id(1) * tm
    z = z_ref[0]
    aw = 4 * LANES
    za = z[:, :2 * aw]
    gl = 0.5 * za * (1.0 + lax.erf(za * (2.0 ** -0.5)))
    u = gl[:, :aw]
    v = gl[:, aw:]
    vn = (_rms(v) * vg_ref[...]).astype(BF16)
    ya_rows = []
    for c in range(tm // chunk):
        rs = slice(c * chunk, (c + 1) * chunk)
        cols = []
        for g in range(4):
            cs = slice(g * LANES, (g + 1) * LANES)
            sg = _dot(ws_ref[g], vn[rs, cs]) + bs_ref[:, cs]
            cols.append(u[rs, cs] * sg)
        ya_rows.append(jnp.concatenate(cols, axis=1))
    ya = jnp.concatenate(ya_rows, axis=0)

    in_ctx = pos0 < ctx_len
    seq_start = jnp.where(in_ctx, 0, ctx_len)
    seq_len = jnp.where(in_ctx, ctx_len, lt - ctx_len)
    p_in = pos0 - seq_start + lax.broadcasted_iota(jnp.int32, (tm, 1), 0)
    has_prev = jnp.logical_and(pos0 != 0, pos0 != ctx_len).astype(F32)
    has_next = jnp.logical_and(pos0 + tm != ctx_len, pos0 + tm != lt).astype(F32)
    zp = z[:, 2 * aw:]
    ext = jnp.concatenate([hp_ref[0] * has_prev, zp, hn_ref[0] * has_next], axis=0)
    n = tm + 2 * HALO
    yb_cols = []
    for g, w in enumerate(POOL_WINDOWS):
        cs = slice(g * LANES, (g + 1) * LANES)
        a = ext[:, cs]
        step = 1
        while step < w:
            a = a + pltpu.roll(a, n - step, 0)
            step *= 2
        shift = HALO - w // 2
        if shift:
            a = pltpu.roll(a, n - shift, 0)
        wsum = a[:tm]
        lo = jnp.clip(p_in - w // 2, 0, seq_len)
        hi = jnp.clip(p_in - w // 2 + w, 0, seq_len)
        pooled = wsum / (hi - lo).astype(F32) - zp[:, cs]
        yb_cols.append(_dot(pooled.astype(BF16), wp_ref[g]) * psc_ref[:, cs])
    y = jnp.concatenate([ya] + yb_cols, axis=1).astype(BF16)
    xn = x_ref[0] + _mod(mx_ref, mt_ref, 2, is_ctx) * _dot(y, wo_ref[...])
    xo_ref[0] = xn
    h2 = _rms(xn) * (1.0 + _mod(mx_ref, mt_ref, 4, is_ctx)) + _mod(mx_ref, mt_ref, 3, is_ctx)
    h2_ref[0] = h2.astype(BF16)


def _mixer(z, s, mods, v_gain, w_s, bs_full, w_pool, p_scale, w_out, *, ctx_len, tm, chunk):
    nb, lt, d = s.shape
    aw = 4 * LANES
    nh = lt // HALO
    th = tm // HALO
    pw = 4 * LANES
    halo_prev = pl.BlockSpec((1, HALO, pw), lambda b, j: (b, jnp.maximum(j * th - 1, 0), 2))
    halo_next = pl.BlockSpec((1, HALO, pw), lambda b, j: (b, jnp.minimum((j + 1) * th, nh - 1), 2))
    return pl.pallas_call(
        functools.partial(_mixer_kernel, tm=tm, ctx_len=ctx_len, lt=lt, chunk=chunk),
        grid=(nb, lt // tm),
        in_specs=[_row_spec(tm, 2 * aw + pw), halo_prev, halo_next, _row_spec(tm, d)] + _mod_specs(nb, d) + [
            _full_spec((1, aw)), _full_spec(w_s.shape), _full_spec(bs_full.shape),
            _full_spec(w_pool.shape), _full_spec((1, pw)), _full_spec(w_out.shape)],
        out_specs=[_row_spec(tm, d), _row_spec(tm, d)],
        out_shape=[jax.ShapeDtypeStruct((nb, lt, d), F32), jax.ShapeDtypeStruct((nb, lt, d), BF16)],
        compiler_params=_cparams(("parallel", "parallel")),
        name="ab_mixer",
    )(z, z, z, s, mods, mods, v_gain, w_s, bs_full, w_pool, p_scale, w_out)


def _swiglu_kernel(h_ref, x_ref, mx_ref, mt_ref, wg_ref, wu_ref, wd_ref, o_ref, acc_ref, *, tm, ctx_len, n_k):
    k = pl.program_id(2)
    h = h_ref[0]
    a = _silu(_dot(h, wg_ref[...])) * _dot(h, wu_ref[...])
    part = _dot(a.astype(BF16), wd_ref[...])

    @pl.when(k == 0)
    def _():
        acc_ref[...] = part

    @pl.when(k > 0)
    def _():
        acc_ref[...] += part

    @pl.when(k == n_k - 1)
    def _():
        is_ctx = _is_ctx_rows(tm, ctx_len)
        o_ref[0] = x_ref[0] + _mod(mx_ref, mt_ref, 5, is_ctx) * acc_ref[...]


def _swiglu(h2, s, mods, w_gate, w_up, w_down, *, ctx_len, tm, tf):
    nb, lt, d = s.shape
    n_k = w_gate.shape[1] // tf
    return pl.pallas_call(
        functools.partial(_swiglu_kernel, tm=tm, ctx_len=ctx_len, n_k=n_k),
        grid=(nb, lt // tm, n_k),
        in_specs=[_row_spec(tm, d), _row_spec(tm, d)] + _mod_specs(nb, d) + [
            pl.BlockSpec((d, tf), lambda b, j, k: (0, k)),
            pl.BlockSpec((d, tf), lambda b, j, k: (0, k)),
            pl.BlockSpec((tf, d), lambda b, j, k: (k, 0))],
        out_specs=_row_spec(tm, d),
        out_shape=jax.ShapeDtypeStruct((nb, lt, d), F32),
        scratch_shapes=[pltpu.VMEM((tm, d), F32)],
        compiler_params=_cparams(("parallel", "parallel", "arbitrary")),
        name="swiglu",
    )(h2, s, mods, mods, w_gate, w_up, w_down)


def _seg_sum(x2, b_ref):
    hi = x2.astype(BF16)
    lo = (x2 - hi.astype(F32)).astype(BF16)
    return _dot(hi, b_ref[...]) + _dot(lo, b_ref[...])


def _rope_slabs(x, cos, sin, half):
    lane = lax.broadcasted_iota(jnp.int32, (x.shape[0], LANES), 1)
    first = (lane % (2 * half)) < half
    out = []
    for s in range(x.shape[1] // LANES):
        xs = x[:, s * LANES:(s + 1) * LANES]
        swapped = jnp.where(first, pltpu.roll(xs, LANES - half, 1), pltpu.roll(xs, half, 1))
        out.append(xs * cos + swapped * sin)
    return out[0] if len(out) == 1 else jnp.concatenate(out, axis=1)


def _qkv_kernel(x_ref, mx_ref, mt_ref, win_ref, cc_ref, sc_ref, cd_ref, sd_ref,
                qg_ref, kg_ref, cqg_ref, ckvg_ref, b512_ref, b128_ref,
                pct_ref, wqnt_ref, wqr_ref, prt_ref, wkn_ref, pkr_ref, wvt_ref, pkc_ref, eye_ref,
                qtc_ref, qtd_ref, kc_ref, kd_ref, vtc_ref, vtd_ref,
                *, tm, ctx_len, head_dim, d_qk):
    is_ctx = _is_ctx_rows(tm, ctx_len)
    h = _rms(x_ref[0]) * (1.0 + _mod(mx_ref, mt_ref, 1, is_ctx)) + _mod(mx_ref, mt_ref, 0, is_ctx)
    z = _dot(h.astype(BF16), win_ref[...])
    cos_c, sin_c, cos_d, sin_d = cc_ref[...], sc_ref[...], cd_ref[...], sd_ref[...]
    o_kc, o_vc, o_cq, o_ckv, o_kr = 512, 640, 768, 1152, 1408

    qc = z[:, :o_kc]
    qc = qc * lax.rsqrt(_seg_sum(qc * qc, b512_ref) * (1.0 / head_dim) + EPS) * qg_ref[...]
    qc = _rope_slabs(qc, cos_c, sin_c, head_dim // 2) * (head_dim ** -0.5 * LOG2E)
    qtc_ref[0] = _dot_nt(pct_ref[...], qc.astype(BF16)).astype(BF16)

    kc = z[:, o_kc:o_vc]
    kc = kc * lax.rsqrt(_seg_sum(kc * kc, b128_ref) * (1.0 / head_dim) + EPS) * kg_ref[...]
    kc = _rope_slabs(kc, cos_c, sin_c, head_dim // 2)
    kc_ref[0, 0] = _dot(kc.astype(BF16), pkc_ref[...]).astype(BF16)
    vtc_ref[0, 0] = _dot_nt(eye_ref[...], z[:, o_vc:o_cq].astype(BF16)).astype(BF16)

    cq = (_rms(z[:, o_cq:o_ckv]) * cqg_ref[...]) * (d_qk ** -0.5 * LOG2E)
    cqb = cq.astype(BF16)
    q_rope = _rope_slabs(_dot(cqb, wqr_ref[...]), cos_d, sin_d, 16)
    qtd = _dot_nt(wqnt_ref[...], cqb) + _dot_nt(prt_ref[...], q_rope.astype(BF16))
    qtd_ref[0] = qtd.astype(BF16)

    ckv = (_rms(z[:, o_ckv:o_kr]) * ckvg_ref[...]).astype(BF16)
    kr = _rope_slabs(z[:, o_kr:], cos_d, sin_d, 16)
    kd_ref[0, 0] = (_dot(ckv, wkn_ref[...]) + _dot(kr.astype(BF16), pkr_ref[...])).astype(BF16)
    vtd_ref[0, 0] = _dot_nt(wvt_ref[...], ckv).astype(BF16)


def _qkv(s, mods, w_in, tabs, consts, *, ctx_len, tm):
    nb, lt, d = s.shape
    nch = lt // tm
    tab_spec = pl.BlockSpec((tm, LANES), lambda b, j: (j, 0))
    out_shapes = [
        jax.ShapeDtypeStruct((nb, 8 * LANES, lt), BF16),
        jax.ShapeDtypeStruct((nb, 8 * LANES, lt), BF16),
        jax.ShapeDtypeStruct((nb, nch, tm, 2 * LANES), BF16),
        jax.ShapeDtypeStruct((nb, nch, tm, 8 * LANES), BF16),
        jax.ShapeDtypeStruct((nb, nch, 2 * 64, tm), BF16),
        jax.ShapeDtypeStruct((nb, nch, 8 * 64, tm), BF16),
    ]
    out_specs = [
        pl.BlockSpec((1, 8 * LANES, tm), lambda b, j: (b, 0, j)),
        pl.BlockSpec((1, 8 * LANES, tm), lambda b, j: (b, 0, j)),
        pl.BlockSpec((1, 1, tm, 2 * LANES), lambda b, j: (b, j, 0, 0)),
        pl.BlockSpec((1, 1, tm, 8 * LANES), lambda b, j: (b, j, 0, 0)),
        pl.BlockSpec((1, 1, 2 * 64, tm), lambda b, j: (b, j, 0, 0)),
        pl.BlockSpec((1, 1, 8 * 64, tm), lambda b, j: (b, j, 0, 0)),
    ]
    return pl.pallas_call(
        functools.partial(_qkv_kernel, tm=tm, ctx_len=ctx_len, head_dim=64, d_qk=96),
        grid=(nb, nch),
        in_specs=[_row_spec(tm, d)] + _mod_specs(nb, d) + [_full_spec(w_in.shape)] + [tab_spec] * 4
        + [_full_spec(c.shape) for c in consts],
        out_specs=out_specs,
        out_shape=out_shapes,
        compiler_params=_cparams(("parallel", "parallel")),
        name="cd_qkv",
    )(s, mods, mods, w_in, *tabs, *consts)


def _attn_kernel(qt_ref, k_ref, vt_ref, o_ref, s0_ref, s1_ref, m_ref, l_ref, acc_ref,
                 *, kv_slots, tq, n_ctx, n_all, dv):
    heads = len(kv_slots)

    def scores(j, dst):
        for g, kg in enumerate(kv_slots):
            dst[g] = _dot(k_ref[0, j, :, kg * LANES:(kg + 1) * LANES], qt_ref[0, g * LANES:(g + 1) * LANES, :])

    def consume(j, src, g, kg):
        st = src[g]
        m_old = m_ref[g]
        m_new = jnp.maximum(m_old, jnp.max(st, axis=0, keepdims=True))
        alpha = jnp.exp2(m_old - m_new)
        p = jnp.exp2(st - m_new)
        l_ref[g] = alpha * l_ref[g] + jnp.sum(p, axis=0, keepdims=True)
        m_ref[g] = m_new
        acc_ref[g] = alpha * acc_ref[g] + _dot(vt_ref[0, j, kg * dv:(kg + 1) * dv, :], p.astype(BF16))

    def step(j, j_next, cur, nxt):
        for g, kg in enumerate(kv_slots):
            if j_next is not None:
                nxt[g] = _dot(k_ref[0, j_next, :, kg * LANES:(kg + 1) * LANES],
                              qt_ref[0, g * LANES:(g + 1) * LANES, :])
            consume(j, cur, g, kg)

    def run(n):
        m_ref[...] = jnp.full(m_ref.shape, NEG, F32)
        l_ref[...] = jnp.zeros(l_ref.shape, F32)
        acc_ref[...] = jnp.zeros(acc_ref.shape, F32)
        scores(0, s0_ref)

        def pair(i, c):
            j = 2 * i
            step(j, j + 1, s0_ref, s1_ref)
            step(j + 1, jnp.minimum(j + 2, n - 1), s1_ref, s0_ref)
            return c

        lax.fori_loop(0, n // 2, pair, 0)
        if n % 2:
            step(n - 1, None, s0_ref, None)
        ot = jnp.concatenate([acc_ref[g] / l_ref[g] for g in range(heads)], axis=0)
        o_ref[0] = ot.T.astype(o_ref.dtype)

    is_ctx_tile = pl.program_id(1) * tq < n_ctx * k_ref.shape[2]

    @pl.when(is_ctx_tile)
    def _():
        run(n_ctx)

    @pl.when(jnp.logical_not(is_ctx_tile))
    def _():
        run(n_all)


def _attention(qt, k, vt, *, kv_slots, ctx_len, tq, name):
    nb, hq, lt = qt.shape
    heads = len(kv_slots)
    assert hq == heads * LANES
    nch, tk = k.shape[1], k.shape[2]
    dv = vt.shape[2] // (k.shape[3] // LANES)
    once = pl.Buffered(1)
    return pl.pallas_call(
        functools.partial(_attn_kernel, kv_slots=kv_slots, tq=tq, n_ctx=ctx_len // tk, n_all=nch, dv=dv),
        grid=(nb, lt // tq),
        in_specs=[
            pl.BlockSpec((1, hq, tq), lambda b, q: (b, 0, q)),
            pl.BlockSpec((1, nch, tk, k.shape[3]), lambda b, q: (b, 0, 0, 0), pipeline_mode=once),
            pl.BlockSpec((1, nch, vt.shape[2], tk), lambda b, q: (b, 0, 0, 0), pipeline_mode=once),
        ],
        out_specs=pl.BlockSpec((1, tq, heads * dv), lambda b, q: (b, q, 0)),
        out_shape=jax.ShapeDtypeStruct((nb, lt, heads * dv), BF16),
        scratch_shapes=[pltpu.VMEM((heads, tk, tq), F32), pltpu.VMEM((heads, tk, tq), F32),
                        pltpu.VMEM((heads, 1, tq), F32), pltpu.VMEM((heads, 1, tq), F32),
                        pltpu.VMEM((heads, dv, tq), F32)],
        compiler_params=_cparams(("parallel", "arbitrary")),
        name=name,
    )(qt, k, vt)


def _oproj_kernel(oc_ref, od_ref, x_ref, mx_ref, mt_ref, w_ref, r_ref, xo_ref, h2_ref, g_ref,
                  *, tm, ctx_len, n_exp):
    is_ctx = _is_ctx_rows(tm, ctx_len)
    o = jnp.concatenate([oc_ref[0], od_ref[0]], axis=-1)
    xn = x_ref[0] + _mod(mx_ref, mt_ref, 2, is_ctx) * _dot(o, w_ref[...])
    xo_ref[0] = xn
    h2 = _rms(xn) * (1.0 + _mod(mx_ref, mt_ref, 4, is_ctx)) + _mod(mx_ref, mt_ref, 3, is_ctx)
    h2_ref[0] = h2
    logits = jnp.dot(h2, r_ref[...], preferred_element_type=F32, precision=lax.Precision.HIGHEST)
    lane = lax.broadcasted_iota(jnp.int32, (tm, LANES), 1).astype(F32)
    lg = jnp.where(lane < n_exp, logits, NEG)
    m1 = jnp.max(lg, axis=-1, keepdims=True)
    i1 = jnp.min(jnp.where(lg == m1, lane, float(LANES)), axis=-1, keepdims=True)
    lg2 = jnp.where(lane == i1, NEG, lg)
    m2 = jnp.max(lg2, axis=-1, keepdims=True)
    i2 = jnp.min(jnp.where(lg2 == m2, lane, float(LANES)), axis=-1, keepdims=True)
    e2 = jnp.exp(m2 - m1)
    den = 1.0 + e2
    g_ref[0] = jnp.where(lane == 0.0, i1, jnp.where(lane == 1.0, i2, jnp.where(lane == 2.0, 1.0 / den, e2 / den)))


def _oproj(oc, od, s, mods, w_out, router, *, ctx_len, tm, n_exp):
    nb, lt, d = s.shape
    return pl.pallas_call(
        functools.partial(_oproj_kernel, tm=tm, ctx_len=ctx_len, n_exp=n_exp),
        grid=(nb, lt // tm),
        in_specs=[_row_spec(tm, oc.shape[2]), _row_spec(tm, od.shape[2]), _row_spec(tm, d)]
        + _mod_specs(nb, d) + [_full_spec(w_out.shape), _full_spec(router.shape)],
        out_specs=[_row_spec(tm, d), _row_spec(tm, d), _row_spec(tm, LANES)],
        out_shape=[jax.ShapeDtypeStruct((nb, lt, d), F32), jax.ShapeDtypeStruct((nb, lt, d), F32),
                   jax.ShapeDtypeStruct((nb, lt, LANES), F32)],
        compiler_params=_cparams(("parallel", "parallel")),
        name="cd_oproj_router",
    )(oc, od, s, mods, mods, w_out, router)


def _sc_gather(table, idx, *, chunk=SC_CHUNK):
    n_rows, d = idx.shape[0], table.shape[1]
    info = plsc.get_sparse_core_info()
    n_workers = info.num_cores * info.num_subcores
    per_w = n_rows // n_workers
    assert n_rows % (n_workers * chunk) == 0 and chunk % 8 == 0 and chunk <= LANES
    mesh = plsc.VectorSubcoreMesh(core_axis_name="c", subcore_axis_name="s")

    def body(table_hbm, idx_hbm, out_hbm, idx_v, rows_v, sem):
        wid = lax.axis_index("s") * info.num_cores + lax.axis_index("c")
        base = wid * per_w
        pltpu.sync_copy(idx_hbm.at[pl.ds(base, per_w)], idx_v)

        @pl.loop(0, per_w // chunk)
        def _(i):
            off = pl.multiple_of(i * chunk, chunk)
            pltpu.async_copy(table_hbm.at[idx_v.at[pl.ds(off, chunk)]], rows_v, sem).wait()
            pltpu.sync_copy(rows_v, out_hbm.at[pl.ds(base + off, chunk)])

    return pl.kernel(
        body,
        out_type=jax.ShapeDtypeStruct((n_rows, d), table.dtype),
        mesh=mesh,
        scratch_types=[pltpu.VMEM((per_w,), jnp.int32), pltpu.VMEM((chunk, d), table.dtype),
                       pltpu.SemaphoreType.DMA],
        name="sc_row_gather",
    )(table, idx)


def _route_plan(route, *, n_exp, ts, align):
    n_tok = route.shape[0]
    experts = route[:, :2].astype(jnp.int32)
    onehot = (experts[:, :, None] == jnp.arange(n_exp)[None, None, :]).astype(jnp.int32).sum(axis=1)
    before = jnp.cumsum(onehot, axis=0) - onehot
    counts = onehot.sum(axis=0)
    gsz = (counts + ts - 1) // ts * ts
    gend = jnp.cumsum(gsz)
    gstart = gend - gsz
    pos = gstart[experts] + jnp.take_along_axis(before, experts, axis=1)
    n_slots = -(-(2 * n_tok + n_exp * ts) // align) * align
    tok = jnp.zeros((n_slots,), jnp.int32).at[pos.reshape(-1)].set(
        jnp.repeat(jnp.arange(n_tok, dtype=jnp.int32), 2))
    n_tiles = n_slots // ts
    tile_start = jnp.arange(n_tiles, dtype=jnp.int32) * ts
    n_used = (gend[-1] // ts).astype(jnp.int32)
    tile_expert = jnp.minimum(jnp.searchsorted(gend, tile_start, side="right"), n_exp - 1).astype(jnp.int32)
    last_used = tile_expert[jnp.maximum(n_used - 1, 0)]
    tile_expert = jnp.where(jnp.arange(n_tiles) < n_used, tile_expert, last_used)
    return tok, pos.T.reshape(-1).astype(jnp.int32), tile_expert, n_used.reshape(1)


def _gffn_kernel(te_ref, nu_ref, h_ref, wg_ref, wu_ref, wd_ref, o_ref, acc_ref, *, n_k):
    j = pl.program_id(0)
    k = pl.program_id(1)
    used = j < nu_ref[0]

    @pl.when(used)
    def _():
        h = h_ref[...].astype(BF16)
        a = _silu(_dot(h, wg_ref[0])) * _dot(h, wu_ref[0])
        part = _dot(a.astype(BF16), wd_ref[0])

        @pl.when(k == 0)
        def _():
            acc_ref[...] = part

        @pl.when(k > 0)
        def _():
            acc_ref[...] += part

        @pl.when(k == n_k - 1)
        def _():
            o_ref[...] = acc_ref[...]

    @pl.when(jnp.logical_and(jnp.logical_not(used), k == n_k - 1))
    def _():
        o_ref[...] = jnp.zeros(o_ref.shape, o_ref.dtype)


def _grouped_ffn(hs, tile_expert, n_used, w_gate, w_up, w_down, *, ts, tf):
    n_slots, d = hs.shape
    f = w_gate.shape[-1]
    n_k = f // tf
    grid_spec = pltpu.PrefetchScalarGridSpec(
        num_scalar_prefetch=2,
        grid=(n_slots // ts, n_k),
        in_specs=[
            pl.BlockSpec((ts, d), lambda j, k, te, nu: (j, 0)),
            pl.BlockSpec((1, d, tf), lambda j, k, te, nu: (te[j], 0, k)),
            pl.BlockSpec((1, d, tf), lambda j, k, te, nu: (te[j], 0, k)),
            pl.BlockSpec((1, tf, d), lambda j, k, te, nu: (te[j], k, 0)),
        ],
        out_specs=pl.BlockSpec((ts, d), lambda j, k, te, nu: (j, 0)),
        scratch_shapes=[pltpu.VMEM((ts, d), F32)],
    )
    return pl.pallas_call(
        functools.partial(_gffn_kernel, n_k=n_k),
        grid_spec=grid_spec,
        out_shape=jax.ShapeDtypeStruct((n_slots, d), F32),
        compiler_params=_cparams(("parallel", "arbitrary")),
        name="moe_grouped_ffn",
    )(tile_expert, n_used, hs, w_gate, w_up, w_down)


def _combine_kernel(y1_ref, y2_ref, r_ref, x_ref, mx_ref, mt_ref, o_ref, *, tm, ctx_len):
    is_ctx = _is_ctx_rows(tm, ctx_len)
    r = r_ref[0]
    f = r[:, 2:3] * y1_ref[0, 0] + r[:, 3:4] * y2_ref[0, 0]
    o_ref[0] = x_ref[0] + _mod(mx_ref, mt_ref, 5, is_ctx) * f


def _combine(yg, route, s, mods, *, ctx_len, tm):
    nb, lt, d = s.shape
    return pl.pallas_call(
        functools.partial(_combine_kernel, tm=tm, ctx_len=ctx_len),
        grid=(nb, lt // tm),
        in_specs=[pl.BlockSpec((1, 1, tm, d), lambda b, j: (0, b, j, 0)),
                  pl.BlockSpec((1, 1, tm, d), lambda b, j: (1, b, j, 0)),
                  _row_spec(tm, LANES), _row_spec(tm, d)] + _mod_specs(nb, d),
        out_specs=_row_spec(tm, d),
        out_shape=jax.ShapeDtypeStruct((nb, lt, d), F32),
        compiler_params=_cparams(("parallel", "parallel")),
        name="moe_combine",
    )(yg, yg, route, s, mods, mods)


def _moe(h2, route, s, mods, w_gate, w_up, w_down, *, ctx_len, n_exp, ts, tf, tm):
    nb, lt, d = s.shape
    n_tok = nb * lt
    info = plsc.get_sparse_core_info()
    sc_rows = info.num_cores * info.num_subcores * SC_CHUNK
    tok, pos, tile_expert, n_used = _route_plan(route.reshape(n_tok, LANES), n_exp=n_exp, ts=ts,
                                                align=int(np.lcm(ts, sc_rows)))
    hs = _sc_gather(h2.reshape(n_tok, d), tok)
    ys = _grouped_ffn(hs, tile_expert, n_used, w_gate, w_up, w_down, ts=ts, tf=tf)
    yg = _sc_gather(ys, pos).reshape(2, nb, lt, d)
    return _combine(yg, route, s, mods, ctx_len=ctx_len, tm=tm)


def _final_kernel(x_ref, g_ref, o_ref):
    o_ref[0] = _rms(x_ref[0]) * g_ref[...]


def _final(s, gain, *, ctx_len, tm):
    nb, lt, d = s.shape
    seq = lt - ctx_len
    off = ctx_len // tm
    return pl.pallas_call(
        _final_kernel,
        grid=(nb, seq // tm),
        in_specs=[pl.BlockSpec((1, tm, d), lambda b, j: (b, j + off, 0)), _full_spec((1, d))],
        out_specs=_row_spec(tm, d),
        out_shape=jax.ShapeDtypeStruct((nb, seq, d), F32),
        compiler_params=_cparams(("parallel", "parallel")),
        name="final_norm",
    )(s, gain.reshape(1, d))


def _rope_tables(seq, ctx_len):
    rows = seq // GRID_W
    row = jnp.repeat(jnp.arange(rows), GRID_W).astype(F32)
    col = jnp.tile(jnp.arange(GRID_W), rows).astype(F32)

    def table(rot_dim, reps):
        axis_dim = rot_dim // 2
        inv = ROPE_THETA ** (-jnp.arange(0, axis_dim, 2, dtype=F32) / axis_dim)
        ang = jnp.concatenate([row[:, None] * inv, col[:, None] * inv], axis=-1)
        cos = jnp.concatenate([jnp.ones((ctx_len, axis_dim), F32), jnp.cos(ang)], axis=0)
        sin = jnp.concatenate([jnp.zeros((ctx_len, axis_dim), F32), jnp.sin(ang)], axis=0)
        return (jnp.tile(jnp.concatenate([cos, cos], axis=-1), (1, reps)),
                jnp.tile(jnp.concatenate([-sin, sin], axis=-1), (1, reps)))

    cos_c, sin_c = table(64, 2)
    cos_d, sin_d = table(32, 4)
    return cos_c, sin_c, cos_d, sin_d


def _placement(n_rows, n_cols, pairs):
    m = np.zeros((n_rows, n_cols), np.float32)
    for r, c in pairs:
        m[r, c] = 1.0
    return jnp.asarray(m, BF16)


def _cd_consts(q_gain, k_gain, cq_gain, ckv_gain, w_uq, w_ukv):
    hd, nh, nkv, d_nope, d_rope, d_v = 64, 8, 2, 64, 32, 64
    d_qk = d_nope + d_rope
    heads = np.arange(nh)
    b512 = jnp.asarray(np.kron(np.eye(nh), np.ones((hd, hd))), BF16)
    b128 = jnp.asarray(np.kron(np.eye(nkv), np.ones((hd, hd))), BF16)
    pct = _placement(nh * LANES, nh * hd, [(h * LANES + i, h * hd + i) for h in heads for i in range(hd)])
    nope_cols = (heads[:, None] * d_qk + np.arange(d_nope)[None, :]).reshape(-1)
    rope_cols = (heads[:, None] * d_qk + d_nope + np.arange(d_rope)[None, :]).reshape(-1)
    nope_rows = (heads[:, None] * LANES + np.arange(d_nope)[None, :]).reshape(-1)
    wqnt = jnp.zeros((nh * LANES, w_uq.shape[0]), F32).at[nope_rows].set(w_uq[:, nope_cols].T).astype(BF16)
    wqr = w_uq[:, rope_cols].astype(BF16)
    prt = _placement(nh * LANES, nh * d_rope,
                     [(h * LANES + d_nope + i, h * d_rope + i) for h in heads for i in range(d_rope)])
    kn_cols = (heads[:, None] * (d_nope + d_v) + np.arange(d_nope)[None, :]).reshape(-1)
    v_cols = (heads[:, None] * (d_nope + d_v) + d_nope + np.arange(d_v)[None, :]).reshape(-1)
    wkn = jnp.zeros((w_ukv.shape[0], nh * LANES), F32).at[:, nope_rows].set(w_ukv[:, kn_cols]).astype(BF16)
    pkr = _placement(LANES, nh * LANES, [(i, h * LANES + d_nope + i) for h in heads for i in range(d_rope)])
    wvt = w_ukv[:, v_cols].T.astype(BF16)
    pkc = _placement(nkv * hd, nkv * LANES, [(h * hd + i, h * LANES + i) for h in range(nkv) for i in range(hd)])
    eye = jnp.asarray(np.eye(nkv * hd), BF16)
    return [jnp.tile(q_gain, nh).reshape(1, -1), jnp.tile(k_gain, nkv).reshape(1, -1),
            cq_gain.reshape(1, -1), ckv_gain.reshape(1, -1), b512, b128,
            pct, wqnt, wqr, prt, wkn, pkr, wvt, pkc, eye]


def _largest_tile(n, candidates):
    for t in candidates:
        if n % t == 0:
            return t
    raise ValueError(f"no tile in {candidates} divides {n}")


def kernel(x, c, ctx, c_ctx, ada_w, ada_b, ab_w_in, a_v_gain, a_w_s, a_b_s, b_w_pool, b_scale, ab_w_out,
           ffn_w_gate, ffn_w_up, ffn_w_down, cd_w_in, c_q_gain, c_k_gain, d_cq_gain, d_ckv_gain, d_w_uq,
           d_w_ukv, cd_w_out, moe_router, moe_w_gate, moe_w_up, moe_w_down, final_gain):
    nb, seq, d = x.shape
    ctx_len = ctx.shape[1]
    lt = ctx_len + seq
    depth = ada_w.shape[0]
    n_exp = moe_router.shape[-1]
    chunk = a_w_s.shape[-1]
    tr = 256
    assert ctx_len % tr == 0 and seq % tr == 0 and seq % GRID_W == 0
    tm_ff = _largest_tile(lt, (384, 256))
    tf = ffn_w_gate.shape[-1] // 2

    s = jnp.concatenate([ctx, x], axis=1)

    pad = -(nb + 1) % 8
    cvec = jnp.concatenate([c, c_ctx[None, :], jnp.zeros((pad, d), F32)], axis=0)
    mods_all = _ada_all(cvec, ada_w, ada_b).reshape(depth, nb + 1 + pad, 6, d)
    tabs = _rope_tables(seq, ctx_len)

    for layer in range(depth):
        i = layer // 2
        mods = mods_all[layer]
        if layer % 2 == 0:
            z = _inproj(s, mods, ab_w_in[i].astype(BF16), ctx_len=ctx_len, tm=tr)
            bs_full = jnp.repeat(a_b_s[i].T, LANES, axis=1)
            s, h2 = _mixer(z, s, mods, a_v_gain[i].reshape(1, -1), a_w_s[i].astype(BF16), bs_full,
                           b_w_pool[i].astype(BF16), b_scale[i].reshape(1, -1), ab_w_out[i].astype(BF16),
                           ctx_len=ctx_len, tm=tr, chunk=chunk)
            s = _swiglu(h2, s, mods, ffn_w_gate[i].astype(BF16), ffn_w_up[i].astype(BF16),
                        ffn_w_down[i].astype(BF16), ctx_len=ctx_len, tm=tm_ff, tf=tf)
        else:
            w_in = jnp.pad(cd_w_in[i], ((0, 0), (0, 12 * LANES - cd_w_in.shape[-1]))).astype(BF16)
            consts = _cd_consts(c_q_gain[i], c_k_gain[i], d_cq_gain[i], d_ckv_gain[i], d_w_uq[i], d_w_ukv[i])
            qtc, qtd, kc, kd, vtc, vtd = _qkv(s, mods, w_in, tabs, consts, ctx_len=ctx_len, tm=tr)
            oc = _attention(qtc, kc, vtc, kv_slots=(0, 0, 0, 0, 1, 1, 1, 1), ctx_len=ctx_len, tq=tr,
                            name="attn_gqa")
            od = _attention(qtd, kd, vtd, kv_slots=tuple(range(8)), ctx_len=ctx_len, tq=tr, name="attn_mla")
            router = jnp.pad(moe_router[i], ((0, 0), (0, LANES - n_exp)))
            s, h2, route = _oproj(oc, od, s, mods, cd_w_out[i].astype(BF16), router,
                                  ctx_len=ctx_len, tm=tr, n_exp=n_exp)
            s = _moe(h2, route, s, mods, moe_w_gate[i].astype(BF16), moe_w_up[i].astype(BF16),
                     moe_w_down[i].astype(BF16), ctx_len=ctx_len, n_exp=n_exp, ts=512, tf=tf, tm=tr)

    return _final(s, final_gain, ctx_len=ctx_len, tm=tr)
```

```python
import functools

import numpy as np
import jax
import jax.numpy as jnp
from jax import lax
from jax.experimental import pallas as pl
from jax.experimental.pallas import tpu as pltpu
from jax.experimental.pallas import tpu_sc as plsc

F32 = jnp.float32
BF16 = jnp.bfloat16
EPS = 1e-6
ROPE_THETA = 10000.0
GRID_W = 64
LANES = 128
HALO = 8
POOL_WINDOWS = (2, 4, 8, 16)
NEG = -1e30
LOG2E = 1.4426950408889634
DEN_ROWS = 16
SC_CHUNK = 32
VMEM_LIMIT = 56 * 1024 * 1024


def _cparams(sem, flags=None):
    return pltpu.CompilerParams(dimension_semantics=sem, vmem_limit_bytes=VMEM_LIMIT, flags=flags)


def _rms(x):
    return x * lax.rsqrt(jnp.mean(x * x, axis=-1, keepdims=True) + EPS)


def _is_ctx_rows(tm, ctx_len):
    pos = pl.program_id(1) * tm + lax.broadcasted_iota(jnp.int32, (tm, 1), 0)
    return pos < ctx_len


def _mod(mx_ref, mt_ref, idx, is_ctx):
    return jnp.where(is_ctx, mt_ref[0, idx:idx + 1, :], mx_ref[0, idx:idx + 1, :])


def _dot(a, b):
    return jnp.dot(a, b, preferred_element_type=F32)


def _dot_nt(a, b):
    return lax.dot_general(a, b, (((1,), (1,)), ((), ())), preferred_element_type=F32)


def _silu(x):
    return x * jax.nn.sigmoid(x)


def _ada_kernel(c_ref, w_ref, b_ref, o_ref):
    c = c_ref[...]
    o_ref[0] = jnp.dot(_silu(c), w_ref[0], preferred_element_type=F32,
                       precision=lax.Precision.HIGHEST) + b_ref[0]


def _ada_all(cvec, ada_w, ada_b):
    depth, d, n = ada_w.shape
    rows = cvec.shape[0]
    tn = 1536
    return pl.pallas_call(
        _ada_kernel,
        grid=(depth, n // tn),
        in_specs=[
            pl.BlockSpec((rows, d), lambda l, j: (0, 0)),
            pl.BlockSpec((1, d, tn), lambda l, j: (l, 0, j)),
            pl.BlockSpec((1, 1, tn), lambda l, j: (l, 0, j)),
        ],
        out_specs=pl.BlockSpec((1, rows, tn), lambda l, j: (l, 0, j)),
        out_shape=jax.ShapeDtypeStruct((depth, rows, n), F32),
        compiler_params=_cparams(("arbitrary", "arbitrary")),
        name="adaln",
    )(cvec, ada_w, ada_b.reshape(depth, 1, n))


def _row_spec(tm, width):
    return pl.BlockSpec((1, tm, width), lambda b, j, *_: (b, j, 0))


def _mod_specs(nb, d):
    return [pl.BlockSpec((1, 6, d), lambda b, j, *_: (b, 0, 0)),
            pl.BlockSpec((1, 6, d), lambda b, j, *_: (nb, 0, 0))]


def _full_spec(shape):
    nd = len(shape)
    return pl.BlockSpec(shape, lambda *_: (0,) * nd)


def _inproj_kernel(x_ref, mx_ref, mt_ref, w_ref, z_ref, *, tm, ctx_len):
    is_ctx = _is_ctx_rows(tm, ctx_len)
    h = _rms(x_ref[0]) * (1.0 + _mod(mx_ref, mt_ref, 1, is_ctx)) + _mod(mx_ref, mt_ref, 0, is_ctx)
    z_ref[0] = _dot(h.astype(BF16), w_ref[...])


def _inproj(s, mods, w, *, ctx_len, tm):
    nb, lt, d = s.shape
    n = w.shape[1]
    return pl.pallas_call(
        functools.partial(_inproj_kernel, tm=tm, ctx_len=ctx_len),
        grid=(nb, lt // tm),
        in_specs=[_row_spec(tm, d)] + _mod_specs(nb, d) + [_full_spec((d, n))],
        out_specs=_row_spec(tm, n),
        out_shape=jax.ShapeDtypeStruct((nb, lt, n), F32),
        compiler_params=_cparams(("parallel", "parallel")),
        name="ab_inproj",
    )(s, mods, mods, w)


def _mixer_kernel(z_ref, hp_ref, hn_ref, x_ref, mx_ref, mt_ref, vg_ref, ws_ref, bs_ref, wp_ref,
                  psc_ref, wo_ref, xo_ref, h2_ref, *, tm, ctx_len, lt, chunk):
    is_ctx = _is_ctx_rows(tm, ctx_len)
    pos0 = pl.program_id(1) * tm
    z = z_ref[0]
    aw = 4 * LANES
    za = z[:, :2 * aw]
    gl = 0.5 * za * (1.0 + lax.erf(za * (2.0 ** -0.5)))
    u = gl[:, :aw]
    v = gl[:, aw:]
    vn = (_rms(v) * vg_ref[...]).astype(BF16)
    ya_rows = []
    for c in range(tm // chunk):
        rs = slice(c * chunk, (c + 1) * chunk)
        cols = []
        for g in range(4):
            cs = slice(g * LANES, (g + 1) * LANES)
            sg = _dot(ws_ref[g], vn[rs, cs]) + bs_ref[:, cs]
            cols.append(u[rs, cs] * sg)
        ya_rows.append(jnp.concatenate(cols, axis=1))
    ya = jnp.concatenate(ya_rows, axis=0)

    in_ctx = pos0 < ctx_len
    seq_start = jnp.where(in_ctx, 0, ctx_len)
    seq_len = jnp.where(in_ctx, ctx_len, lt - ctx_len)
    p_in = pos0 - seq_start + lax.broadcasted_iota(jnp.int32, (tm, 1), 0)
    has_prev = jnp.logical_and(pos0 != 0, pos0 != ctx_len).astype(F32)
    has_next = jnp.logical_and(pos0 + tm != ctx_len, pos0 + tm != lt).astype(F32)
    zp = z[:, 2 * aw:]
    ext = jnp.concatenate([hp_ref[0] * has_prev, zp, hn_ref[0] * has_next], axis=0)
    n = tm + 2 * HALO
    yb_cols = []
    for g, w in enumerate(POOL_WINDOWS):
        cs = slice(g * LANES, (g + 1) * LANES)
        a = ext[:, cs]
        step = 1
        while step < w:
            a = a + pltpu.roll(a, n - step, 0)
            step *= 2
        shift = HALO - w // 2
        if shift:
            a = pltpu.roll(a, n - shift, 0)
        wsum = a[:tm]
        lo = jnp.clip(p_in - w // 2, 0, seq_len)
        hi = jnp.clip(p_in - w // 2 + w, 0, seq_len)
        pooled = wsum / (hi - lo).astype(F32) - zp[:, cs]
        yb_cols.append(_dot(pooled.astype(BF16), wp_ref[g]) * psc_ref[:, cs])
    y = jnp.concatenate([ya] + yb_cols, axis=1).astype(BF16)
    xn = x_ref[0] + _mod(mx_ref, mt_ref, 2, is_ctx) * _dot(y, wo_ref[...])
    xo_ref[0] = xn
    h2 = _rms(xn) * (1.0 + _mod(mx_ref, mt_ref, 4, is_ctx)) + _mod(mx_ref, mt_ref, 3, is_ctx)
    h2_ref[0] = h2.astype(BF16)


def _mixer(z, s, mods, v_gain, w_s, bs_full, w_pool, p_scale, w_out, *, ctx_len, tm, chunk):
    nb, lt, d = s.shape
    aw = 4 * LANES
    nh = lt // HALO
    th = tm // HALO
    pw = 4 * LANES
    halo_prev = pl.BlockSpec((1, HALO, pw), lambda b, j: (b, jnp.maximum(j * th - 1, 0), 2))
    halo_next = pl.BlockSpec((1, HALO, pw), lambda b, j: (b, jnp.minimum((j + 1) * th, nh - 1), 2))
    return pl.pallas_call(
        functools.partial(_mixer_kernel, tm=tm, ctx_len=ctx_len, lt=lt, chunk=chunk),
        grid=(nb, lt // tm),
        in_specs=[_row_spec(tm, 2 * aw + pw), halo_prev, halo_next, _row_spec(tm, d)] + _mod_specs(nb, d) + [
            _full_spec((1, aw)), _full_spec(w_s.shape), _full_spec(bs_full.shape),
            _full_spec(w_pool.shape), _full_spec((1, pw)), _full_spec(w_out.shape)],
        out_specs=[_row_spec(tm, d), _row_spec(tm, d)],
        out_shape=[jax.ShapeDtypeStruct((nb, lt, d), F32), jax.ShapeDtypeStruct((nb, lt, d), BF16)],
        compiler_params=_cparams(("parallel", "parallel")),
        name="ab_mixer",
    )(z, z, z, s, mods, mods, v_gain, w_s, bs_full, w_pool, p_scale, w_out)


def _swiglu_kernel(h_ref, x_ref, mx_ref, mt_ref, wg_ref, wu_ref, wd_ref, o_ref, acc_ref, *, tm, ctx_len, n_k):
    k = pl.program_id(2)
    h = h_ref[0]
    a = _silu(_dot(h, wg_ref[...])) * _dot(h, wu_ref[...])
    part = _dot(a.astype(BF16), wd_ref[...])

    @pl.when(k == 0)
    def _():
        acc_ref[...] = part

    @pl.when(k > 0)
    def _():
        acc_ref[...] += part

    @pl.when(k == n_k - 1)
    def _():
        is_ctx = _is_ctx_rows(tm, ctx_len)
        o_ref[0] = x_ref[0] + _mod(mx_ref, mt_ref, 5, is_ctx) * acc_ref[...]


def _swiglu(h2, s, mods, w_gate, w_up, w_down, *, ctx_len, tm, tf):
    nb, lt, d = s.shape
    n_k = w_gate.shape[1] // tf
    return pl.pallas_call(
        functools.partial(_swiglu_kernel, tm=tm, ctx_len=ctx_len, n_k=n_k),
        grid=(nb, lt // tm, n_k),
        in_specs=[_row_spec(tm, d), _row_spec(tm, d)] + _mod_specs(nb, d) + [
            pl.BlockSpec((d, tf), lambda b, j, k: (0, k)),
            pl.BlockSpec((d, tf), lambda b, j, k: (0, k)),
            pl.BlockSpec((tf, d), lambda b, j, k: (k, 0))],
        out_specs=_row_spec(tm, d),
        out_shape=jax.ShapeDtypeStruct((nb, lt, d), F32),
        scratch_shapes=[pltpu.VMEM((tm, d), F32)],
        compiler_params=_cparams(("parallel", "parallel", "arbitrary")),
        name="swiglu",
    )(h2, s, mods, mods, w_gate, w_up, w_down)


def _seg_sum(x2, b_ref):
    hi = x2.astype(BF16)
    lo = (x2 - hi.astype(F32)).astype(BF16)
    return _dot(hi, b_ref[...]) + _dot(lo, b_ref[...])


def _rope_slabs(x, cos, sin, half):
    lane = lax.broadcasted_iota(jnp.int32, (x.shape[0], LANES), 1)
    first = (lane % (2 * half)) < half
    out = []
    for s in range(x.shape[1] // LANES):
        xs = x[:, s * LANES:(s + 1) * LANES]
        swapped = jnp.where(first, pltpu.roll(xs, LANES - half, 1), pltpu.roll(xs, half, 1))
        out.append(xs * cos + swapped * sin)
    return out[0] if len(out) == 1 else jnp.concatenate(out, axis=1)


def _qkv_kernel(x_ref, mx_ref, mt_ref, win_ref, cc_ref, sc_ref, cd_ref, sd_ref,
                qg_ref, kg_ref, cqg_ref, ckvg_ref, b512_ref, b128_ref,
                pct_ref, wqnt_ref, wqr_ref, prt_ref, wkn_ref, pkr_ref, wvt_ref, pkc_ref, eye_ref,
                qtc_ref, qtd_ref, kc_ref, kd_ref, vtc_ref, vtd_ref,
                *, tm, ctx_len, head_dim, d_qk):
    is_ctx = _is_ctx_rows(tm, ctx_len)
    h = _rms(x_ref[0]) * (1.0 + _mod(mx_ref, mt_ref, 1, is_ctx)) + _mod(mx_ref, mt_ref, 0, is_ctx)
    z = _dot(h.astype(BF16), win_ref[...])
    cos_c, sin_c, cos_d, sin_d = cc_ref[...], sc_ref[...], cd_ref[...], sd_ref[...]
    o_kc, o_vc, o_cq, o_ckv, o_kr = 512, 640, 768, 1152, 1408

    qc = z[:, :o_kc]
    qc = qc * lax.rsqrt(_seg_sum(qc * qc, b512_ref) * (1.0 / head_dim) + EPS) * qg_ref[...]
    qc = _rope_slabs(qc, cos_c, sin_c, head_dim // 2) * (head_dim ** -0.5 * LOG2E)
    qtc_ref[0] = _dot_nt(pct_ref[...], qc.astype(BF16)).astype(BF16)

    kc = z[:, o_kc:o_vc]
    kc = kc * lax.rsqrt(_seg_sum(kc * kc, b128_ref) * (1.0 / head_dim) + EPS) * kg_ref[...]
    kc = _rope_slabs(kc, cos_c, sin_c, head_dim // 2)
    kc_ref[0, 0] = _dot(kc.astype(BF16), pkc_ref[...]).astype(BF16)
    vtc_ref[0, 0] = _dot_nt(eye_ref[...], z[:, o_vc:o_cq].astype(BF16)).astype(BF16)

    cq = (_rms(z[:, o_cq:o_ckv]) * cqg_ref[...]) * (d_qk ** -0.5 * LOG2E)
    cqb = cq.astype(BF16)
    q_rope = _rope_slabs(_dot(cqb, wqr_ref[...]), cos_d, sin_d, 16)
    qtd = _dot_nt(wqnt_ref[...], cqb) + _dot_nt(prt_ref[...], q_rope.astype(BF16))
    qtd_ref[0] = qtd.astype(BF16)

    ckv = (_rms(z[:, o_ckv:o_kr]) * ckvg_ref[...]).astype(BF16)
    kr = _rope_slabs(z[:, o_kr:], cos_d, sin_d, 16)
    kd_ref[0, 0] = (_dot(ckv, wkn_ref[...]) + _dot(kr.astype(BF16), pkr_ref[...])).astype(BF16)
    vtd_ref[0, 0] = _dot_nt(wvt_ref[...], ckv).astype(BF16)


def _qkv(s, mods, w_in, tabs, consts, *, ctx_len, tm):
    nb, lt, d = s.shape
    nch = lt // tm
    tab_spec = pl.BlockSpec((tm, LANES), lambda b, j: (j, 0))
    out_shapes = [
        jax.ShapeDtypeStruct((nb, 8 * LANES, lt), BF16),
        jax.ShapeDtypeStruct((nb, 8 * LANES, lt), BF16),
        jax.ShapeDtypeStruct((nb, nch, tm, 2 * LANES), BF16),
        jax.ShapeDtypeStruct((nb, nch, tm, 8 * LANES), BF16),
        jax.ShapeDtypeStruct((nb, nch, 2 * 64, tm), BF16),
        jax.ShapeDtypeStruct((nb, nch, 8 * 64, tm), BF16),
    ]
    out_specs = [
        pl.BlockSpec((1, 8 * LANES, tm), lambda b, j: (b, 0, j)),
        pl.BlockSpec((1, 8 * LANES, tm), lambda b, j: (b, 0, j)),
        pl.BlockSpec((1, 1, tm, 2 * LANES), lambda b, j: (b, j, 0, 0)),
        pl.BlockSpec((1, 1, tm, 8 * LANES), lambda b, j: (b, j, 0, 0)),
        pl.BlockSpec((1, 1, 2 * 64, tm), lambda b, j: (b, j, 0, 0)),
        pl.BlockSpec((1, 1, 8 * 64, tm), lambda b, j: (b, j, 0, 0)),
    ]
    return pl.pallas_call(
        functools.partial(_qkv_kernel, tm=tm, ctx_len=ctx_len, head_dim=64, d_qk=96),
        grid=(nb, nch),
        in_specs=[_row_spec(tm, d)] + _mod_specs(nb, d) + [_full_spec(w_in.shape)] + [tab_spec] * 4
        + [_full_spec(c.shape) for c in consts],
        out_specs=out_specs,
        out_shape=out_shapes,
        compiler_params=_cparams(("parallel", "parallel")),
        name="cd_qkv",
    )(s, mods, mods, w_in, *tabs, *consts)


def _attn_kernel(qt_ref, k_ref, vt_ref, o_ref, s0_ref, s1_ref, c0_ref, c1_ref, m_ref, acc_ref,
                 *, kv_slots, tq, n_ctx, n_all, dv):
    heads = len(kv_slots)

    def scores(j, g, kg, dst):
        s_dst, c_dst = dst
        st = _dot(k_ref[0, j, :, kg * LANES:(kg + 1) * LANES], qt_ref[0, g * LANES:(g + 1) * LANES, :])
        s_dst[g] = st
        c_dst[g] = jnp.max(st, axis=0, keepdims=True)

    def consume(j, src, g, kg):
        s_src, c_src = src
        m_old = m_ref[g]
        m_new = jnp.maximum(m_old, c_src[g])
        alpha = jnp.exp2(m_old - m_new)
        p = jnp.exp2((s_src[g] - m_new).astype(BF16))
        m_ref[g] = m_new
        vt = jnp.concatenate([vt_ref[0, j, kg * dv:(kg + 1) * dv, :], jnp.ones((DEN_ROWS, p.shape[0]), BF16)],
                             axis=0)
        acc_ref[g] = alpha * acc_ref[g] + _dot(vt, p)

    def step(j, j_next, cur, nxt):
        for g, kg in enumerate(kv_slots):
            if j_next is not None:
                scores(j_next, g, kg, nxt)
            consume(j, cur, g, kg)

    def run(n):
        buf0, buf1 = (s0_ref, c0_ref), (s1_ref, c1_ref)
        m_ref[...] = jnp.full(m_ref.shape, NEG, F32)
        acc_ref[...] = jnp.zeros(acc_ref.shape, F32)
        for g, kg in enumerate(kv_slots):
            scores(0, g, kg, buf0)

        def pair(i, c):
            j = 2 * i
            step(j, j + 1, buf0, buf1)
            step(j + 1, jnp.minimum(j + 2, n - 1), buf1, buf0)
            return c

        lax.fori_loop(0, n // 2, pair, 0)
        if n % 2:
            step(n - 1, None, buf0, None)
        ot = jnp.concatenate([acc_ref[g, :dv, :] / acc_ref[g, dv:dv + 1, :] for g in range(heads)],
                             axis=0)
        o_ref[0] = ot.T.astype(o_ref.dtype)

    is_ctx_tile = pl.program_id(1) * tq < n_ctx * k_ref.shape[2]

    @pl.when(is_ctx_tile)
    def _():
        run(n_ctx)

    @pl.when(jnp.logical_not(is_ctx_tile))
    def _():
        run(n_all)


def _attention(qt, k, vt, *, kv_slots, ctx_len, tq, name):
    nb, hq, lt = qt.shape
    heads = len(kv_slots)
    assert hq == heads * LANES
    nch, tk = k.shape[1], k.shape[2]
    dv = vt.shape[2] // (k.shape[3] // LANES)
    once = pl.Buffered(1)
    return pl.pallas_call(
        functools.partial(_attn_kernel, kv_slots=kv_slots, tq=tq, n_ctx=ctx_len // tk, n_all=nch, dv=dv),
        grid=(nb, lt // tq),
        in_specs=[
            pl.BlockSpec((1, hq, tq), lambda b, q: (b, 0, q)),
            pl.BlockSpec((1, nch, tk, k.shape[3]), lambda b, q: (b, 0, 0, 0), pipeline_mode=once),
            pl.BlockSpec((1, nch, vt.shape[2], tk), lambda b, q: (b, 0, 0, 0), pipeline_mode=once),
        ],
        out_specs=pl.BlockSpec((1, tq, heads * dv), lambda b, q: (b, q, 0)),
        out_shape=jax.ShapeDtypeStruct((nb, lt, heads * dv), BF16),
        scratch_shapes=[pltpu.VMEM((heads, tk, tq), F32), pltpu.VMEM((heads, tk, tq), F32),
                        pltpu.VMEM((heads, 1, tq), F32), pltpu.VMEM((heads, 1, tq), F32),
                        pltpu.VMEM((heads, 1, tq), F32), pltpu.VMEM((heads, dv + DEN_ROWS, tq), F32)],
        compiler_params=_cparams(("parallel", "arbitrary")),
        name=name,
    )(qt, k, vt)


def _oproj_kernel(oc_ref, od_ref, x_ref, mx_ref, mt_ref, w_ref, r_ref, xo_ref, h2_ref, g_ref,
                  *, tm, ctx_len, n_exp):
    is_ctx = _is_ctx_rows(tm, ctx_len)
    o = jnp.concatenate([oc_ref[0], od_ref[0]], axis=-1)
    xn = x_ref[0] + _mod(mx_ref, mt_ref, 2, is_ctx) * _dot(o, w_ref[...])
    xo_ref[0] = xn
    h2 = _rms(xn) * (1.0 + _mod(mx_ref, mt_ref, 4, is_ctx)) + _mod(mx_ref, mt_ref, 3, is_ctx)
    h2_ref[0] = h2
    logits = jnp.dot(h2, r_ref[...], preferred_element_type=F32, precision=lax.Precision.HIGHEST)
    lane = lax.broadcasted_iota(jnp.int32, (tm, LANES), 1).astype(F32)
    lg = jnp.where(lane < n_exp, logits, NEG)
    m1 = jnp.max(lg, axis=-1, keepdims=True)
    i1 = jnp.min(jnp.where(lg == m1, lane, float(LANES)), axis=-1, keepdims=True)
    lg2 = jnp.where(lane == i1, NEG, lg)
    m2 = jnp.max(lg2, axis=-1, keepdims=True)
    i2 = jnp.min(jnp.where(lg2 == m2, lane, float(LANES)), axis=-1, keepdims=True)
    e2 = jnp.exp(m2 - m1)
    den = 1.0 + e2
    g_ref[0] = jnp.where(lane == 0.0, i1, jnp.where(lane == 1.0, i2, jnp.where(lane == 2.0, 1.0 / den, e2 / den)))


def _oproj(oc, od, s, mods, w_out, router, *, ctx_len, tm, n_exp):
    nb, lt, d = s.shape
    return pl.pallas_call(
        functools.partial(_oproj_kernel, tm=tm, ctx_len=ctx_len, n_exp=n_exp),
        grid=(nb, lt // tm),
        in_specs=[_row_spec(tm, oc.shape[2]), _row_spec(tm, od.shape[2]), _row_spec(tm, d)]
        + _mod_specs(nb, d) + [_full_spec(w_out.shape), _full_spec(router.shape)],
        out_specs=[_row_spec(tm, d), _row_spec(tm, d), _row_spec(tm, LANES)],
        out_shape=[jax.ShapeDtypeStruct((nb, lt, d), F32), jax.ShapeDtypeStruct((nb, lt, d), F32),
                   jax.ShapeDtypeStruct((nb, lt, LANES), F32)],
        compiler_params=_cparams(("parallel", "parallel")),
        name="cd_oproj_router",
    )(oc, od, s, mods, mods, w_out, router)


def _sc_gather(table, idx, *, chunk=SC_CHUNK):
    n_rows, d = idx.shape[0], table.shape[1]
    info = plsc.get_sparse_core_info()
    n_workers = info.num_cores * info.num_subcores
    per_w = n_rows // n_workers
    assert n_rows % (n_workers * chunk) == 0 and chunk % 8 == 0 and chunk <= LANES
    mesh = plsc.VectorSubcoreMesh(core_axis_name="c", subcore_axis_name="s")

    def body(table_hbm, idx_hbm, out_hbm, idx_v, rows_v, sem):
        wid = lax.axis_index("s") * info.num_cores + lax.axis_index("c")
        base = wid * per_w
        pltpu.sync_copy(idx_hbm.at[pl.ds(base, per_w)], idx_v)

        @pl.loop(0, per_w // chunk)
        def _(i):
            off = pl.multiple_of(i * chunk, chunk)
            pltpu.async_copy(table_hbm.at[idx_v.at[pl.ds(off, chunk)]], rows_v, sem).wait()
            pltpu.sync_copy(rows_v, out_hbm.at[pl.ds(base + off, chunk)])

    return pl.kernel(
        body,
        out_type=jax.ShapeDtypeStruct((n_rows, d), table.dtype),
        mesh=mesh,
        scratch_types=[pltpu.VMEM((per_w,), jnp.int32), pltpu.VMEM((chunk, d), table.dtype),
                       pltpu.SemaphoreType.DMA],
        name="sc_row_gather",
    )(table, idx)


def _route_plan(route, *, n_exp, ts, align):
    n_tok = route.shape[0]
    experts = route[:, :2].astype(jnp.int32)
    onehot = (experts[:, :, None] == jnp.arange(n_exp)[None, None, :]).astype(jnp.int32).sum(axis=1)
    before = jnp.cumsum(onehot, axis=0) - onehot
    counts = onehot.sum(axis=0)
    gsz = (counts + ts - 1) // ts * ts
    gend = jnp.cumsum(gsz)
    gstart = gend - gsz
    pos = gstart[experts] + jnp.take_along_axis(before, experts, axis=1)
    n_slots = -(-(2 * n_tok + n_exp * ts) // align) * align
    tok = jnp.zeros((n_slots,), jnp.int32).at[pos.reshape(-1)].set(
        jnp.repeat(jnp.arange(n_tok, dtype=jnp.int32), 2))
    n_tiles = n_slots // ts
    tile_start = jnp.arange(n_tiles, dtype=jnp.int32) * ts
    n_used = (gend[-1] // ts).astype(jnp.int32)
    tile_expert = jnp.minimum(jnp.searchsorted(gend, tile_start, side="right"), n_exp - 1).astype(jnp.int32)
    last_used = tile_expert[jnp.maximum(n_used - 1, 0)]
    tile_expert = jnp.where(jnp.arange(n_tiles) < n_used, tile_expert, last_used)
    return tok, pos.T.reshape(-1).astype(jnp.int32), tile_expert, n_used.reshape(1)


def _gffn_kernel(te_ref, nu_ref, h_ref, wg_ref, wu_ref, wd_ref, o_ref, acc_ref, *, n_k):
    j = pl.program_id(0)
    k = pl.program_id(1)
    used = j < nu_ref[0]

    @pl.when(used)
    def _():
        h = h_ref[...].astype(BF16)
        a = _silu(_dot(h, wg_ref[0])) * _dot(h, wu_ref[0])
        part = _dot(a.astype(BF16), wd_ref[0])

        @pl.when(k == 0)
        def _():
            acc_ref[...] = part

        @pl.when(k > 0)
        def _():
            acc_ref[...] += part

        @pl.when(k == n_k - 1)
        def _():
            o_ref[...] = acc_ref[...]

    @pl.when(jnp.logical_and(jnp.logical_not(used), k == n_k - 1))
    def _():
        o_ref[...] = jnp.zeros(o_ref.shape, o_ref.dtype)


def _grouped_ffn(hs, tile_expert, n_used, w_gate, w_up, w_down, *, ts, tf):
    n_slots, d = hs.shape
    f = w_gate.shape[-1]
    n_k = f // tf
    grid_spec = pltpu.PrefetchScalarGridSpec(
        num_scalar_prefetch=2,
        grid=(n_slots // ts, n_k),
        in_specs=[
            pl.BlockSpec((ts, d), lambda j, k, te, nu: (j, 0)),
            pl.BlockSpec((1, d, tf), lambda j, k, te, nu: (te[j], 0, k)),
            pl.BlockSpec((1, d, tf), lambda j, k, te, nu: (te[j], 0, k)),
            pl.BlockSpec((1, tf, d), lambda j, k, te, nu: (te[j], k, 0)),
        ],
        out_specs=pl.BlockSpec((ts, d), lambda j, k, te, nu: (j, 0)),
        scratch_shapes=[pltpu.VMEM((ts, d), F32)],
    )
    return pl.pallas_call(
        functools.partial(_gffn_kernel, n_k=n_k),
        grid_spec=grid_spec,
        out_shape=jax.ShapeDtypeStruct((n_slots, d), F32),
        compiler_params=_cparams(("parallel", "arbitrary")),
        name="moe_grouped_ffn",
    )(tile_expert, n_used, hs, w_gate, w_up, w_down)


def _combine_kernel(y1_ref, y2_ref, r_ref, x_ref, mx_ref, mt_ref, o_ref, *, tm, ctx_len):
    is_ctx = _is_ctx_rows(tm, ctx_len)
    r = r_ref[0]
    f = r[:, 2:3] * y1_ref[0, 0] + r[:, 3:4] * y2_ref[0, 0]
    o_ref[0] = x_ref[0] + _mod(mx_ref, mt_ref, 5, is_ctx) * f


def _combine(yg, route, s, mods, *, ctx_len, tm):
    nb, lt, d = s.shape
    return pl.pallas_call(
        functools.partial(_combine_kernel, tm=tm, ctx_len=ctx_len),
        grid=(nb, lt // tm),
        in_specs=[pl.BlockSpec((1, 1, tm, d), lambda b, j: (0, b, j, 0)),
                  pl.BlockSpec((1, 1, tm, d), lambda b, j: (1, b, j, 0)),
                  _row_spec(tm, LANES), _row_spec(tm, d)] + _mod_specs(nb, d),
        out_specs=_row_spec(tm, d),
        out_shape=jax.ShapeDtypeStruct((nb, lt, d), F32),
        compiler_params=_cparams(("parallel", "parallel")),
        name="moe_combine",
    )(yg, yg, route, s, mods, mods)


def _moe(h2, route, s, mods, w_gate, w_up, w_down, *, ctx_len, n_exp, ts, tf, tm):
    nb, lt, d = s.shape
    n_tok = nb * lt
    info = plsc.get_sparse_core_info()
    sc_rows = info.num_cores * info.num_subcores * SC_CHUNK
    tok, pos, tile_expert, n_used = _route_plan(route.reshape(n_tok, LANES), n_exp=n_exp, ts=ts,
                                                align=int(np.lcm(ts, sc_rows)))
    hs = _sc_gather(h2.reshape(n_tok, d), tok)
    ys = _grouped_ffn(hs, tile_expert, n_used, w_gate, w_up, w_down, ts=ts, tf=tf)
    yg = _sc_gather(ys, pos).reshape(2, nb, lt, d)
    return _combine(yg, route, s, mods, ctx_len=ctx_len, tm=tm)


def _final_kernel(x_ref, g_ref, o_ref):
    o_ref[0] = _rms(x_ref[0]) * g_ref[...]


def _final(s, gain, *, ctx_len, tm):
    nb, lt, d = s.shape
    seq = lt - ctx_len
    off = ctx_len // tm
    return pl.pallas_call(
        _final_kernel,
        grid=(nb, seq // tm),
        in_specs=[pl.BlockSpec((1, tm, d), lambda b, j: (b, j + off, 0)), _full_spec((1, d))],
        out_specs=_row_spec(tm, d),
        out_shape=jax.ShapeDtypeStruct((nb, seq, d), F32),
        compiler_params=_cparams(("parallel", "parallel")),
        name="final_norm",
    )(s, gain.reshape(1, d))


def _rope_tables(seq, ctx_len):
    rows = seq // GRID_W
    row = jnp.repeat(jnp.arange(rows), GRID_W).astype(F32)
    col = jnp.tile(jnp.arange(GRID_W), rows).astype(F32)

    def table(rot_dim, reps):
        axis_dim = rot_dim // 2
        inv = ROPE_THETA ** (-jnp.arange(0, axis_dim, 2, dtype=F32) / axis_dim)
        ang = jnp.concatenate([row[:, None] * inv, col[:, None] * inv], axis=-1)
        cos = jnp.concatenate([jnp.ones((ctx_len, axis_dim), F32), jnp.cos(ang)], axis=0)
        sin = jnp.concatenate([jnp.zeros((ctx_len, axis_dim), F32), jnp.sin(ang)], axis=0)
        return (jnp.tile(jnp.concatenate([cos, cos], axis=-1), (1, reps)),
                jnp.tile(jnp.concatenate([-sin, sin], axis=-1), (1, reps)))

    cos_c, sin_c = table(64, 2)
    cos_d, sin_d = table(32, 4)
    return cos_c, sin_c, cos_d, sin_d


def _placement(n_rows, n_cols, pairs):
    m = np.zeros((n_rows, n_cols), np.float32)
    for r, c in pairs:
        m[r, c] = 1.0
    return jnp.asarray(m, BF16)


def _cd_consts(q_gain, k_gain, cq_gain, ckv_gain, w_uq, w_ukv):
    hd, nh, nkv, d_nope, d_rope, d_v = 64, 8, 2, 64, 32, 64
    d_qk = d_nope + d_rope
    heads = np.arange(nh)
    b512 = jnp.asarray(np.kron(np.eye(nh), np.ones((hd, hd))), BF16)
    b128 = jnp.asarray(np.kron(np.eye(nkv), np.ones((hd, hd))), BF16)
    pct = _placement(nh * LANES, nh * hd, [(h * LANES + i, h * hd + i) for h in heads for i in range(hd)])
    nope_cols = (heads[:, None] * d_qk + np.arange(d_nope)[None, :]).reshape(-1)
    rope_cols = (heads[:, None] * d_qk + d_nope + np.arange(d_rope)[None, :]).reshape(-1)
    nope_rows = (heads[:, None] * LANES + np.arange(d_nope)[None, :]).reshape(-1)
    wqnt = jnp.zeros((nh * LANES, w_uq.shape[0]), F32).at[nope_rows].set(w_uq[:, nope_cols].T).astype(BF16)
    wqr = w_uq[:, rope_cols].astype(BF16)
    prt = _placement(nh * LANES, nh * d_rope,
                     [(h * LANES + d_nope + i, h * d_rope + i) for h in heads for i in range(d_rope)])
    kn_cols = (heads[:, None] * (d_nope + d_v) + np.arange(d_nope)[None, :]).reshape(-1)
    v_cols = (heads[:, None] * (d_nope + d_v) + d_nope + np.arange(d_v)[None, :]).reshape(-1)
    wkn = jnp.zeros((w_ukv.shape[0], nh * LANES), F32).at[:, nope_rows].set(w_ukv[:, kn_cols]).astype(BF16)
    pkr = _placement(LANES, nh * LANES, [(i, h * LANES + d_nope + i) for h in heads for i in range(d_rope)])
    wvt = w_ukv[:, v_cols].T.astype(BF16)
    pkc = _placement(nkv * hd, nkv * LANES, [(h * hd + i, h * LANES + i) for h in range(nkv) for i in range(hd)])
    eye = jnp.asarray(np.eye(nkv * hd), BF16)
    return [jnp.tile(q_gain, nh).reshape(1, -1), jnp.tile(k_gain, nkv).reshape(1, -1),
            cq_gain.reshape(1, -1), ckv_gain.reshape(1, -1), b512, b128,
            pct, wqnt, wqr, prt, wkn, pkr, wvt, pkc, eye]


def _largest_tile(n, candidates):
    for t in candidates:
        if n % t == 0:
            return t
    raise ValueError(f"no tile in {candidates} divides {n}")


def kernel(x, c, ctx, c_ctx, ada_w, ada_b, ab_w_in, a_v_gain, a_w_s, a_b_s, b_w_pool, b_scale, ab_w_out,
           ffn_w_gate, ffn_w_up, ffn_w_down, cd_w_in, c_q_gain, c_k_gain, d_cq_gain, d_ckv_gain, d_w_uq,
           d_w_ukv, cd_w_out, moe_router, moe_w_gate, moe_w_up, moe_w_down, final_gain):
    nb, seq, d = x.shape
    ctx_len = ctx.shape[1]
    lt = ctx_len + seq
    depth = ada_w.shape[0]
    n_exp = moe_router.shape[-1]
    chunk = a_w_s.shape[-1]
    tr = 256
    assert ctx_len % tr == 0 and seq % tr == 0 and seq % GRID_W == 0
    tm_ff = _largest_tile(lt, (384, 256))
    tf = ffn_w_gate.shape[-1] // 2

    s = jnp.concatenate([ctx, x], axis=1)

    pad = -(nb + 1) % 8
    cvec = jnp.concatenate([c, c_ctx[None, :], jnp.zeros((pad, d), F32)], axis=0)
    mods_all = _ada_all(cvec, ada_w, ada_b).reshape(depth, nb + 1 + pad, 6, d)
    tabs = _rope_tables(seq, ctx_len)

    for layer in range(depth):
        i = layer // 2
        mods = mods_all[layer]
        if layer % 2 == 0:
            z = _inproj(s, mods, ab_w_in[i].astype(BF16), ctx_len=ctx_len, tm=tr)
            bs_full = jnp.repeat(a_b_s[i].T, LANES, axis=1)
            s, h2 = _mixer(z, s, mods, a_v_gain[i].reshape(1, -1), a_w_s[i].astype(BF16), bs_full,
                           b_w_pool[i].astype(BF16), b_scale[i].reshape(1, -1), ab_w_out[i].astype(BF16),
                           ctx_len=ctx_len, tm=tr, chunk=chunk)
            s = _swiglu(h2, s, mods, ffn_w_gate[i].astype(BF16), ffn_w_up[i].astype(BF16),
                        ffn_w_down[i].astype(BF16), ctx_len=ctx_len, tm=tm_ff, tf=tf)
        else:
            w_in = jnp.pad(cd_w_in[i], ((0, 0), (0, 12 * LANES - cd_w_in.shape[-1]))).astype(BF16)
            consts = _cd_consts(c_q_gain[i], c_k_gain[i], d_cq_gain[i], d_ckv_gain[i], d_w_uq[i], d_w_ukv[i])
            qtc, qtd, kc, kd, vtc, vtd = _qkv(s, mods, w_in, tabs, consts, ctx_len=ctx_len, tm=tr)
            oc = _attention(qtc, kc, vtc, kv_slots=(0, 0, 0, 0, 1, 1, 1, 1), ctx_len=ctx_len, tq=tr,
                            name="attn_gqa")
            od = _attention(qtd, kd, vtd, kv_slots=tuple(range(8)), ctx_len=ctx_len, tq=tr, name="attn_mla")
            router = jnp.pad(moe_router[i], ((0, 0), (0, LANES - n_exp)))
            s, h2, route = _oproj(oc, od, s, mods, cd_w_out[i].astype(BF16), router,
                                  ctx_len=ctx_len, tm=tr, n_exp=n_exp)
            s = _moe(h2, route, s, mods, moe_w_gate[i].astype(BF16), moe_w_up[i].astype(BF16),
                     moe_w_down[i].astype(BF16), ctx_len=ctx_len, n_exp=n_exp, ts=512, tf=tf, tm=tr)

    return _final(s, final_gain, ctx_len=ctx_len, tm=tr)
```

```python
import functools

import numpy as np
import jax
import jax.numpy as jnp
from jax import lax
from jax.experimental import pallas as pl
from jax.experimental.pallas import tpu as pltpu
from jax.experimental.pallas import tpu_sc as plsc

F32 = jnp.float32
BF16 = jnp.bfloat16
EPS = 1e-6
ROPE_THETA = 10000.0
GRID_W = 64
LANES = 128
HALO = 8
POOL_WINDOWS = (2, 4, 8, 16)
NEG = -1e30
LOG2E = 1.4426950408889634
DEN_ROWS = 16
SC_CHUNK = 64
VMEM_LIMIT = 56 * 1024 * 1024


def _cparams(sem, flags=None):
    return pltpu.CompilerParams(dimension_semantics=sem, vmem_limit_bytes=VMEM_LIMIT, flags=flags)


def _rms(x):
    return x * lax.rsqrt(jnp.mean(x * x, axis=-1, keepdims=True) + EPS)


def _is_ctx_rows(tm, ctx_len):
    pos = pl.program_id(1) * tm + lax.broadcasted_iota(jnp.int32, (tm, 1), 0)
    return pos < ctx_len


def _mod(mx_ref, mt_ref, idx, is_ctx):
    return jnp.where(is_ctx, mt_ref[0, idx:idx + 1, :], mx_ref[0, idx:idx + 1, :])


def _dot(a, b):
    return jnp.dot(a, b, preferred_element_type=F32)


def _dot_nt(a, b):
    return lax.dot_general(a, b, (((1,), (1,)), ((), ())), preferred_element_type=F32)


def _silu(x):
    return x * jax.nn.sigmoid(x)


PACKED = jnp.uint32


def _packed_width(d):
    return d // 2


def _pack_pairs(x):
    w = x.shape[1] // 2
    xb = x.astype(BF16).astype(F32)
    hi = pltpu.bitcast(xb[:, :w], jnp.uint32)
    lo = pltpu.bitcast(xb[:, w:], jnp.uint32)
    return hi | (lo >> 16)


def _unpack_pairs(p):
    hi = pltpu.bitcast(p & jnp.uint32(0xFFFF0000), F32)
    lo = pltpu.bitcast(p << 16, F32)
    return jnp.concatenate([hi, lo], axis=1)


def _ada_kernel(c_ref, w_ref, b_ref, o_ref):
    c = c_ref[...]
    o_ref[0] = jnp.dot(_silu(c), w_ref[0], preferred_element_type=F32,
                       precision=lax.Precision.HIGHEST) + b_ref[0]


def _ada_all(cvec, ada_w, ada_b):
    depth, d, n = ada_w.shape
    rows = cvec.shape[0]
    tn = 1536
    return pl.pallas_call(
        _ada_kernel,
        grid=(depth, n // tn),
        in_specs=[
            pl.BlockSpec((rows, d), lambda l, j: (0, 0)),
            pl.BlockSpec((1, d, tn), lambda l, j: (l, 0, j)),
            pl.BlockSpec((1, 1, tn), lambda l, j: (l, 0, j)),
        ],
        out_specs=pl.BlockSpec((1, rows, tn), lambda l, j: (l, 0, j)),
        out_shape=jax.ShapeDtypeStruct((depth, rows, n), F32),
        compiler_params=_cparams(("arbitrary", "arbitrary")),
        name="adaln",
    )(cvec, ada_w, ada_b.reshape(depth, 1, n))


def _row_spec(tm, width):
    return pl.BlockSpec((1, tm, width), lambda b, j, *_: (b, j, 0))


def _mod_specs(nb, d):
    return [pl.BlockSpec((1, 6, d), lambda b, j, *_: (b, 0, 0)),
            pl.BlockSpec((1, 6, d), lambda b, j, *_: (nb, 0, 0))]


def _full_spec(shape):
    nd = len(shape)
    return pl.BlockSpec(shape, lambda *_: (0,) * nd)


def _inproj_kernel(x_ref, mx_ref, mt_ref, w_ref, z_ref, *, tm, ctx_len):
    is_ctx = _is_ctx_rows(tm, ctx_len)
    h = _rms(x_ref[0]) * (1.0 + _mod(mx_ref, mt_ref, 1, is_ctx)) + _mod(mx_ref, mt_ref, 0, is_ctx)
    z_ref[0] = _dot(h.astype(BF16), w_ref[...])


def _inproj(s, mods, w, *, ctx_len, tm):
    nb, lt, d = s.shape
    n = w.shape[1]
    return pl.pallas_call(
        functools.partial(_inproj_kernel, tm=tm, ctx_len=ctx_len),
        grid=(nb, lt // tm),
        in_specs=[_row_spec(tm, d)] + _mod_specs(nb, d) + [_full_spec((d, n))],
        out_specs=_row_spec(tm, n),
        out_shape=jax.ShapeDtypeStruct((nb, lt, n), F32),
        compiler_params=_cparams(("parallel", "parallel")),
        name="ab_inproj",
    )(s, mods, mods, w)


def _mixer_kernel(z_ref, hp_ref, hn_ref, x_ref, mx_ref, mt_ref, vg_ref, ws_ref, bs_ref, wp_ref,
                  psc_ref, wo_ref, xo_ref, h2_ref, *, tm, ctx_len, lt, chunk):
    is_ctx = _is_ctx_rows(tm, ctx_len)
    pos0 = pl.program_id(1) * tm
    z = z_ref[0]
    aw = 4 * LANES
    za = z[:, :2 * aw]
    gl = 0.5 * za * (1.0 + lax.erf(za * (2.0 ** -0.5)))
    u = gl[:, :aw]
    v = gl[:, aw:]
    vn = (_rms(v) * vg_ref[...]).astype(BF16)
    ya_rows = []
    for c in range(tm // chunk):
        rs = slice(c * chunk, (c + 1) * chunk)
        cols = []
        for g in range(4):
            cs = slice(g * LANES, (g + 1) * LANES)
            sg = _dot(ws_ref[g], vn[rs, cs]) + bs_ref[:, cs]
            cols.append(u[rs, cs] * sg)
        ya_rows.append(jnp.concatenate(cols, axis=1))
    ya = jnp.concatenate(ya_rows, axis=0)

    in_ctx = pos0 < ctx_len
    seq_start = jnp.where(in_ctx, 0, ctx_len)
    seq_len = jnp.where(in_ctx, ctx_len, lt - ctx_len)
    p_in = pos0 - seq_start + lax.broadcasted_iota(jnp.int32, (tm, 1), 0)
    has_prev = jnp.logical_and(pos0 != 0, pos0 != ctx_len).astype(F32)
    has_next = jnp.logical_and(pos0 + tm != ctx_len, pos0 + tm != lt).astype(F32)
    zp = z[:, 2 * aw:]
    ext = jnp.concatenate([hp_ref[0] * has_prev, zp, hn_ref[0] * has_next], axis=0)
    n = tm + 2 * HALO
    yb_cols = []
    for g, w in enumerate(POOL_WINDOWS):
        cs = slice(g * LANES, (g + 1) * LANES)
        a = ext[:, cs]
        step = 1
        while step < w:
            a = a + pltpu.roll(a, n - step, 0)
            step *= 2
        shift = HALO - w // 2
        if shift:
            a = pltpu.roll(a, n - shift, 0)
        wsum = a[:tm]
        lo = jnp.clip(p_in - w // 2, 0, seq_len)
        hi = jnp.clip(p_in - w // 2 + w, 0, seq_len)
        pooled = wsum / (hi - lo).astype(F32) - zp[:, cs]
        yb_cols.append(_dot(pooled.astype(BF16), wp_ref[g]) * psc_ref[:, cs])
    y = jnp.concatenate([ya] + yb_cols, axis=1).astype(BF16)
    xn = x_ref[0] + _mod(mx_ref, mt_ref, 2, is_ctx) * _dot(y, wo_ref[...])
    xo_ref[0] = xn
    h2 = _rms(xn) * (1.0 + _mod(mx_ref, mt_ref, 4, is_ctx)) + _mod(mx_ref, mt_ref, 3, is_ctx)
    h2_ref[0] = h2.astype(BF16)


def _mixer(z, s, mods, v_gain, w_s, bs_full, w_pool, p_scale, w_out, *, ctx_len, tm, chunk):
    nb, lt, d = s.shape
    aw = 4 * LANES
    nh = lt // HALO
    th = tm // HALO
    pw = 4 * LANES
    halo_prev = pl.BlockSpec((1, HALO, pw), lambda b, j: (b, jnp.maximum(j * th - 1, 0), 2))
    halo_next = pl.BlockSpec((1, HALO, pw), lambda b, j: (b, jnp.minimum((j + 1) * th, nh - 1), 2))
    return pl.pallas_call(
        functools.partial(_mixer_kernel, tm=tm, ctx_len=ctx_len, lt=lt, chunk=chunk),
        grid=(nb, lt // tm),
        in_specs=[_row_spec(tm, 2 * aw + pw), halo_prev, halo_next, _row_spec(tm, d)] + _mod_specs(nb, d) + [
            _full_spec((1, aw)), _full_spec(w_s.shape), _full_spec(bs_full.shape),
            _full_spec(w_pool.shape), _full_spec((1, pw)), _full_spec(w_out.shape)],
        out_specs=[_row_spec(tm, d), _row_spec(tm, d)],
        out_shape=[jax.ShapeDtypeStruct((nb, lt, d), F32), jax.ShapeDtypeStruct((nb, lt, d), BF16)],
        compiler_params=_cparams(("parallel", "parallel")),
        name="ab_mixer",
    )(z, z, z, s, mods, mods, v_gain, w_s, bs_full, w_pool, p_scale, w_out)


def _swiglu_kernel(h_ref, x_ref, mx_ref, mt_ref, wg_ref, wu_ref, wd_ref, o_ref, acc_ref, *, tm, ctx_len, n_k):
    k = pl.program_id(2)
    h = h_ref[0]
    a = _silu(_dot(h, wg_ref[...])) * _dot(h, wu_ref[...])
    part = _dot(a.astype(BF16), wd_ref[...])

    @pl.when(k == 0)
    def _():
        acc_ref[...] = part

    @pl.when(k > 0)
    def _():
        acc_ref[...] += part

    @pl.when(k == n_k - 1)
    def _():
        is_ctx = _is_ctx_rows(tm, ctx_len)
        o_ref[0] = x_ref[0] + _mod(mx_ref, mt_ref, 5, is_ctx) * acc_ref[...]


def _swiglu(h2, s, mods, w_gate, w_up, w_down, *, ctx_len, tm, tf):
    nb, lt, d = s.shape
    n_k = w_gate.shape[1] // tf
    return pl.pallas_call(
        functools.partial(_swiglu_kernel, tm=tm, ctx_len=ctx_len, n_k=n_k),
        grid=(nb, lt // tm, n_k),
        in_specs=[_row_spec(tm, d), _row_spec(tm, d)] + _mod_specs(nb, d) + [
            pl.BlockSpec((d, tf), lambda b, j, k: (0, k)),
            pl.BlockSpec((d, tf), lambda b, j, k: (0, k)),
            pl.BlockSpec((tf, d), lambda b, j, k: (k, 0))],
        out_specs=_row_spec(tm, d),
        out_shape=jax.ShapeDtypeStruct((nb, lt, d), F32),
        scratch_shapes=[pltpu.VMEM((tm, d), F32)],
        compiler_params=_cparams(("parallel", "parallel", "arbitrary")),
        name="swiglu",
    )(h2, s, mods, mods, w_gate, w_up, w_down)


def _seg_sum(x2, b_ref):
    hi = x2.astype(BF16)
    lo = (x2 - hi.astype(F32)).astype(BF16)
    return _dot(hi, b_ref[...]) + _dot(lo, b_ref[...])


def _rope_slabs(x, cos, sin, half):
    lane = lax.broadcasted_iota(jnp.int32, (x.shape[0], LANES), 1)
    first = (lane % (2 * half)) < half
    out = []
    for s in range(x.shape[1] // LANES):
        xs = x[:, s * LANES:(s + 1) * LANES]
        swapped = jnp.where(first, pltpu.roll(xs, LANES - half, 1), pltpu.roll(xs, half, 1))
        out.append(xs * cos + swapped * sin)
    return out[0] if len(out) == 1 else jnp.concatenate(out, axis=1)


def _qkv_kernel(x_ref, mx_ref, mt_ref, win_ref, cc_ref, sc_ref, cd_ref, sd_ref,
                qg_ref, kg_ref, cqg_ref, ckvg_ref, b512_ref, b128_ref,
                pct_ref, wqnt_ref, wqr_ref, prt_ref, wkn_ref, pkr_ref, wvt_ref, pkc_ref, eye_ref,
                qtc_ref, qtd_ref, kc_ref, kd_ref, vtc_ref, vtd_ref,
                *, tm, ctx_len, head_dim, d_qk):
    is_ctx = _is_ctx_rows(tm, ctx_len)
    h = _rms(x_ref[0]) * (1.0 + _mod(mx_ref, mt_ref, 1, is_ctx)) + _mod(mx_ref, mt_ref, 0, is_ctx)
    z = _dot(h.astype(BF16), win_ref[...])
    cos_c, sin_c, cos_d, sin_d = cc_ref[...], sc_ref[...], cd_ref[...], sd_ref[...]
    o_kc, o_vc, o_cq, o_ckv, o_kr = 512, 640, 768, 1152, 1408

    qc = z[:, :o_kc]
    qc = qc * lax.rsqrt(_seg_sum(qc * qc, b512_ref) * (1.0 / head_dim) + EPS) * qg_ref[...]
    qc = _rope_slabs(qc, cos_c, sin_c, head_dim // 2) * (head_dim ** -0.5 * LOG2E)
    qtc_ref[0] = _dot_nt(pct_ref[...], qc.astype(BF16)).astype(BF16)

    kc = z[:, o_kc:o_vc]
    kc = kc * lax.rsqrt(_seg_sum(kc * kc, b128_ref) * (1.0 / head_dim) + EPS) * kg_ref[...]
    kc = _rope_slabs(kc, cos_c, sin_c, head_dim // 2)
    kc_ref[0, 0] = _dot(kc.astype(BF16), pkc_ref[...]).astype(BF16)
    vtc_ref[0, 0] = _dot_nt(eye_ref[...], z[:, o_vc:o_cq].astype(BF16)).astype(BF16)

    cq = (_rms(z[:, o_cq:o_ckv]) * cqg_ref[...]) * (d_qk ** -0.5 * LOG2E)
    cqb = cq.astype(BF16)
    q_rope = _rope_slabs(_dot(cqb, wqr_ref[...]), cos_d, sin_d, 16)
    qtd = _dot_nt(wqnt_ref[...], cqb) + _dot_nt(prt_ref[...], q_rope.astype(BF16))
    qtd_ref[0] = qtd.astype(BF16)

    ckv = (_rms(z[:, o_ckv:o_kr]) * ckvg_ref[...]).astype(BF16)
    kr = _rope_slabs(z[:, o_kr:], cos_d, sin_d, 16)
    kd_ref[0, 0] = (_dot(ckv, wkn_ref[...]) + _dot(kr.astype(BF16), pkr_ref[...])).astype(BF16)
    vtd_ref[0, 0] = _dot_nt(wvt_ref[...], ckv).astype(BF16)


def _qkv(s, mods, w_in, tabs, consts, *, ctx_len, tm):
    nb, lt, d = s.shape
    nch = lt // tm
    tab_spec = pl.BlockSpec((tm, LANES), lambda b, j: (j, 0))
    out_shapes = [
        jax.ShapeDtypeStruct((nb, 8 * LANES, lt), BF16),
        jax.ShapeDtypeStruct((nb, 8 * LANES, lt), BF16),
        jax.ShapeDtypeStruct((nb, nch, tm, 2 * LANES), BF16),
        jax.ShapeDtypeStruct((nb, nch, tm, 8 * LANES), BF16),
        jax.ShapeDtypeStruct((nb, nch, 2 * 64, tm), BF16),
        jax.ShapeDtypeStruct((nb, nch, 8 * 64, tm), BF16),
    ]
    out_specs = [
        pl.BlockSpec((1, 8 * LANES, tm), lambda b, j: (b, 0, j)),
        pl.BlockSpec((1, 8 * LANES, tm), lambda b, j: (b, 0, j)),
        pl.BlockSpec((1, 1, tm, 2 * LANES), lambda b, j: (b, j, 0, 0)),
        pl.BlockSpec((1, 1, tm, 8 * LANES), lambda b, j: (b, j, 0, 0)),
        pl.BlockSpec((1, 1, 2 * 64, tm), lambda b, j: (b, j, 0, 0)),
        pl.BlockSpec((1, 1, 8 * 64, tm), lambda b, j: (b, j, 0, 0)),
    ]
    return pl.pallas_call(
        functools.partial(_qkv_kernel, tm=tm, ctx_len=ctx_len, head_dim=64, d_qk=96),
        grid=(nb, nch),
        in_specs=[_row_spec(tm, d)] + _mod_specs(nb, d) + [_full_spec(w_in.shape)] + [tab_spec] * 4
        + [_full_spec(c.shape) for c in consts],
        out_specs=out_specs,
        out_shape=out_shapes,
        compiler_params=_cparams(("parallel", "parallel")),
        name="cd_qkv",
    )(s, mods, mods, w_in, *tabs, *consts)


def _attn_kernel(qt_ref, k_ref, vt_ref, o_ref, s0_ref, s1_ref, c0_ref, c1_ref, m_ref, acc_ref,
                 *, kv_slots, tq, n_ctx, n_all, dv):
    heads = len(kv_slots)

    def scores(j, g, kg, dst):
        s_dst, c_dst = dst
        st = _dot(k_ref[0, j, :, kg * LANES:(kg + 1) * LANES], qt_ref[0, g * LANES:(g + 1) * LANES, :])
        s_dst[g] = st
        c_dst[g] = jnp.max(st, axis=0, keepdims=True)

    def consume(j, src, g, kg):
        s_src, c_src = src
        m_old = m_ref[g]
        m_new = jnp.maximum(m_old, c_src[g])
        alpha = jnp.exp2(m_old - m_new)
        p = jnp.exp2((s_src[g] - m_new).astype(BF16))
        m_ref[g] = m_new
        vt = jnp.concatenate([vt_ref[0, j, kg * dv:(kg + 1) * dv, :], jnp.ones((DEN_ROWS, p.shape[0]), BF16)],
                             axis=0)
        acc_ref[g] = alpha * acc_ref[g] + _dot(vt, p)

    def step(j, j_next, cur, nxt):
        for g, kg in enumerate(kv_slots):
            if j_next is not None:
                scores(j_next, g, kg, nxt)
            consume(j, cur, g, kg)

    def run(n):
        buf0, buf1 = (s0_ref, c0_ref), (s1_ref, c1_ref)
        m_ref[...] = jnp.full(m_ref.shape, NEG, F32)
        acc_ref[...] = jnp.zeros(acc_ref.shape, F32)
        for g, kg in enumerate(kv_slots):
            scores(0, g, kg, buf0)

        def pair(i, c):
            j = 2 * i
            step(j, j + 1, buf0, buf1)
            step(j + 1, jnp.minimum(j + 2, n - 1), buf1, buf0)
            return c

        lax.fori_loop(0, n // 2, pair, 0)
        if n % 2:
            step(n - 1, None, buf0, None)
        ot = jnp.concatenate([acc_ref[g, :dv, :] / acc_ref[g, dv:dv + 1, :] for g in range(heads)],
                             axis=0)
        o_ref[0] = ot.T.astype(o_ref.dtype)

    is_ctx_tile = pl.program_id(1) * tq < n_ctx * k_ref.shape[2]

    @pl.when(is_ctx_tile)
    def _():
        run(n_ctx)

    @pl.when(jnp.logical_not(is_ctx_tile))
    def _():
        run(n_all)


def _attention(qt, k, vt, *, kv_slots, ctx_len, tq, name):
    nb, hq, lt = qt.shape
    heads = len(kv_slots)
    assert hq == heads * LANES
    nch, tk = k.shape[1], k.shape[2]
    dv = vt.shape[2] // (k.shape[3] // LANES)
    once = pl.Buffered(1)
    return pl.pallas_call(
        functools.partial(_attn_kernel, kv_slots=kv_slots, tq=tq, n_ctx=ctx_len // tk, n_all=nch, dv=dv),
        grid=(nb, lt // tq),
        in_specs=[
            pl.BlockSpec((1, hq, tq), lambda b, q: (b, 0, q)),
            pl.BlockSpec((1, nch, tk, k.shape[3]), lambda b, q: (b, 0, 0, 0), pipeline_mode=once),
            pl.BlockSpec((1, nch, vt.shape[2], tk), lambda b, q: (b, 0, 0, 0), pipeline_mode=once),
        ],
        out_specs=pl.BlockSpec((1, tq, heads * dv), lambda b, q: (b, q, 0)),
        out_shape=jax.ShapeDtypeStruct((nb, lt, heads * dv), BF16),
        scratch_shapes=[pltpu.VMEM((heads, tk, tq), F32), pltpu.VMEM((heads, tk, tq), F32),
                        pltpu.VMEM((heads, 1, tq), F32), pltpu.VMEM((heads, 1, tq), F32),
                        pltpu.VMEM((heads, 1, tq), F32), pltpu.VMEM((heads, dv + DEN_ROWS, tq), F32)],
        compiler_params=_cparams(("parallel", "arbitrary")),
        name=name,
    )(qt, k, vt)


def _oproj_kernel(oc_ref, od_ref, x_ref, mx_ref, mt_ref, w_ref, r_ref, xo_ref, h2_ref, g_ref,
                  *, tm, ctx_len, n_exp):
    is_ctx = _is_ctx_rows(tm, ctx_len)
    o = jnp.concatenate([oc_ref[0], od_ref[0]], axis=-1)
    xn = x_ref[0] + _mod(mx_ref, mt_ref, 2, is_ctx) * _dot(o, w_ref[...])
    xo_ref[0] = xn
    h2 = _rms(xn) * (1.0 + _mod(mx_ref, mt_ref, 4, is_ctx)) + _mod(mx_ref, mt_ref, 3, is_ctx)
    h2_ref[0] = _pack_pairs(h2)
    logits = jnp.dot(h2, r_ref[...], preferred_element_type=F32, precision=lax.Precision.HIGHEST)
    lane = lax.broadcasted_iota(jnp.int32, (tm, LANES), 1).astype(F32)
    lg = jnp.where(lane < n_exp, logits, NEG)
    m1 = jnp.max(lg, axis=-1, keepdims=True)
    i1 = jnp.min(jnp.where(lg == m1, lane, float(LANES)), axis=-1, keepdims=True)
    lg2 = jnp.where(lane == i1, NEG, lg)
    m2 = jnp.max(lg2, axis=-1, keepdims=True)
    i2 = jnp.min(jnp.where(lg2 == m2, lane, float(LANES)), axis=-1, keepdims=True)
    e2 = jnp.exp(m2 - m1)
    den = 1.0 + e2
    g_ref[0] = jnp.where(lane == 0.0, i1, jnp.where(lane == 1.0, i2, jnp.where(lane == 2.0, 1.0 / den, e2 / den)))


def _oproj(oc, od, s, mods, w_out, router, *, ctx_len, tm, n_exp):
    nb, lt, d = s.shape
    return pl.pallas_call(
        functools.partial(_oproj_kernel, tm=tm, ctx_len=ctx_len, n_exp=n_exp),
        grid=(nb, lt // tm),
        in_specs=[_row_spec(tm, oc.shape[2]), _row_spec(tm, od.shape[2]), _row_spec(tm, d)]
        + _mod_specs(nb, d) + [_full_spec(w_out.shape), _full_spec(router.shape)],
        out_specs=[_row_spec(tm, d), _row_spec(tm, _packed_width(d)), _row_spec(tm, LANES)],
        out_shape=[jax.ShapeDtypeStruct((nb, lt, d), F32), jax.ShapeDtypeStruct((nb, lt, _packed_width(d)), PACKED),
                   jax.ShapeDtypeStruct((nb, lt, LANES), F32)],
        compiler_params=_cparams(("parallel", "parallel")),
        name="cd_oproj_router",
    )(oc, od, s, mods, mods, w_out, router)


def _sc_gather(table, idx, *, chunk=SC_CHUNK):
    n_rows, d = idx.shape[0], table.shape[1]
    info = plsc.get_sparse_core_info()
    n_workers = info.num_cores * info.num_subcores
    per_w = n_rows // n_workers
    assert n_rows % (n_workers * chunk) == 0 and chunk % 8 == 0 and chunk <= LANES
    mesh = plsc.VectorSubcoreMesh(core_axis_name="c", subcore_axis_name="s")

    def body(table_hbm, idx_hbm, out_hbm, idx_v, rows_v, sem):
        wid = lax.axis_index("s") * info.num_cores + lax.axis_index("c")
        base = wid * per_w
        pltpu.sync_copy(idx_hbm.at[pl.ds(base, per_w)], idx_v)

        @pl.loop(0, per_w // chunk)
        def _(i):
            off = pl.multiple_of(i * chunk, chunk)
            pltpu.async_copy(table_hbm.at[idx_v.at[pl.ds(off, chunk)]], rows_v, sem).wait()
            pltpu.sync_copy(rows_v, out_hbm.at[pl.ds(base + off, chunk)])

    return pl.kernel(
        body,
        out_type=jax.ShapeDtypeStruct((n_rows, d), table.dtype),
        mesh=mesh,
        scratch_types=[pltpu.VMEM((per_w,), jnp.int32), pltpu.VMEM((chunk, d), table.dtype),
                       pltpu.SemaphoreType.DMA],
        name="sc_row_gather",
    )(table, idx)


def _route_plan(route, *, n_exp, ts, align):
    n_tok = route.shape[0]
    experts = route[:, :2].astype(jnp.int32)
    onehot = (experts[:, :, None] == jnp.arange(n_exp)[None, None, :]).astype(jnp.int32).sum(axis=1)
    before = jnp.cumsum(onehot, axis=0) - onehot
    counts = onehot.sum(axis=0)
    gsz = (counts + ts - 1) // ts * ts
    gend = jnp.cumsum(gsz)
    gstart = gend - gsz
    pos = gstart[experts] + jnp.take_along_axis(before, experts, axis=1)
    n_slots = -(-(2 * n_tok + n_exp * ts) // align) * align
    n_tiles = n_slots // ts
    tile_start = jnp.arange(n_tiles, dtype=jnp.int32) * ts
    n_used = (gend[-1] // ts).astype(jnp.int32)
    tile_expert = jnp.minimum(jnp.searchsorted(gend, tile_start, side="right"), n_exp - 1).astype(jnp.int32)
    order = jnp.argsort(experts.reshape(-1), stable=True).astype(jnp.int32)
    slot_e = jnp.repeat(tile_expert, ts)
    rank = jnp.arange(n_slots, dtype=jnp.int32) - gstart[slot_e]
    cstart = jnp.cumsum(counts) - counts
    src = jnp.clip(cstart[slot_e] + rank, 0, 2 * n_tok - 1)
    tok = jnp.where(rank < counts[slot_e], order[src] // 2, 0).astype(jnp.int32)
    last_used = tile_expert[jnp.maximum(n_used - 1, 0)]
    tile_expert = jnp.where(jnp.arange(n_tiles) < n_used, tile_expert, last_used)
    return tok, pos.T.reshape(-1).astype(jnp.int32), tile_expert, n_used.reshape(1)


def _gffn_kernel(te_ref, nu_ref, h_ref, wg_ref, wu_ref, wd_ref, o_ref, acc_ref, *, n_k):
    j = pl.program_id(0)
    k = pl.program_id(1)
    used = j < nu_ref[0]

    @pl.when(used)
    def _():
        h = _unpack_pairs(h_ref[...]).astype(BF16)
        a = _silu(_dot(h, wg_ref[0])) * _dot(h, wu_ref[0])
        part = _dot(a.astype(BF16), wd_ref[0])

        if n_k == 1:
            o_ref[...] = _pack_pairs(part)
        else:
            @pl.when(k == 0)
            def _():
                acc_ref[...] = part

            @pl.when(jnp.logical_and(k > 0, k < n_k - 1))
            def _():
                acc_ref[...] += part

            @pl.when(k == n_k - 1)
            def _():
                o_ref[...] = _pack_pairs(acc_ref[...] + part)

    @pl.when(jnp.logical_and(jnp.logical_not(used), k == n_k - 1))
    def _():
        o_ref[...] = jnp.zeros(o_ref.shape, o_ref.dtype)


def _grouped_ffn(hs, tile_expert, n_used, w_gate, w_up, w_down, *, ts, tf):
    n_slots, dp = hs.shape
    d, f = w_gate.shape[1:]
    n_k = f // tf
    grid_spec = pltpu.PrefetchScalarGridSpec(
        num_scalar_prefetch=2,
        grid=(n_slots // ts, n_k),
        in_specs=[
            pl.BlockSpec((ts, dp), lambda j, k, te, nu: (j, 0)),
            pl.BlockSpec((1, d, tf), lambda j, k, te, nu: (te[j], 0, k)),
            pl.BlockSpec((1, d, tf), lambda j, k, te, nu: (te[j], 0, k)),
            pl.BlockSpec((1, tf, d), lambda j, k, te, nu: (te[j], k, 0)),
        ],
        out_specs=pl.BlockSpec((ts, dp), lambda j, k, te, nu: (j, 0)),
        scratch_shapes=[pltpu.VMEM((ts, d), F32)],
    )
    return pl.pallas_call(
        functools.partial(_gffn_kernel, n_k=n_k),
        grid_spec=grid_spec,
        out_shape=jax.ShapeDtypeStruct((n_slots, dp), PACKED),
        compiler_params=_cparams(("parallel", "arbitrary")),
        name="moe_grouped_ffn",
    )(tile_expert, n_used, hs, w_gate, w_up, w_down)


def _combine_kernel(y1_ref, y2_ref, r_ref, x_ref, mx_ref, mt_ref, fg_ref, o_ref, *, tm, ctx_len, row0, final):
    pos = (pl.program_id(1) + row0) * tm + lax.broadcasted_iota(jnp.int32, (tm, 1), 0)
    is_ctx = pos < ctx_len
    r = r_ref[0]
    f = r[:, 2:3] * _unpack_pairs(y1_ref[0, 0]) + r[:, 3:4] * _unpack_pairs(y2_ref[0, 0])
    xn = x_ref[0] + _mod(mx_ref, mt_ref, 5, is_ctx) * f
    o_ref[0] = _rms(xn) * fg_ref[...] if final else xn


def _combine(yg, route, s, mods, final_gain, *, ctx_len, tm, final):
    nb, lt, d = s.shape
    row0 = ctx_len // tm if final else 0
    rows = lt - row0 * tm

    def rspec(width):
        return pl.BlockSpec((1, tm, width), lambda b, j: (b, j + row0, 0))

    return pl.pallas_call(
        functools.partial(_combine_kernel, tm=tm, ctx_len=ctx_len, row0=row0, final=final),
        grid=(nb, rows // tm),
        in_specs=[pl.BlockSpec((1, 1, tm, yg.shape[-1]), lambda b, j: (0, b, j + row0, 0)),
                  pl.BlockSpec((1, 1, tm, yg.shape[-1]), lambda b, j: (1, b, j + row0, 0)),
                  rspec(LANES), rspec(d)] + _mod_specs(nb, d) + [_full_spec((1, d))],
        out_specs=_row_spec(tm, d),
        out_shape=jax.ShapeDtypeStruct((nb, rows, d), F32),
        compiler_params=_cparams(("parallel", "parallel")),
        name="moe_combine",
    )(yg, yg, route, s, mods, mods, final_gain.reshape(1, d))


def _moe(h2, route, s, mods, w_gate, w_up, w_down, final_gain, *, ctx_len, n_exp, ts, tf, tm, final):
    nb, lt, d = s.shape
    n_tok = nb * lt
    info = plsc.get_sparse_core_info()
    sc_rows = info.num_cores * info.num_subcores * SC_CHUNK
    tok, pos, tile_expert, n_used = _route_plan(route.reshape(n_tok, LANES), n_exp=n_exp, ts=ts,
                                                align=int(np.lcm(ts, sc_rows)))
    hs = _sc_gather(h2.reshape(n_tok, h2.shape[-1]), tok)
    ys = _grouped_ffn(hs, tile_expert, n_used, w_gate, w_up, w_down, ts=ts, tf=tf)
    yg = _sc_gather(ys, pos).reshape(2, nb, lt, ys.shape[-1])
    return _combine(yg, route, s, mods, final_gain, ctx_len=ctx_len, tm=tm, final=final)


def _final_kernel(x_ref, g_ref, o_ref):
    o_ref[0] = _rms(x_ref[0]) * g_ref[...]


def _final(s, gain, *, ctx_len, tm):
    nb, lt, d = s.shape
    seq = lt - ctx_len
    off = ctx_len // tm
    return pl.pallas_call(
        _final_kernel,
        grid=(nb, seq // tm),
        in_specs=[pl.BlockSpec((1, tm, d), lambda b, j: (b, j + off, 0)), _full_spec((1, d))],
        out_specs=_row_spec(tm, d),
        out_shape=jax.ShapeDtypeStruct((nb, seq, d), F32),
        compiler_params=_cparams(("parallel", "parallel")),
        name="final_norm",
    )(s, gain.reshape(1, d))


def _rope_tables(seq, ctx_len):
    rows = seq // GRID_W
    row = jnp.repeat(jnp.arange(rows), GRID_W).astype(F32)
    col = jnp.tile(jnp.arange(GRID_W), rows).astype(F32)

    def table(rot_dim, reps):
        axis_dim = rot_dim // 2
        inv = ROPE_THETA ** (-jnp.arange(0, axis_dim, 2, dtype=F32) / axis_dim)
        ang = jnp.concatenate([row[:, None] * inv, col[:, None] * inv], axis=-1)
        cos = jnp.concatenate([jnp.ones((ctx_len, axis_dim), F32), jnp.cos(ang)], axis=0)
        sin = jnp.concatenate([jnp.zeros((ctx_len, axis_dim), F32), jnp.sin(ang)], axis=0)
        return (jnp.tile(jnp.concatenate([cos, cos], axis=-1), (1, reps)),
                jnp.tile(jnp.concatenate([-sin, sin], axis=-1), (1, reps)))

    cos_c, sin_c = table(64, 2)
    cos_d, sin_d = table(32, 4)
    return cos_c, sin_c, cos_d, sin_d


def _placement(n_rows, n_cols, pairs):
    m = np.zeros((n_rows, n_cols), np.float32)
    for r, c in pairs:
        m[r, c] = 1.0
    return jnp.asarray(m, BF16)


def _cd_consts(q_gain, k_gain, cq_gain, ckv_gain, w_uq, w_ukv):
    hd, nh, nkv, d_nope, d_rope, d_v = 64, 8, 2, 64, 32, 64
    d_qk = d_nope + d_rope
    heads = np.arange(nh)
    b512 = jnp.asarray(np.kron(np.eye(nh), np.ones((hd, hd))), BF16)
    b128 = jnp.asarray(np.kron(np.eye(nkv), np.ones((hd, hd))), BF16)
    pct = _placement(nh * LANES, nh * hd, [(h * LANES + i, h * hd + i) for h in heads for i in range(hd)])
    nope_cols = (heads[:, None] * d_qk + np.arange(d_nope)[None, :]).reshape(-1)
    rope_cols = (heads[:, None] * d_qk + d_nope + np.arange(d_rope)[None, :]).reshape(-1)
    nope_rows = (heads[:, None] * LANES + np.arange(d_nope)[None, :]).reshape(-1)
    wqnt = jnp.zeros((nh * LANES, w_uq.shape[0]), F32).at[nope_rows].set(w_uq[:, nope_cols].T).astype(BF16)
    wqr = w_uq[:, rope_cols].astype(BF16)
    prt = _placement(nh * LANES, nh * d_rope,
                     [(h * LANES + d_nope + i, h * d_rope + i) for h in heads for i in range(d_rope)])
    kn_cols = (heads[:, None] * (d_nope + d_v) + np.arange(d_nope)[None, :]).reshape(-1)
    v_cols = (heads[:, None] * (d_nope + d_v) + d_nope + np.arange(d_v)[None, :]).reshape(-1)
    wkn = jnp.zeros((w_ukv.shape[0], nh * LANES), F32).at[:, nope_rows].set(w_ukv[:, kn_cols]).astype(BF16)
    pkr = _placement(LANES, nh * LANES, [(i, h * LANES + d_nope + i) for h in heads for i in range(d_rope)])
    wvt = w_ukv[:, v_cols].T.astype(BF16)
    pkc = _placement(nkv * hd, nkv * LANES, [(h * hd + i, h * LANES + i) for h in range(nkv) for i in range(hd)])
    eye = jnp.asarray(np.eye(nkv * hd), BF16)
    return [jnp.tile(q_gain, nh).reshape(1, -1), jnp.tile(k_gain, nkv).reshape(1, -1),
            cq_gain.reshape(1, -1), ckv_gain.reshape(1, -1), b512, b128,
            pct, wqnt, wqr, prt, wkn, pkr, wvt, pkc, eye]


def _largest_tile(n, candidates):
    for t in candidates:
        if n % t == 0:
            return t
    raise ValueError(f"no tile in {candidates} divides {n}")


def kernel(x, c, ctx, c_ctx, ada_w, ada_b, ab_w_in, a_v_gain, a_w_s, a_b_s, b_w_pool, b_scale, ab_w_out,
           ffn_w_gate, ffn_w_up, ffn_w_down, cd_w_in, c_q_gain, c_k_gain, d_cq_gain, d_ckv_gain, d_w_uq,
           d_w_ukv, cd_w_out, moe_router, moe_w_gate, moe_w_up, moe_w_down, final_gain):
    nb, seq, d = x.shape
    ctx_len = ctx.shape[1]
    lt = ctx_len + seq
    depth = ada_w.shape[0]
    n_exp = moe_router.shape[-1]
    chunk = a_w_s.shape[-1]
    tr = 256
    assert ctx_len % tr == 0 and seq % tr == 0 and seq % GRID_W == 0
    tm_ff = _largest_tile(lt, (384, 256))
    tf = ffn_w_gate.shape[-1] // 2

    s = jnp.concatenate([ctx, x], axis=1)

    pad = -(nb + 1) % 8
    cvec = jnp.concatenate([c, c_ctx[None, :], jnp.zeros((pad, d), F32)], axis=0)
    mods_all = _ada_all(cvec, ada_w, ada_b).reshape(depth, nb + 1 + pad, 6, d)
    tabs = _rope_tables(seq, ctx_len)

    for layer in range(depth):
        i = layer // 2
        mods = mods_all[layer]
        if layer % 2 == 0:
            z = _inproj(s, mods, ab_w_in[i].astype(BF16), ctx_len=ctx_len, tm=tr)
            bs_full = jnp.repeat(a_b_s[i].T, LANES, axis=1)
            s, h2 = _mixer(z, s, mods, a_v_gain[i].reshape(1, -1), a_w_s[i].astype(BF16), bs_full,
                           b_w_pool[i].astype(BF16), b_scale[i].reshape(1, -1), ab_w_out[i].astype(BF16),
                           ctx_len=ctx_len, tm=tr, chunk=chunk)
            s = _swiglu(h2, s, mods, ffn_w_gate[i].astype(BF16), ffn_w_up[i].astype(BF16),
                        ffn_w_down[i].astype(BF16), ctx_len=ctx_len, tm=tm_ff, tf=tf)
        else:
            w_in = jnp.pad(cd_w_in[i], ((0, 0), (0, 12 * LANES - cd_w_in.shape[-1]))).astype(BF16)
            consts = _cd_consts(c_q_gain[i], c_k_gain[i], d_cq_gain[i], d_ckv_gain[i], d_w_uq[i], d_w_ukv[i])
            qtc, qtd, kc, kd, vtc, vtd = _qkv(s, mods, w_in, tabs, consts, ctx_len=ctx_len, tm=tr)
            oc = _attention(qtc, kc, vtc, kv_slots=(0, 0, 0, 0, 1, 1, 1, 1), ctx_len=ctx_len, tq=tr,
                            name="attn_gqa")
            od = _attention(qtd, kd, vtd, kv_slots=tuple(range(8)), ctx_len=ctx_len, tq=tr, name="attn_mla")
            router = jnp.pad(moe_router[i], ((0, 0), (0, LANES - n_exp)))
            s, h2, route = _oproj(oc, od, s, mods, cd_w_out[i].astype(BF16), router,
                                  ctx_len=ctx_len, tm=tr, n_exp=n_exp)
            s = _moe(h2, route, s, mods, moe_w_gate[i].astype(BF16), moe_w_up[i].astype(BF16),
                     moe_w_down[i].astype(BF16), final_gain, ctx_len=ctx_len, n_exp=n_exp, ts=512, tf=tf,
                     tm=tr, final=layer == depth - 1)

    return s if depth % 2 == 0 else _final(s, final_gain, ctx_len=ctx_len, tm=tr)
```

```python
import functools

import numpy as np
import jax
import jax.numpy as jnp
from jax import lax
from jax.experimental import pallas as pl
from jax.experimental.pallas import tpu as pltpu
from jax.experimental.pallas import tpu_sc as plsc

F32 = jnp.float32
BF16 = jnp.bfloat16
EPS = 1e-6
ROPE_THETA = 10000.0
GRID_W = 64
LANES = 128
HALO = 8
POOL_WINDOWS = (2, 4, 8, 16)
NEG = -1e30
LOG2E = 1.4426950408889634
FF_CHUNK = 256
DEN_ROWS = 16
SC_CHUNK = 64
VMEM_LIMIT = 56 * 1024 * 1024


def _cparams(sem, flags=None):
    return pltpu.CompilerParams(dimension_semantics=sem, vmem_limit_bytes=VMEM_LIMIT, flags=flags)


def _rms(x):
    return x * lax.rsqrt(jnp.mean(x * x, axis=-1, keepdims=True) + EPS)


def _is_ctx_rows(tm, ctx_len):
    pos = pl.program_id(1) * tm + lax.broadcasted_iota(jnp.int32, (tm, 1), 0)
    return pos < ctx_len


def _mod(mx_ref, mt_ref, idx, is_ctx):
    return jnp.where(is_ctx, mt_ref[0, idx:idx + 1, :], mx_ref[0, idx:idx + 1, :])


def _dot(a, b):
    return jnp.dot(a, b, preferred_element_type=F32)


def _dot_nt(a, b):
    return lax.dot_general(a, b, (((1,), (1,)), ((), ())), preferred_element_type=F32)


def _silu(x):
    return x * jax.nn.sigmoid(x)


PACKED = jnp.uint32


def _packed_width(d):
    return d // 2


def _pack_pairs(x):
    w = x.shape[1] // 2
    xb = x.astype(BF16).astype(F32)
    hi = pltpu.bitcast(xb[:, :w], jnp.uint32)
    lo = pltpu.bitcast(xb[:, w:], jnp.uint32)
    return hi | (lo >> 16)


def _unpack_pairs(p):
    hi = pltpu.bitcast(p & jnp.uint32(0xFFFF0000), F32)
    lo = pltpu.bitcast(p << 16, F32)
    return jnp.concatenate([hi, lo], axis=1)


def _ada_kernel(c_ref, w_ref, b_ref, o_ref):
    c = c_ref[...]
    o_ref[0] = jnp.dot(_silu(c), w_ref[0], preferred_element_type=F32,
                       precision=lax.Precision.HIGHEST) + b_ref[0]


def _ada_all(cvec, ada_w, ada_b):
    depth, d, n = ada_w.shape
    rows = cvec.shape[0]
    tn = 1536
    return pl.pallas_call(
        _ada_kernel,
        grid=(depth, n // tn),
        in_specs=[
            pl.BlockSpec((rows, d), lambda l, j: (0, 0)),
            pl.BlockSpec((1, d, tn), lambda l, j: (l, 0, j)),
            pl.BlockSpec((1, 1, tn), lambda l, j: (l, 0, j)),
        ],
        out_specs=pl.BlockSpec((1, rows, tn), lambda l, j: (l, 0, j)),
        out_shape=jax.ShapeDtypeStruct((depth, rows, n), F32),
        compiler_params=_cparams(("arbitrary", "arbitrary")),
        name="adaln",
    )(cvec, ada_w, ada_b.reshape(depth, 1, n))


def _row_spec(tm, width):
    return pl.BlockSpec((1, tm, width), lambda b, j, *_: (b, j, 0))


def _mod_specs(nb, d):
    return [pl.BlockSpec((1, 6, d), lambda b, j, *_: (b, 0, 0)),
            pl.BlockSpec((1, 6, d), lambda b, j, *_: (nb, 0, 0))]


def _full_spec(shape):
    nd = len(shape)
    return pl.BlockSpec(shape, lambda *_: (0,) * nd)


def _inproj_kernel(x_ref, mx_ref, mt_ref, w_ref, z_ref, *, tm, ctx_len):
    is_ctx = _is_ctx_rows(tm, ctx_len)
    h = _rms(x_ref[0]) * (1.0 + _mod(mx_ref, mt_ref, 1, is_ctx)) + _mod(mx_ref, mt_ref, 0, is_ctx)
    z_ref[0] = _dot(h.astype(BF16), w_ref[...])


def _inproj(s, mods, w, *, ctx_len, tm):
    nb, lt, d = s.shape
    n = w.shape[1]
    return pl.pallas_call(
        functools.partial(_inproj_kernel, tm=tm, ctx_len=ctx_len),
        grid=(nb, lt // tm),
        in_specs=[_row_spec(tm, d)] + _mod_specs(nb, d) + [_full_spec((d, n))],
        out_specs=_row_spec(tm, n),
        out_shape=jax.ShapeDtypeStruct((nb, lt, n), F32),
        compiler_params=_cparams(("parallel", "parallel")),
        name="ab_inproj",
    )(s, mods, mods, w)


def _mixer_kernel(z_ref, hp_ref, hn_ref, x_ref, mx_ref, mt_ref, vg_ref, ws_ref, bs_ref, wp_ref,
                  psc_ref, wo_ref, xo_ref, h2_ref, *, tm, ctx_len, lt, chunk):
    is_ctx = _is_ctx_rows(tm, ctx_len)
    pos0 = pl.program_id(1) * tm
    z = z_ref[0]
    aw = 4 * LANES
    za = z[:, :2 * aw]
    gl = 0.5 * za * (1.0 + lax.erf(za * (2.0 ** -0.5)))
    u = gl[:, :aw]
    v = gl[:, aw:]
    vn = (_rms(v) * vg_ref[...]).astype(BF16)
    ya_rows = []
    for c in range(tm // chunk):
        rs = slice(c * chunk, (c + 1) * chunk)
        cols = []
        for g in range(4):
            cs = slice(g * LANES, (g + 1) * LANES)
            sg = _dot(ws_ref[g], vn[rs, cs]) + bs_ref[:, cs]
            cols.append(u[rs, cs] * sg)
        ya_rows.append(jnp.concatenate(cols, axis=1))
    ya = jnp.concatenate(ya_rows, axis=0)

    in_ctx = pos0 < ctx_len
    seq_start = jnp.where(in_ctx, 0, ctx_len)
    seq_len = jnp.where(in_ctx, ctx_len, lt - ctx_len)
    p_in = pos0 - seq_start + lax.broadcasted_iota(jnp.int32, (tm, 1), 0)
    has_prev = jnp.logical_and(pos0 != 0, pos0 != ctx_len).astype(F32)
    has_next = jnp.logical_and(pos0 + tm != ctx_len, pos0 + tm != lt).astype(F32)
    zp = z[:, 2 * aw:]
    ext = jnp.concatenate([hp_ref[0] * has_prev, zp, hn_ref[0] * has_next], axis=0)
    n = tm + 2 * HALO
    yb_cols = []
    for g, w in enumerate(POOL_WINDOWS):
        cs = slice(g * LANES, (g + 1) * LANES)
        a = ext[:, cs]
        step = 1
        while step < w:
            a = a + pltpu.roll(a, n - step, 0)
            step *= 2
        shift = HALO - w // 2
        if shift:
            a = pltpu.roll(a, n - shift, 0)
        wsum = a[:tm]
        lo = jnp.clip(p_in - w // 2, 0, seq_len)
        hi = jnp.clip(p_in - w // 2 + w, 0, seq_len)
        pooled = wsum / (hi - lo).astype(F32) - zp[:, cs]
        yb_cols.append(_dot(pooled.astype(BF16), wp_ref[g]) * psc_ref[:, cs])
    y = jnp.concatenate([ya] + yb_cols, axis=1).astype(BF16)
    xn = x_ref[0] + _mod(mx_ref, mt_ref, 2, is_ctx) * _dot(y, wo_ref[...])
    xo_ref[0] = xn
    h2 = _rms(xn) * (1.0 + _mod(mx_ref, mt_ref, 4, is_ctx)) + _mod(mx_ref, mt_ref, 3, is_ctx)
    h2_ref[0] = h2.astype(BF16)


def _mixer(z, s, mods, v_gain, w_s, bs_full, w_pool, p_scale, w_out, *, ctx_len, tm, chunk):
    nb, lt, d = s.shape
    aw = 4 * LANES
    nh = lt // HALO
    th = tm // HALO
    pw = 4 * LANES
    halo_prev = pl.BlockSpec((1, HALO, pw), lambda b, j: (b, jnp.maximum(j * th - 1, 0), 2))
    halo_next = pl.BlockSpec((1, HALO, pw), lambda b, j: (b, jnp.minimum((j + 1) * th, nh - 1), 2))
    return pl.pallas_call(
        functools.partial(_mixer_kernel, tm=tm, ctx_len=ctx_len, lt=lt, chunk=chunk),
        grid=(nb, lt // tm),
        in_specs=[_row_spec(tm, 2 * aw + pw), halo_prev, halo_next, _row_spec(tm, d)] + _mod_specs(nb, d) + [
            _full_spec((1, aw)), _full_spec(w_s.shape), _full_spec(bs_full.shape),
            _full_spec(w_pool.shape), _full_spec((1, pw)), _full_spec(w_out.shape)],
        out_specs=[_row_spec(tm, d), _row_spec(tm, d)],
        out_shape=[jax.ShapeDtypeStruct((nb, lt, d), F32), jax.ShapeDtypeStruct((nb, lt, d), BF16)],
        compiler_params=_cparams(("parallel", "parallel")),
        name="ab_mixer",
    )(z, z, z, s, mods, mods, v_gain, w_s, bs_full, w_pool, p_scale, w_out)


def _swiglu_rows(h, wg_ref, wu_ref, wd_ref, a_ref):
    for c in range(a_ref.shape[1] // FF_CHUNK):
        cs = slice(c * FF_CHUNK, (c + 1) * FF_CHUNK)
        a_ref[:, cs] = (_silu(_dot(h, wg_ref[0, :, cs])) * _dot(h, wu_ref[0, :, cs])).astype(BF16)
    return _dot(a_ref[...], wd_ref[0])


def _swiglu_kernel(h_ref, x_ref, mx_ref, mt_ref, wg_ref, wu_ref, wd_ref, o_ref, a_ref, *, tm, ctx_len):
    is_ctx = _is_ctx_rows(tm, ctx_len)
    f = _swiglu_rows(h_ref[0], wg_ref, wu_ref, wd_ref, a_ref)
    o_ref[0] = x_ref[0] + _mod(mx_ref, mt_ref, 5, is_ctx) * f


def _swiglu(h2, s, mods, w_gate, w_up, w_down, *, ctx_len, tm):
    nb, lt, d = s.shape
    f = w_gate.shape[2]
    once = pl.Buffered(1)
    return pl.pallas_call(
        functools.partial(_swiglu_kernel, tm=tm, ctx_len=ctx_len),
        grid=(nb, lt // tm),
        in_specs=[_row_spec(tm, d), _row_spec(tm, d)] + _mod_specs(nb, d) + [
            pl.BlockSpec((1, d, f), lambda b, j: (0, 0, 0), pipeline_mode=once),
            pl.BlockSpec((1, d, f), lambda b, j: (0, 0, 0), pipeline_mode=once),
            pl.BlockSpec((1, f, d), lambda b, j: (0, 0, 0), pipeline_mode=once)],
        out_specs=_row_spec(tm, d),
        out_shape=jax.ShapeDtypeStruct((nb, lt, d), F32),
        scratch_shapes=[pltpu.VMEM((tm, f), BF16)],
        compiler_params=_cparams(("parallel", "parallel")),
        name="swiglu",
    )(h2, s, mods, mods, w_gate, w_up, w_down)


def _seg_sum(x2, b_ref):
    hi = x2.astype(BF16)
    lo = (x2 - hi.astype(F32)).astype(BF16)
    return _dot(hi, b_ref[...]) + _dot(lo, b_ref[...])


def _rope_slabs(x, cos, sin, half):
    lane = lax.broadcasted_iota(jnp.int32, (x.shape[0], LANES), 1)
    first = (lane % (2 * half)) < half
    out = []
    for s in range(x.shape[1] // LANES):
        xs = x[:, s * LANES:(s + 1) * LANES]
        swapped = jnp.where(first, pltpu.roll(xs, LANES - half, 1), pltpu.roll(xs, half, 1))
        out.append(xs * cos + swapped * sin)
    return out[0] if len(out) == 1 else jnp.concatenate(out, axis=1)


def _qkv_kernel(x_ref, mx_ref, mt_ref, win_ref, cc_ref, sc_ref, cd_ref, sd_ref,
                qg_ref, kg_ref, cqg_ref, ckvg_ref, b512_ref, b128_ref,
                pct_ref, wqnt_ref, wqr_ref, prt_ref, wkn_ref, pkr_ref, wvt_ref, pkc_ref, eye_ref,
                qtc_ref, qtd_ref, kc_ref, kd_ref, vtc_ref, vtd_ref,
                *, tm, ctx_len, head_dim, d_qk):
    is_ctx = _is_ctx_rows(tm, ctx_len)
    h = _rms(x_ref[0]) * (1.0 + _mod(mx_ref, mt_ref, 1, is_ctx)) + _mod(mx_ref, mt_ref, 0, is_ctx)
    z = _dot(h.astype(BF16), win_ref[...])
    cos_c, sin_c, cos_d, sin_d = cc_ref[...], sc_ref[...], cd_ref[...], sd_ref[...]
    o_kc, o_vc, o_cq, o_ckv, o_kr = 512, 640, 768, 1152, 1408

    qc = z[:, :o_kc]
    qc = qc * lax.rsqrt(_seg_sum(qc * qc, b512_ref) * (1.0 / head_dim) + EPS) * qg_ref[...]
    qc = _rope_slabs(qc, cos_c, sin_c, head_dim // 2) * (head_dim ** -0.5 * LOG2E)
    qtc_ref[0] = _dot_nt(pct_ref[...], qc.astype(BF16)).astype(BF16)

    kc = z[:, o_kc:o_vc]
    kc = kc * lax.rsqrt(_seg_sum(kc * kc, b128_ref) * (1.0 / head_dim) + EPS) * kg_ref[...]
    kc = _rope_slabs(kc, cos_c, sin_c, head_dim // 2)
    kc_ref[0, 0] = _dot(kc.astype(BF16), pkc_ref[...]).astype(BF16)
    vtc_ref[0, 0] = _dot_nt(eye_ref[...], z[:, o_vc:o_cq].astype(BF16)).astype(BF16)

    cq = (_rms(z[:, o_cq:o_ckv]) * cqg_ref[...]) * (d_qk ** -0.5 * LOG2E)
    cqb = cq.astype(BF16)
    q_rope = _rope_slabs(_dot(cqb, wqr_ref[...]), cos_d, sin_d, 16)
    qtd = _dot_nt(wqnt_ref[...], cqb) + _dot_nt(prt_ref[...], q_rope.astype(BF16))
    qtd_ref[0] = qtd.astype(BF16)

    ckv = (_rms(z[:, o_ckv:o_kr]) * ckvg_ref[...]).astype(BF16)
    kr = _rope_slabs(z[:, o_kr:], cos_d, sin_d, 16)
    kd_ref[0, 0] = (_dot(ckv, wkn_ref[...]) + _dot(kr.astype(BF16), pkr_ref[...])).astype(BF16)
    vtd_ref[0, 0] = _dot_nt(wvt_ref[...], ckv).astype(BF16)


def _qkv(s, mods, w_in, tabs, consts, *, ctx_len, tm):
    nb, lt, d = s.shape
    nch = lt // tm
    tab_spec = pl.BlockSpec((tm, LANES), lambda b, j: (j, 0))
    out_shapes = [
        jax.ShapeDtypeStruct((nb, 8 * LANES, lt), BF16),
        jax.ShapeDtypeStruct((nb, 8 * LANES, lt), BF16),
        jax.ShapeDtypeStruct((nb, nch, tm, 2 * LANES), BF16),
        jax.ShapeDtypeStruct((nb, nch, tm, 8 * LANES), BF16),
        jax.ShapeDtypeStruct((nb, nch, 2 * 64, tm), BF16),
        jax.ShapeDtypeStruct((nb, nch, 8 * 64, tm), BF16),
    ]
    out_specs = [
        pl.BlockSpec((1, 8 * LANES, tm), lambda b, j: (b, 0, j)),
        pl.BlockSpec((1, 8 * LANES, tm), lambda b, j: (b, 0, j)),
        pl.BlockSpec((1, 1, tm, 2 * LANES), lambda b, j: (b, j, 0, 0)),
        pl.BlockSpec((1, 1, tm, 8 * LANES), lambda b, j: (b, j, 0, 0)),
        pl.BlockSpec((1, 1, 2 * 64, tm), lambda b, j: (b, j, 0, 0)),
        pl.BlockSpec((1, 1, 8 * 64, tm), lambda b, j: (b, j, 0, 0)),
    ]
    return pl.pallas_call(
        functools.partial(_qkv_kernel, tm=tm, ctx_len=ctx_len, head_dim=64, d_qk=96),
        grid=(nb, nch),
        in_specs=[_row_spec(tm, d)] + _mod_specs(nb, d) + [_full_spec(w_in.shape)] + [tab_spec] * 4
        + [_full_spec(c.shape) for c in consts],
        out_specs=out_specs,
        out_shape=out_shapes,
        compiler_params=_cparams(("parallel", "parallel")),
        name="cd_qkv",
    )(s, mods, mods, w_in, *tabs, *consts)


def _attn_kernel(qt_ref, k_ref, vt_ref, o_ref, s0_ref, s1_ref, c0_ref, c1_ref, m_ref, acc_ref,
                 *, kv_slots, tq, n_ctx, n_all, dv):
    heads = len(kv_slots)

    def scores(j, g, kg, dst):
        s_dst, c_dst = dst
        st = _dot(k_ref[0, j, :, kg * LANES:(kg + 1) * LANES], qt_ref[0, g * LANES:(g + 1) * LANES, :])
        s_dst[g] = st
        c_dst[g] = jnp.max(st, axis=0, keepdims=True)

    def consume(j, src, g, kg):
        s_src, c_src = src
        m_old = m_ref[g]
        m_new = jnp.maximum(m_old, c_src[g])
        alpha = jnp.exp2(m_old - m_new)
        p = jnp.exp2((s_src[g] - m_new).astype(BF16))
        m_ref[g] = m_new
        vt = jnp.concatenate([vt_ref[0, j, kg * dv:(kg + 1) * dv, :], jnp.ones((DEN_ROWS, p.shape[0]), BF16)],
                             axis=0)
        acc_ref[g] = alpha * acc_ref[g] + _dot(vt, p)

    def step(j, j_next, cur, nxt):
        for g, kg in enumerate(kv_slots):
            if j_next is not None:
                scores(j_next, g, kg, nxt)
            consume(j, cur, g, kg)

    def run(n):
        buf0, buf1 = (s0_ref, c0_ref), (s1_ref, c1_ref)
        m_ref[...] = jnp.full(m_ref.shape, NEG, F32)
        acc_ref[...] = jnp.zeros(acc_ref.shape, F32)
        for g, kg in enumerate(kv_slots):
            scores(0, g, kg, buf0)

        def pair(i, c):
            j = 2 * i
            step(j, j + 1, buf0, buf1)
            step(j + 1, jnp.minimum(j + 2, n - 1), buf1, buf0)
            return c

        lax.fori_loop(0, n // 2, pair, 0)
        if n % 2:
            step(n - 1, None, buf0, None)
        ot = jnp.concatenate([acc_ref[g, :dv, :] / acc_ref[g, dv:dv + 1, :] for g in range(heads)],
                             axis=0)
        o_ref[0] = ot.T.astype(o_ref.dtype)

    is_ctx_tile = pl.program_id(1) * tq < n_ctx * k_ref.shape[2]

    @pl.when(is_ctx_tile)
    def _():
        run(n_ctx)

    @pl.when(jnp.logical_not(is_ctx_tile))
    def _():
        run(n_all)


def _attention(qt, k, vt, *, kv_slots, ctx_len, tq, name):
    nb, hq, lt = qt.shape
    heads = len(kv_slots)
    assert hq == heads * LANES
    nch, tk = k.shape[1], k.shape[2]
    dv = vt.shape[2] // (k.shape[3] // LANES)
    once = pl.Buffered(1)
    return pl.pallas_call(
        functools.partial(_attn_kernel, kv_slots=kv_slots, tq=tq, n_ctx=ctx_len // tk, n_all=nch, dv=dv),
        grid=(nb, lt // tq),
        in_specs=[
            pl.BlockSpec((1, hq, tq), lambda b, q: (b, 0, q)),
            pl.BlockSpec((1, nch, tk, k.shape[3]), lambda b, q: (b, 0, 0, 0), pipeline_mode=once),
            pl.BlockSpec((1, nch, vt.shape[2], tk), lambda b, q: (b, 0, 0, 0), pipeline_mode=once),
        ],
        out_specs=pl.BlockSpec((1, tq, heads * dv), lambda b, q: (b, q, 0)),
        out_shape=jax.ShapeDtypeStruct((nb, lt, heads * dv), BF16),
        scratch_shapes=[pltpu.VMEM((heads, tk, tq), F32), pltpu.VMEM((heads, tk, tq), F32),
                        pltpu.VMEM((heads, 1, tq), F32), pltpu.VMEM((heads, 1, tq), F32),
                        pltpu.VMEM((heads, 1, tq), F32), pltpu.VMEM((heads, dv + DEN_ROWS, tq), F32)],
        compiler_params=_cparams(("parallel", "arbitrary")),
        name=name,
    )(qt, k, vt)


def _oproj_kernel(oc_ref, od_ref, x_ref, mx_ref, mt_ref, w_ref, r_ref, xo_ref, h2_ref, g_ref,
                  *, tm, ctx_len, n_exp):
    is_ctx = _is_ctx_rows(tm, ctx_len)
    o = jnp.concatenate([oc_ref[0], od_ref[0]], axis=-1)
    xn = x_ref[0] + _mod(mx_ref, mt_ref, 2, is_ctx) * _dot(o, w_ref[...])
    xo_ref[0] = xn
    h2 = _rms(xn) * (1.0 + _mod(mx_ref, mt_ref, 4, is_ctx)) + _mod(mx_ref, mt_ref, 3, is_ctx)
    h2_ref[0] = _pack_pairs(h2)
    h_hi = h2.astype(BF16)
    h_lo = (h2 - h_hi.astype(F32)).astype(BF16)
    logits = _dot(h_hi, r_ref[0]) + _dot(h_lo, r_ref[0]) + _dot(h_hi, r_ref[1])
    lane = lax.broadcasted_iota(jnp.int32, (tm, LANES), 1).astype(F32)
    lg = jnp.where(lane < n_exp, logits, NEG)
    m1 = jnp.max(lg, axis=-1, keepdims=True)
    i1 = jnp.min(jnp.where(lg == m1, lane, float(LANES)), axis=-1, keepdims=True)
    lg2 = jnp.where(lane == i1, NEG, lg)
    m2 = jnp.max(lg2, axis=-1, keepdims=True)
    i2 = jnp.min(jnp.where(lg2 == m2, lane, float(LANES)), axis=-1, keepdims=True)
    e2 = jnp.exp(m2 - m1)
    den = 1.0 + e2
    g_ref[0] = jnp.where(lane == 0.0, i1, jnp.where(lane == 1.0, i2, jnp.where(lane == 2.0, 1.0 / den, e2 / den)))


def _oproj(oc, od, s, mods, w_out, router, *, ctx_len, tm, n_exp):
    nb, lt, d = s.shape
    return pl.pallas_call(
        functools.partial(_oproj_kernel, tm=tm, ctx_len=ctx_len, n_exp=n_exp),
        grid=(nb, lt // tm),
        in_specs=[_row_spec(tm, oc.shape[2]), _row_spec(tm, od.shape[2]), _row_spec(tm, d)]
        + _mod_specs(nb, d) + [_full_spec(w_out.shape), _full_spec(router.shape)],
        out_specs=[_row_spec(tm, d), _row_spec(tm, _packed_width(d)), _row_spec(tm, LANES)],
        out_shape=[jax.ShapeDtypeStruct((nb, lt, d), F32), jax.ShapeDtypeStruct((nb, lt, _packed_width(d)), PACKED),
                   jax.ShapeDtypeStruct((nb, lt, LANES), F32)],
        compiler_params=_cparams(("parallel", "parallel")),
        name="cd_oproj_router",
    )(oc, od, s, mods, mods, w_out, router)


def _sc_gather(table, idx, *, chunk=SC_CHUNK):
    n_rows, d = idx.shape[0], table.shape[1]
    info = plsc.get_sparse_core_info()
    n_workers = info.num_cores * info.num_subcores
    per_w = n_rows // n_workers
    assert n_rows % (n_workers * chunk) == 0 and chunk % 8 == 0 and chunk <= LANES
    mesh = plsc.VectorSubcoreMesh(core_axis_name="c", subcore_axis_name="s")

    def body(table_hbm, idx_hbm, out_hbm, idx_v, rows_v, sem):
        wid = lax.axis_index("s") * info.num_cores + lax.axis_index("c")
        base = wid * per_w
        pltpu.sync_copy(idx_hbm.at[pl.ds(base, per_w)], idx_v)

        @pl.loop(0, per_w // chunk)
        def _(i):
            off = pl.multiple_of(i * chunk, chunk)
            pltpu.async_copy(table_hbm.at[idx_v.at[pl.ds(off, chunk)]], rows_v, sem).wait()
            pltpu.sync_copy(rows_v, out_hbm.at[pl.ds(base + off, chunk)])

    return pl.kernel(
        body,
        out_type=jax.ShapeDtypeStruct((n_rows, d), table.dtype),
        mesh=mesh,
        scratch_types=[pltpu.VMEM((per_w,), jnp.int32), pltpu.VMEM((chunk, d), table.dtype),
                       pltpu.SemaphoreType.DMA],
        name="sc_row_gather",
    )(table, idx)


def _route_plan(route, *, n_exp, ts, align):
    n_tok = route.shape[0]
    experts = route[:, :2].astype(jnp.int32)
    onehot = (experts[:, :, None] == jnp.arange(n_exp)[None, None, :]).astype(jnp.int32).sum(axis=1)
    before = jnp.cumsum(onehot, axis=0) - onehot
    counts = onehot.sum(axis=0)
    gsz = (counts + ts - 1) // ts * ts
    gend = jnp.cumsum(gsz)
    gstart = gend - gsz
    pos = gstart[experts] + jnp.take_along_axis(before, experts, axis=1)
    n_slots = -(-(2 * n_tok + n_exp * ts) // align) * align
    n_tiles = n_slots // ts
    tile_start = jnp.arange(n_tiles, dtype=jnp.int32) * ts
    n_used = (gend[-1] // ts).astype(jnp.int32)
    tile_expert = jnp.minimum(jnp.searchsorted(gend, tile_start, side="right"), n_exp - 1).astype(jnp.int32)
    order = jnp.argsort(experts.reshape(-1), stable=True).astype(jnp.int32)
    slot_e = jnp.repeat(tile_expert, ts)
    rank = jnp.arange(n_slots, dtype=jnp.int32) - gstart[slot_e]
    cstart = jnp.cumsum(counts) - counts
    src = jnp.clip(cstart[slot_e] + rank, 0, 2 * n_tok - 1)
    tok = jnp.where(rank < counts[slot_e], order[src] // 2, 0).astype(jnp.int32)
    last_used = tile_expert[jnp.maximum(n_used - 1, 0)]
    tile_expert = jnp.where(jnp.arange(n_tiles) < n_used, tile_expert, last_used)
    return tok, pos.T.reshape(-1).astype(jnp.int32), tile_expert, n_used.reshape(1)


def _gffn_kernel(te_ref, nu_ref, h_ref, wg_ref, wu_ref, wd_ref, o_ref, a_ref):
    used = pl.program_id(0) < nu_ref[0]

    @pl.when(used)
    def _():
        h = _unpack_pairs(h_ref[...]).astype(BF16)
        o_ref[...] = _pack_pairs(_swiglu_rows(h, wg_ref, wu_ref, wd_ref, a_ref))

    @pl.when(jnp.logical_not(used))
    def _():
        o_ref[...] = jnp.zeros(o_ref.shape, o_ref.dtype)


def _grouped_ffn(hs, tile_expert, n_used, w_gate, w_up, w_down, *, ts):
    n_slots, dp = hs.shape
    d, f = w_gate.shape[1:]
    once = pl.Buffered(1)
    grid_spec = pltpu.PrefetchScalarGridSpec(
        num_scalar_prefetch=2,
        grid=(n_slots // ts,),
        in_specs=[
            pl.BlockSpec((ts, dp), lambda j, te, nu: (j, 0)),
            pl.BlockSpec((1, d, f), lambda j, te, nu: (te[j], 0, 0), pipeline_mode=once),
            pl.BlockSpec((1, d, f), lambda j, te, nu: (te[j], 0, 0), pipeline_mode=once),
            pl.BlockSpec((1, f, d), lambda j, te, nu: (te[j], 0, 0), pipeline_mode=once),
        ],
        out_specs=pl.BlockSpec((ts, dp), lambda j, te, nu: (j, 0)),
        scratch_shapes=[pltpu.VMEM((ts, f), BF16)],
    )
    return pl.pallas_call(
        _gffn_kernel,
        grid_spec=grid_spec,
        out_shape=jax.ShapeDtypeStruct((n_slots, dp), PACKED),
        compiler_params=_cparams(("arbitrary",)),
        name="moe_grouped_ffn",
    )(tile_expert, n_used, hs, w_gate, w_up, w_down)


def _combine_kernel(y1_ref, y2_ref, r_ref, x_ref, mx_ref, mt_ref, fg_ref, o_ref, *, tm, ctx_len, row0, final):
    pos = (pl.program_id(1) + row0) * tm + lax.broadcasted_iota(jnp.int32, (tm, 1), 0)
    is_ctx = pos < ctx_len
    r = r_ref[0]
    f = r[:, 2:3] * _unpack_pairs(y1_ref[0, 0]) + r[:, 3:4] * _unpack_pairs(y2_ref[0, 0])
    xn = x_ref[0] + _mod(mx_ref, mt_ref, 5, is_ctx) * f
    o_ref[0] = _rms(xn) * fg_ref[...] if final else xn


def _combine(yg, route, s, mods, final_gain, *, ctx_len, tm, final):
    nb, lt, d = s.shape
    row0 = ctx_len // tm if final else 0
    rows = lt - row0 * tm

    def rspec(width):
        return pl.BlockSpec((1, tm, width), lambda b, j: (b, j + row0, 0))

    return pl.pallas_call(
        functools.partial(_combine_kernel, tm=tm, ctx_len=ctx_len, row0=row0, final=final),
        grid=(nb, rows // tm),
        in_specs=[pl.BlockSpec((1, 1, tm, yg.shape[-1]), lambda b, j: (0, b, j + row0, 0)),
                  pl.BlockSpec((1, 1, tm, yg.shape[-1]), lambda b, j: (1, b, j + row0, 0)),
                  rspec(LANES), rspec(d)] + _mod_specs(nb, d) + [_full_spec((1, d))],
        out_specs=_row_spec(tm, d),
        out_shape=jax.ShapeDtypeStruct((nb, rows, d), F32),
        compiler_params=_cparams(("parallel", "parallel")),
        name="moe_combine",
    )(yg, yg, route, s, mods, mods, final_gain.reshape(1, d))


def _moe(h2, route, s, mods, w_gate, w_up, w_down, final_gain, *, ctx_len, n_exp, ts, tm, final):
    nb, lt, d = s.shape
    n_tok = nb * lt
    info = plsc.get_sparse_core_info()
    sc_rows = info.num_cores * info.num_subcores * SC_CHUNK
    tok, pos, tile_expert, n_used = _route_plan(route.reshape(n_tok, LANES), n_exp=n_exp, ts=ts,
                                                align=int(np.lcm(ts, sc_rows)))
    hs = _sc_gather(h2.reshape(n_tok, h2.shape[-1]), tok)
    ys = _grouped_ffn(hs, tile_expert, n_used, w_gate, w_up, w_down, ts=ts)
    yg = _sc_gather(ys, pos).reshape(2, nb, lt, ys.shape[-1])
    return _combine(yg, route, s, mods, final_gain, ctx_len=ctx_len, tm=tm, final=final)


def _final_kernel(x_ref, g_ref, o_ref):
    o_ref[0] = _rms(x_ref[0]) * g_ref[...]


def _final(s, gain, *, ctx_len, tm):
    nb, lt, d = s.shape
    seq = lt - ctx_len
    off = ctx_len // tm
    return pl.pallas_call(
        _final_kernel,
        grid=(nb, seq // tm),
        in_specs=[pl.BlockSpec((1, tm, d), lambda b, j: (b, j + off, 0)), _full_spec((1, d))],
        out_specs=_row_spec(tm, d),
        out_shape=jax.ShapeDtypeStruct((nb, seq, d), F32),
        compiler_params=_cparams(("parallel", "parallel")),
        name="final_norm",
    )(s, gain.reshape(1, d))


def _rope_tables(seq, ctx_len):
    rows = seq // GRID_W
    row = jnp.repeat(jnp.arange(rows), GRID_W).astype(F32)
    col = jnp.tile(jnp.arange(GRID_W), rows).astype(F32)

    def table(rot_dim, reps):
        axis_dim = rot_dim // 2
        inv = ROPE_THETA ** (-jnp.arange(0, axis_dim, 2, dtype=F32) / axis_dim)
        ang = jnp.concatenate([row[:, None] * inv, col[:, None] * inv], axis=-1)
        cos = jnp.concatenate([jnp.ones((ctx_len, axis_dim), F32), jnp.cos(ang)], axis=0)
        sin = jnp.concatenate([jnp.zeros((ctx_len, axis_dim), F32), jnp.sin(ang)], axis=0)
        return (jnp.tile(jnp.concatenate([cos, cos], axis=-1), (1, reps)),
                jnp.tile(jnp.concatenate([-sin, sin], axis=-1), (1, reps)))

    cos_c, sin_c = table(64, 2)
    cos_d, sin_d = table(32, 4)
    return cos_c, sin_c, cos_d, sin_d


def _placement(n_rows, n_cols, pairs):
    m = np.zeros((n_rows, n_cols), np.float32)
    for r, c in pairs:
        m[r, c] = 1.0
    return jnp.asarray(m, BF16)


def _cd_consts(q_gain, k_gain, cq_gain, ckv_gain, w_uq, w_ukv):
    hd, nh, nkv, d_nope, d_rope, d_v = 64, 8, 2, 64, 32, 64
    d_qk = d_nope + d_rope
    heads = np.arange(nh)
    b512 = jnp.asarray(np.kron(np.eye(nh), np.ones((hd, hd))), BF16)
    b128 = jnp.asarray(np.kron(np.eye(nkv), np.ones((hd, hd))), BF16)
    pct = _placement(nh * LANES, nh * hd, [(h * LANES + i, h * hd + i) for h in heads for i in range(hd)])
    nope_cols = (heads[:, None] * d_qk + np.arange(d_nope)[None, :]).reshape(-1)
    rope_cols = (heads[:, None] * d_qk + d_nope + np.arange(d_rope)[None, :]).reshape(-1)
    nope_rows = (heads[:, None] * LANES + np.arange(d_nope)[None, :]).reshape(-1)
    wqnt = jnp.zeros((nh * LANES, w_uq.shape[0]), F32).at[nope_rows].set(w_uq[:, nope_cols].T).astype(BF16)
    wqr = w_uq[:, rope_cols].astype(BF16)
    prt = _placement(nh * LANES, nh * d_rope,
                     [(h * LANES + d_nope + i, h * d_rope + i) for h in heads for i in range(d_rope)])
    kn_cols = (heads[:, None] * (d_nope + d_v) + np.arange(d_nope)[None, :]).reshape(-1)
    v_cols = (heads[:, None] * (d_nope + d_v) + d_nope + np.arange(d_v)[None, :]).reshape(-1)
    wkn = jnp.zeros((w_ukv.shape[0], nh * LANES), F32).at[:, nope_rows].set(w_ukv[:, kn_cols]).astype(BF16)
    pkr = _placement(LANES, nh * LANES, [(i, h * LANES + d_nope + i) for h in heads for i in range(d_rope)])
    wvt = w_ukv[:, v_cols].T.astype(BF16)
    pkc = _placement(nkv * hd, nkv * LANES, [(h * hd + i, h * LANES + i) for h in range(nkv) for i in range(hd)])
    eye = jnp.asarray(np.eye(nkv * hd), BF16)
    return [jnp.tile(q_gain, nh).reshape(1, -1), jnp.tile(k_gain, nkv).reshape(1, -1),
            cq_gain.reshape(1, -1), ckv_gain.reshape(1, -1), b512, b128,
            pct, wqnt, wqr, prt, wkn, pkr, wvt, pkc, eye]


def _largest_tile(n, candidates):
    for t in candidates:
        if n % t == 0:
            return t
    raise ValueError(f"no tile in {candidates} divides {n}")


def kernel(x, c, ctx, c_ctx, ada_w, ada_b, ab_w_in, a_v_gain, a_w_s, a_b_s, b_w_pool, b_scale, ab_w_out,
           ffn_w_gate, ffn_w_up, ffn_w_down, cd_w_in, c_q_gain, c_k_gain, d_cq_gain, d_ckv_gain, d_w_uq,
           d_w_ukv, cd_w_out, moe_router, moe_w_gate, moe_w_up, moe_w_down, final_gain):
    nb, seq, d = x.shape
    ctx_len = ctx.shape[1]
    lt = ctx_len + seq
    depth = ada_w.shape[0]
    n_exp = moe_router.shape[-1]
    chunk = a_w_s.shape[-1]
    tr = 256
    assert ctx_len % tr == 0 and seq % tr == 0 and seq % GRID_W == 0
    tm_ff = _largest_tile(lt, (768, 384, 256))
    assert ffn_w_gate.shape[-1] % FF_CHUNK == 0 and moe_w_gate.shape[-1] % FF_CHUNK == 0

    s = jnp.concatenate([ctx, x], axis=1)

    pad = -(nb + 1) % 8
    cvec = jnp.concatenate([c, c_ctx[None, :], jnp.zeros((pad, d), F32)], axis=0)
    mods_all = _ada_all(cvec, ada_w, ada_b).reshape(depth, nb + 1 + pad, 6, d)
    tabs = _rope_tables(seq, ctx_len)

    for layer in range(depth):
        i = layer // 2
        mods = mods_all[layer]
        if layer % 2 == 0:
            z = _inproj(s, mods, ab_w_in[i].astype(BF16), ctx_len=ctx_len, tm=tr)
            bs_full = jnp.repeat(a_b_s[i].T, LANES, axis=1)
            s, h2 = _mixer(z, s, mods, a_v_gain[i].reshape(1, -1), a_w_s[i].astype(BF16), bs_full,
                           b_w_pool[i].astype(BF16), b_scale[i].reshape(1, -1), ab_w_out[i].astype(BF16),
                           ctx_len=ctx_len, tm=tr, chunk=chunk)
            s = _swiglu(h2, s, mods, ffn_w_gate[i][None].astype(BF16), ffn_w_up[i][None].astype(BF16),
                        ffn_w_down[i][None].astype(BF16), ctx_len=ctx_len, tm=tm_ff)
        else:
            w_in = jnp.pad(cd_w_in[i], ((0, 0), (0, 12 * LANES - cd_w_in.shape[-1]))).astype(BF16)
            consts = _cd_consts(c_q_gain[i], c_k_gain[i], d_cq_gain[i], d_ckv_gain[i], d_w_uq[i], d_w_ukv[i])
            qtc, qtd, kc, kd, vtc, vtd = _qkv(s, mods, w_in, tabs, consts, ctx_len=ctx_len, tm=tr)
            oc = _attention(qtc, kc, vtc, kv_slots=(0, 0, 0, 0, 1, 1, 1, 1), ctx_len=ctx_len, tq=tr,
                            name="attn_gqa")
            od = _attention(qtd, kd, vtd, kv_slots=tuple(range(8)), ctx_len=ctx_len, tq=tr, name="attn_mla")
            router = jnp.pad(moe_router[i], ((0, 0), (0, LANES - n_exp)))
            r_hi = router.astype(BF16)
            router = jnp.stack([r_hi, (router - r_hi.astype(F32)).astype(BF16)])
            s, h2, route = _oproj(oc, od, s, mods, cd_w_out[i].astype(BF16), router,
                                  ctx_len=ctx_len, tm=tr, n_exp=n_exp)
            s = _moe(h2, route, s, mods, moe_w_gate[i].astype(BF16), moe_w_up[i].astype(BF16),
                     moe_w_down[i].astype(BF16), final_gain, ctx_len=ctx_len, n_exp=n_exp, ts=512,
                     tm=tr, final=layer == depth - 1)

    return s if depth % 2 == 0 else _final(s, final_gain, ctx_len=ctx_len, tm=tr)
```

```python
import functools

import numpy as np
import jax
import jax.numpy as jnp
from jax import lax
from jax.experimental import pallas as pl
from jax.experimental.pallas import tpu as pltpu
from jax.experimental.pallas import tpu_sc as plsc

F32 = jnp.float32
BF16 = jnp.bfloat16
EPS = 1e-6
ROPE_THETA = 10000.0
GRID_W = 64
LANES = 128
HALO = 8
POOL_WINDOWS = (2, 4, 8, 16)
NEG = -1e30
LOG2E = 1.4426950408889634
ATTN_UNROLL = 8
FF_CHUNK = 256
DEN_ROWS = 16
SC_CHUNK = 64
VMEM_LIMIT = 56 * 1024 * 1024


def _cparams(sem, flags=None):
    return pltpu.CompilerParams(dimension_semantics=sem, vmem_limit_bytes=VMEM_LIMIT, flags=flags)


def _rms(x):
    return x * lax.rsqrt(jnp.mean(x * x, axis=-1, keepdims=True) + EPS)


def _is_ctx_rows(tm, ctx_len):
    pos = pl.program_id(1) * tm + lax.broadcasted_iota(jnp.int32, (tm, 1), 0)
    return pos < ctx_len


def _mod(mx_ref, mt_ref, idx, is_ctx):
    return jnp.where(is_ctx, mt_ref[0, idx:idx + 1, :], mx_ref[0, idx:idx + 1, :])


def _dot(a, b):
    return jnp.dot(a, b, preferred_element_type=F32)


def _dot_nt(a, b):
    return lax.dot_general(a, b, (((1,), (1,)), ((), ())), preferred_element_type=F32)


def _silu(x):
    return x * jax.nn.sigmoid(x)


PACKED = jnp.uint32


def _packed_width(d):
    return d // 2


def _pack_pairs(x):
    w = x.shape[1] // 2
    xb = x.astype(BF16).astype(F32)
    hi = pltpu.bitcast(xb[:, :w], jnp.uint32)
    lo = pltpu.bitcast(xb[:, w:], jnp.uint32)
    return hi | (lo >> 16)


def _unpack_pairs(p):
    hi = pltpu.bitcast(p & jnp.uint32(0xFFFF0000), F32)
    lo = pltpu.bitcast(p << 16, F32)
    return jnp.concatenate([hi, lo], axis=1)


def _ada_kernel(c_ref, w_ref, b_ref, o_ref):
    c = c_ref[...]
    o_ref[0] = jnp.dot(_silu(c), w_ref[0], preferred_element_type=F32,
                       precision=lax.Precision.HIGHEST) + b_ref[0]


def _ada_all(cvec, ada_w, ada_b):
    depth, d, n = ada_w.shape
    rows = cvec.shape[0]
    tn = 1536
    return pl.pallas_call(
        _ada_kernel,
        grid=(depth, n // tn),
        in_specs=[
            pl.BlockSpec((rows, d), lambda l, j: (0, 0)),
            pl.BlockSpec((1, d, tn), lambda l, j: (l, 0, j)),
            pl.BlockSpec((1, 1, tn), lambda l, j: (l, 0, j)),
        ],
        out_specs=pl.BlockSpec((1, rows, tn), lambda l, j: (l, 0, j)),
        out_shape=jax.ShapeDtypeStruct((depth, rows, n), F32),
        compiler_params=_cparams(("arbitrary", "arbitrary")),
        name="adaln",
    )(cvec, ada_w, ada_b.reshape(depth, 1, n))


def _row_spec(tm, width):
    return pl.BlockSpec((1, tm, width), lambda b, j, *_: (b, j, 0))


def _mod_specs(nb, d):
    return [pl.BlockSpec((1, 6, d), lambda b, j, *_: (b, 0, 0)),
            pl.BlockSpec((1, 6, d), lambda b, j, *_: (nb, 0, 0))]


def _full_spec(shape):
    nd = len(shape)
    return pl.BlockSpec(shape, lambda *_: (0,) * nd)


def _inproj_kernel(x_ref, mx_ref, mt_ref, w_ref, z_ref, *, tm, ctx_len):
    is_ctx = _is_ctx_rows(tm, ctx_len)
    h = _rms(x_ref[0]) * (1.0 + _mod(mx_ref, mt_ref, 1, is_ctx)) + _mod(mx_ref, mt_ref, 0, is_ctx)
    z_ref[0] = _dot(h.astype(BF16), w_ref[...])


def _inproj(s, mods, w, *, ctx_len, tm):
    nb, lt, d = s.shape
    n = w.shape[1]
    return pl.pallas_call(
        functools.partial(_inproj_kernel, tm=tm, ctx_len=ctx_len),
        grid=(nb, lt // tm),
        in_specs=[_row_spec(tm, d)] + _mod_specs(nb, d) + [_full_spec((d, n))],
        out_specs=_row_spec(tm, n),
        out_shape=jax.ShapeDtypeStruct((nb, lt, n), F32),
        compiler_params=_cparams(("parallel", "parallel")),
        name="ab_inproj",
    )(s, mods, mods, w)


def _mixer_kernel(z_ref, hp_ref, hn_ref, x_ref, mx_ref, mt_ref, vg_ref, ws_ref, bs_ref, wp_ref,
                  psc_ref, wo_ref, xo_ref, h2_ref, *, tm, ctx_len, lt, chunk):
    is_ctx = _is_ctx_rows(tm, ctx_len)
    pos0 = pl.program_id(1) * tm
    z = z_ref[0]
    aw = 4 * LANES
    za = z[:, :2 * aw]
    gl = 0.5 * za * (1.0 + lax.erf(za * (2.0 ** -0.5)))
    u = gl[:, :aw]
    v = gl[:, aw:]
    vn = (_rms(v) * vg_ref[...]).astype(BF16)
    ya_rows = []
    for c in range(tm // chunk):
        rs = slice(c * chunk, (c + 1) * chunk)
        cols = []
        for g in range(4):
            cs = slice(g * LANES, (g + 1) * LANES)
            sg = _dot(ws_ref[g], vn[rs, cs]) + bs_ref[:, cs]
            cols.append(u[rs, cs] * sg)
        ya_rows.append(jnp.concatenate(cols, axis=1))
    ya = jnp.concatenate(ya_rows, axis=0)

    in_ctx = pos0 < ctx_len
    seq_start = jnp.where(in_ctx, 0, ctx_len)
    seq_len = jnp.where(in_ctx, ctx_len, lt - ctx_len)
    p_in = pos0 - seq_start + lax.broadcasted_iota(jnp.int32, (tm, 1), 0)
    has_prev = jnp.logical_and(pos0 != 0, pos0 != ctx_len).astype(F32)
    has_next = jnp.logical_and(pos0 + tm != ctx_len, pos0 + tm != lt).astype(F32)
    zp = z[:, 2 * aw:]
    ext = jnp.concatenate([hp_ref[0] * has_prev, zp, hn_ref[0] * has_next], axis=0)
    n = tm + 2 * HALO
    yb_cols = []
    for g, w in enumerate(POOL_WINDOWS):
        cs = slice(g * LANES, (g + 1) * LANES)
        a = ext[:, cs]
        step = 1
        while step < w:
            a = a + pltpu.roll(a, n - step, 0)
            step *= 2
        shift = HALO - w // 2
        if shift:
            a = pltpu.roll(a, n - shift, 0)
        wsum = a[:tm]
        lo = jnp.clip(p_in - w // 2, 0, seq_len)
        hi = jnp.clip(p_in - w // 2 + w, 0, seq_len)
        pooled = wsum / (hi - lo).astype(F32) - zp[:, cs]
        yb_cols.append(_dot(pooled.astype(BF16), wp_ref[g]) * psc_ref[:, cs])
    y = jnp.concatenate([ya] + yb_cols, axis=1).astype(BF16)
    xn = x_ref[0] + _mod(mx_ref, mt_ref, 2, is_ctx) * _dot(y, wo_ref[...])
    xo_ref[0] = xn
    h2 = _rms(xn) * (1.0 + _mod(mx_ref, mt_ref, 4, is_ctx)) + _mod(mx_ref, mt_ref, 3, is_ctx)
    h2_ref[0] = h2.astype(BF16)


def _mixer(z, s, mods, v_gain, w_s, bs_full, w_pool, p_scale, w_out, *, ctx_len, tm, chunk):
    nb, lt, d = s.shape
    aw = 4 * LANES
    nh = lt // HALO
    th = tm // HALO
    pw = 4 * LANES
    halo_prev = pl.BlockSpec((1, HALO, pw), lambda b, j: (b, jnp.maximum(j * th - 1, 0), 2))
    halo_next = pl.BlockSpec((1, HALO, pw), lambda b, j: (b, jnp.minimum((j + 1) * th, nh - 1), 2))
    return pl.pallas_call(
        functools.partial(_mixer_kernel, tm=tm, ctx_len=ctx_len, lt=lt, chunk=chunk),
        grid=(nb, lt // tm),
        in_specs=[_row_spec(tm, 2 * aw + pw), halo_prev, halo_next, _row_spec(tm, d)] + _mod_specs(nb, d) + [
            _full_spec((1, aw)), _full_spec(w_s.shape), _full_spec(bs_full.shape),
            _full_spec(w_pool.shape), _full_spec((1, pw)), _full_spec(w_out.shape)],
        out_specs=[_row_spec(tm, d), _row_spec(tm, d)],
        out_shape=[jax.ShapeDtypeStruct((nb, lt, d), F32), jax.ShapeDtypeStruct((nb, lt, d), BF16)],
        compiler_params=_cparams(("parallel", "parallel")),
        name="ab_mixer",
    )(z, z, z, s, mods, mods, v_gain, w_s, bs_full, w_pool, p_scale, w_out)


def _swiglu_rows(h, wg_ref, wu_ref, wd_ref, a_ref):
    for c in range(a_ref.shape[1] // FF_CHUNK):
        cs = slice(c * FF_CHUNK, (c + 1) * FF_CHUNK)
        a_ref[:, cs] = (_silu(_dot(h, wg_ref[0, :, cs])) * _dot(h, wu_ref[0, :, cs])).astype(BF16)
    return _dot(a_ref[...], wd_ref[0])


def _swiglu_kernel(h_ref, x_ref, mx_ref, mt_ref, wg_ref, wu_ref, wd_ref, o_ref, a_ref, *, tm, ctx_len):
    is_ctx = _is_ctx_rows(tm, ctx_len)
    f = _swiglu_rows(h_ref[0], wg_ref, wu_ref, wd_ref, a_ref)
    o_ref[0] = x_ref[0] + _mod(mx_ref, mt_ref, 5, is_ctx) * f


def _swiglu(h2, s, mods, w_gate, w_up, w_down, *, ctx_len, tm):
    nb, lt, d = s.shape
    f = w_gate.shape[2]
    once = pl.Buffered(1)
    return pl.pallas_call(
        functools.partial(_swiglu_kernel, tm=tm, ctx_len=ctx_len),
        grid=(nb, lt // tm),
        in_specs=[_row_spec(tm, d), _row_spec(tm, d)] + _mod_specs(nb, d) + [
            pl.BlockSpec((1, d, f), lambda b, j: (0, 0, 0), pipeline_mode=once),
            pl.BlockSpec((1, d, f), lambda b, j: (0, 0, 0), pipeline_mode=once),
            pl.BlockSpec((1, f, d), lambda b, j: (0, 0, 0), pipeline_mode=once)],
        out_specs=_row_spec(tm, d),
        out_shape=jax.ShapeDtypeStruct((nb, lt, d), F32),
        scratch_shapes=[pltpu.VMEM((tm, f), BF16)],
        compiler_params=_cparams(("parallel", "parallel")),
        name="swiglu",
    )(h2, s, mods, mods, w_gate, w_up, w_down)


def _seg_sum(x2, b_ref):
    hi = x2.astype(BF16)
    lo = (x2 - hi.astype(F32)).astype(BF16)
    return _dot(hi, b_ref[...]) + _dot(lo, b_ref[...])


def _rope_slabs(x, cos, sin, half):
    lane = lax.broadcasted_iota(jnp.int32, (x.shape[0], LANES), 1)
    first = (lane % (2 * half)) < half
    out = []
    for s in range(x.shape[1] // LANES):
        xs = x[:, s * LANES:(s + 1) * LANES]
        swapped = jnp.where(first, pltpu.roll(xs, LANES - half, 1), pltpu.roll(xs, half, 1))
        out.append(xs * cos + swapped * sin)
    return out[0] if len(out) == 1 else jnp.concatenate(out, axis=1)


def _qkv_kernel(x_ref, mx_ref, mt_ref, win_ref, cc_ref, sc_ref, cd_ref, sd_ref,
                qg_ref, kg_ref, cqg_ref, ckvg_ref, b512_ref, b128_ref,
                pct_ref, wqnt_ref, wqr_ref, prt_ref, wkn_ref, pkr_ref, wvt_ref, pkc_ref, eye_ref,
                qtc_ref, qtd_ref, kc_ref, kd_ref, vtc_ref, vtd_ref,
                *, tm, ctx_len, head_dim, d_qk):
    is_ctx = _is_ctx_rows(tm, ctx_len)
    h = _rms(x_ref[0]) * (1.0 + _mod(mx_ref, mt_ref, 1, is_ctx)) + _mod(mx_ref, mt_ref, 0, is_ctx)
    z = _dot(h.astype(BF16), win_ref[...])
    cos_c, sin_c, cos_d, sin_d = cc_ref[...], sc_ref[...], cd_ref[...], sd_ref[...]
    o_kc, o_vc, o_cq, o_ckv, o_kr = 512, 640, 768, 1152, 1408

    qc = z[:, :o_kc]
    qc = qc * lax.rsqrt(_seg_sum(qc * qc, b512_ref) * (1.0 / head_dim) + EPS) * qg_ref[...]
    qc = _rope_slabs(qc, cos_c, sin_c, head_dim // 2) * (head_dim ** -0.5 * LOG2E)
    qtc_ref[0] = _dot_nt(pct_ref[...], qc.astype(BF16)).astype(BF16)

    kc = z[:, o_kc:o_vc]
    kc = kc * lax.rsqrt(_seg_sum(kc * kc, b128_ref) * (1.0 / head_dim) + EPS) * kg_ref[...]
    kc = _rope_slabs(kc, cos_c, sin_c, head_dim // 2)
    kc_ref[0, 0] = _dot(kc.astype(BF16), pkc_ref[...]).astype(BF16)
    vtc_ref[0, 0] = _dot_nt(eye_ref[...], z[:, o_vc:o_cq].astype(BF16)).astype(BF16)

    cq = (_rms(z[:, o_cq:o_ckv]) * cqg_ref[...]) * (d_qk ** -0.5 * LOG2E)
    cqb = cq.astype(BF16)
    q_rope = _rope_slabs(_dot(cqb, wqr_ref[...]), cos_d, sin_d, 16)
    qtd = _dot_nt(wqnt_ref[...], cqb) + _dot_nt(prt_ref[...], q_rope.astype(BF16))
    qtd_ref[0] = qtd.astype(BF16)

    ckv = (_rms(z[:, o_ckv:o_kr]) * ckvg_ref[...]).astype(BF16)
    kr = _rope_slabs(z[:, o_kr:], cos_d, sin_d, 16)
    kd_ref[0, 0] = (_dot(ckv, wkn_ref[...]) + _dot(kr.astype(BF16), pkr_ref[...])).astype(BF16)
    vtd_ref[0, 0] = _dot_nt(wvt_ref[...], ckv).astype(BF16)


def _qkv(s, mods, w_in, tabs, consts, *, ctx_len, tm):
    nb, lt, d = s.shape
    nch = lt // tm
    tab_spec = pl.BlockSpec((tm, LANES), lambda b, j: (j, 0))
    out_shapes = [
        jax.ShapeDtypeStruct((nb, 8 * LANES, lt), BF16),
        jax.ShapeDtypeStruct((nb, 8 * LANES, lt), BF16),
        jax.ShapeDtypeStruct((nb, nch, tm, 2 * LANES), BF16),
        jax.ShapeDtypeStruct((nb, nch, tm, 8 * LANES), BF16),
        jax.ShapeDtypeStruct((nb, nch, 2 * 64, tm), BF16),
        jax.ShapeDtypeStruct((nb, nch, 8 * 64, tm), BF16),
    ]
    out_specs = [
        pl.BlockSpec((1, 8 * LANES, tm), lambda b, j: (b, 0, j)),
        pl.BlockSpec((1, 8 * LANES, tm), lambda b, j: (b, 0, j)),
        pl.BlockSpec((1, 1, tm, 2 * LANES), lambda b, j: (b, j, 0, 0)),
        pl.BlockSpec((1, 1, tm, 8 * LANES), lambda b, j: (b, j, 0, 0)),
        pl.BlockSpec((1, 1, 2 * 64, tm), lambda b, j: (b, j, 0, 0)),
        pl.BlockSpec((1, 1, 8 * 64, tm), lambda b, j: (b, j, 0, 0)),
    ]
    return pl.pallas_call(
        functools.partial(_qkv_kernel, tm=tm, ctx_len=ctx_len, head_dim=64, d_qk=96),
        grid=(nb, nch),
        in_specs=[_row_spec(tm, d)] + _mod_specs(nb, d) + [_full_spec(w_in.shape)] + [tab_spec] * 4
        + [_full_spec(c.shape) for c in consts],
        out_specs=out_specs,
        out_shape=out_shapes,
        compiler_params=_cparams(("parallel", "parallel")),
        name="cd_qkv",
    )(s, mods, mods, w_in, *tabs, *consts)


def _attn_kernel(qt_ref, k_ref, vt_ref, o_ref, s0_ref, s1_ref, c0_ref, c1_ref, m_ref, acc_ref,
                 *, kv_slots, tq, n_ctx, n_all, dv):
    heads = len(kv_slots)

    def scores(j, g, kg, dst):
        s_dst, c_dst = dst
        st = _dot(k_ref[0, j, :, kg * LANES:(kg + 1) * LANES], qt_ref[0, g * LANES:(g + 1) * LANES, :])
        s_dst[g] = st
        c_dst[g] = jnp.max(st, axis=0, keepdims=True)

    def consume(j, src, g, kg):
        s_src, c_src = src
        m_old = m_ref[g]
        m_new = jnp.maximum(m_old, c_src[g])
        alpha = jnp.exp2(m_old - m_new)
        p = jnp.exp2((s_src[g] - m_new).astype(BF16))
        m_ref[g] = m_new
        vt = jnp.concatenate([vt_ref[0, j, kg * dv:(kg + 1) * dv, :], jnp.ones((DEN_ROWS, p.shape[0]), BF16)],
                             axis=0)
        acc_ref[g] = alpha * acc_ref[g] + _dot(vt, p)

    def step(j, j_next, cur, nxt):
        for g, kg in enumerate(kv_slots):
            if j_next is not None:
                scores(j_next, g, kg, nxt)
            consume(j, cur, g, kg)

    def run(n):
        bufs = ((s0_ref, c0_ref), (s1_ref, c1_ref))
        m_ref[...] = jnp.full(m_ref.shape, NEG, F32)
        acc_ref[...] = jnp.zeros(acc_ref.shape, F32)
        for g, kg in enumerate(kv_slots):
            scores(0, g, kg, bufs[0])
        n_main = (n - 1) // ATTN_UNROLL * ATTN_UNROLL

        def trip(i, c):
            for k in range(ATTN_UNROLL):
                j = ATTN_UNROLL * i + k
                step(j, j + 1, bufs[k % 2], bufs[(k + 1) % 2])
            return c

        lax.fori_loop(0, n_main // ATTN_UNROLL, trip, 0)
        for j in range(n_main, n):
            step(j, j + 1 if j + 1 < n else None, bufs[j % 2], bufs[(j + 1) % 2])
        ot = jnp.concatenate([acc_ref[g, :dv, :] / acc_ref[g, dv:dv + 1, :] for g in range(heads)],
                             axis=0)
        o_ref[0] = ot.T.astype(o_ref.dtype)

    is_ctx_tile = pl.program_id(1) * tq < n_ctx * k_ref.shape[2]

    @pl.when(is_ctx_tile)
    def _():
        run(n_ctx)

    @pl.when(jnp.logical_not(is_ctx_tile))
    def _():
        run(n_all)


def _attention(qt, k, vt, *, kv_slots, ctx_len, tq, name):
    nb, hq, lt = qt.shape
    heads = len(kv_slots)
    assert hq == heads * LANES
    nch, tk = k.shape[1], k.shape[2]
    dv = vt.shape[2] // (k.shape[3] // LANES)
    once = pl.Buffered(1)
    return pl.pallas_call(
        functools.partial(_attn_kernel, kv_slots=kv_slots, tq=tq, n_ctx=ctx_len // tk, n_all=nch, dv=dv),
        grid=(nb, lt // tq),
        in_specs=[
            pl.BlockSpec((1, hq, tq), lambda b, q: (b, 0, q)),
            pl.BlockSpec((1, nch, tk, k.shape[3]), lambda b, q: (b, 0, 0, 0), pipeline_mode=once),
            pl.BlockSpec((1, nch, vt.shape[2], tk), lambda b, q: (b, 0, 0, 0), pipeline_mode=once),
        ],
        out_specs=pl.BlockSpec((1, tq, heads * dv), lambda b, q: (b, q, 0)),
        out_shape=jax.ShapeDtypeStruct((nb, lt, heads * dv), BF16),
        scratch_shapes=[pltpu.VMEM((heads, tk, tq), F32), pltpu.VMEM((heads, tk, tq), F32),
                        pltpu.VMEM((heads, 1, tq), F32), pltpu.VMEM((heads, 1, tq), F32),
                        pltpu.VMEM((heads, 1, tq), F32), pltpu.VMEM((heads, dv + DEN_ROWS, tq), F32)],
        compiler_params=_cparams(("parallel", "arbitrary")),
        name=name,
    )(qt, k, vt)


def _oproj_kernel(oc_ref, od_ref, x_ref, mx_ref, mt_ref, w_ref, r_ref, xo_ref, h2_ref, g_ref,
                  *, tm, ctx_len, n_exp):
    is_ctx = _is_ctx_rows(tm, ctx_len)
    o = jnp.concatenate([oc_ref[0], od_ref[0]], axis=-1)
    xn = x_ref[0] + _mod(mx_ref, mt_ref, 2, is_ctx) * _dot(o, w_ref[...])
    xo_ref[0] = xn
    h2 = _rms(xn) * (1.0 + _mod(mx_ref, mt_ref, 4, is_ctx)) + _mod(mx_ref, mt_ref, 3, is_ctx)
    h2_ref[0] = _pack_pairs(h2)
    h_hi = h2.astype(BF16)
    h_lo = (h2 - h_hi.astype(F32)).astype(BF16)
    logits = _dot(h_hi, r_ref[0]) + _dot(h_lo, r_ref[0]) + _dot(h_hi, r_ref[1])
    lane = lax.broadcasted_iota(jnp.int32, (tm, LANES), 1).astype(F32)
    lg = jnp.where(lane < n_exp, logits, NEG)
    m1 = jnp.max(lg, axis=-1, keepdims=True)
    i1 = jnp.min(jnp.where(lg == m1, lane, float(LANES)), axis=-1, keepdims=True)
    lg2 = jnp.where(lane == i1, NEG, lg)
    m2 = jnp.max(lg2, axis=-1, keepdims=True)
    i2 = jnp.min(jnp.where(lg2 == m2, lane, float(LANES)), axis=-1, keepdims=True)
    e2 = jnp.exp(m2 - m1)
    den = 1.0 + e2
    g_ref[0] = jnp.where(lane == 0.0, i1, jnp.where(lane == 1.0, i2, jnp.where(lane == 2.0, 1.0 / den, e2 / den)))


def _oproj(oc, od, s, mods, w_out, router, *, ctx_len, tm, n_exp):
    nb, lt, d = s.shape
    return pl.pallas_call(
        functools.partial(_oproj_kernel, tm=tm, ctx_len=ctx_len, n_exp=n_exp),
        grid=(nb, lt // tm),
        in_specs=[_row_spec(tm, oc.shape[2]), _row_spec(tm, od.shape[2]), _row_spec(tm, d)]
        + _mod_specs(nb, d) + [_full_spec(w_out.shape), _full_spec(router.shape)],
        out_specs=[_row_spec(tm, d), _row_spec(tm, _packed_width(d)), _row_spec(tm, LANES)],
        out_shape=[jax.ShapeDtypeStruct((nb, lt, d), F32), jax.ShapeDtypeStruct((nb, lt, _packed_width(d)), PACKED),
                   jax.ShapeDtypeStruct((nb, lt, LANES), F32)],
        compiler_params=_cparams(("parallel", "parallel")),
        name="cd_oproj_router",
    )(oc, od, s, mods, mods, w_out, router)


def _sc_gather(table, idx, *, chunk=SC_CHUNK):
    n_rows, d = idx.shape[0], table.shape[1]
    info = plsc.get_sparse_core_info()
    n_workers = info.num_cores * info.num_subcores
    per_w = n_rows // n_workers
    assert n_rows % (n_workers * chunk) == 0 and chunk % 8 == 0 and chunk <= LANES
    mesh = plsc.VectorSubcoreMesh(core_axis_name="c", subcore_axis_name="s")

    def body(table_hbm, idx_hbm, out_hbm, idx_v, rows_v, sem):
        wid = lax.axis_index("s") * info.num_cores + lax.axis_index("c")
        base = wid * per_w
        pltpu.sync_copy(idx_hbm.at[pl.ds(base, per_w)], idx_v)

        @pl.loop(0, per_w // chunk)
        def _(i):
            off = pl.multiple_of(i * chunk, chunk)
            pltpu.async_copy(table_hbm.at[idx_v.at[pl.ds(off, chunk)]], rows_v, sem).wait()
            pltpu.sync_copy(rows_v, out_hbm.at[pl.ds(base + off, chunk)])

    return pl.kernel(
        body,
        out_type=jax.ShapeDtypeStruct((n_rows, d), table.dtype),
        mesh=mesh,
        scratch_types=[pltpu.VMEM((per_w,), jnp.int32), pltpu.VMEM((chunk, d), table.dtype),
                       pltpu.SemaphoreType.DMA],
        name="sc_row_gather",
    )(table, idx)


def _route_plan(route, *, n_exp, ts, align):
    n_tok = route.shape[0]
    experts = route[:, :2].astype(jnp.int32)
    onehot = (experts[:, :, None] == jnp.arange(n_exp)[None, None, :]).astype(jnp.int32).sum(axis=1)
    before = jnp.cumsum(onehot, axis=0) - onehot
    counts = onehot.sum(axis=0)
    gsz = (counts + ts - 1) // ts * ts
    gend = jnp.cumsum(gsz)
    gstart = gend - gsz
    pos = gstart[experts] + jnp.take_along_axis(before, experts, axis=1)
    n_slots = -(-(2 * n_tok + n_exp * ts) // align) * align
    n_tiles = n_slots // ts
    tile_start = jnp.arange(n_tiles, dtype=jnp.int32) * ts
    n_used = (gend[-1] // ts).astype(jnp.int32)
    tile_expert = jnp.minimum(jnp.searchsorted(gend, tile_start, side="right"), n_exp - 1).astype(jnp.int32)
    order = jnp.argsort(experts.reshape(-1), stable=True).astype(jnp.int32)
    slot_e = jnp.repeat(tile_expert, ts)
    rank = jnp.arange(n_slots, dtype=jnp.int32) - gstart[slot_e]
    cstart = jnp.cumsum(counts) - counts
    src = jnp.clip(cstart[slot_e] + rank, 0, 2 * n_tok - 1)
    tok = jnp.where(rank < counts[slot_e], order[src] // 2, 0).astype(jnp.int32)
    last_used = tile_expert[jnp.maximum(n_used - 1, 0)]
    tile_expert = jnp.where(jnp.arange(n_tiles) < n_used, tile_expert, last_used)
    return tok, pos.T.reshape(-1).astype(jnp.int32), tile_expert, n_used.reshape(1)


def _gffn_kernel(te_ref, nu_ref, h_ref, wg_ref, wu_ref, wd_ref, o_ref, a_ref):
    used = pl.program_id(0) < nu_ref[0]

    @pl.when(used)
    def _():
        h = _unpack_pairs(h_ref[...]).astype(BF16)
        o_ref[...] = _pack_pairs(_swiglu_rows(h, wg_ref, wu_ref, wd_ref, a_ref))

    @pl.when(jnp.logical_not(used))
    def _():
        o_ref[...] = jnp.zeros(o_ref.shape, o_ref.dtype)


def _grouped_ffn(hs, tile_expert, n_used, w_gate, w_up, w_down, *, ts):
    n_slots, dp = hs.shape
    d, f = w_gate.shape[1:]
    once = pl.Buffered(1)
    grid_spec = pltpu.PrefetchScalarGridSpec(
        num_scalar_prefetch=2,
        grid=(n_slots // ts,),
        in_specs=[
            pl.BlockSpec((ts, dp), lambda j, te, nu: (j, 0)),
            pl.BlockSpec((1, d, f), lambda j, te, nu: (te[j], 0, 0), pipeline_mode=once),
            pl.BlockSpec((1, d, f), lambda j, te, nu: (te[j], 0, 0), pipeline_mode=once),
            pl.BlockSpec((1, f, d), lambda j, te, nu: (te[j], 0, 0), pipeline_mode=once),
        ],
        out_specs=pl.BlockSpec((ts, dp), lambda j, te, nu: (j, 0)),
        scratch_shapes=[pltpu.VMEM((ts, f), BF16)],
    )
    return pl.pallas_call(
        _gffn_kernel,
        grid_spec=grid_spec,
        out_shape=jax.ShapeDtypeStruct((n_slots, dp), PACKED),
        compiler_params=_cparams(("arbitrary",)),
        name="moe_grouped_ffn",
    )(tile_expert, n_used, hs, w_gate, w_up, w_down)


def _combine_kernel(y1_ref, y2_ref, r_ref, x_ref, mx_ref, mt_ref, fg_ref, o_ref, *, tm, ctx_len, row0, final):
    pos = (pl.program_id(1) + row0) * tm + lax.broadcasted_iota(jnp.int32, (tm, 1), 0)
    is_ctx = pos < ctx_len
    r = r_ref[0]
    f = r[:, 2:3] * _unpack_pairs(y1_ref[0, 0]) + r[:, 3:4] * _unpack_pairs(y2_ref[0, 0])
    xn = x_ref[0] + _mod(mx_ref, mt_ref, 5, is_ctx) * f
    o_ref[0] = _rms(xn) * fg_ref[...] if final else xn


def _combine(yg, route, s, mods, final_gain, *, ctx_len, tm, final):
    nb, lt, d = s.shape
    row0 = ctx_len // tm if final else 0
    rows = lt - row0 * tm

    def rspec(width):
        return pl.BlockSpec((1, tm, width), lambda b, j: (b, j + row0, 0))

    return pl.pallas_call(
        functools.partial(_combine_kernel, tm=tm, ctx_len=ctx_len, row0=row0, final=final),
        grid=(nb, rows // tm),
        in_specs=[pl.BlockSpec((1, 1, tm, yg.shape[-1]), lambda b, j: (0, b, j + row0, 0)),
                  pl.BlockSpec((1, 1, tm, yg.shape[-1]), lambda b, j: (1, b, j + row0, 0)),
                  rspec(LANES), rspec(d)] + _mod_specs(nb, d) + [_full_spec((1, d))],
        out_specs=_row_spec(tm, d),
        out_shape=jax.ShapeDtypeStruct((nb, rows, d), F32),
        compiler_params=_cparams(("parallel", "parallel")),
        name="moe_combine",
    )(yg, yg, route, s, mods, mods, final_gain.reshape(1, d))


def _moe(h2, route, s, mods, w_gate, w_up, w_down, final_gain, *, ctx_len, n_exp, ts, tm, final):
    nb, lt, d = s.shape
    n_tok = nb * lt
    info = plsc.get_sparse_core_info()
    sc_rows = info.num_cores * info.num_subcores * SC_CHUNK
    tok, pos, tile_expert, n_used = _route_plan(route.reshape(n_tok, LANES), n_exp=n_exp, ts=ts,
                                                align=int(np.lcm(ts, sc_rows)))
    hs = _sc_gather(h2.reshape(n_tok, h2.shape[-1]), tok)
    ys = _grouped_ffn(hs, tile_expert, n_used, w_gate, w_up, w_down, ts=ts)
    yg = _sc_gather(ys, pos).reshape(2, nb, lt, ys.shape[-1])
    return _combine(yg, route, s, mods, final_gain, ctx_len=ctx_len, tm=tm, final=final)


def _final_kernel(x_ref, g_ref, o_ref):
    o_ref[0] = _rms(x_ref[0]) * g_ref[...]


def _final(s, gain, *, ctx_len, tm):
    nb, lt, d = s.shape
    seq = lt - ctx_len
    off = ctx_len // tm
    return pl.pallas_call(
        _final_kernel,
        grid=(nb, seq // tm),
        in_specs=[pl.BlockSpec((1, tm, d), lambda b, j: (b, j + off, 0)), _full_spec((1, d))],
        out_specs=_row_spec(tm, d),
        out_shape=jax.ShapeDtypeStruct((nb, seq, d), F32),
        compiler_params=_cparams(("parallel", "parallel")),
        name="final_norm",
    )(s, gain.reshape(1, d))


def _rope_tables(seq, ctx_len):
    rows = seq // GRID_W
    row = jnp.repeat(jnp.arange(rows), GRID_W).astype(F32)
    col = jnp.tile(jnp.arange(GRID_W), rows).astype(F32)

    def table(rot_dim, reps):
        axis_dim = rot_dim // 2
        inv = ROPE_THETA ** (-jnp.arange(0, axis_dim, 2, dtype=F32) / axis_dim)
        ang = jnp.concatenate([row[:, None] * inv, col[:, None] * inv], axis=-1)
        cos = jnp.concatenate([jnp.ones((ctx_len, axis_dim), F32), jnp.cos(ang)], axis=0)
        sin = jnp.concatenate([jnp.zeros((ctx_len, axis_dim), F32), jnp.sin(ang)], axis=0)
        return (jnp.tile(jnp.concatenate([cos, cos], axis=-1), (1, reps)),
                jnp.tile(jnp.concatenate([-sin, sin], axis=-1), (1, reps)))

    cos_c, sin_c = table(64, 2)
    cos_d, sin_d = table(32, 4)
    return cos_c, sin_c, cos_d, sin_d


def _placement(n_rows, n_cols, pairs):
    m = np.zeros((n_rows, n_cols), np.float32)
    for r, c in pairs:
        m[r, c] = 1.0
    return jnp.asarray(m, BF16)


def _cd_consts(q_gain, k_gain, cq_gain, ckv_gain, w_uq, w_ukv):
    hd, nh, nkv, d_nope, d_rope, d_v = 64, 8, 2, 64, 32, 64
    d_qk = d_nope + d_rope
    heads = np.arange(nh)
    b512 = jnp.asarray(np.kron(np.eye(nh), np.ones((hd, hd))), BF16)
    b128 = jnp.asarray(np.kron(np.eye(nkv), np.ones((hd, hd))), BF16)
    pct = _placement(nh * LANES, nh * hd, [(h * LANES + i, h * hd + i) for h in heads for i in range(hd)])
    nope_cols = (heads[:, None] * d_qk + np.arange(d_nope)[None, :]).reshape(-1)
    rope_cols = (heads[:, None] * d_qk + d_nope + np.arange(d_rope)[None, :]).reshape(-1)
    nope_rows = (heads[:, None] * LANES + np.arange(d_nope)[None, :]).reshape(-1)
    wqnt = jnp.zeros((nh * LANES, w_uq.shape[0]), F32).at[nope_rows].set(w_uq[:, nope_cols].T).astype(BF16)
    wqr = w_uq[:, rope_cols].astype(BF16)
    prt = _placement(nh * LANES, nh * d_rope,
                     [(h * LANES + d_nope + i, h * d_rope + i) for h in heads for i in range(d_rope)])
    kn_cols = (heads[:, None] * (d_nope + d_v) + np.arange(d_nope)[None, :]).reshape(-1)
    v_cols = (heads[:, None] * (d_nope + d_v) + d_nope + np.arange(d_v)[None, :]).reshape(-1)
    wkn = jnp.zeros((w_ukv.shape[0], nh * LANES), F32).at[:, nope_rows].set(w_ukv[:, kn_cols]).astype(BF16)
    pkr = _placement(LANES, nh * LANES, [(i, h * LANES + d_nope + i) for h in heads for i in range(d_rope)])
    wvt = w_ukv[:, v_cols].T.astype(BF16)
    pkc = _placement(nkv * hd, nkv * LANES, [(h * hd + i, h * LANES + i) for h in range(nkv) for i in range(hd)])
    eye = jnp.asarray(np.eye(nkv * hd), BF16)
    return [jnp.tile(q_gain, nh).reshape(1, -1), jnp.tile(k_gain, nkv).reshape(1, -1),
            cq_gain.reshape(1, -1), ckv_gain.reshape(1, -1), b512, b128,
            pct, wqnt, wqr, prt, wkn, pkr, wvt, pkc, eye]


def _largest_tile(n, candidates):
    for t in candidates:
        if n % t == 0:
            return t
    raise ValueError(f"no tile in {candidates} divides {n}")


def kernel(x, c, ctx, c_ctx, ada_w, ada_b, ab_w_in, a_v_gain, a_w_s, a_b_s, b_w_pool, b_scale, ab_w_out,
           ffn_w_gate, ffn_w_up, ffn_w_down, cd_w_in, c_q_gain, c_k_gain, d_cq_gain, d_ckv_gain, d_w_uq,
           d_w_ukv, cd_w_out, moe_router, moe_w_gate, moe_w_up, moe_w_down, final_gain):
    nb, seq, d = x.shape
    ctx_len = ctx.shape[1]
    lt = ctx_len + seq
    depth = ada_w.shape[0]
    n_exp = moe_router.shape[-1]
    chunk = a_w_s.shape[-1]
    tr = 256
    assert ctx_len % tr == 0 and seq % tr == 0 and seq % GRID_W == 0
    tm_ff = _largest_tile(lt, (768, 384, 256))
    assert ffn_w_gate.shape[-1] % FF_CHUNK == 0 and moe_w_gate.shape[-1] % FF_CHUNK == 0

    s = jnp.concatenate([ctx, x], axis=1)

    pad = -(nb + 1) % 8
    cvec = jnp.concatenate([c, c_ctx[None, :], jnp.zeros((pad, d), F32)], axis=0)
    mods_all = _ada_all(cvec, ada_w, ada_b).reshape(depth, nb + 1 + pad, 6, d)
    tabs = _rope_tables(seq, ctx_len)

    for layer in range(depth):
        i = layer // 2
        mods = mods_all[layer]
        if layer % 2 == 0:
            z = _inproj(s, mods, ab_w_in[i].astype(BF16), ctx_len=ctx_len, tm=tr)
            bs_full = jnp.repeat(a_b_s[i].T, LANES, axis=1)
            s, h2 = _mixer(z, s, mods, a_v_gain[i].reshape(1, -1), a_w_s[i].astype(BF16), bs_full,
                           b_w_pool[i].astype(BF16), b_scale[i].reshape(1, -1), ab_w_out[i].astype(BF16),
                           ctx_len=ctx_len, tm=tr, chunk=chunk)
            s = _swiglu(h2, s, mods, ffn_w_gate[i][None].astype(BF16), ffn_w_up[i][None].astype(BF16),
                        ffn_w_down[i][None].astype(BF16), ctx_len=ctx_len, tm=tm_ff)
        else:
            w_in = jnp.pad(cd_w_in[i], ((0, 0), (0, 12 * LANES - cd_w_in.shape[-1]))).astype(BF16)
            consts = _cd_consts(c_q_gain[i], c_k_gain[i], d_cq_gain[i], d_ckv_gain[i], d_w_uq[i], d_w_ukv[i])
            qtc, qtd, kc, kd, vtc, vtd = _qkv(s, mods, w_in, tabs, consts, ctx_len=ctx_len, tm=tr)
            oc = _attention(qtc, kc, vtc, kv_slots=(0, 0, 0, 0, 1, 1, 1, 1), ctx_len=ctx_len, tq=tr,
                            name="attn_gqa")
            od = _attention(qtd, kd, vtd, kv_slots=tuple(range(8)), ctx_len=ctx_len, tq=tr, name="attn_mla")
            router = jnp.pad(moe_router[i], ((0, 0), (0, LANES - n_exp)))
            r_hi = router.astype(BF16)
            router = jnp.stack([r_hi, (router - r_hi.astype(F32)).astype(BF16)])
            s, h2, route = _oproj(oc, od, s, mods, cd_w_out[i].astype(BF16), router,
                                  ctx_len=ctx_len, tm=tr, n_exp=n_exp)
            s = _moe(h2, route, s, mods, moe_w_gate[i].astype(BF16), moe_w_up[i].astype(BF16),
                     moe_w_down[i].astype(BF16), final_gain, ctx_len=ctx_len, n_exp=n_exp, ts=512,
                     tm=tr, final=layer == depth - 1)

    return s if depth % 2 == 0 else _final(s, final_gain, ctx_len=ctx_len, tm=tr)
```

```python
import functools

import numpy as np
import jax
import jax.numpy as jnp
from jax import lax
from jax.experimental import pallas as pl
from jax.experimental.pallas import tpu as pltpu
from jax.experimental.pallas import tpu_sc as plsc

F32 = jnp.float32
BF16 = jnp.bfloat16
EPS = 1e-6
ROPE_THETA = 10000.0
GRID_W = 64
LANES = 128
HALO = 8
POOL_WINDOWS = (2, 4, 8, 16)
NEG = -1e30
LOG2E = 1.4426950408889634
ATTN_UNROLL = 8
FF_CHUNK = 256
DEN_ROWS = 16
SC_CHUNK = 64
VMEM_LIMIT = 56 * 1024 * 1024


def _cparams(sem, flags=None):
    return pltpu.CompilerParams(dimension_semantics=sem, vmem_limit_bytes=VMEM_LIMIT, flags=flags)


def _rms(x):
    return x * lax.rsqrt(jnp.mean(x * x, axis=-1, keepdims=True) + EPS)


def _is_ctx_rows(tm, ctx_len):
    pos = pl.program_id(1) * tm + lax.broadcasted_iota(jnp.int32, (tm, 1), 0)
    return pos < ctx_len


def _mod(mx_ref, mt_ref, idx, is_ctx):
    return jnp.where(is_ctx, mt_ref[0, idx:idx + 1, :], mx_ref[0, idx:idx + 1, :])


def _dot(a, b):
    return jnp.dot(a, b, preferred_element_type=F32)


def _dot_nt(a, b):
    return lax.dot_general(a, b, (((1,), (1,)), ((), ())), preferred_element_type=F32)


def _silu(x):
    return x * jax.nn.sigmoid(x)


PACKED = jnp.uint32


def _packed_width(d):
    return d // 2


def _pack_pairs(x):
    w = x.shape[1] // 2
    xb = x.astype(BF16).astype(F32)
    hi = pltpu.bitcast(xb[:, :w], jnp.uint32)
    lo = pltpu.bitcast(xb[:, w:], jnp.uint32)
    return hi | (lo >> 16)


def _unpack_pairs(p):
    hi = pltpu.bitcast(p & jnp.uint32(0xFFFF0000), F32)
    lo = pltpu.bitcast(p << 16, F32)
    return jnp.concatenate([hi, lo], axis=1)


def _ada_kernel(c_ref, w_ref, b_ref, o_ref):
    c = c_ref[...]
    o_ref[0] = jnp.dot(_silu(c), w_ref[0], preferred_element_type=F32,
                       precision=lax.Precision.HIGHEST) + b_ref[0]


def _ada_all(cvec, ada_w, ada_b):
    depth, d, n = ada_w.shape
    rows = cvec.shape[0]
    tn = 1536
    return pl.pallas_call(
        _ada_kernel,
        grid=(depth, n // tn),
        in_specs=[
            pl.BlockSpec((rows, d), lambda l, j: (0, 0)),
            pl.BlockSpec((1, d, tn), lambda l, j: (l, 0, j)),
            pl.BlockSpec((1, 1, tn), lambda l, j: (l, 0, j)),
        ],
        out_specs=pl.BlockSpec((1, rows, tn), lambda l, j: (l, 0, j)),
        out_shape=jax.ShapeDtypeStruct((depth, rows, n), F32),
        compiler_params=_cparams(("arbitrary", "arbitrary")),
        name="adaln",
    )(cvec, ada_w, ada_b.reshape(depth, 1, n))


def _row_spec(tm, width):
    return pl.BlockSpec((1, tm, width), lambda b, j, *_: (b, j, 0))


def _mod_specs(nb, d):
    return [pl.BlockSpec((1, 6, d), lambda b, j, *_: (b, 0, 0)),
            pl.BlockSpec((1, 6, d), lambda b, j, *_: (nb, 0, 0))]


def _full_spec(shape):
    nd = len(shape)
    return pl.BlockSpec(shape, lambda *_: (0,) * nd)


def _mixer_kernel(x_ref, xp_ref, xn_ref, mx_ref, mt_ref, win_ref, vg_ref, ws_ref, bs_ref, wp_ref,
                  psc_ref, wo_ref, xo_ref, h2_ref, *, tm, ctx_len, lt, chunk):
    is_ctx = _is_ctx_rows(tm, ctx_len)
    pos0 = pl.program_id(1) * tm
    aw = 4 * LANES
    sc1 = 1.0 + _mod(mx_ref, mt_ref, 1, is_ctx)
    sh1 = _mod(mx_ref, mt_ref, 0, is_ctx)
    z = _dot((_rms(x_ref[0]) * sc1 + sh1).astype(BF16), win_ref[...])
    halo_x = jnp.concatenate([xp_ref[0], xn_ref[0]], axis=0)
    halo_h = _rms(halo_x) * sc1[:2 * HALO] + sh1[:2 * HALO]
    halo_z = _dot(halo_h.astype(BF16), win_ref[:, 2 * aw:])
    za = z[:, :2 * aw]
    gl = 0.5 * za * (1.0 + lax.erf(za * (2.0 ** -0.5)))
    u = gl[:, :aw]
    v = gl[:, aw:]
    vn = (_rms(v) * vg_ref[...]).astype(BF16)
    ya_rows = []
    for c in range(tm // chunk):
        rs = slice(c * chunk, (c + 1) * chunk)
        cols = []
        for g in range(4):
            cs = slice(g * LANES, (g + 1) * LANES)
            sg = _dot(ws_ref[g], vn[rs, cs]) + bs_ref[:, cs]
            cols.append(u[rs, cs] * sg)
        ya_rows.append(jnp.concatenate(cols, axis=1))
    ya = jnp.concatenate(ya_rows, axis=0)

    in_ctx = pos0 < ctx_len
    seq_start = jnp.where(in_ctx, 0, ctx_len)
    seq_len = jnp.where(in_ctx, ctx_len, lt - ctx_len)
    p_in = pos0 - seq_start + lax.broadcasted_iota(jnp.int32, (tm, 1), 0)
    has_prev = jnp.logical_and(pos0 != 0, pos0 != ctx_len).astype(F32)
    has_next = jnp.logical_and(pos0 + tm != ctx_len, pos0 + tm != lt).astype(F32)
    zp = z[:, 2 * aw:]
    ext = jnp.concatenate([halo_z[:HALO] * has_prev, zp, halo_z[HALO:] * has_next], axis=0)
    n = tm + 2 * HALO
    yb_cols = []
    for g, w in enumerate(POOL_WINDOWS):
        cs = slice(g * LANES, (g + 1) * LANES)
        a = ext[:, cs]
        step = 1
        while step < w:
            a = a + pltpu.roll(a, n - step, 0)
            step *= 2
        shift = HALO - w // 2
        if shift:
            a = pltpu.roll(a, n - shift, 0)
        wsum = a[:tm]
        lo = jnp.clip(p_in - w // 2, 0, seq_len)
        hi = jnp.clip(p_in - w // 2 + w, 0, seq_len)
        pooled = wsum / (hi - lo).astype(F32) - zp[:, cs]
        yb_cols.append(_dot(pooled.astype(BF16), wp_ref[g]) * psc_ref[:, cs])
    y = jnp.concatenate([ya] + yb_cols, axis=1).astype(BF16)
    xn = x_ref[0] + _mod(mx_ref, mt_ref, 2, is_ctx) * _dot(y, wo_ref[...])
    xo_ref[0] = xn
    h2 = _rms(xn) * (1.0 + _mod(mx_ref, mt_ref, 4, is_ctx)) + _mod(mx_ref, mt_ref, 3, is_ctx)
    h2_ref[0] = h2.astype(BF16)


def _mixer(s, mods, w_in, v_gain, w_s, bs_full, w_pool, p_scale, w_out, *, ctx_len, tm, chunk):
    nb, lt, d = s.shape
    aw = 4 * LANES
    nh = lt // HALO
    th = tm // HALO
    pw = 4 * LANES
    halo_prev = pl.BlockSpec((1, HALO, d), lambda b, j: (b, jnp.maximum(j * th - 1, 0), 0))
    halo_next = pl.BlockSpec((1, HALO, d), lambda b, j: (b, jnp.minimum((j + 1) * th, nh - 1), 0))
    return pl.pallas_call(
        functools.partial(_mixer_kernel, tm=tm, ctx_len=ctx_len, lt=lt, chunk=chunk),
        grid=(nb, lt // tm),
        in_specs=[_row_spec(tm, d), halo_prev, halo_next] + _mod_specs(nb, d) + [
            _full_spec(w_in.shape), _full_spec((1, aw)), _full_spec(w_s.shape), _full_spec(bs_full.shape),
            _full_spec(w_pool.shape), _full_spec((1, pw)), _full_spec(w_out.shape)],
        out_specs=[_row_spec(tm, d), _row_spec(tm, d)],
        out_shape=[jax.ShapeDtypeStruct((nb, lt, d), F32), jax.ShapeDtypeStruct((nb, lt, d), BF16)],
        compiler_params=_cparams(("parallel", "parallel")),
        name="ab_mixer",
    )(s, s, s, mods, mods, w_in, v_gain, w_s, bs_full, w_pool, p_scale, w_out)


def _swiglu_rows(h, wg_ref, wu_ref, wd_ref, a_ref):
    for c in range(a_ref.shape[1] // FF_CHUNK):
        cs = slice(c * FF_CHUNK, (c + 1) * FF_CHUNK)
        a_ref[:, cs] = (_silu(_dot(h, wg_ref[0, :, cs])) * _dot(h, wu_ref[0, :, cs])).astype(BF16)
    return _dot(a_ref[...], wd_ref[0])


def _swiglu_kernel(h_ref, x_ref, mx_ref, mt_ref, wg_ref, wu_ref, wd_ref, o_ref, a_ref, *, tm, ctx_len):
    is_ctx = _is_ctx_rows(tm, ctx_len)
    f = _swiglu_rows(h_ref[0], wg_ref, wu_ref, wd_ref, a_ref)
    o_ref[0] = x_ref[0] + _mod(mx_ref, mt_ref, 5, is_ctx) * f


def _swiglu(h2, s, mods, w_gate, w_up, w_down, *, ctx_len, tm):
    nb, lt, d = s.shape
    f = w_gate.shape[2]
    once = pl.Buffered(1)
    return pl.pallas_call(
        functools.partial(_swiglu_kernel, tm=tm, ctx_len=ctx_len),
        grid=(nb, lt // tm),
        in_specs=[_row_spec(tm, d), _row_spec(tm, d)] + _mod_specs(nb, d) + [
            pl.BlockSpec((1, d, f), lambda b, j: (0, 0, 0), pipeline_mode=once),
            pl.BlockSpec((1, d, f), lambda b, j: (0, 0, 0), pipeline_mode=once),
            pl.BlockSpec((1, f, d), lambda b, j: (0, 0, 0), pipeline_mode=once)],
        out_specs=_row_spec(tm, d),
        out_shape=jax.ShapeDtypeStruct((nb, lt, d), F32),
        scratch_shapes=[pltpu.VMEM((tm, f), BF16)],
        compiler_params=_cparams(("parallel", "parallel")),
        name="swiglu",
    )(h2, s, mods, mods, w_gate, w_up, w_down)


def _seg_sum(x2, b_ref):
    return _dot(x2.astype(BF16), b_ref[...])


def _rope_slabs(x, cos, sin, half):
    lane = lax.broadcasted_iota(jnp.int32, (x.shape[0], LANES), 1)
    first = (lane % (2 * half)) < half
    out = []
    for s in range(x.shape[1] // LANES):
        xs = x[:, s * LANES:(s + 1) * LANES]
        swapped = jnp.where(first, pltpu.roll(xs, LANES - half, 1), pltpu.roll(xs, half, 1))
        out.append(xs * cos + swapped * sin)
    return out[0] if len(out) == 1 else jnp.concatenate(out, axis=1)


def _qkv_kernel(x_ref, mx_ref, mt_ref, win_ref, cc_ref, sc_ref, cd_ref, sd_ref,
                qg_ref, kg_ref, cqg_ref, ckvg_ref, b512_ref, b128_ref,
                pct_ref, wqnt_ref, wqr_ref, prt_ref, wkn_ref, pkr_ref, wvt_ref, pkc_ref, eye_ref,
                qtc_ref, qtd_ref, kc_ref, kd_ref, vtc_ref, vtd_ref,
                *, tm, ctx_len, head_dim, d_qk):
    is_ctx = _is_ctx_rows(tm, ctx_len)
    h = _rms(x_ref[0]) * (1.0 + _mod(mx_ref, mt_ref, 1, is_ctx)) + _mod(mx_ref, mt_ref, 0, is_ctx)
    z = _dot(h.astype(BF16), win_ref[...])
    cos_c, sin_c, cos_d, sin_d = cc_ref[...], sc_ref[...], cd_ref[...], sd_ref[...]
    o_kc, o_vc, o_cq, o_ckv, o_kr = 512, 640, 768, 1152, 1408

    qc = z[:, :o_kc]
    qc = qc * lax.rsqrt(_seg_sum(qc * qc, b512_ref) * (1.0 / head_dim) + EPS) * qg_ref[...]
    qc = _rope_slabs(qc, cos_c, sin_c, head_dim // 2) * (head_dim ** -0.5 * LOG2E)
    qtc_ref[0] = _dot_nt(pct_ref[...], qc.astype(BF16)).astype(BF16)

    kc = z[:, o_kc:o_vc]
    kc = kc * lax.rsqrt(_seg_sum(kc * kc, b128_ref) * (1.0 / head_dim) + EPS) * kg_ref[...]
    kc = _rope_slabs(kc, cos_c, sin_c, head_dim // 2)
    kc_ref[0, 0] = _dot(kc.astype(BF16), pkc_ref[...]).astype(BF16)
    vtc_ref[0, 0] = _dot_nt(eye_ref[...], z[:, o_vc:o_cq].astype(BF16)).astype(BF16)

    cq = (_rms(z[:, o_cq:o_ckv]) * cqg_ref[...]) * (d_qk ** -0.5 * LOG2E)
    cqb = cq.astype(BF16)
    q_rope = _rope_slabs(_dot(cqb, wqr_ref[...]), cos_d, sin_d, 16)
    qtd = _dot_nt(wqnt_ref[...], cqb) + _dot_nt(prt_ref[...], q_rope.astype(BF16))
    qtd_ref[0] = qtd.astype(BF16)

    ckv = (_rms(z[:, o_ckv:o_kr]) * ckvg_ref[...]).astype(BF16)
    kr = _rope_slabs(z[:, o_kr:], cos_d, sin_d, 16)
    kd_ref[0, 0] = (_dot(ckv, wkn_ref[...]) + _dot(kr.astype(BF16), pkr_ref[...])).astype(BF16)
    vtd_ref[0, 0] = _dot_nt(wvt_ref[...], ckv).astype(BF16)


def _qkv(s, mods, w_in, tabs, consts, *, ctx_len, tm):
    nb, lt, d = s.shape
    nch = lt // tm
    tab_spec = pl.BlockSpec((tm, LANES), lambda b, j: (j, 0))
    out_shapes = [
        jax.ShapeDtypeStruct((nb, 8 * LANES, lt), BF16),
        jax.ShapeDtypeStruct((nb, 8 * LANES, lt), BF16),
        jax.ShapeDtypeStruct((nb, nch, tm, 2 * LANES), BF16),
        jax.ShapeDtypeStruct((nb, nch, tm, 8 * LANES), BF16),
        jax.ShapeDtypeStruct((nb, nch, 2 * 64, tm), BF16),
        jax.ShapeDtypeStruct((nb, nch, 8 * 64, tm), BF16),
    ]
    out_specs = [
        pl.BlockSpec((1, 8 * LANES, tm), lambda b, j: (b, 0, j)),
        pl.BlockSpec((1, 8 * LANES, tm), lambda b, j: (b, 0, j)),
        pl.BlockSpec((1, 1, tm, 2 * LANES), lambda b, j: (b, j, 0, 0)),
        pl.BlockSpec((1, 1, tm, 8 * LANES), lambda b, j: (b, j, 0, 0)),
        pl.BlockSpec((1, 1, 2 * 64, tm), lambda b, j: (b, j, 0, 0)),
        pl.BlockSpec((1, 1, 8 * 64, tm), lambda b, j: (b, j, 0, 0)),
    ]
    return pl.pallas_call(
        functools.partial(_qkv_kernel, tm=tm, ctx_len=ctx_len, head_dim=64, d_qk=96),
        grid=(nb, nch),
        in_specs=[_row_spec(tm, d)] + _mod_specs(nb, d) + [_full_spec(w_in.shape)] + [tab_spec] * 4
        + [_full_spec(c.shape) for c in consts],
        out_specs=out_specs,
        out_shape=out_shapes,
        compiler_params=_cparams(("parallel", "parallel")),
        name="cd_qkv",
    )(s, mods, mods, w_in, *tabs, *consts)


def _attn_kernel(qt_ref, k_ref, vt_ref, o_ref, s0_ref, s1_ref, c0_ref, c1_ref, m_ref, acc_ref,
                 *, kv_slots, tq, n_ctx, n_all, dv):
    heads = len(kv_slots)

    def scores(j, g, kg, dst):
        s_dst, c_dst = dst
        st = _dot(k_ref[0, j, :, kg * LANES:(kg + 1) * LANES], qt_ref[0, g * LANES:(g + 1) * LANES, :])
        s_dst[g] = st
        c_dst[g] = jnp.max(st, axis=0, keepdims=True)

    def consume(j, src, g, kg):
        s_src, c_src = src
        m_old = m_ref[g]
        m_new = jnp.maximum(m_old, c_src[g])
        alpha = jnp.exp2(m_old - m_new)
        p = jnp.exp2((s_src[g] - m_new).astype(BF16))
        m_ref[g] = m_new
        vt = jnp.concatenate([vt_ref[0, j, kg * dv:(kg + 1) * dv, :], jnp.ones((DEN_ROWS, p.shape[0]), BF16)],
                             axis=0)
        acc_ref[g] = alpha * acc_ref[g] + _dot(vt, p)

    def step(j, j_next, cur, nxt):
        for g, kg in enumerate(kv_slots):
            if j_next is not None:
                scores(j_next, g, kg, nxt)
            consume(j, cur, g, kg)

    def run(n):
        bufs = ((s0_ref, c0_ref), (s1_ref, c1_ref))
        m_ref[...] = jnp.full(m_ref.shape, NEG, F32)
        acc_ref[...] = jnp.zeros(acc_ref.shape, F32)
        for g, kg in enumerate(kv_slots):
            scores(0, g, kg, bufs[0])
        n_main = (n - 1) // ATTN_UNROLL * ATTN_UNROLL

        def trip(i, c):
            for k in range(ATTN_UNROLL):
                j = ATTN_UNROLL * i + k
                step(j, j + 1, bufs[k % 2], bufs[(k + 1) % 2])
            return c

        lax.fori_loop(0, n_main // ATTN_UNROLL, trip, 0)
        for j in range(n_main, n):
            step(j, j + 1 if j + 1 < n else None, bufs[j % 2], bufs[(j + 1) % 2])
        ot = jnp.concatenate([acc_ref[g, :dv, :] / acc_ref[g, dv:dv + 1, :] for g in range(heads)],
                             axis=0)
        o_ref[0] = ot.T.astype(o_ref.dtype)

    is_ctx_tile = pl.program_id(1) * tq < n_ctx * k_ref.shape[2]

    @pl.when(is_ctx_tile)
    def _():
        run(n_ctx)

    @pl.when(jnp.logical_not(is_ctx_tile))
    def _():
        run(n_all)


def _attention(qt, k, vt, *, kv_slots, ctx_len, tq, name):
    nb, hq, lt = qt.shape
    heads = len(kv_slots)
    assert hq == heads * LANES
    nch, tk = k.shape[1], k.shape[2]
    dv = vt.shape[2] // (k.shape[3] // LANES)
    once = pl.Buffered(1)
    return pl.pallas_call(
        functools.partial(_attn_kernel, kv_slots=kv_slots, tq=tq, n_ctx=ctx_len // tk, n_all=nch, dv=dv),
        grid=(nb, lt // tq),
        in_specs=[
            pl.BlockSpec((1, hq, tq), lambda b, q: (b, 0, q)),
            pl.BlockSpec((1, nch, tk, k.shape[3]), lambda b, q: (b, 0, 0, 0), pipeline_mode=once),
            pl.BlockSpec((1, nch, vt.shape[2], tk), lambda b, q: (b, 0, 0, 0), pipeline_mode=once),
        ],
        out_specs=pl.BlockSpec((1, tq, heads * dv), lambda b, q: (b, q, 0)),
        out_shape=jax.ShapeDtypeStruct((nb, lt, heads * dv), BF16),
        scratch_shapes=[pltpu.VMEM((heads, tk, tq), F32), pltpu.VMEM((heads, tk, tq), F32),
                        pltpu.VMEM((heads, 1, tq), F32), pltpu.VMEM((heads, 1, tq), F32),
                        pltpu.VMEM((heads, 1, tq), F32), pltpu.VMEM((heads, dv + DEN_ROWS, tq), F32)],
        compiler_params=_cparams(("parallel", "arbitrary")),
        name=name,
    )(qt, k, vt)


def _oproj_kernel(oc_ref, od_ref, x_ref, mx_ref, mt_ref, w_ref, r_ref, xo_ref, h2_ref, g_ref,
                  *, tm, ctx_len, n_exp):
    is_ctx = _is_ctx_rows(tm, ctx_len)
    o = jnp.concatenate([oc_ref[0], od_ref[0]], axis=-1)
    xn = x_ref[0] + _mod(mx_ref, mt_ref, 2, is_ctx) * _dot(o, w_ref[...])
    xo_ref[0] = xn
    h2 = _rms(xn) * (1.0 + _mod(mx_ref, mt_ref, 4, is_ctx)) + _mod(mx_ref, mt_ref, 3, is_ctx)
    h2_ref[0] = _pack_pairs(h2)
    h_hi = h2.astype(BF16)
    h_lo = (h2 - h_hi.astype(F32)).astype(BF16)
    logits = _dot(h_hi, r_ref[0]) + _dot(h_lo, r_ref[0]) + _dot(h_hi, r_ref[1])
    lane = lax.broadcasted_iota(jnp.int32, (tm, LANES), 1).astype(F32)
    lg = jnp.where(lane < n_exp, logits, NEG)
    m1 = jnp.max(lg, axis=-1, keepdims=True)
    i1 = jnp.min(jnp.where(lg == m1, lane, float(LANES)), axis=-1, keepdims=True)
    lg2 = jnp.where(lane == i1, NEG, lg)
    m2 = jnp.max(lg2, axis=-1, keepdims=True)
    i2 = jnp.min(jnp.where(lg2 == m2, lane, float(LANES)), axis=-1, keepdims=True)
    e2 = jnp.exp(m2 - m1)
    den = 1.0 + e2
    g_ref[0] = jnp.where(lane == 0.0, i1, jnp.where(lane == 1.0, i2, jnp.where(lane == 2.0, 1.0 / den, e2 / den)))


def _oproj(oc, od, s, mods, w_out, router, *, ctx_len, tm, n_exp):
    nb, lt, d = s.shape
    return pl.pallas_call(
        functools.partial(_oproj_kernel, tm=tm, ctx_len=ctx_len, n_exp=n_exp),
        grid=(nb, lt // tm),
        in_specs=[_row_spec(tm, oc.shape[2]), _row_spec(tm, od.shape[2]), _row_spec(tm, d)]
        + _mod_specs(nb, d) + [_full_spec(w_out.shape), _full_spec(router.shape)],
        out_specs=[_row_spec(tm, d), _row_spec(tm, _packed_width(d)), _row_spec(tm, LANES)],
        out_shape=[jax.ShapeDtypeStruct((nb, lt, d), F32), jax.ShapeDtypeStruct((nb, lt, _packed_width(d)), PACKED),
                   jax.ShapeDtypeStruct((nb, lt, LANES), F32)],
        compiler_params=_cparams(("parallel", "parallel")),
        name="cd_oproj_router",
    )(oc, od, s, mods, mods, w_out, router)


def _sc_gather(table, idx, *, chunk=SC_CHUNK):
    n_rows, d = idx.shape[0], table.shape[1]
    info = plsc.get_sparse_core_info()
    n_workers = info.num_cores * info.num_subcores
    per_w = n_rows // n_workers
    assert n_rows % (n_workers * chunk) == 0 and chunk % 8 == 0 and chunk <= LANES
    mesh = plsc.VectorSubcoreMesh(core_axis_name="c", subcore_axis_name="s")

    def body(table_hbm, idx_hbm, out_hbm, idx_v, rows_v, sem):
        wid = lax.axis_index("s") * info.num_cores + lax.axis_index("c")
        base = wid * per_w
        pltpu.sync_copy(idx_hbm.at[pl.ds(base, per_w)], idx_v)

        @pl.loop(0, per_w // chunk)
        def _(i):
            off = pl.multiple_of(i * chunk, chunk)
            pltpu.async_copy(table_hbm.at[idx_v.at[pl.ds(off, chunk)]], rows_v, sem).wait()
            pltpu.sync_copy(rows_v, out_hbm.at[pl.ds(base + off, chunk)])

    return pl.kernel(
        body,
        out_type=jax.ShapeDtypeStruct((n_rows, d), table.dtype),
        mesh=mesh,
        scratch_types=[pltpu.VMEM((per_w,), jnp.int32), pltpu.VMEM((chunk, d), table.dtype),
                       pltpu.SemaphoreType.DMA],
        name="sc_row_gather",
    )(table, idx)


def _route_plan(route, *, n_exp, ts, align):
    n_tok = route.shape[0]
    experts = route[:, :2].astype(jnp.int32)
    onehot = (experts[:, :, None] == jnp.arange(n_exp)[None, None, :]).astype(jnp.int32).sum(axis=1)
    before = jnp.cumsum(onehot, axis=0) - onehot
    counts = onehot.sum(axis=0)
    gsz = (counts + ts - 1) // ts * ts
    gend = jnp.cumsum(gsz)
    gstart = gend - gsz
    pos = gstart[experts] + jnp.take_along_axis(before, experts, axis=1)
    n_slots = -(-(2 * n_tok + n_exp * ts) // align) * align
    n_tiles = n_slots // ts
    tile_start = jnp.arange(n_tiles, dtype=jnp.int32) * ts
    n_used = (gend[-1] // ts).astype(jnp.int32)
    tile_expert = jnp.minimum(jnp.searchsorted(gend, tile_start, side="right"), n_exp - 1).astype(jnp.int32)
    order = jnp.argsort(experts.reshape(-1), stable=True).astype(jnp.int32)
    slot_e = jnp.repeat(tile_expert, ts)
    rank = jnp.arange(n_slots, dtype=jnp.int32) - gstart[slot_e]
    cstart = jnp.cumsum(counts) - counts
    src = jnp.clip(cstart[slot_e] + rank, 0, 2 * n_tok - 1)
    tok = jnp.where(rank < counts[slot_e], order[src] // 2, 0).astype(jnp.int32)
    last_used = tile_expert[jnp.maximum(n_used - 1, 0)]
    tile_expert = jnp.where(jnp.arange(n_tiles) < n_used, tile_expert, last_used)
    return tok, pos.T.reshape(-1).astype(jnp.int32), tile_expert, n_used.reshape(1)


def _gffn_kernel(te_ref, nu_ref, h_ref, wg_ref, wu_ref, wd_ref, o_ref, a_ref):
    used = pl.program_id(0) < nu_ref[0]

    @pl.when(used)
    def _():
        h = _unpack_pairs(h_ref[...]).astype(BF16)
        o_ref[...] = _pack_pairs(_swiglu_rows(h, wg_ref.at[0], wu_ref.at[0], wd_ref.at[0], a_ref))

    @pl.when(jnp.logical_not(used))
    def _():
        o_ref[...] = jnp.zeros(o_ref.shape, o_ref.dtype)


def _grouped_ffn(hs, tile_expert, n_used, w_gate, w_up, w_down, *, layer, ts):
    n_slots, dp = hs.shape
    d, f = w_gate.shape[2:]
    once = pl.Buffered(1)
    grid_spec = pltpu.PrefetchScalarGridSpec(
        num_scalar_prefetch=2,
        grid=(n_slots // ts,),
        in_specs=[
            pl.BlockSpec((ts, dp), lambda j, te, nu: (j, 0)),
            pl.BlockSpec((1, 1, d, f), lambda j, te, nu: (layer, te[j], 0, 0), pipeline_mode=once),
            pl.BlockSpec((1, 1, d, f), lambda j, te, nu: (layer, te[j], 0, 0), pipeline_mode=once),
            pl.BlockSpec((1, 1, f, d), lambda j, te, nu: (layer, te[j], 0, 0), pipeline_mode=once),
        ],
        out_specs=pl.BlockSpec((ts, dp), lambda j, te, nu: (j, 0)),
        scratch_shapes=[pltpu.VMEM((ts, f), BF16)],
    )
    return pl.pallas_call(
        _gffn_kernel,
        grid_spec=grid_spec,
        out_shape=jax.ShapeDtypeStruct((n_slots, dp), PACKED),
        compiler_params=_cparams(("arbitrary",)),
        name="moe_grouped_ffn",
    )(tile_expert, n_used, hs, w_gate, w_up, w_down)


def _combine_kernel(y1_ref, y2_ref, r_ref, x_ref, mx_ref, mt_ref, fg_ref, o_ref, *, tm, ctx_len, row0, final):
    pos = (pl.program_id(1) + row0) * tm + lax.broadcasted_iota(jnp.int32, (tm, 1), 0)
    is_ctx = pos < ctx_len
    r = r_ref[0]
    f = r[:, 2:3] * _unpack_pairs(y1_ref[0, 0]) + r[:, 3:4] * _unpack_pairs(y2_ref[0, 0])
    xn = x_ref[0] + _mod(mx_ref, mt_ref, 5, is_ctx) * f
    o_ref[0] = _rms(xn) * fg_ref[...] if final else xn


def _combine(yg, route, s, mods, final_gain, *, ctx_len, tm, final):
    nb, lt, d = s.shape
    row0 = ctx_len // tm if final else 0
    rows = lt - row0 * tm

    def rspec(width):
        return pl.BlockSpec((1, tm, width), lambda b, j: (b, j + row0, 0))

    return pl.pallas_call(
        functools.partial(_combine_kernel, tm=tm, ctx_len=ctx_len, row0=row0, final=final),
        grid=(nb, rows // tm),
        in_specs=[pl.BlockSpec((1, 1, tm, yg.shape[-1]), lambda b, j: (0, b, j + row0, 0)),
                  pl.BlockSpec((1, 1, tm, yg.shape[-1]), lambda b, j: (1, b, j + row0, 0)),
                  rspec(LANES), rspec(d)] + _mod_specs(nb, d) + [_full_spec((1, d))],
        out_specs=_row_spec(tm, d),
        out_shape=jax.ShapeDtypeStruct((nb, rows, d), F32),
        compiler_params=_cparams(("parallel", "parallel")),
        name="moe_combine",
    )(yg, yg, route, s, mods, mods, final_gain.reshape(1, d))


def _moe(h2, route, s, mods, w_gate, w_up, w_down, final_gain, *, layer, ctx_len, n_exp, ts, tm, final):
    nb, lt, d = s.shape
    n_tok = nb * lt
    info = plsc.get_sparse_core_info()
    sc_rows = info.num_cores * info.num_subcores * SC_CHUNK
    tok, pos, tile_expert, n_used = _route_plan(route.reshape(n_tok, LANES), n_exp=n_exp, ts=ts,
                                                align=int(np.lcm(ts, sc_rows)))
    hs = _sc_gather(h2.reshape(n_tok, h2.shape[-1]), tok)
    ys = _grouped_ffn(hs, tile_expert, n_used, w_gate, w_up, w_down, layer=layer, ts=ts)
    yg = _sc_gather(ys, pos).reshape(2, nb, lt, ys.shape[-1])
    return _combine(yg, route, s, mods, final_gain, ctx_len=ctx_len, tm=tm, final=final)


def _final_kernel(x_ref, g_ref, o_ref):
    o_ref[0] = _rms(x_ref[0]) * g_ref[...]


def _final(s, gain, *, ctx_len, tm):
    nb, lt, d = s.shape
    seq = lt - ctx_len
    off = ctx_len // tm
    return pl.pallas_call(
        _final_kernel,
        grid=(nb, seq // tm),
        in_specs=[pl.BlockSpec((1, tm, d), lambda b, j: (b, j + off, 0)), _full_spec((1, d))],
        out_specs=_row_spec(tm, d),
        out_shape=jax.ShapeDtypeStruct((nb, seq, d), F32),
        compiler_params=_cparams(("parallel", "parallel")),
        name="final_norm",
    )(s, gain.reshape(1, d))


def _rope_tables(seq, ctx_len):
    rows = seq // GRID_W
    row = jnp.repeat(jnp.arange(rows), GRID_W).astype(F32)
    col = jnp.tile(jnp.arange(GRID_W), rows).astype(F32)

    def table(rot_dim, reps):
        axis_dim = rot_dim // 2
        inv = ROPE_THETA ** (-jnp.arange(0, axis_dim, 2, dtype=F32) / axis_dim)
        ang = jnp.concatenate([row[:, None] * inv, col[:, None] * inv], axis=-1)
        cos = jnp.concatenate([jnp.ones((ctx_len, axis_dim), F32), jnp.cos(ang)], axis=0)
        sin = jnp.concatenate([jnp.zeros((ctx_len, axis_dim), F32), jnp.sin(ang)], axis=0)
        return (jnp.tile(jnp.concatenate([cos, cos], axis=-1), (1, reps)),
                jnp.tile(jnp.concatenate([-sin, sin], axis=-1), (1, reps)))

    cos_c, sin_c = table(64, 2)
    cos_d, sin_d = table(32, 4)
    return cos_c, sin_c, cos_d, sin_d


def _placement(n_rows, n_cols, pairs):
    m = np.zeros((n_rows, n_cols), np.float32)
    for r, c in pairs:
        m[r, c] = 1.0
    return jnp.asarray(m, BF16)


def _cd_consts(q_gain, k_gain, cq_gain, ckv_gain, w_uq, w_ukv):
    hd, nh, nkv, d_nope, d_rope, d_v = 64, 8, 2, 64, 32, 64
    d_qk = d_nope + d_rope
    heads = np.arange(nh)
    b512 = jnp.asarray(np.kron(np.eye(nh), np.ones((hd, hd))), BF16)
    b128 = jnp.asarray(np.kron(np.eye(nkv), np.ones((hd, hd))), BF16)
    pct = _placement(nh * LANES, nh * hd, [(h * LANES + i, h * hd + i) for h in heads for i in range(hd)])
    nope_cols = (heads[:, None] * d_qk + np.arange(d_nope)[None, :]).reshape(-1)
    rope_cols = (heads[:, None] * d_qk + d_nope + np.arange(d_rope)[None, :]).reshape(-1)
    nope_rows = (heads[:, None] * LANES + np.arange(d_nope)[None, :]).reshape(-1)
    wqnt = jnp.zeros((nh * LANES, w_uq.shape[0]), F32).at[nope_rows].set(w_uq[:, nope_cols].T).astype(BF16)
    wqr = w_uq[:, rope_cols].astype(BF16)
    prt = _placement(nh * LANES, nh * d_rope,
                     [(h * LANES + d_nope + i, h * d_rope + i) for h in heads for i in range(d_rope)])
    kn_cols = (heads[:, None] * (d_nope + d_v) + np.arange(d_nope)[None, :]).reshape(-1)
    v_cols = (heads[:, None] * (d_nope + d_v) + d_nope + np.arange(d_v)[None, :]).reshape(-1)
    wkn = jnp.zeros((w_ukv.shape[0], nh * LANES), F32).at[:, nope_rows].set(w_ukv[:, kn_cols]).astype(BF16)
    pkr = _placement(LANES, nh * LANES, [(i, h * LANES + d_nope + i) for h in heads for i in range(d_rope)])
    wvt = w_ukv[:, v_cols].T.astype(BF16)
    pkc = _placement(nkv * hd, nkv * LANES, [(h * hd + i, h * LANES + i) for h in range(nkv) for i in range(hd)])
    eye = jnp.asarray(np.eye(nkv * hd), BF16)
    return [jnp.tile(q_gain, nh).reshape(1, -1), jnp.tile(k_gain, nkv).reshape(1, -1),
            cq_gain.reshape(1, -1), ckv_gain.reshape(1, -1), b512, b128,
            pct, wqnt, wqr, prt, wkn, pkr, wvt, pkc, eye]


def _largest_tile(n, candidates):
    for t in candidates:
        if n % t == 0:
            return t
    raise ValueError(f"no tile in {candidates} divides {n}")


def kernel(x, c, ctx, c_ctx, ada_w, ada_b, ab_w_in, a_v_gain, a_w_s, a_b_s, b_w_pool, b_scale, ab_w_out,
           ffn_w_gate, ffn_w_up, ffn_w_down, cd_w_in, c_q_gain, c_k_gain, d_cq_gain, d_ckv_gain, d_w_uq,
           d_w_ukv, cd_w_out, moe_router, moe_w_gate, moe_w_up, moe_w_down, final_gain):
    nb, seq, d = x.shape
    ctx_len = ctx.shape[1]
    lt = ctx_len + seq
    depth = ada_w.shape[0]
    n_exp = moe_router.shape[-1]
    chunk = a_w_s.shape[-1]
    tr = 256
    assert ctx_len % tr == 0 and seq % tr == 0 and seq % GRID_W == 0
    tm_ff = _largest_tile(lt, (768, 384, 256))
    assert ffn_w_gate.shape[-1] % FF_CHUNK == 0 and moe_w_gate.shape[-1] % FF_CHUNK == 0

    s = jnp.concatenate([ctx, x], axis=1)

    pad = -(nb + 1) % 8
    cvec = jnp.concatenate([c, c_ctx[None, :], jnp.zeros((pad, d), F32)], axis=0)
    mods_all = _ada_all(cvec, ada_w, ada_b).reshape(depth, nb + 1 + pad, 6, d)
    tabs = _rope_tables(seq, ctx_len)
    moe_wg, moe_wu, moe_wd = (w.astype(BF16) for w in (moe_w_gate, moe_w_up, moe_w_down))

    for layer in range(depth):
        i = layer // 2
        mods = mods_all[layer]
        if layer % 2 == 0:
            bs_full = jnp.repeat(a_b_s[i].T, LANES, axis=1)
            s, h2 = _mixer(s, mods, ab_w_in[i].astype(BF16), a_v_gain[i].reshape(1, -1),
                           a_w_s[i].astype(BF16), bs_full, b_w_pool[i].astype(BF16), b_scale[i].reshape(1, -1),
                           ab_w_out[i].astype(BF16), ctx_len=ctx_len, tm=tr, chunk=chunk)
            s = _swiglu(h2, s, mods, ffn_w_gate[i][None].astype(BF16), ffn_w_up[i][None].astype(BF16),
                        ffn_w_down[i][None].astype(BF16), ctx_len=ctx_len, tm=tm_ff)
        else:
            w_in = jnp.pad(cd_w_in[i], ((0, 0), (0, 12 * LANES - cd_w_in.shape[-1]))).astype(BF16)
            consts = _cd_consts(c_q_gain[i], c_k_gain[i], d_cq_gain[i], d_ckv_gain[i], d_w_uq[i], d_w_ukv[i])
            qtc, qtd, kc, kd, vtc, vtd = _qkv(s, mods, w_in, tabs, consts, ctx_len=ctx_len, tm=tr)
            oc = _attention(qtc, kc, vtc, kv_slots=(0, 0, 0, 0, 1, 1, 1, 1), ctx_len=ctx_len, tq=tr,
                            name="attn_gqa")
            od = _attention(qtd, kd, vtd, kv_slots=tuple(range(8)), ctx_len=ctx_len, tq=tr, name="attn_mla")
            router = jnp.pad(moe_router[i], ((0, 0), (0, LANES - n_exp)))
            r_hi = router.astype(BF16)
            router = jnp.stack([r_hi, (router - r_hi.astype(F32)).astype(BF16)])
            s, h2, route = _oproj(oc, od, s, mods, cd_w_out[i].astype(BF16), router,
                                  ctx_len=ctx_len, tm=tr, n_exp=n_exp)
            s = _moe(h2, route, s, mods, moe_wg, moe_wu, moe_wd, final_gain, layer=i, ctx_len=ctx_len,
                     n_exp=n_exp, ts=512, tm=tr, final=layer == depth - 1)

    return s if depth % 2 == 0 else _final(s, final_gain, ctx_len=ctx_len, tm=tr)
```

```python
import functools

import numpy as np
import jax
import jax.numpy as jnp
from jax import lax
from jax.experimental import pallas as pl
from jax.experimental.pallas import tpu as pltpu
from jax.experimental.pallas import tpu_sc as plsc

F32 = jnp.float32
BF16 = jnp.bfloat16
EPS = 1e-6
ROPE_THETA = 10000.0
GRID_W = 64
LANES = 128
HALO = 8
POOL_WINDOWS = (2, 4, 8, 16)
NEG = -1e30
LOG2E = 1.4426950408889634
ATTN_UNROLL = 8
FF_CHUNK = 256
DEN_ROWS = 16
SC_CHUNK = 64
VMEM_LIMIT = 56 * 1024 * 1024


def _cparams(sem, flags=None):
    return pltpu.CompilerParams(dimension_semantics=sem, vmem_limit_bytes=VMEM_LIMIT, flags=flags)


def _rms(x):
    return x * lax.rsqrt(jnp.mean(x * x, axis=-1, keepdims=True) + EPS)


def _is_ctx_rows(tm, ctx_len):
    pos = pl.program_id(1) * tm + lax.broadcasted_iota(jnp.int32, (tm, 1), 0)
    return pos < ctx_len


def _mod(mx_ref, mt_ref, idx, is_ctx):
    return jnp.where(is_ctx, mt_ref[0, idx:idx + 1, :], mx_ref[0, idx:idx + 1, :])


def _dot(a, b):
    return jnp.dot(a, b, preferred_element_type=F32)


def _dot_nt(a, b):
    return lax.dot_general(a, b, (((1,), (1,)), ((), ())), preferred_element_type=F32)


def _silu(x):
    return x * jax.nn.sigmoid(x)


PACKED = jnp.uint32


def _packed_width(d):
    return d // 2


def _pack_pairs(x):
    w = x.shape[1] // 2
    xb = x.astype(BF16).astype(F32)
    hi = pltpu.bitcast(xb[:, :w], jnp.uint32)
    lo = pltpu.bitcast(xb[:, w:], jnp.uint32)
    return hi | (lo >> 16)


def _unpack_pairs(p):
    hi = pltpu.bitcast(p & jnp.uint32(0xFFFF0000), F32)
    lo = pltpu.bitcast(p << 16, F32)
    return jnp.concatenate([hi, lo], axis=1)


def _ada_kernel(c_ref, w_ref, b_ref, o_ref):
    c = c_ref[...]
    o_ref[0] = jnp.dot(_silu(c), w_ref[0], preferred_element_type=F32,
                       precision=lax.Precision.HIGHEST) + b_ref[0]


def _ada_all(cvec, ada_w, ada_b):
    depth, d, n = ada_w.shape
    rows = cvec.shape[0]
    tn = 1536
    return pl.pallas_call(
        _ada_kernel,
        grid=(depth, n // tn),
        in_specs=[
            pl.BlockSpec((rows, d), lambda l, j: (0, 0)),
            pl.BlockSpec((1, d, tn), lambda l, j: (l, 0, j)),
            pl.BlockSpec((1, 1, tn), lambda l, j: (l, 0, j)),
        ],
        out_specs=pl.BlockSpec((1, rows, tn), lambda l, j: (l, 0, j)),
        out_shape=jax.ShapeDtypeStruct((depth, rows, n), F32),
        compiler_params=_cparams(("arbitrary", "arbitrary")),
        name="adaln",
    )(cvec, ada_w, ada_b.reshape(depth, 1, n))


def _row_spec(tm, width):
    return pl.BlockSpec((1, tm, width), lambda b, j, *_: (b, j, 0))


def _mod_specs(nb, d):
    return [pl.BlockSpec((1, 6, d), lambda b, j, *_: (b, 0, 0)),
            pl.BlockSpec((1, 6, d), lambda b, j, *_: (nb, 0, 0))]


def _full_spec(shape):
    nd = len(shape)
    return pl.BlockSpec(shape, lambda *_: (0,) * nd)


def _mixer_kernel(x_ref, xp_ref, xn_ref, mx_ref, mt_ref, win_ref, vg_ref, ws_ref, bs_ref, wp_ref,
                  psc_ref, wo_ref, xo_ref, h2_ref, *, tm, ctx_len, lt, chunk):
    is_ctx = _is_ctx_rows(tm, ctx_len)
    pos0 = pl.program_id(1) * tm
    aw = 4 * LANES
    sc1 = 1.0 + _mod(mx_ref, mt_ref, 1, is_ctx)
    sh1 = _mod(mx_ref, mt_ref, 0, is_ctx)
    z = _dot((_rms(x_ref[0]) * sc1 + sh1).astype(BF16), win_ref[...])
    halo_x = jnp.concatenate([xp_ref[0], xn_ref[0]], axis=0)
    halo_h = _rms(halo_x) * sc1[:2 * HALO] + sh1[:2 * HALO]
    halo_z = _dot(halo_h.astype(BF16), win_ref[:, 2 * aw:])
    za = z[:, :2 * aw]
    gl = 0.5 * za * (1.0 + lax.erf(za * (2.0 ** -0.5)))
    u = gl[:, :aw]
    v = gl[:, aw:]
    vn = (_rms(v) * vg_ref[...]).astype(BF16)
    ya_rows = []
    for c in range(tm // chunk):
        rs = slice(c * chunk, (c + 1) * chunk)
        cols = []
        for g in range(4):
            cs = slice(g * LANES, (g + 1) * LANES)
            sg = _dot(ws_ref[g], vn[rs, cs]) + bs_ref[:, cs]
            cols.append(u[rs, cs] * sg)
        ya_rows.append(jnp.concatenate(cols, axis=1))
    ya = jnp.concatenate(ya_rows, axis=0)

    in_ctx = pos0 < ctx_len
    seq_start = jnp.where(in_ctx, 0, ctx_len)
    seq_len = jnp.where(in_ctx, ctx_len, lt - ctx_len)
    p_in = pos0 - seq_start + lax.broadcasted_iota(jnp.int32, (tm, 1), 0)
    has_prev = jnp.logical_and(pos0 != 0, pos0 != ctx_len).astype(F32)
    has_next = jnp.logical_and(pos0 + tm != ctx_len, pos0 + tm != lt).astype(F32)
    zp = z[:, 2 * aw:]
    ext = jnp.concatenate([halo_z[:HALO] * has_prev, zp, halo_z[HALO:] * has_next], axis=0)
    n = tm + 2 * HALO
    yb_cols = []
    for g, w in enumerate(POOL_WINDOWS):
        cs = slice(g * LANES, (g + 1) * LANES)
        a = ext[:, cs]
        step = 1
        while step < w:
            a = a + pltpu.roll(a, n - step, 0)
            step *= 2
        shift = HALO - w // 2
        if shift:
            a = pltpu.roll(a, n - shift, 0)
        wsum = a[:tm]
        lo = jnp.clip(p_in - w // 2, 0, seq_len)
        hi = jnp.clip(p_in - w // 2 + w, 0, seq_len)
        pooled = wsum / (hi - lo).astype(F32) - zp[:, cs]
        yb_cols.append(_dot(pooled.astype(BF16), wp_ref[g]) * psc_ref[:, cs])
    y = jnp.concatenate([ya] + yb_cols, axis=1).astype(BF16)
    xn = x_ref[0] + _mod(mx_ref, mt_ref, 2, is_ctx) * _dot(y, wo_ref[...])
    xo_ref[0] = xn
    h2 = _rms(xn) * (1.0 + _mod(mx_ref, mt_ref, 4, is_ctx)) + _mod(mx_ref, mt_ref, 3, is_ctx)
    h2_ref[0] = h2.astype(BF16)


def _mixer(s, mods, w_in, v_gain, w_s, bs_full, w_pool, p_scale, w_out, *, ctx_len, tm, chunk):
    nb, lt, d = s.shape
    aw = 4 * LANES
    nh = lt // HALO
    th = tm // HALO
    pw = 4 * LANES
    halo_prev = pl.BlockSpec((1, HALO, d), lambda b, j: (b, jnp.maximum(j * th - 1, 0), 0))
    halo_next = pl.BlockSpec((1, HALO, d), lambda b, j: (b, jnp.minimum((j + 1) * th, nh - 1), 0))
    return pl.pallas_call(
        functools.partial(_mixer_kernel, tm=tm, ctx_len=ctx_len, lt=lt, chunk=chunk),
        grid=(nb, lt // tm),
        in_specs=[_row_spec(tm, d), halo_prev, halo_next] + _mod_specs(nb, d) + [
            _full_spec(w_in.shape), _full_spec((1, aw)), _full_spec(w_s.shape), _full_spec(bs_full.shape),
            _full_spec(w_pool.shape), _full_spec((1, pw)), _full_spec(w_out.shape)],
        out_specs=[_row_spec(tm, d), _row_spec(tm, d)],
        out_shape=[jax.ShapeDtypeStruct((nb, lt, d), F32), jax.ShapeDtypeStruct((nb, lt, d), BF16)],
        compiler_params=_cparams(("parallel", "parallel")),
        name="ab_mixer",
    )(s, s, s, mods, mods, w_in, v_gain, w_s, bs_full, w_pool, p_scale, w_out)


def _swiglu_rows(h, wg_ref, wu_ref, wd_ref, a_ref):
    for c in range(a_ref.shape[1] // FF_CHUNK):
        cs = slice(c * FF_CHUNK, (c + 1) * FF_CHUNK)
        g = _dot(h, wg_ref[0, :, cs].astype(BF16))
        a_ref[:, cs] = (_silu(g) * _dot(h, wu_ref[0, :, cs].astype(BF16))).astype(BF16)
    return _dot(a_ref[...], wd_ref[0])


def _swiglu_kernel(h_ref, x_ref, mx_ref, mt_ref, wg_ref, wu_ref, wd_ref, o_ref, a_ref, *, tm, ctx_len):
    is_ctx = _is_ctx_rows(tm, ctx_len)
    f = _swiglu_rows(h_ref[0], wg_ref, wu_ref, wd_ref, a_ref)
    o_ref[0] = x_ref[0] + _mod(mx_ref, mt_ref, 5, is_ctx) * f


def _swiglu(h2, s, mods, w_gate, w_up, w_down, *, ctx_len, tm):
    nb, lt, d = s.shape
    f = w_gate.shape[2]
    once = pl.Buffered(1)
    return pl.pallas_call(
        functools.partial(_swiglu_kernel, tm=tm, ctx_len=ctx_len),
        grid=(nb, lt // tm),
        in_specs=[_row_spec(tm, d), _row_spec(tm, d)] + _mod_specs(nb, d) + [
            pl.BlockSpec((1, d, f), lambda b, j: (0, 0, 0), pipeline_mode=once),
            pl.BlockSpec((1, d, f), lambda b, j: (0, 0, 0), pipeline_mode=once),
            pl.BlockSpec((1, f, d), lambda b, j: (0, 0, 0), pipeline_mode=once)],
        out_specs=_row_spec(tm, d),
        out_shape=jax.ShapeDtypeStruct((nb, lt, d), F32),
        scratch_shapes=[pltpu.VMEM((tm, f), BF16)],
        compiler_params=_cparams(("parallel", "parallel")),
        name="swiglu",
    )(h2, s, mods, mods, w_gate, w_up, w_down)


def _seg_sum(x2, b_ref):
    return _dot(x2.astype(BF16), b_ref[...])


def _rope_slabs(x, cos, sin, half):
    lane = lax.broadcasted_iota(jnp.int32, (x.shape[0], LANES), 1)
    first = (lane % (2 * half)) < half
    out = []
    for s in range(x.shape[1] // LANES):
        xs = x[:, s * LANES:(s + 1) * LANES]
        swapped = jnp.where(first, pltpu.roll(xs, LANES - half, 1), pltpu.roll(xs, half, 1))
        out.append(xs * cos + swapped * sin)
    return out[0] if len(out) == 1 else jnp.concatenate(out, axis=1)


def _qkv_kernel(x_ref, mx_ref, mt_ref, win_ref, cc_ref, sc_ref, cd_ref, sd_ref,
                qg_ref, kg_ref, cqg_ref, ckvg_ref, b512_ref, b128_ref,
                pct_ref, wqnt_ref, wqr_ref, prt_ref, wkn_ref, pkr_ref, wvt_ref, pkc_ref, eye_ref,
                qtc_ref, qtd_ref, kc_ref, kd_ref, vtc_ref, vtd_ref,
                *, tm, ctx_len, head_dim, d_qk):
    is_ctx = _is_ctx_rows(tm, ctx_len)
    h = _rms(x_ref[0]) * (1.0 + _mod(mx_ref, mt_ref, 1, is_ctx)) + _mod(mx_ref, mt_ref, 0, is_ctx)
    z = _dot(h.astype(BF16), win_ref[...])
    cos_c, sin_c, cos_d, sin_d = cc_ref[...], sc_ref[...], cd_ref[...], sd_ref[...]
    o_kc, o_vc, o_cq, o_ckv, o_kr = 512, 640, 768, 1152, 1408

    qc = z[:, :o_kc]
    qc = qc * lax.rsqrt(_seg_sum(qc * qc, b512_ref) * (1.0 / head_dim) + EPS) * qg_ref[...]
    qc = _rope_slabs(qc, cos_c, sin_c, head_dim // 2) * (head_dim ** -0.5 * LOG2E)
    qtc_ref[0] = _dot_nt(pct_ref[...], qc.astype(BF16)).astype(BF16)

    kc = z[:, o_kc:o_vc]
    kc = kc * lax.rsqrt(_seg_sum(kc * kc, b128_ref) * (1.0 / head_dim) + EPS) * kg_ref[...]
    kc = _rope_slabs(kc, cos_c, sin_c, head_dim // 2)
    kc_ref[0, 0] = _dot(kc.astype(BF16), pkc_ref[...]).astype(BF16)
    vtc_ref[0, 0] = _dot_nt(eye_ref[...], z[:, o_vc:o_cq].astype(BF16)).astype(BF16)

    cq = (_rms(z[:, o_cq:o_ckv]) * cqg_ref[...]) * (d_qk ** -0.5 * LOG2E)
    cqb = cq.astype(BF16)
    q_rope = _rope_slabs(_dot(cqb, wqr_ref[...]), cos_d, sin_d, 16)
    qtd = _dot_nt(wqnt_ref[...], cqb) + _dot_nt(prt_ref[...], q_rope.astype(BF16))
    qtd_ref[0] = qtd.astype(BF16)

    ckv = (_rms(z[:, o_ckv:o_kr]) * ckvg_ref[...]).astype(BF16)
    kr = _rope_slabs(z[:, o_kr:], cos_d, sin_d, 16)
    kd_ref[0, 0] = (_dot(ckv, wkn_ref[...]) + _dot(kr.astype(BF16), pkr_ref[...])).astype(BF16)
    vtd_ref[0, 0] = _dot_nt(wvt_ref[...], ckv).astype(BF16)


def _qkv(s, mods, w_in, tabs, consts, *, ctx_len, tm):
    nb, lt, d = s.shape
    nch = lt // tm
    tab_spec = pl.BlockSpec((tm, LANES), lambda b, j: (j, 0))
    out_shapes = [
        jax.ShapeDtypeStruct((nb, 8 * LANES, lt), BF16),
        jax.ShapeDtypeStruct((nb, 8 * LANES, lt), BF16),
        jax.ShapeDtypeStruct((nb, nch, tm, 2 * LANES), BF16),
        jax.ShapeDtypeStruct((nb, nch, tm, 8 * LANES), BF16),
        jax.ShapeDtypeStruct((nb, nch, 2 * 64, tm), BF16),
        jax.ShapeDtypeStruct((nb, nch, 8 * 64, tm), BF16),
    ]
    out_specs = [
        pl.BlockSpec((1, 8 * LANES, tm), lambda b, j: (b, 0, j)),
        pl.BlockSpec((1, 8 * LANES, tm), lambda b, j: (b, 0, j)),
        pl.BlockSpec((1, 1, tm, 2 * LANES), lambda b, j: (b, j, 0, 0)),
        pl.BlockSpec((1, 1, tm, 8 * LANES), lambda b, j: (b, j, 0, 0)),
        pl.BlockSpec((1, 1, 2 * 64, tm), lambda b, j: (b, j, 0, 0)),
        pl.BlockSpec((1, 1, 8 * 64, tm), lambda b, j: (b, j, 0, 0)),
    ]
    return pl.pallas_call(
        functools.partial(_qkv_kernel, tm=tm, ctx_len=ctx_len, head_dim=64, d_qk=96),
        grid=(nb, nch),
        in_specs=[_row_spec(tm, d)] + _mod_specs(nb, d) + [_full_spec(w_in.shape)] + [tab_spec] * 4
        + [_full_spec(c.shape) for c in consts],
        out_specs=out_specs,
        out_shape=out_shapes,
        compiler_params=_cparams(("parallel", "parallel")),
        name="cd_qkv",
    )(s, mods, mods, w_in, *tabs, *consts)


def _attn_kernel(qt_ref, k_ref, vt_ref, o_ref, s0_ref, s1_ref, c0_ref, c1_ref, m_ref, acc_ref,
                 *, kv_slots, tq, n_ctx, n_all, dv):
    heads = len(kv_slots)

    def scores(j, g, kg, dst):
        s_dst, c_dst = dst
        st = _dot(k_ref[0, j, :, kg * LANES:(kg + 1) * LANES], qt_ref[0, g * LANES:(g + 1) * LANES, :])
        s_dst[g] = st
        c_dst[g] = jnp.max(st, axis=0, keepdims=True)

    def consume(j, src, g, kg):
        s_src, c_src = src
        m_old = m_ref[g]
        m_new = jnp.maximum(m_old, c_src[g])
        alpha = jnp.exp2(m_old - m_new)
        p = jnp.exp2((s_src[g] - m_new).astype(BF16))
        m_ref[g] = m_new
        vt = jnp.concatenate([vt_ref[0, j, kg * dv:(kg + 1) * dv, :], jnp.ones((DEN_ROWS, p.shape[0]), BF16)],
                             axis=0)
        acc_ref[g] = alpha * acc_ref[g] + _dot(vt, p)

    def step(j, j_next, cur, nxt):
        for g, kg in enumerate(kv_slots):
            if j_next is not None:
                scores(j_next, g, kg, nxt)
            consume(j, cur, g, kg)

    def run(n):
        bufs = ((s0_ref, c0_ref), (s1_ref, c1_ref))
        m_ref[...] = jnp.full(m_ref.shape, NEG, F32)
        acc_ref[...] = jnp.zeros(acc_ref.shape, F32)
        for g, kg in enumerate(kv_slots):
            scores(0, g, kg, bufs[0])
        n_main = (n - 1) // ATTN_UNROLL * ATTN_UNROLL

        def trip(i, c):
            for k in range(ATTN_UNROLL):
                j = ATTN_UNROLL * i + k
                step(j, j + 1, bufs[k % 2], bufs[(k + 1) % 2])
            return c

        lax.fori_loop(0, n_main // ATTN_UNROLL, trip, 0)
        for j in range(n_main, n):
            step(j, j + 1 if j + 1 < n else None, bufs[j % 2], bufs[(j + 1) % 2])
        ot = jnp.concatenate([acc_ref[g, :dv, :] / acc_ref[g, dv:dv + 1, :] for g in range(heads)],
                             axis=0)
        o_ref[0] = ot.T.astype(o_ref.dtype)

    is_ctx_tile = pl.program_id(1) * tq < n_ctx * k_ref.shape[2]

    @pl.when(is_ctx_tile)
    def _():
        run(n_ctx)

    @pl.when(jnp.logical_not(is_ctx_tile))
    def _():
        run(n_all)


def _attention(qt, k, vt, *, kv_slots, ctx_len, tq, name):
    nb, hq, lt = qt.shape
    heads = len(kv_slots)
    assert hq == heads * LANES
    nch, tk = k.shape[1], k.shape[2]
    dv = vt.shape[2] // (k.shape[3] // LANES)
    once = pl.Buffered(1)
    return pl.pallas_call(
        functools.partial(_attn_kernel, kv_slots=kv_slots, tq=tq, n_ctx=ctx_len // tk, n_all=nch, dv=dv),
        grid=(nb, lt // tq),
        in_specs=[
            pl.BlockSpec((1, hq, tq), lambda b, q: (b, 0, q)),
            pl.BlockSpec((1, nch, tk, k.shape[3]), lambda b, q: (b, 0, 0, 0), pipeline_mode=once),
            pl.BlockSpec((1, nch, vt.shape[2], tk), lambda b, q: (b, 0, 0, 0), pipeline_mode=once),
        ],
        out_specs=pl.BlockSpec((1, tq, heads * dv), lambda b, q: (b, q, 0)),
        out_shape=jax.ShapeDtypeStruct((nb, lt, heads * dv), BF16),
        scratch_shapes=[pltpu.VMEM((heads, tk, tq), F32), pltpu.VMEM((heads, tk, tq), F32),
                        pltpu.VMEM((heads, 1, tq), F32), pltpu.VMEM((heads, 1, tq), F32),
                        pltpu.VMEM((heads, 1, tq), F32), pltpu.VMEM((heads, dv + DEN_ROWS, tq), F32)],
        compiler_params=_cparams(("parallel", "arbitrary")),
        name=name,
    )(qt, k, vt)


def _oproj_kernel(oc_ref, od_ref, x_ref, mx_ref, mt_ref, w_ref, r_ref, xo_ref, h2_ref, g_ref,
                  *, tm, ctx_len, n_exp):
    is_ctx = _is_ctx_rows(tm, ctx_len)
    o = jnp.concatenate([oc_ref[0], od_ref[0]], axis=-1)
    xn = x_ref[0] + _mod(mx_ref, mt_ref, 2, is_ctx) * _dot(o, w_ref[...])
    xo_ref[0] = xn
    h2 = _rms(xn) * (1.0 + _mod(mx_ref, mt_ref, 4, is_ctx)) + _mod(mx_ref, mt_ref, 3, is_ctx)
    h2_ref[0] = _pack_pairs(h2)
    h_hi = h2.astype(BF16)
    h_lo = (h2 - h_hi.astype(F32)).astype(BF16)
    logits = _dot(h_hi, r_ref[0]) + _dot(h_lo, r_ref[0]) + _dot(h_hi, r_ref[1])
    lane = lax.broadcasted_iota(jnp.int32, (tm, LANES), 1).astype(F32)
    lg = jnp.where(lane < n_exp, logits, NEG)
    m1 = jnp.max(lg, axis=-1, keepdims=True)
    i1 = jnp.min(jnp.where(lg == m1, lane, float(LANES)), axis=-1, keepdims=True)
    lg2 = jnp.where(lane == i1, NEG, lg)
    m2 = jnp.max(lg2, axis=-1, keepdims=True)
    i2 = jnp.min(jnp.where(lg2 == m2, lane, float(LANES)), axis=-1, keepdims=True)
    e2 = jnp.exp(m2 - m1)
    den = 1.0 + e2
    g_ref[0] = jnp.where(lane == 0.0, i1, jnp.where(lane == 1.0, i2, jnp.where(lane == 2.0, 1.0 / den, e2 / den)))


def _oproj(oc, od, s, mods, w_out, router, *, ctx_len, tm, n_exp):
    nb, lt, d = s.shape
    return pl.pallas_call(
        functools.partial(_oproj_kernel, tm=tm, ctx_len=ctx_len, n_exp=n_exp),
        grid=(nb, lt // tm),
        in_specs=[_row_spec(tm, oc.shape[2]), _row_spec(tm, od.shape[2]), _row_spec(tm, d)]
        + _mod_specs(nb, d) + [_full_spec(w_out.shape), _full_spec(router.shape)],
        out_specs=[_row_spec(tm, d), _row_spec(tm, _packed_width(d)), _row_spec(tm, LANES)],
        out_shape=[jax.ShapeDtypeStruct((nb, lt, d), F32), jax.ShapeDtypeStruct((nb, lt, _packed_width(d)), PACKED),
                   jax.ShapeDtypeStruct((nb, lt, LANES), F32)],
        compiler_params=_cparams(("parallel", "parallel")),
        name="cd_oproj_router",
    )(oc, od, s, mods, mods, w_out, router)


def _sc_gather(table, idx, *, chunk=SC_CHUNK):
    n_rows, d = idx.shape[0], table.shape[1]
    info = plsc.get_sparse_core_info()
    n_workers = info.num_cores * info.num_subcores
    per_w = n_rows // n_workers
    assert n_rows % (n_workers * chunk) == 0 and chunk % 8 == 0 and chunk <= LANES
    mesh = plsc.VectorSubcoreMesh(core_axis_name="c", subcore_axis_name="s")

    def body(table_hbm, idx_hbm, out_hbm, idx_v, rows_v, sem):
        wid = lax.axis_index("s") * info.num_cores + lax.axis_index("c")
        base = wid * per_w
        pltpu.sync_copy(idx_hbm.at[pl.ds(base, per_w)], idx_v)

        @pl.loop(0, per_w // chunk)
        def _(i):
            off = pl.multiple_of(i * chunk, chunk)
            pltpu.async_copy(table_hbm.at[idx_v.at[pl.ds(off, chunk)]], rows_v, sem).wait()
            pltpu.sync_copy(rows_v, out_hbm.at[pl.ds(base + off, chunk)])

    return pl.kernel(
        body,
        out_type=jax.ShapeDtypeStruct((n_rows, d), table.dtype),
        mesh=mesh,
        scratch_types=[pltpu.VMEM((per_w,), jnp.int32), pltpu.VMEM((chunk, d), table.dtype),
                       pltpu.SemaphoreType.DMA],
        name="sc_row_gather",
    )(table, idx)


def _route_plan(route, *, n_exp, ts, align):
    n_tok = route.shape[0]
    experts = route[:, :2].astype(jnp.int32)
    onehot = (experts[:, :, None] == jnp.arange(n_exp)[None, None, :]).astype(jnp.int32).sum(axis=1)
    blk = LANES if n_tok % LANES == 0 else 1
    oh = onehot.astype(F32).reshape(n_tok // blk, blk, n_exp)
    inside = jnp.einsum("ij,bje->bie", jnp.tril(jnp.ones((blk, blk), F32), -1), oh)
    blk_tot = oh.sum(axis=1)
    before = (inside + (jnp.cumsum(blk_tot, axis=0) - blk_tot)[:, None, :]).reshape(n_tok, n_exp)
    before = before.astype(jnp.int32)
    counts = onehot.sum(axis=0)
    gsz = (counts + ts - 1) // ts * ts
    gend = jnp.cumsum(gsz)
    gstart = gend - gsz
    pos = gstart[experts] + jnp.take_along_axis(before, experts, axis=1)
    n_slots = -(-(2 * n_tok + n_exp * ts) // align) * align
    n_tiles = n_slots // ts
    tile_start = jnp.arange(n_tiles, dtype=jnp.int32) * ts
    n_used = (gend[-1] // ts).astype(jnp.int32)
    tile_expert = jnp.minimum(jnp.searchsorted(gend, tile_start, side="right"), n_exp - 1).astype(jnp.int32)
    order = jnp.argsort(experts.reshape(-1), stable=True).astype(jnp.int32)
    slot_e = jnp.repeat(tile_expert, ts)
    rank = jnp.arange(n_slots, dtype=jnp.int32) - gstart[slot_e]
    cstart = jnp.cumsum(counts) - counts
    src = jnp.clip(cstart[slot_e] + rank, 0, 2 * n_tok - 1)
    tok = jnp.where(rank < counts[slot_e], order[src] // 2, 0).astype(jnp.int32)
    last_used = tile_expert[jnp.maximum(n_used - 1, 0)]
    tile_expert = jnp.where(jnp.arange(n_tiles) < n_used, tile_expert, last_used)
    return tok, pos.T.reshape(-1).astype(jnp.int32), tile_expert, n_used.reshape(1)


def _gffn_kernel(te_ref, nu_ref, h_ref, wg_ref, wu_ref, wd_ref, o_ref, a_ref):
    used = pl.program_id(0) < nu_ref[0]

    @pl.when(used)
    def _():
        h = _unpack_pairs(h_ref[...]).astype(BF16)
        o_ref[...] = _pack_pairs(_swiglu_rows(h, wg_ref.at[0], wu_ref.at[0], wd_ref.at[0], a_ref))

    @pl.when(jnp.logical_not(used))
    def _():
        o_ref[...] = jnp.zeros(o_ref.shape, o_ref.dtype)


def _grouped_ffn(hs, tile_expert, n_used, w_gate, w_up, w_down, *, layer, ts):
    n_slots, dp = hs.shape
    d, f = w_gate.shape[2:]
    once = pl.Buffered(1)
    grid_spec = pltpu.PrefetchScalarGridSpec(
        num_scalar_prefetch=2,
        grid=(n_slots // ts,),
        in_specs=[
            pl.BlockSpec((ts, dp), lambda j, te, nu: (j, 0)),
            pl.BlockSpec((1, 1, d, f), lambda j, te, nu: (layer, te[j], 0, 0), pipeline_mode=once),
            pl.BlockSpec((1, 1, d, f), lambda j, te, nu: (layer, te[j], 0, 0), pipeline_mode=once),
            pl.BlockSpec((1, 1, f, d), lambda j, te, nu: (layer, te[j], 0, 0), pipeline_mode=once),
        ],
        out_specs=pl.BlockSpec((ts, dp), lambda j, te, nu: (j, 0)),
        scratch_shapes=[pltpu.VMEM((ts, f), BF16)],
    )
    return pl.pallas_call(
        _gffn_kernel,
        grid_spec=grid_spec,
        out_shape=jax.ShapeDtypeStruct((n_slots, dp), PACKED),
        compiler_params=_cparams(("arbitrary",)),
        name="moe_grouped_ffn",
    )(tile_expert, n_used, hs, w_gate, w_up, w_down)


def _combine_kernel(y1_ref, y2_ref, r_ref, x_ref, mx_ref, mt_ref, fg_ref, o_ref, *, tm, ctx_len, row0, final):
    pos = (pl.program_id(1) + row0) * tm + lax.broadcasted_iota(jnp.int32, (tm, 1), 0)
    is_ctx = pos < ctx_len
    r = r_ref[0]
    f = r[:, 2:3] * _unpack_pairs(y1_ref[0, 0]) + r[:, 3:4] * _unpack_pairs(y2_ref[0, 0])
    xn = x_ref[0] + _mod(mx_ref, mt_ref, 5, is_ctx) * f
    o_ref[0] = _rms(xn) * fg_ref[...] if final else xn


def _combine(yg, route, s, mods, final_gain, *, ctx_len, tm, final):
    nb, lt, d = s.shape
    row0 = ctx_len // tm if final else 0
    rows = lt - row0 * tm

    def rspec(width):
        return pl.BlockSpec((1, tm, width), lambda b, j: (b, j + row0, 0))

    return pl.pallas_call(
        functools.partial(_combine_kernel, tm=tm, ctx_len=ctx_len, row0=row0, final=final),
        grid=(nb, rows // tm),
        in_specs=[pl.BlockSpec((1, 1, tm, yg.shape[-1]), lambda b, j: (0, b, j + row0, 0)),
                  pl.BlockSpec((1, 1, tm, yg.shape[-1]), lambda b, j: (1, b, j + row0, 0)),
                  rspec(LANES), rspec(d)] + _mod_specs(nb, d) + [_full_spec((1, d))],
        out_specs=_row_spec(tm, d),
        out_shape=jax.ShapeDtypeStruct((nb, rows, d), F32),
        compiler_params=_cparams(("parallel", "parallel")),
        name="moe_combine",
    )(yg, yg, route, s, mods, mods, final_gain.reshape(1, d))


def _moe(h2, route, s, mods, w_gate, w_up, w_down, final_gain, *, layer, ctx_len, n_exp, ts, tm, final):
    nb, lt, d = s.shape
    n_tok = nb * lt
    info = plsc.get_sparse_core_info()
    sc_rows = info.num_cores * info.num_subcores * SC_CHUNK
    tok, pos, tile_expert, n_used = _route_plan(route.reshape(n_tok, LANES), n_exp=n_exp, ts=ts,
                                                align=int(np.lcm(ts, sc_rows)))
    hs = _sc_gather(h2.reshape(n_tok, h2.shape[-1]), tok)
    ys = _grouped_ffn(hs, tile_expert, n_used, w_gate, w_up, w_down, layer=layer, ts=ts)
    yg = _sc_gather(ys, pos).reshape(2, nb, lt, ys.shape[-1])
    return _combine(yg, route, s, mods, final_gain, ctx_len=ctx_len, tm=tm, final=final)


def _final_kernel(x_ref, g_ref, o_ref):
    o_ref[0] = _rms(x_ref[0]) * g_ref[...]


def _final(s, gain, *, ctx_len, tm):
    nb, lt, d = s.shape
    seq = lt - ctx_len
    off = ctx_len // tm
    return pl.pallas_call(
        _final_kernel,
        grid=(nb, seq // tm),
        in_specs=[pl.BlockSpec((1, tm, d), lambda b, j: (b, j + off, 0)), _full_spec((1, d))],
        out_specs=_row_spec(tm, d),
        out_shape=jax.ShapeDtypeStruct((nb, seq, d), F32),
        compiler_params=_cparams(("parallel", "parallel")),
        name="final_norm",
    )(s, gain.reshape(1, d))


def _rope_tables(seq, ctx_len):
    rows = seq // GRID_W
    row = jnp.repeat(jnp.arange(rows), GRID_W).astype(F32)
    col = jnp.tile(jnp.arange(GRID_W), rows).astype(F32)

    def table(rot_dim, reps):
        axis_dim = rot_dim // 2
        inv = ROPE_THETA ** (-jnp.arange(0, axis_dim, 2, dtype=F32) / axis_dim)
        ang = jnp.concatenate([row[:, None] * inv, col[:, None] * inv], axis=-1)
        cos = jnp.concatenate([jnp.ones((ctx_len, axis_dim), F32), jnp.cos(ang)], axis=0)
        sin = jnp.concatenate([jnp.zeros((ctx_len, axis_dim), F32), jnp.sin(ang)], axis=0)
        return (jnp.tile(jnp.concatenate([cos, cos], axis=-1), (1, reps)),
                jnp.tile(jnp.concatenate([-sin, sin], axis=-1), (1, reps)))

    cos_c, sin_c = table(64, 2)
    cos_d, sin_d = table(32, 4)
    return cos_c, sin_c, cos_d, sin_d


def _placement(n_rows, n_cols, pairs):
    m = np.zeros((n_rows, n_cols), np.float32)
    for r, c in pairs:
        m[r, c] = 1.0
    return jnp.asarray(m, BF16)


def _cd_consts(q_gain, k_gain, cq_gain, ckv_gain, w_uq, w_ukv):
    hd, nh, nkv, d_nope, d_rope, d_v = 64, 8, 2, 64, 32, 64
    d_qk = d_nope + d_rope
    heads = np.arange(nh)
    b512 = jnp.asarray(np.kron(np.eye(nh), np.ones((hd, hd))), BF16)
    b128 = jnp.asarray(np.kron(np.eye(nkv), np.ones((hd, hd))), BF16)
    pct = _placement(nh * LANES, nh * hd, [(h * LANES + i, h * hd + i) for h in heads for i in range(hd)])
    nope_cols = (heads[:, None] * d_qk + np.arange(d_nope)[None, :]).reshape(-1)
    rope_cols = (heads[:, None] * d_qk + d_nope + np.arange(d_rope)[None, :]).reshape(-1)
    nope_rows = (heads[:, None] * LANES + np.arange(d_nope)[None, :]).reshape(-1)
    wqnt = jnp.zeros((nh * LANES, w_uq.shape[0]), F32).at[nope_rows].set(w_uq[:, nope_cols].T).astype(BF16)
    wqr = w_uq[:, rope_cols].astype(BF16)
    prt = _placement(nh * LANES, nh * d_rope,
                     [(h * LANES + d_nope + i, h * d_rope + i) for h in heads for i in range(d_rope)])
    kn_cols = (heads[:, None] * (d_nope + d_v) + np.arange(d_nope)[None, :]).reshape(-1)
    v_cols = (heads[:, None] * (d_nope + d_v) + d_nope + np.arange(d_v)[None, :]).reshape(-1)
    wkn = jnp.zeros((w_ukv.shape[0], nh * LANES), F32).at[:, nope_rows].set(w_ukv[:, kn_cols]).astype(BF16)
    pkr = _placement(LANES, nh * LANES, [(i, h * LANES + d_nope + i) for h in heads for i in range(d_rope)])
    wvt = w_ukv[:, v_cols].T.astype(BF16)
    pkc = _placement(nkv * hd, nkv * LANES, [(h * hd + i, h * LANES + i) for h in range(nkv) for i in range(hd)])
    eye = jnp.asarray(np.eye(nkv * hd), BF16)
    return [jnp.tile(q_gain, nh).reshape(1, -1), jnp.tile(k_gain, nkv).reshape(1, -1),
            cq_gain.reshape(1, -1), ckv_gain.reshape(1, -1), b512, b128,
            pct, wqnt, wqr, prt, wkn, pkr, wvt, pkc, eye]


def _largest_tile(n, candidates):
    for t in candidates:
        if n % t == 0:
            return t
    raise ValueError(f"no tile in {candidates} divides {n}")


def kernel(x, c, ctx, c_ctx, ada_w, ada_b, ab_w_in, a_v_gain, a_w_s, a_b_s, b_w_pool, b_scale, ab_w_out,
           ffn_w_gate, ffn_w_up, ffn_w_down, cd_w_in, c_q_gain, c_k_gain, d_cq_gain, d_ckv_gain, d_w_uq,
           d_w_ukv, cd_w_out, moe_router, moe_w_gate, moe_w_up, moe_w_down, final_gain):
    nb, seq, d = x.shape
    ctx_len = ctx.shape[1]
    lt = ctx_len + seq
    depth = ada_w.shape[0]
    n_exp = moe_router.shape[-1]
    chunk = a_w_s.shape[-1]
    tr = 256
    assert ctx_len % tr == 0 and seq % tr == 0 and seq % GRID_W == 0
    tm_ff = _largest_tile(lt, (768, 384, 256))
    assert ffn_w_gate.shape[-1] % FF_CHUNK == 0 and moe_w_gate.shape[-1] % FF_CHUNK == 0

    s = jnp.concatenate([ctx, x], axis=1)

    pad = -(nb + 1) % 8
    cvec = jnp.concatenate([c, c_ctx[None, :], jnp.zeros((pad, d), F32)], axis=0)
    mods_all = _ada_all(cvec, ada_w, ada_b).reshape(depth, nb + 1 + pad, 6, d)
    tabs = _rope_tables(seq, ctx_len)
    moe_wg, moe_wu, moe_wd = moe_w_gate, moe_w_up, moe_w_down.astype(BF16)

    for layer in range(depth):
        i = layer // 2
        mods = mods_all[layer]
        if layer % 2 == 0:
            bs_full = jnp.repeat(a_b_s[i].T, LANES, axis=1)
            s, h2 = _mixer(s, mods, ab_w_in[i].astype(BF16), a_v_gain[i].reshape(1, -1),
                           a_w_s[i].astype(BF16), bs_full, b_w_pool[i].astype(BF16), b_scale[i].reshape(1, -1),
                           ab_w_out[i].astype(BF16), ctx_len=ctx_len, tm=tr, chunk=chunk)
            s = _swiglu(h2, s, mods, ffn_w_gate[i][None].astype(BF16), ffn_w_up[i][None].astype(BF16),
                        ffn_w_down[i][None].astype(BF16), ctx_len=ctx_len, tm=tm_ff)
        else:
            w_in = jnp.pad(cd_w_in[i], ((0, 0), (0, 12 * LANES - cd_w_in.shape[-1]))).astype(BF16)
            consts = _cd_consts(c_q_gain[i], c_k_gain[i], d_cq_gain[i], d_ckv_gain[i], d_w_uq[i], d_w_ukv[i])
            qtc, qtd, kc, kd, vtc, vtd = _qkv(s, mods, w_in, tabs, consts, ctx_len=ctx_len, tm=tr)
            oc = _attention(qtc, kc, vtc, kv_slots=(0, 0, 0, 0, 1, 1, 1, 1), ctx_len=ctx_len, tq=tr,
                            name="attn_gqa")
            od = _attention(qtd, kd, vtd, kv_slots=tuple(range(8)), ctx_len=ctx_len, tq=tr, name="attn_mla")
            router = jnp.pad(moe_router[i], ((0, 0), (0, LANES - n_exp)))
            r_hi = router.astype(BF16)
            router = jnp.stack([r_hi, (router - r_hi.astype(F32)).astype(BF16)])
            s, h2, route = _oproj(oc, od, s, mods, cd_w_out[i].astype(BF16), router,
                                  ctx_len=ctx_len, tm=tr, n_exp=n_exp)
            s = _moe(h2, route, s, mods, moe_wg, moe_wu, moe_wd, final_gain, layer=i, ctx_len=ctx_len,
                     n_exp=n_exp, ts=512, tm=tr, final=layer == depth - 1)

    return s if depth % 2 == 0 else _final(s, final_gain, ctx_len=ctx_len, tm=tr)
```

```python
import functools

import numpy as np
import jax
import jax.numpy as jnp
from jax import lax
from jax.experimental import pallas as pl
from jax.experimental.pallas import tpu as pltpu
from jax.experimental.pallas import tpu_sc as plsc

F32 = jnp.float32
BF16 = jnp.bfloat16
EPS = 1e-6
ROPE_THETA = 10000.0
GRID_W = 64
LANES = 128
HALO = 8
POOL_WINDOWS = (2, 4, 8, 16)
NEG = -1e30
LOG2E = 1.4426950408889634
ATTN_UNROLL = 8
FF_CHUNK = 256
DEN_ROWS = 16
SC_CHUNK = 64
VMEM_LIMIT = 56 * 1024 * 1024


def _cparams(sem, flags=None):
    return pltpu.CompilerParams(dimension_semantics=sem, vmem_limit_bytes=VMEM_LIMIT, flags=flags)


def _rms(x):
    return x * lax.rsqrt(jnp.mean(x * x, axis=-1, keepdims=True) + EPS)


def _is_ctx_rows(tm, ctx_len):
    pos = pl.program_id(1) * tm + lax.broadcasted_iota(jnp.int32, (tm, 1), 0)
    return pos < ctx_len


def _mod(mx_ref, mt_ref, idx, is_ctx):
    return jnp.where(is_ctx, mt_ref[0, idx:idx + 1, :], mx_ref[0, idx:idx + 1, :])


def _dot(a, b):
    return jnp.dot(a, b, preferred_element_type=F32)


def _dot_nt(a, b):
    return lax.dot_general(a, b, (((1,), (1,)), ((), ())), preferred_element_type=F32)


def _silu(x):
    return x * jax.nn.sigmoid(x)


PACKED = jnp.uint32


def _packed_width(d):
    return d // 2


def _pack_pairs(x):
    w = x.shape[1] // 2
    xb = x.astype(BF16).astype(F32)
    hi = pltpu.bitcast(xb[:, :w], jnp.uint32)
    lo = pltpu.bitcast(xb[:, w:], jnp.uint32)
    return hi | (lo >> 16)


def _unpack_pairs(p):
    hi = pltpu.bitcast(p & jnp.uint32(0xFFFF0000), F32)
    lo = pltpu.bitcast(p << 16, F32)
    return jnp.concatenate([hi, lo], axis=1)


def _ada_kernel(c_ref, w_ref, b_ref, o_ref):
    c = c_ref[...]
    o_ref[0] = jnp.dot(_silu(c), w_ref[0], preferred_element_type=F32,
                       precision=lax.Precision.HIGHEST) + b_ref[0]


def _ada_all(cvec, ada_w, ada_b):
    depth, d, n = ada_w.shape
    rows = cvec.shape[0]
    tn = 1536
    return pl.pallas_call(
        _ada_kernel,
        grid=(depth, n // tn),
        in_specs=[
            pl.BlockSpec((rows, d), lambda l, j: (0, 0)),
            pl.BlockSpec((1, d, tn), lambda l, j: (l, 0, j)),
            pl.BlockSpec((1, 1, tn), lambda l, j: (l, 0, j)),
        ],
        out_specs=pl.BlockSpec((1, rows, tn), lambda l, j: (l, 0, j)),
        out_shape=jax.ShapeDtypeStruct((depth, rows, n), F32),
        compiler_params=_cparams(("arbitrary", "arbitrary")),
        name="adaln",
    )(cvec, ada_w, ada_b.reshape(depth, 1, n))


def _row_spec(tm, width):
    return pl.BlockSpec((1, tm, width), lambda b, j, *_: (b, j, 0))


def _mod_specs(nb, d):
    return [pl.BlockSpec((1, 6, d), lambda b, j, *_: (b, 0, 0)),
            pl.BlockSpec((1, 6, d), lambda b, j, *_: (nb, 0, 0))]


def _full_spec(shape):
    nd = len(shape)
    return pl.BlockSpec(shape, lambda *_: (0,) * nd)


def _mixer_kernel(x_ref, xp_ref, xn_ref, mx_ref, mt_ref, win_ref, vg_ref, ws_ref, bs_ref, wp_ref,
                  psc_ref, wo_ref, xo_ref, h2_ref, *, tm, ctx_len, lt, chunk):
    is_ctx = _is_ctx_rows(tm, ctx_len)
    pos0 = pl.program_id(1) * tm
    aw = 4 * LANES
    sc1 = 1.0 + _mod(mx_ref, mt_ref, 1, is_ctx)
    sh1 = _mod(mx_ref, mt_ref, 0, is_ctx)
    z = _dot((_rms(x_ref[0]) * sc1 + sh1).astype(BF16), win_ref[...])
    halo_x = jnp.concatenate([xp_ref[0], xn_ref[0]], axis=0)
    halo_h = _rms(halo_x) * sc1[:2 * HALO] + sh1[:2 * HALO]
    halo_z = _dot(halo_h.astype(BF16), win_ref[:, 2 * aw:])
    za = z[:, :2 * aw]
    gl = 0.5 * za * (1.0 + lax.erf(za * (2.0 ** -0.5)))
    u = gl[:, :aw]
    v = gl[:, aw:]
    vn = (_rms(v) * vg_ref[...]).astype(BF16)
    ya_rows = []
    for c in range(tm // chunk):
        rs = slice(c * chunk, (c + 1) * chunk)
        cols = []
        for g in range(4):
            cs = slice(g * LANES, (g + 1) * LANES)
            sg = _dot(ws_ref[g], vn[rs, cs]) + bs_ref[:, cs]
            cols.append(u[rs, cs] * sg)
        ya_rows.append(jnp.concatenate(cols, axis=1))
    ya = jnp.concatenate(ya_rows, axis=0)

    in_ctx = pos0 < ctx_len
    seq_start = jnp.where(in_ctx, 0, ctx_len)
    seq_len = jnp.where(in_ctx, ctx_len, lt - ctx_len)
    p_in = pos0 - seq_start + lax.broadcasted_iota(jnp.int32, (tm, 1), 0)
    has_prev = jnp.logical_and(pos0 != 0, pos0 != ctx_len).astype(F32)
    has_next = jnp.logical_and(pos0 + tm != ctx_len, pos0 + tm != lt).astype(F32)
    zp = z[:, 2 * aw:]
    ext = jnp.concatenate([halo_z[:HALO] * has_prev, zp, halo_z[HALO:] * has_next], axis=0)
    n = tm + 2 * HALO
    yb_cols = []
    for g, w in enumerate(POOL_WINDOWS):
        cs = slice(g * LANES, (g + 1) * LANES)
        a = ext[:, cs]
        step = 1
        while step < w:
            a = a + pltpu.roll(a, n - step, 0)
            step *= 2
        shift = HALO - w // 2
        if shift:
            a = pltpu.roll(a, n - shift, 0)
        wsum = a[:tm]
        lo = jnp.clip(p_in - w // 2, 0, seq_len)
        hi = jnp.clip(p_in - w // 2 + w, 0, seq_len)
        pooled = wsum / (hi - lo).astype(F32) - zp[:, cs]
        yb_cols.append(_dot(pooled.astype(BF16), wp_ref[g]) * psc_ref[:, cs])
    y = jnp.concatenate([ya] + yb_cols, axis=1).astype(BF16)
    xn = x_ref[0] + _mod(mx_ref, mt_ref, 2, is_ctx) * _dot(y, wo_ref[...])
    xo_ref[0] = xn
    h2 = _rms(xn) * (1.0 + _mod(mx_ref, mt_ref, 4, is_ctx)) + _mod(mx_ref, mt_ref, 3, is_ctx)
    h2_ref[0] = h2.astype(BF16)


def _mixer(s, mods, w_in, v_gain, w_s, bs_full, w_pool, p_scale, w_out, *, ctx_len, tm, chunk):
    nb, lt, d = s.shape
    aw = 4 * LANES
    nh = lt // HALO
    th = tm // HALO
    pw = 4 * LANES
    halo_prev = pl.BlockSpec((1, HALO, d), lambda b, j: (b, jnp.maximum(j * th - 1, 0), 0))
    halo_next = pl.BlockSpec((1, HALO, d), lambda b, j: (b, jnp.minimum((j + 1) * th, nh - 1), 0))
    return pl.pallas_call(
        functools.partial(_mixer_kernel, tm=tm, ctx_len=ctx_len, lt=lt, chunk=chunk),
        grid=(nb, lt // tm),
        in_specs=[_row_spec(tm, d), halo_prev, halo_next] + _mod_specs(nb, d) + [
            _full_spec(w_in.shape), _full_spec((1, aw)), _full_spec(w_s.shape), _full_spec(bs_full.shape),
            _full_spec(w_pool.shape), _full_spec((1, pw)), _full_spec(w_out.shape)],
        out_specs=[_row_spec(tm, d), _row_spec(tm, d)],
        out_shape=[jax.ShapeDtypeStruct((nb, lt, d), F32), jax.ShapeDtypeStruct((nb, lt, d), BF16)],
        compiler_params=_cparams(("parallel", "parallel")),
        name="ab_mixer",
    )(s, s, s, mods, mods, w_in, v_gain, w_s, bs_full, w_pool, p_scale, w_out)


def _swiglu_rows(h, wg_ref, wu_ref, wd_ref, a_ref):
    for c in range(a_ref.shape[1] // FF_CHUNK):
        cs = slice(c * FF_CHUNK, (c + 1) * FF_CHUNK)
        g = _dot(h, wg_ref[0, :, cs].astype(BF16))
        a_ref[:, cs] = (_silu(g) * _dot(h, wu_ref[0, :, cs].astype(BF16))).astype(BF16)
    return _dot(a_ref[...], wd_ref[0])


def _swiglu_kernel(h_ref, x_ref, mx_ref, mt_ref, wg_ref, wu_ref, wd_ref, o_ref, a_ref, *, tm, ctx_len):
    is_ctx = _is_ctx_rows(tm, ctx_len)
    f = _swiglu_rows(h_ref[0], wg_ref, wu_ref, wd_ref, a_ref)
    o_ref[0] = x_ref[0] + _mod(mx_ref, mt_ref, 5, is_ctx) * f


def _swiglu(h2, s, mods, w_gate, w_up, w_down, *, ctx_len, tm):
    nb, lt, d = s.shape
    f = w_gate.shape[2]
    once = pl.Buffered(1)
    return pl.pallas_call(
        functools.partial(_swiglu_kernel, tm=tm, ctx_len=ctx_len),
        grid=(nb, lt // tm),
        in_specs=[_row_spec(tm, d), _row_spec(tm, d)] + _mod_specs(nb, d) + [
            pl.BlockSpec((1, d, f), lambda b, j: (0, 0, 0), pipeline_mode=once),
            pl.BlockSpec((1, d, f), lambda b, j: (0, 0, 0), pipeline_mode=once),
            pl.BlockSpec((1, f, d), lambda b, j: (0, 0, 0), pipeline_mode=once)],
        out_specs=_row_spec(tm, d),
        out_shape=jax.ShapeDtypeStruct((nb, lt, d), F32),
        scratch_shapes=[pltpu.VMEM((tm, f), BF16)],
        compiler_params=_cparams(("parallel", "parallel")),
        name="swiglu",
    )(h2, s, mods, mods, w_gate, w_up, w_down)


def _seg_sum(x2, b_ref):
    return _dot(x2.astype(BF16), b_ref[...])


def _rope_slabs(x, cos, sin, half):
    lane = lax.broadcasted_iota(jnp.int32, (x.shape[0], LANES), 1)
    first = (lane % (2 * half)) < half
    out = []
    for s in range(x.shape[1] // LANES):
        xs = x[:, s * LANES:(s + 1) * LANES]
        swapped = jnp.where(first, pltpu.roll(xs, LANES - half, 1), pltpu.roll(xs, half, 1))
        out.append(xs * cos + swapped * sin)
    return out[0] if len(out) == 1 else jnp.concatenate(out, axis=1)


def _qkv_kernel(x_ref, mx_ref, mt_ref, win_ref, cc_ref, sc_ref, cd_ref, sd_ref,
                qg_ref, kg_ref, cqg_ref, ckvg_ref, b512_ref, b128_ref,
                pct_ref, wqnt_ref, wqr_ref, prt_ref, wkn_ref, pkr_ref, wvt_ref, pkc_ref, eye_ref,
                qtc_ref, qtd_ref, kc_ref, kd_ref, vtc_ref, vtd_ref,
                *, tm, ctx_len, head_dim, d_qk):
    is_ctx = _is_ctx_rows(tm, ctx_len)
    h = _rms(x_ref[0]) * (1.0 + _mod(mx_ref, mt_ref, 1, is_ctx)) + _mod(mx_ref, mt_ref, 0, is_ctx)
    z = _dot(h.astype(BF16), win_ref[...])
    cos_c, sin_c, cos_d, sin_d = cc_ref[...], sc_ref[...], cd_ref[...], sd_ref[...]
    o_kc, o_vc, o_cq, o_ckv, o_kr = 512, 640, 768, 1152, 1408

    qc = z[:, :o_kc]
    qc = qc * lax.rsqrt(_seg_sum(qc * qc, b512_ref) * (1.0 / head_dim) + EPS) * qg_ref[...]
    qc = _rope_slabs(qc, cos_c, sin_c, head_dim // 2) * (head_dim ** -0.5 * LOG2E)
    qtc_ref[0] = _dot_nt(pct_ref[...], qc.astype(BF16)).astype(BF16)

    kc = z[:, o_kc:o_vc]
    kc = kc * lax.rsqrt(_seg_sum(kc * kc, b128_ref) * (1.0 / head_dim) + EPS) * kg_ref[...]
    kc = _rope_slabs(kc, cos_c, sin_c, head_dim // 2)
    kc_ref[0, 0] = _dot(kc.astype(BF16), pkc_ref[...]).astype(BF16)
    vtc_ref[0, 0] = _dot_nt(eye_ref[...], z[:, o_vc:o_cq].astype(BF16)).astype(BF16)

    cq = (_rms(z[:, o_cq:o_ckv]) * cqg_ref[...]) * (d_qk ** -0.5 * LOG2E)
    cqb = cq.astype(BF16)
    q_rope = _rope_slabs(_dot(cqb, wqr_ref[...]), cos_d, sin_d, 16)
    qtd = _dot_nt(wqnt_ref[...], cqb) + _dot_nt(prt_ref[...], q_rope.astype(BF16))
    qtd_ref[0] = qtd.astype(BF16)

    ckv = (_rms(z[:, o_ckv:o_kr]) * ckvg_ref[...]).astype(BF16)
    kr = _rope_slabs(z[:, o_kr:], cos_d, sin_d, 16)
    kd_ref[0, 0] = (_dot(ckv, wkn_ref[...]) + _dot(kr.astype(BF16), pkr_ref[...])).astype(BF16)
    vtd_ref[0, 0] = _dot_nt(wvt_ref[...], ckv).astype(BF16)


def _qkv(s, mods, w_in, tabs, consts, *, ctx_len, tm):
    nb, lt, d = s.shape
    nch = lt // tm
    tab_spec = pl.BlockSpec((tm, LANES), lambda b, j: (j, 0))
    out_shapes = [
        jax.ShapeDtypeStruct((nb, 8 * LANES, lt), BF16),
        jax.ShapeDtypeStruct((nb, 8 * LANES, lt), BF16),
        jax.ShapeDtypeStruct((nb, nch, tm, 2 * LANES), BF16),
        jax.ShapeDtypeStruct((nb, nch, tm, 8 * LANES), BF16),
        jax.ShapeDtypeStruct((nb, nch, 2 * 64, tm), BF16),
        jax.ShapeDtypeStruct((nb, nch, 8 * 64, tm), BF16),
    ]
    out_specs = [
        pl.BlockSpec((1, 8 * LANES, tm), lambda b, j: (b, 0, j)),
        pl.BlockSpec((1, 8 * LANES, tm), lambda b, j: (b, 0, j)),
        pl.BlockSpec((1, 1, tm, 2 * LANES), lambda b, j: (b, j, 0, 0)),
        pl.BlockSpec((1, 1, tm, 8 * LANES), lambda b, j: (b, j, 0, 0)),
        pl.BlockSpec((1, 1, 2 * 64, tm), lambda b, j: (b, j, 0, 0)),
        pl.BlockSpec((1, 1, 8 * 64, tm), lambda b, j: (b, j, 0, 0)),
    ]
    return pl.pallas_call(
        functools.partial(_qkv_kernel, tm=tm, ctx_len=ctx_len, head_dim=64, d_qk=96),
        grid=(nb, nch),
        in_specs=[_row_spec(tm, d)] + _mod_specs(nb, d) + [_full_spec(w_in.shape)] + [tab_spec] * 4
        + [_full_spec(c.shape) for c in consts],
        out_specs=out_specs,
        out_shape=out_shapes,
        compiler_params=_cparams(("parallel", "parallel")),
        name="cd_qkv",
    )(s, mods, mods, w_in, *tabs, *consts)


def _attn_kernel(qt_ref, k_ref, vt_ref, o_ref, s0_ref, s1_ref, c0_ref, c1_ref, m_ref, acc_ref,
                 *, kv_slots, tq, n_ctx, n_all, dv):
    heads = len(kv_slots)

    def scores(j, g, kg, dst):
        s_dst, c_dst = dst
        st = _dot(k_ref[0, j, :, kg * LANES:(kg + 1) * LANES], qt_ref[0, g * LANES:(g + 1) * LANES, :])
        s_dst[g] = st
        c_dst[g] = jnp.max(st, axis=0, keepdims=True)

    def consume(j, src, g, kg):
        s_src, c_src = src
        m_old = m_ref[g]
        m_new = jnp.maximum(m_old, c_src[g])
        alpha = jnp.exp2(m_old - m_new)
        p = jnp.exp2((s_src[g] - m_new).astype(BF16))
        m_ref[g] = m_new
        vt = jnp.concatenate([vt_ref[0, j, kg * dv:(kg + 1) * dv, :], jnp.ones((DEN_ROWS, p.shape[0]), BF16)],
                             axis=0)
        acc_ref[g] = alpha * acc_ref[g] + _dot(vt, p)

    def step(j, j_next, cur, nxt):
        for g, kg in enumerate(kv_slots):
            if j_next is not None:
                scores(j_next, g, kg, nxt)
            consume(j, cur, g, kg)

    def run(n):
        bufs = ((s0_ref, c0_ref), (s1_ref, c1_ref))
        m_ref[...] = jnp.full(m_ref.shape, NEG, F32)
        acc_ref[...] = jnp.zeros(acc_ref.shape, F32)
        for g, kg in enumerate(kv_slots):
            scores(0, g, kg, bufs[0])
        n_main = (n - 1) // ATTN_UNROLL * ATTN_UNROLL

        def trip(i, c):
            for k in range(ATTN_UNROLL):
                j = ATTN_UNROLL * i + k
                step(j, j + 1, bufs[k % 2], bufs[(k + 1) % 2])
            return c

        lax.fori_loop(0, n_main // ATTN_UNROLL, trip, 0)
        for j in range(n_main, n):
            step(j, j + 1 if j + 1 < n else None, bufs[j % 2], bufs[(j + 1) % 2])
        ot = jnp.concatenate([acc_ref[g, :dv, :] / acc_ref[g, dv:dv + 1, :] for g in range(heads)],
                             axis=0)
        o_ref[0] = ot.T.astype(o_ref.dtype)

    is_ctx_tile = pl.program_id(1) * tq < n_ctx * k_ref.shape[2]

    @pl.when(is_ctx_tile)
    def _():
        run(n_ctx)

    @pl.when(jnp.logical_not(is_ctx_tile))
    def _():
        run(n_all)


def _attention(qt, k, vt, *, kv_slots, ctx_len, tq, name):
    nb, hq, lt = qt.shape
    heads = len(kv_slots)
    assert hq == heads * LANES
    nch, tk = k.shape[1], k.shape[2]
    dv = vt.shape[2] // (k.shape[3] // LANES)
    once = pl.Buffered(1)
    return pl.pallas_call(
        functools.partial(_attn_kernel, kv_slots=kv_slots, tq=tq, n_ctx=ctx_len // tk, n_all=nch, dv=dv),
        grid=(nb, lt // tq),
        in_specs=[
            pl.BlockSpec((1, hq, tq), lambda b, q: (b, 0, q)),
            pl.BlockSpec((1, nch, tk, k.shape[3]), lambda b, q: (b, 0, 0, 0), pipeline_mode=once),
            pl.BlockSpec((1, nch, vt.shape[2], tk), lambda b, q: (b, 0, 0, 0), pipeline_mode=once),
        ],
        out_specs=pl.BlockSpec((1, tq, heads * dv), lambda b, q: (b, q, 0)),
        out_shape=jax.ShapeDtypeStruct((nb, lt, heads * dv), BF16),
        scratch_shapes=[pltpu.VMEM((heads, tk, tq), F32), pltpu.VMEM((heads, tk, tq), F32),
                        pltpu.VMEM((heads, 1, tq), F32), pltpu.VMEM((heads, 1, tq), F32),
                        pltpu.VMEM((heads, 1, tq), F32), pltpu.VMEM((heads, dv + DEN_ROWS, tq), F32)],
        compiler_params=_cparams(("parallel", "arbitrary")),
        name=name,
    )(qt, k, vt)


def _oproj_kernel(oc_ref, od_ref, x_ref, mx_ref, mt_ref, w_ref, r_ref, xo_ref, h2_ref, g_ref,
                  *, tm, ctx_len, n_exp):
    is_ctx = _is_ctx_rows(tm, ctx_len)
    o = jnp.concatenate([oc_ref[0], od_ref[0]], axis=-1)
    xn = x_ref[0] + _mod(mx_ref, mt_ref, 2, is_ctx) * _dot(o, w_ref[...])
    xo_ref[0] = xn
    h2 = _rms(xn) * (1.0 + _mod(mx_ref, mt_ref, 4, is_ctx)) + _mod(mx_ref, mt_ref, 3, is_ctx)
    h2_ref[0] = _pack_pairs(h2)
    h_hi = h2.astype(BF16)
    h_lo = (h2 - h_hi.astype(F32)).astype(BF16)
    logits = _dot(h_hi, r_ref[0]) + _dot(h_lo, r_ref[0]) + _dot(h_hi, r_ref[1])
    lane = lax.broadcasted_iota(jnp.int32, (tm, LANES), 1).astype(F32)
    lg = jnp.where(lane < n_exp, logits, NEG)
    m1 = jnp.max(lg, axis=-1, keepdims=True)
    i1 = jnp.min(jnp.where(lg == m1, lane, float(LANES)), axis=-1, keepdims=True)
    lg2 = jnp.where(lane == i1, NEG, lg)
    m2 = jnp.max(lg2, axis=-1, keepdims=True)
    i2 = jnp.min(jnp.where(lg2 == m2, lane, float(LANES)), axis=-1, keepdims=True)
    e2 = jnp.exp(m2 - m1)
    den = 1.0 + e2
    g_ref[0] = jnp.where(lane == 0.0, i1, jnp.where(lane == 1.0, i2, jnp.where(lane == 2.0, 1.0 / den, e2 / den)))


def _oproj(oc, od, s, mods, w_out, router, *, ctx_len, tm, n_exp):
    nb, lt, d = s.shape
    return pl.pallas_call(
        functools.partial(_oproj_kernel, tm=tm, ctx_len=ctx_len, n_exp=n_exp),
        grid=(nb, lt // tm),
        in_specs=[_row_spec(tm, oc.shape[2]), _row_spec(tm, od.shape[2]), _row_spec(tm, d)]
        + _mod_specs(nb, d) + [_full_spec(w_out.shape), _full_spec(router.shape)],
        out_specs=[_row_spec(tm, d), _row_spec(tm, _packed_width(d)), _row_spec(tm, LANES)],
        out_shape=[jax.ShapeDtypeStruct((nb, lt, d), F32), jax.ShapeDtypeStruct((nb, lt, _packed_width(d)), PACKED),
                   jax.ShapeDtypeStruct((nb, lt, LANES), F32)],
        compiler_params=_cparams(("parallel", "parallel")),
        name="cd_oproj_router",
    )(oc, od, s, mods, mods, w_out, router)


def _sc_gather(table, idx, *, chunk=SC_CHUNK):
    n_rows, d = idx.shape[0], table.shape[1]
    info = plsc.get_sparse_core_info()
    n_workers = info.num_cores * info.num_subcores
    per_w = n_rows // n_workers
    assert n_rows % (n_workers * chunk) == 0 and chunk % 8 == 0 and chunk <= LANES
    mesh = plsc.VectorSubcoreMesh(core_axis_name="c", subcore_axis_name="s")

    n_chunks = per_w // chunk

    def body(table_hbm, idx_hbm, out_hbm, idx_v, rows0, rows1, sem0, sem1):
        wid = lax.axis_index("s") * info.num_cores + lax.axis_index("c")
        base = wid * per_w
        pltpu.sync_copy(idx_hbm.at[pl.ds(base, per_w)], idx_v)
        bufs = ((rows0, sem0), (rows1, sem1))

        def gather(i, buf):
            off = pl.multiple_of(i * chunk, chunk)
            return pltpu.make_async_copy(table_hbm.at[idx_v.at[pl.ds(off, chunk)]], buf[0], buf[1])

        def drain(i, buf):
            gather(i, buf).wait()
            pltpu.sync_copy(buf[0], out_hbm.at[pl.ds(base + pl.multiple_of(i * chunk, chunk), chunk)])

        gather(0, bufs[0]).start()

        @pl.loop(0, n_chunks // 2)
        def _(t):
            i = 2 * t
            gather(i + 1, bufs[1]).start()
            drain(i, bufs[0])

            @pl.when(i + 2 < n_chunks)
            def _():
                gather(i + 2, bufs[0]).start()

            drain(i + 1, bufs[1])

        if n_chunks % 2:
            drain(n_chunks - 1, bufs[0])

    return pl.kernel(
        body,
        out_type=jax.ShapeDtypeStruct((n_rows, d), table.dtype),
        mesh=mesh,
        scratch_types=[pltpu.VMEM((per_w,), jnp.int32), pltpu.VMEM((chunk, d), table.dtype),
                       pltpu.VMEM((chunk, d), table.dtype), pltpu.SemaphoreType.DMA, pltpu.SemaphoreType.DMA],
        name="sc_row_gather",
    )(table, idx)


def _route_plan(route, *, n_exp, ts, align):
    n_tok = route.shape[0]
    experts = route[:, :2].astype(jnp.int32)
    onehot = (experts[:, :, None] == jnp.arange(n_exp)[None, None, :]).astype(jnp.int32).sum(axis=1)
    blk = LANES if n_tok % LANES == 0 else 1
    oh = onehot.astype(F32).reshape(n_tok // blk, blk, n_exp)
    inside = jnp.einsum("ij,bje->bie", jnp.tril(jnp.ones((blk, blk), F32), -1), oh)
    blk_tot = oh.sum(axis=1)
    before = (inside + (jnp.cumsum(blk_tot, axis=0) - blk_tot)[:, None, :]).reshape(n_tok, n_exp)
    before = before.astype(jnp.int32)
    counts = onehot.sum(axis=0)
    gsz = (counts + ts - 1) // ts * ts
    gend = jnp.cumsum(gsz)
    gstart = gend - gsz
    pos = gstart[experts] + jnp.take_along_axis(before, experts, axis=1)
    n_slots = -(-(2 * n_tok + n_exp * ts) // align) * align
    n_tiles = n_slots // ts
    tile_start = jnp.arange(n_tiles, dtype=jnp.int32) * ts
    n_used = (gend[-1] // ts).astype(jnp.int32)
    tile_expert = jnp.minimum(jnp.searchsorted(gend, tile_start, side="right"), n_exp - 1).astype(jnp.int32)
    order = jnp.argsort(experts.reshape(-1), stable=True).astype(jnp.int32)
    slot_e = jnp.repeat(tile_expert, ts)
    rank = jnp.arange(n_slots, dtype=jnp.int32) - gstart[slot_e]
    cstart = jnp.cumsum(counts) - counts
    src = jnp.clip(cstart[slot_e] + rank, 0, 2 * n_tok - 1)
    tok = jnp.where(rank < counts[slot_e], order[src] // 2, 0).astype(jnp.int32)
    last_used = tile_expert[jnp.maximum(n_used - 1, 0)]
    tile_expert = jnp.where(jnp.arange(n_tiles) < n_used, tile_expert, last_used)
    return tok, pos.T.reshape(-1).astype(jnp.int32), tile_expert, n_used.reshape(1)


def _gffn_kernel(te_ref, nu_ref, h_ref, wg_ref, wu_ref, wd_ref, o_ref, a_ref):
    used = pl.program_id(0) < nu_ref[0]

    @pl.when(used)
    def _():
        h = _unpack_pairs(h_ref[...]).astype(BF16)
        o_ref[...] = _pack_pairs(_swiglu_rows(h, wg_ref.at[0], wu_ref.at[0], wd_ref.at[0], a_ref))

    @pl.when(jnp.logical_not(used))
    def _():
        o_ref[...] = jnp.zeros(o_ref.shape, o_ref.dtype)


def _grouped_ffn(hs, tile_expert, n_used, w_gate, w_up, w_down, *, layer, ts):
    n_slots, dp = hs.shape
    d, f = w_gate.shape[2:]
    once = pl.Buffered(1)
    grid_spec = pltpu.PrefetchScalarGridSpec(
        num_scalar_prefetch=2,
        grid=(n_slots // ts,),
        in_specs=[
            pl.BlockSpec((ts, dp), lambda j, te, nu: (j, 0)),
            pl.BlockSpec((1, 1, d, f), lambda j, te, nu: (layer, te[j], 0, 0), pipeline_mode=once),
            pl.BlockSpec((1, 1, d, f), lambda j, te, nu: (layer, te[j], 0, 0), pipeline_mode=once),
            pl.BlockSpec((1, 1, f, d), lambda j, te, nu: (layer, te[j], 0, 0), pipeline_mode=once),
        ],
        out_specs=pl.BlockSpec((ts, dp), lambda j, te, nu: (j, 0)),
        scratch_shapes=[pltpu.VMEM((ts, f), BF16)],
    )
    return pl.pallas_call(
        _gffn_kernel,
        grid_spec=grid_spec,
        out_shape=jax.ShapeDtypeStruct((n_slots, dp), PACKED),
        compiler_params=_cparams(("arbitrary",)),
        name="moe_grouped_ffn",
    )(tile_expert, n_used, hs, w_gate, w_up, w_down)


def _combine_kernel(y1_ref, y2_ref, r_ref, x_ref, mx_ref, mt_ref, fg_ref, o_ref, *, tm, ctx_len, row0, final):
    pos = (pl.program_id(1) + row0) * tm + lax.broadcasted_iota(jnp.int32, (tm, 1), 0)
    is_ctx = pos < ctx_len
    r = r_ref[0]
    f = r[:, 2:3] * _unpack_pairs(y1_ref[0, 0]) + r[:, 3:4] * _unpack_pairs(y2_ref[0, 0])
    xn = x_ref[0] + _mod(mx_ref, mt_ref, 5, is_ctx) * f
    o_ref[0] = _rms(xn) * fg_ref[...] if final else xn


def _combine(yg, route, s, mods, final_gain, *, ctx_len, tm, final):
    nb, lt, d = s.shape
    row0 = ctx_len // tm if final else 0
    rows = lt - row0 * tm

    def rspec(width):
        return pl.BlockSpec((1, tm, width), lambda b, j: (b, j + row0, 0))

    return pl.pallas_call(
        functools.partial(_combine_kernel, tm=tm, ctx_len=ctx_len, row0=row0, final=final),
        grid=(nb, rows // tm),
        in_specs=[pl.BlockSpec((1, 1, tm, yg.shape[-1]), lambda b, j: (0, b, j + row0, 0)),
                  pl.BlockSpec((1, 1, tm, yg.shape[-1]), lambda b, j: (1, b, j + row0, 0)),
                  rspec(LANES), rspec(d)] + _mod_specs(nb, d) + [_full_spec((1, d))],
        out_specs=_row_spec(tm, d),
        out_shape=jax.ShapeDtypeStruct((nb, rows, d), F32),
        compiler_params=_cparams(("parallel", "parallel")),
        name="moe_combine",
    )(yg, yg, route, s, mods, mods, final_gain.reshape(1, d))


def _moe(h2, route, s, mods, w_gate, w_up, w_down, final_gain, *, layer, ctx_len, n_exp, ts, tm, final):
    nb, lt, d = s.shape
    n_tok = nb * lt
    info = plsc.get_sparse_core_info()
    sc_rows = info.num_cores * info.num_subcores * SC_CHUNK
    tok, pos, tile_expert, n_used = _route_plan(route.reshape(n_tok, LANES), n_exp=n_exp, ts=ts,
                                                align=int(np.lcm(ts, sc_rows)))
    hs = _sc_gather(h2.reshape(n_tok, h2.shape[-1]), tok)
    ys = _grouped_ffn(hs, tile_expert, n_used, w_gate, w_up, w_down, layer=layer, ts=ts)
    yg = _sc_gather(ys, pos).reshape(2, nb, lt, ys.shape[-1])
    return _combine(yg, route, s, mods, final_gain, ctx_len=ctx_len, tm=tm, final=final)


def _final_kernel(x_ref, g_ref, o_ref):
    o_ref[0] = _rms(x_ref[0]) * g_ref[...]


def _final(s, gain, *, ctx_len, tm):
    nb, lt, d = s.shape
    seq = lt - ctx_len
    off = ctx_len // tm
    return pl.pallas_call(
        _final_kernel,
        grid=(nb, seq // tm),
        in_specs=[pl.BlockSpec((1, tm, d), lambda b, j: (b, j + off, 0)), _full_spec((1, d))],
        out_specs=_row_spec(tm, d),
        out_shape=jax.ShapeDtypeStruct((nb, seq, d), F32),
        compiler_params=_cparams(("parallel", "parallel")),
        name="final_norm",
    )(s, gain.reshape(1, d))


def _rope_tables(seq, ctx_len):
    rows = seq // GRID_W
    row = jnp.repeat(jnp.arange(rows), GRID_W).astype(F32)
    col = jnp.tile(jnp.arange(GRID_W), rows).astype(F32)

    def table(rot_dim, reps):
        axis_dim = rot_dim // 2
        inv = ROPE_THETA ** (-jnp.arange(0, axis_dim, 2, dtype=F32) / axis_dim)
        ang = jnp.concatenate([row[:, None] * inv, col[:, None] * inv], axis=-1)
        cos = jnp.concatenate([jnp.ones((ctx_len, axis_dim), F32), jnp.cos(ang)], axis=0)
        sin = jnp.concatenate([jnp.zeros((ctx_len, axis_dim), F32), jnp.sin(ang)], axis=0)
        return (jnp.tile(jnp.concatenate([cos, cos], axis=-1), (1, reps)),
                jnp.tile(jnp.concatenate([-sin, sin], axis=-1), (1, reps)))

    cos_c, sin_c = table(64, 2)
    cos_d, sin_d = table(32, 4)
    return cos_c, sin_c, cos_d, sin_d


def _placement(n_rows, n_cols, pairs):
    m = np.zeros((n_rows, n_cols), np.float32)
    for r, c in pairs:
        m[r, c] = 1.0
    return jnp.asarray(m, BF16)


def _cd_consts(q_gain, k_gain, cq_gain, ckv_gain, w_uq, w_ukv):
    hd, nh, nkv, d_nope, d_rope, d_v = 64, 8, 2, 64, 32, 64
    d_qk = d_nope + d_rope
    heads = np.arange(nh)
    b512 = jnp.asarray(np.kron(np.eye(nh), np.ones((hd, hd))), BF16)
    b128 = jnp.asarray(np.kron(np.eye(nkv), np.ones((hd, hd))), BF16)
    pct = _placement(nh * LANES, nh * hd, [(h * LANES + i, h * hd + i) for h in heads for i in range(hd)])
    nope_cols = (heads[:, None] * d_qk + np.arange(d_nope)[None, :]).reshape(-1)
    rope_cols = (heads[:, None] * d_qk + d_nope + np.arange(d_rope)[None, :]).reshape(-1)
    nope_rows = (heads[:, None] * LANES + np.arange(d_nope)[None, :]).reshape(-1)
    wqnt = jnp.zeros((nh * LANES, w_uq.shape[0]), F32).at[nope_rows].set(w_uq[:, nope_cols].T).astype(BF16)
    wqr = w_uq[:, rope_cols].astype(BF16)
    prt = _placement(nh * LANES, nh * d_rope,
                     [(h * LANES + d_nope + i, h * d_rope + i) for h in heads for i in range(d_rope)])
    kn_cols = (heads[:, None] * (d_nope + d_v) + np.arange(d_nope)[None, :]).reshape(-1)
    v_cols = (heads[:, None] * (d_nope + d_v) + d_nope + np.arange(d_v)[None, :]).reshape(-1)
    wkn = jnp.zeros((w_ukv.shape[0], nh * LANES), F32).at[:, nope_rows].set(w_ukv[:, kn_cols]).astype(BF16)
    pkr = _placement(LANES, nh * LANES, [(i, h * LANES + d_nope + i) for h in heads for i in range(d_rope)])
    wvt = w_ukv[:, v_cols].T.astype(BF16)
    pkc = _placement(nkv * hd, nkv * LANES, [(h * hd + i, h * LANES + i) for h in range(nkv) for i in range(hd)])
    eye = jnp.asarray(np.eye(nkv * hd), BF16)
    return [jnp.tile(q_gain, nh).reshape(1, -1), jnp.tile(k_gain, nkv).reshape(1, -1),
            cq_gain.reshape(1, -1), ckv_gain.reshape(1, -1), b512, b128,
            pct, wqnt, wqr, prt, wkn, pkr, wvt, pkc, eye]


def _largest_tile(n, candidates):
    for t in candidates:
        if n % t == 0:
            return t
    raise ValueError(f"no tile in {candidates} divides {n}")


def kernel(x, c, ctx, c_ctx, ada_w, ada_b, ab_w_in, a_v_gain, a_w_s, a_b_s, b_w_pool, b_scale, ab_w_out,
           ffn_w_gate, ffn_w_up, ffn_w_down, cd_w_in, c_q_gain, c_k_gain, d_cq_gain, d_ckv_gain, d_w_uq,
           d_w_ukv, cd_w_out, moe_router, moe_w_gate, moe_w_up, moe_w_down, final_gain):
    nb, seq, d = x.shape
    ctx_len = ctx.shape[1]
    lt = ctx_len + seq
    depth = ada_w.shape[0]
    n_exp = moe_router.shape[-1]
    chunk = a_w_s.shape[-1]
    tr = 256
    assert ctx_len % tr == 0 and seq % tr == 0 and seq % GRID_W == 0
    tm_ff = _largest_tile(lt, (768, 384, 256))
    assert ffn_w_gate.shape[-1] % FF_CHUNK == 0 and moe_w_gate.shape[-1] % FF_CHUNK == 0

    s = jnp.concatenate([ctx, x], axis=1)

    pad = -(nb + 1) % 8
    cvec = jnp.concatenate([c, c_ctx[None, :], jnp.zeros((pad, d), F32)], axis=0)
    mods_all = _ada_all(cvec, ada_w, ada_b).reshape(depth, nb + 1 + pad, 6, d)
    tabs = _rope_tables(seq, ctx_len)
    moe_wg, moe_wu, moe_wd = moe_w_gate, moe_w_up, moe_w_down.astype(BF16)

    for layer in range(depth):
        i = layer // 2
        mods = mods_all[layer]
        if layer % 2 == 0:
            bs_full = jnp.repeat(a_b_s[i].T, LANES, axis=1)
            s, h2 = _mixer(s, mods, ab_w_in[i].astype(BF16), a_v_gain[i].reshape(1, -1),
                           a_w_s[i].astype(BF16), bs_full, b_w_pool[i].astype(BF16), b_scale[i].reshape(1, -1),
                           ab_w_out[i].astype(BF16), ctx_len=ctx_len, tm=tr, chunk=chunk)
            s = _swiglu(h2, s, mods, ffn_w_gate[i][None].astype(BF16), ffn_w_up[i][None].astype(BF16),
                        ffn_w_down[i][None].astype(BF16), ctx_len=ctx_len, tm=tm_ff)
        else:
            w_in = jnp.pad(cd_w_in[i], ((0, 0), (0, 12 * LANES - cd_w_in.shape[-1]))).astype(BF16)
            consts = _cd_consts(c_q_gain[i], c_k_gain[i], d_cq_gain[i], d_ckv_gain[i], d_w_uq[i], d_w_ukv[i])
            qtc, qtd, kc, kd, vtc, vtd = _qkv(s, mods, w_in, tabs, consts, ctx_len=ctx_len, tm=tr)
            oc = _attention(qtc, kc, vtc, kv_slots=(0, 0, 0, 0, 1, 1, 1, 1), ctx_len=ctx_len, tq=tr,
                            name="attn_gqa")
            od = _attention(qtd, kd, vtd, kv_slots=tuple(range(8)), ctx_len=ctx_len, tq=tr, name="attn_mla")
            router = jnp.pad(moe_router[i], ((0, 0), (0, LANES - n_exp)))
            r_hi = router.astype(BF16)
            router = jnp.stack([r_hi, (router - r_hi.astype(F32)).astype(BF16)])
            s, h2, route = _oproj(oc, od, s, mods, cd_w_out[i].astype(BF16), router,
                                  ctx_len=ctx_len, tm=tr, n_exp=n_exp)
            s = _moe(h2, route, s, mods, moe_wg, moe_wu, moe_wd, final_gain, layer=i, ctx_len=ctx_len,
                     n_exp=n_exp, ts=512, tm=tr, final=layer == depth - 1)

    return s if depth % 2 == 0 else _final(s, final_gain, ctx_len=ctx_len, tm=tr)
```

```python
import functools

import numpy as np
import jax
import jax.numpy as jnp
from jax import lax
from jax.experimental import pallas as pl
from jax.experimental.pallas import tpu as pltpu
from jax.experimental.pallas import tpu_sc as plsc

F32 = jnp.float32
BF16 = jnp.bfloat16
EPS = 1e-6
ROPE_THETA = 10000.0
GRID_W = 64
LANES = 128
HALO = 8
POOL_WINDOWS = (2, 4, 8, 16)
NEG = -1e30
LOG2E = 1.4426950408889634
ATTN_UNROLL = 16
FF_CHUNK = 256
DEN_ROWS = 16
SC_CHUNK = 64
VMEM_LIMIT = 56 * 1024 * 1024


def _cparams(sem, flags=None):
    return pltpu.CompilerParams(dimension_semantics=sem, vmem_limit_bytes=VMEM_LIMIT, flags=flags)


def _rms(x):
    return x * lax.rsqrt(jnp.mean(x * x, axis=-1, keepdims=True) + EPS)


def _is_ctx_rows(tm, ctx_len):
    pos = pl.program_id(1) * tm + lax.broadcasted_iota(jnp.int32, (tm, 1), 0)
    return pos < ctx_len


def _mod(mx_ref, mt_ref, idx, is_ctx):
    return jnp.where(is_ctx, mt_ref[0, idx:idx + 1, :], mx_ref[0, idx:idx + 1, :])


def _dot(a, b):
    return jnp.dot(a, b, preferred_element_type=F32)


def _dot_nt(a, b):
    return lax.dot_general(a, b, (((1,), (1,)), ((), ())), preferred_element_type=F32)


def _silu(x):
    return x * jax.nn.sigmoid(x)


PACKED = jnp.uint32


def _packed_width(d):
    return d // 2


def _pack_pairs(x):
    w = x.shape[1] // 2
    xb = x.astype(BF16).astype(F32)
    hi = pltpu.bitcast(xb[:, :w], jnp.uint32)
    lo = pltpu.bitcast(xb[:, w:], jnp.uint32)
    return hi | (lo >> 16)


def _unpack_pairs(p):
    hi = pltpu.bitcast(p & jnp.uint32(0xFFFF0000), F32)
    lo = pltpu.bitcast(p << 16, F32)
    return jnp.concatenate([hi, lo], axis=1)


def _ada_kernel(c_ref, w_ref, b_ref, o_ref):
    c = c_ref[...]
    o_ref[0] = jnp.dot(_silu(c), w_ref[0], preferred_element_type=F32,
                       precision=lax.Precision.HIGHEST) + b_ref[0]


def _ada_all(cvec, ada_w, ada_b):
    depth, d, n = ada_w.shape
    rows = cvec.shape[0]
    tn = 1536
    return pl.pallas_call(
        _ada_kernel,
        grid=(depth, n // tn),
        in_specs=[
            pl.BlockSpec((rows, d), lambda l, j: (0, 0)),
            pl.BlockSpec((1, d, tn), lambda l, j: (l, 0, j)),
            pl.BlockSpec((1, 1, tn), lambda l, j: (l, 0, j)),
        ],
        out_specs=pl.BlockSpec((1, rows, tn), lambda l, j: (l, 0, j)),
        out_shape=jax.ShapeDtypeStruct((depth, rows, n), F32),
        compiler_params=_cparams(("arbitrary", "arbitrary")),
        name="adaln",
    )(cvec, ada_w, ada_b.reshape(depth, 1, n))


def _row_spec(tm, width):
    return pl.BlockSpec((1, tm, width), lambda b, j, *_: (b, j, 0))


def _mod_specs(nb, d):
    return [pl.BlockSpec((1, 6, d), lambda b, j, *_: (b, 0, 0)),
            pl.BlockSpec((1, 6, d), lambda b, j, *_: (nb, 0, 0))]


def _full_spec(shape):
    nd = len(shape)
    return pl.BlockSpec(shape, lambda *_: (0,) * nd)


def _mixer_kernel(x_ref, xp_ref, xn_ref, mx_ref, mt_ref, win_ref, vg_ref, ws_ref, bs_ref, wp_ref,
                  psc_ref, wo_ref, xo_ref, h2_ref, *, tm, ctx_len, lt, chunk):
    is_ctx = _is_ctx_rows(tm, ctx_len)
    pos0 = pl.program_id(1) * tm
    aw = 4 * LANES
    sc1 = 1.0 + _mod(mx_ref, mt_ref, 1, is_ctx)
    sh1 = _mod(mx_ref, mt_ref, 0, is_ctx)
    z = _dot((_rms(x_ref[0]) * sc1 + sh1).astype(BF16), win_ref[...])
    halo_x = jnp.concatenate([xp_ref[0], xn_ref[0]], axis=0)
    halo_h = _rms(halo_x) * sc1[:2 * HALO] + sh1[:2 * HALO]
    halo_z = _dot(halo_h.astype(BF16), win_ref[:, 2 * aw:])
    za = z[:, :2 * aw]
    gl = 0.5 * za * (1.0 + lax.erf(za * (2.0 ** -0.5)))
    u = gl[:, :aw]
    v = gl[:, aw:]
    vn = (_rms(v) * vg_ref[...]).astype(BF16)
    ya_rows = []
    for c in range(tm // chunk):
        rs = slice(c * chunk, (c + 1) * chunk)
        cols = []
        for g in range(4):
            cs = slice(g * LANES, (g + 1) * LANES)
            sg = _dot(ws_ref[g], vn[rs, cs]) + bs_ref[:, cs]
            cols.append(u[rs, cs] * sg)
        ya_rows.append(jnp.concatenate(cols, axis=1))
    ya = jnp.concatenate(ya_rows, axis=0)

    in_ctx = pos0 < ctx_len
    seq_start = jnp.where(in_ctx, 0, ctx_len)
    seq_len = jnp.where(in_ctx, ctx_len, lt - ctx_len)
    p_in = pos0 - seq_start + lax.broadcasted_iota(jnp.int32, (tm, 1), 0)
    has_prev = jnp.logical_and(pos0 != 0, pos0 != ctx_len).astype(F32)
    has_next = jnp.logical_and(pos0 + tm != ctx_len, pos0 + tm != lt).astype(F32)
    zp = z[:, 2 * aw:]
    ext = jnp.concatenate([halo_z[:HALO] * has_prev, zp, halo_z[HALO:] * has_next], axis=0)
    n = tm + 2 * HALO
    yb_cols = []
    for g, w in enumerate(POOL_WINDOWS):
        cs = slice(g * LANES, (g + 1) * LANES)
        a = ext[:, cs]
        step = 1
        while step < w:
            a = a + pltpu.roll(a, n - step, 0)
            step *= 2
        shift = HALO - w // 2
        if shift:
            a = pltpu.roll(a, n - shift, 0)
        wsum = a[:tm]
        lo = jnp.clip(p_in - w // 2, 0, seq_len)
        hi = jnp.clip(p_in - w // 2 + w, 0, seq_len)
        pooled = wsum / (hi - lo).astype(F32) - zp[:, cs]
        yb_cols.append(_dot(pooled.astype(BF16), wp_ref[g]) * psc_ref[:, cs])
    y = jnp.concatenate([ya] + yb_cols, axis=1).astype(BF16)
    xn = x_ref[0] + _mod(mx_ref, mt_ref, 2, is_ctx) * _dot(y, wo_ref[...])
    xo_ref[0] = xn
    h2 = _rms(xn) * (1.0 + _mod(mx_ref, mt_ref, 4, is_ctx)) + _mod(mx_ref, mt_ref, 3, is_ctx)
    h2_ref[0] = h2.astype(BF16)


def _mixer(s, mods, w_in, v_gain, w_s, bs_full, w_pool, p_scale, w_out, *, ctx_len, tm, chunk):
    nb, lt, d = s.shape
    aw = 4 * LANES
    nh = lt // HALO
    th = tm // HALO
    pw = 4 * LANES
    halo_prev = pl.BlockSpec((1, HALO, d), lambda b, j: (b, jnp.maximum(j * th - 1, 0), 0))
    halo_next = pl.BlockSpec((1, HALO, d), lambda b, j: (b, jnp.minimum((j + 1) * th, nh - 1), 0))
    return pl.pallas_call(
        functools.partial(_mixer_kernel, tm=tm, ctx_len=ctx_len, lt=lt, chunk=chunk),
        grid=(nb, lt // tm),
        in_specs=[_row_spec(tm, d), halo_prev, halo_next] + _mod_specs(nb, d) + [
            _full_spec(w_in.shape), _full_spec((1, aw)), _full_spec(w_s.shape), _full_spec(bs_full.shape),
            _full_spec(w_pool.shape), _full_spec((1, pw)), _full_spec(w_out.shape)],
        out_specs=[_row_spec(tm, d), _row_spec(tm, d)],
        out_shape=[jax.ShapeDtypeStruct((nb, lt, d), F32), jax.ShapeDtypeStruct((nb, lt, d), BF16)],
        compiler_params=_cparams(("parallel", "parallel")),
        name="ab_mixer",
    )(s, s, s, mods, mods, w_in, v_gain, w_s, bs_full, w_pool, p_scale, w_out)


def _swiglu_rows(h, wg_ref, wu_ref, wd_ref, a_ref):
    for c in range(a_ref.shape[1] // FF_CHUNK):
        cs = slice(c * FF_CHUNK, (c + 1) * FF_CHUNK)
        g = _dot(h, wg_ref[0, :, cs].astype(BF16))
        a_ref[:, cs] = (_silu(g) * _dot(h, wu_ref[0, :, cs].astype(BF16))).astype(BF16)
    return _dot(a_ref[...], wd_ref[0])


def _swiglu_kernel(h_ref, x_ref, mx_ref, mt_ref, wg_ref, wu_ref, wd_ref, o_ref, a_ref, *, tm, ctx_len):
    is_ctx = _is_ctx_rows(tm, ctx_len)
    f = _swiglu_rows(h_ref[0], wg_ref, wu_ref, wd_ref, a_ref)
    o_ref[0] = x_ref[0] + _mod(mx_ref, mt_ref, 5, is_ctx) * f


def _swiglu(h2, s, mods, w_gate, w_up, w_down, *, ctx_len, tm):
    nb, lt, d = s.shape
    f = w_gate.shape[2]
    once = pl.Buffered(1)
    return pl.pallas_call(
        functools.partial(_swiglu_kernel, tm=tm, ctx_len=ctx_len),
        grid=(nb, lt // tm),
        in_specs=[_row_spec(tm, d), _row_spec(tm, d)] + _mod_specs(nb, d) + [
            pl.BlockSpec((1, d, f), lambda b, j: (0, 0, 0), pipeline_mode=once),
            pl.BlockSpec((1, d, f), lambda b, j: (0, 0, 0), pipeline_mode=once),
            pl.BlockSpec((1, f, d), lambda b, j: (0, 0, 0), pipeline_mode=once)],
        out_specs=_row_spec(tm, d),
        out_shape=jax.ShapeDtypeStruct((nb, lt, d), F32),
        scratch_shapes=[pltpu.VMEM((tm, f), BF16)],
        compiler_params=_cparams(("parallel", "parallel")),
        name="swiglu",
    )(h2, s, mods, mods, w_gate, w_up, w_down)


def _seg_sum(x2, b_ref):
    return _dot(x2.astype(BF16), b_ref[...])


def _rope_slabs(x, cos, sin, half):
    lane = lax.broadcasted_iota(jnp.int32, (x.shape[0], LANES), 1)
    first = (lane % (2 * half)) < half
    out = []
    for s in range(x.shape[1] // LANES):
        xs = x[:, s * LANES:(s + 1) * LANES]
        swapped = jnp.where(first, pltpu.roll(xs, LANES - half, 1), pltpu.roll(xs, half, 1))
        out.append(xs * cos + swapped * sin)
    return out[0] if len(out) == 1 else jnp.concatenate(out, axis=1)


def _qkv_kernel(x_ref, mx_ref, mt_ref, win_ref, cc_ref, sc_ref, cd_ref, sd_ref,
                qg_ref, kg_ref, cqg_ref, ckvg_ref, b512_ref, b128_ref,
                pct_ref, wqnt_ref, wqr_ref, prt_ref, wkn_ref, pkr_ref, wvt_ref, pkc_ref, eye_ref,
                qtc_ref, qtd_ref, kc_ref, kd_ref, vtc_ref, vtd_ref,
                *, tm, ctx_len, head_dim, d_qk):
    is_ctx = _is_ctx_rows(tm, ctx_len)
    h = _rms(x_ref[0]) * (1.0 + _mod(mx_ref, mt_ref, 1, is_ctx)) + _mod(mx_ref, mt_ref, 0, is_ctx)
    z = _dot(h.astype(BF16), win_ref[...])
    cos_c, sin_c, cos_d, sin_d = cc_ref[...], sc_ref[...], cd_ref[...], sd_ref[...]
    o_kc, o_vc, o_cq, o_ckv, o_kr = 512, 640, 768, 1152, 1408

    qc = z[:, :o_kc]
    qc = qc * lax.rsqrt(_seg_sum(qc * qc, b512_ref) * (1.0 / head_dim) + EPS) * qg_ref[...]
    qc = _rope_slabs(qc, cos_c, sin_c, head_dim // 2) * (head_dim ** -0.5 * LOG2E)
    qtc_ref[0] = _dot_nt(pct_ref[...], qc.astype(BF16)).astype(BF16)

    kc = z[:, o_kc:o_vc]
    kc = kc * lax.rsqrt(_seg_sum(kc * kc, b128_ref) * (1.0 / head_dim) + EPS) * kg_ref[...]
    kc = _rope_slabs(kc, cos_c, sin_c, head_dim // 2)
    kc_ref[0, 0] = _dot(kc.astype(BF16), pkc_ref[...]).astype(BF16)
    vtc_ref[0, 0] = _dot_nt(eye_ref[...], z[:, o_vc:o_cq].astype(BF16)).astype(BF16)

    cq = (_rms(z[:, o_cq:o_ckv]) * cqg_ref[...]) * (d_qk ** -0.5 * LOG2E)
    cqb = cq.astype(BF16)
    q_rope = _rope_slabs(_dot(cqb, wqr_ref[...]), cos_d, sin_d, 16)
    qtd = _dot_nt(wqnt_ref[...], cqb) + _dot_nt(prt_ref[...], q_rope.astype(BF16))
    qtd_ref[0] = qtd.astype(BF16)

    ckv = (_rms(z[:, o_ckv:o_kr]) * ckvg_ref[...]).astype(BF16)
    kr = _rope_slabs(z[:, o_kr:], cos_d, sin_d, 16)
    kd_ref[0, 0] = (_dot(ckv, wkn_ref[...]) + _dot(kr.astype(BF16), pkr_ref[...])).astype(BF16)
    vtd_ref[0, 0] = _dot_nt(wvt_ref[...], ckv).astype(BF16)


def _qkv(s, mods, w_in, tabs, consts, *, ctx_len, tm):
    nb, lt, d = s.shape
    nch = lt // tm
    tab_spec = pl.BlockSpec((tm, LANES), lambda b, j: (j, 0))
    out_shapes = [
        jax.ShapeDtypeStruct((nb, 8 * LANES, lt), BF16),
        jax.ShapeDtypeStruct((nb, 8 * LANES, lt), BF16),
        jax.ShapeDtypeStruct((nb, nch, tm, 2 * LANES), BF16),
        jax.ShapeDtypeStruct((nb, nch, tm, 8 * LANES), BF16),
        jax.ShapeDtypeStruct((nb, nch, 2 * 64, tm), BF16),
        jax.ShapeDtypeStruct((nb, nch, 8 * 64, tm), BF16),
    ]
    out_specs = [
        pl.BlockSpec((1, 8 * LANES, tm), lambda b, j: (b, 0, j)),
        pl.BlockSpec((1, 8 * LANES, tm), lambda b, j: (b, 0, j)),
        pl.BlockSpec((1, 1, tm, 2 * LANES), lambda b, j: (b, j, 0, 0)),
        pl.BlockSpec((1, 1, tm, 8 * LANES), lambda b, j: (b, j, 0, 0)),
        pl.BlockSpec((1, 1, 2 * 64, tm), lambda b, j: (b, j, 0, 0)),
        pl.BlockSpec((1, 1, 8 * 64, tm), lambda b, j: (b, j, 0, 0)),
    ]
    return pl.pallas_call(
        functools.partial(_qkv_kernel, tm=tm, ctx_len=ctx_len, head_dim=64, d_qk=96),
        grid=(nb, nch),
        in_specs=[_row_spec(tm, d)] + _mod_specs(nb, d) + [_full_spec(w_in.shape)] + [tab_spec] * 4
        + [_full_spec(c.shape) for c in consts],
        out_specs=out_specs,
        out_shape=out_shapes,
        compiler_params=_cparams(("parallel", "parallel")),
        name="cd_qkv",
    )(s, mods, mods, w_in, *tabs, *consts)


def _attn_kernel(qt_ref, k_ref, vt_ref, o_ref, s0_ref, s1_ref, c0_ref, c1_ref, m_ref, acc_ref,
                 *, kv_slots, tq, n_ctx, n_all, dv):
    heads = len(kv_slots)

    def scores(j, g, kg, dst):
        s_dst, c_dst = dst
        st = _dot(k_ref[0, j, :, kg * LANES:(kg + 1) * LANES], qt_ref[0, g * LANES:(g + 1) * LANES, :])
        s_dst[g] = st
        c_dst[g] = jnp.max(st, axis=0, keepdims=True)

    def consume(j, src, g, kg):
        s_src, c_src = src
        m_old = m_ref[g]
        m_new = jnp.maximum(m_old, c_src[g])
        alpha = jnp.exp2(m_old - m_new)
        p = jnp.exp2((s_src[g] - m_new).astype(BF16))
        m_ref[g] = m_new
        vt = jnp.concatenate([vt_ref[0, j, kg * dv:(kg + 1) * dv, :], jnp.ones((DEN_ROWS, p.shape[0]), BF16)],
                             axis=0)
        acc_ref[g] = alpha * acc_ref[g] + _dot(vt, p)

    def step(j, j_next, cur, nxt):
        for g, kg in enumerate(kv_slots):
            if j_next is not None:
                scores(j_next, g, kg, nxt)
            consume(j, cur, g, kg)

    def run(n):
        bufs = ((s0_ref, c0_ref), (s1_ref, c1_ref))
        m_ref[...] = jnp.full(m_ref.shape, NEG, F32)
        acc_ref[...] = jnp.zeros(acc_ref.shape, F32)
        for g, kg in enumerate(kv_slots):
            scores(0, g, kg, bufs[0])
        n_main = (n - 1) // ATTN_UNROLL * ATTN_UNROLL

        def trip(i, c):
            for k in range(ATTN_UNROLL):
                j = ATTN_UNROLL * i + k
                step(j, j + 1, bufs[k % 2], bufs[(k + 1) % 2])
            return c

        lax.fori_loop(0, n_main // ATTN_UNROLL, trip, 0)
        for j in range(n_main, n):
            step(j, j + 1 if j + 1 < n else None, bufs[j % 2], bufs[(j + 1) % 2])
        ot = jnp.concatenate([acc_ref[g, :dv, :] / acc_ref[g, dv:dv + 1, :] for g in range(heads)],
                             axis=0)
        o_ref[0] = ot.T.astype(o_ref.dtype)

    is_ctx_tile = pl.program_id(1) * tq < n_ctx * k_ref.shape[2]

    @pl.when(is_ctx_tile)
    def _():
        run(n_ctx)

    @pl.when(jnp.logical_not(is_ctx_tile))
    def _():
        run(n_all)


def _attention(qt, k, vt, *, kv_slots, ctx_len, tq, name):
    nb, hq, lt = qt.shape
    heads = len(kv_slots)
    assert hq == heads * LANES
    nch, tk = k.shape[1], k.shape[2]
    dv = vt.shape[2] // (k.shape[3] // LANES)
    once = pl.Buffered(1)
    return pl.pallas_call(
        functools.partial(_attn_kernel, kv_slots=kv_slots, tq=tq, n_ctx=ctx_len // tk, n_all=nch, dv=dv),
        grid=(nb, lt // tq),
        in_specs=[
            pl.BlockSpec((1, hq, tq), lambda b, q: (b, 0, q)),
            pl.BlockSpec((1, nch, tk, k.shape[3]), lambda b, q: (b, 0, 0, 0), pipeline_mode=once),
            pl.BlockSpec((1, nch, vt.shape[2], tk), lambda b, q: (b, 0, 0, 0), pipeline_mode=once),
        ],
        out_specs=pl.BlockSpec((1, tq, heads * dv), lambda b, q: (b, q, 0)),
        out_shape=jax.ShapeDtypeStruct((nb, lt, heads * dv), BF16),
        scratch_shapes=[pltpu.VMEM((heads, tk, tq), F32), pltpu.VMEM((heads, tk, tq), F32),
                        pltpu.VMEM((heads, 1, tq), F32), pltpu.VMEM((heads, 1, tq), F32),
                        pltpu.VMEM((heads, 1, tq), F32), pltpu.VMEM((heads, dv + DEN_ROWS, tq), F32)],
        compiler_params=_cparams(("parallel", "arbitrary")),
        name=name,
    )(qt, k, vt)


def _oproj_kernel(oc_ref, od_ref, x_ref, mx_ref, mt_ref, w_ref, r_ref, xo_ref, h2_ref, g_ref,
                  *, tm, ctx_len, n_exp):
    is_ctx = _is_ctx_rows(tm, ctx_len)
    o = jnp.concatenate([oc_ref[0], od_ref[0]], axis=-1)
    xn = x_ref[0] + _mod(mx_ref, mt_ref, 2, is_ctx) * _dot(o, w_ref[...])
    xo_ref[0] = xn
    h2 = _rms(xn) * (1.0 + _mod(mx_ref, mt_ref, 4, is_ctx)) + _mod(mx_ref, mt_ref, 3, is_ctx)
    h2_ref[0] = _pack_pairs(h2)
    h_hi = h2.astype(BF16)
    h_lo = (h2 - h_hi.astype(F32)).astype(BF16)
    logits = _dot(h_hi, r_ref[0]) + _dot(h_lo, r_ref[0]) + _dot(h_hi, r_ref[1])
    lane = lax.broadcasted_iota(jnp.int32, (tm, LANES), 1).astype(F32)
    lg = jnp.where(lane < n_exp, logits, NEG)
    m1 = jnp.max(lg, axis=-1, keepdims=True)
    i1 = jnp.min(jnp.where(lg == m1, lane, float(LANES)), axis=-1, keepdims=True)
    lg2 = jnp.where(lane == i1, NEG, lg)
    m2 = jnp.max(lg2, axis=-1, keepdims=True)
    i2 = jnp.min(jnp.where(lg2 == m2, lane, float(LANES)), axis=-1, keepdims=True)
    e2 = jnp.exp(m2 - m1)
    den = 1.0 + e2
    g_ref[0] = jnp.where(lane == 0.0, i1, jnp.where(lane == 1.0, i2, jnp.where(lane == 2.0, 1.0 / den, e2 / den)))


def _oproj(oc, od, s, mods, w_out, router, *, ctx_len, tm, n_exp):
    nb, lt, d = s.shape
    return pl.pallas_call(
        functools.partial(_oproj_kernel, tm=tm, ctx_len=ctx_len, n_exp=n_exp),
        grid=(nb, lt // tm),
        in_specs=[_row_spec(tm, oc.shape[2]), _row_spec(tm, od.shape[2]), _row_spec(tm, d)]
        + _mod_specs(nb, d) + [_full_spec(w_out.shape), _full_spec(router.shape)],
        out_specs=[_row_spec(tm, d), _row_spec(tm, _packed_width(d)), _row_spec(tm, LANES)],
        out_shape=[jax.ShapeDtypeStruct((nb, lt, d), F32), jax.ShapeDtypeStruct((nb, lt, _packed_width(d)), PACKED),
                   jax.ShapeDtypeStruct((nb, lt, LANES), F32)],
        compiler_params=_cparams(("parallel", "parallel")),
        name="cd_oproj_router",
    )(oc, od, s, mods, mods, w_out, router)


def _sc_gather(table, idx, *, chunk=SC_CHUNK):
    n_rows, d = idx.shape[0], table.shape[1]
    info = plsc.get_sparse_core_info()
    n_workers = info.num_cores * info.num_subcores
    per_w = n_rows // n_workers
    assert n_rows % (n_workers * chunk) == 0 and chunk % 8 == 0 and chunk <= LANES
    mesh = plsc.VectorSubcoreMesh(core_axis_name="c", subcore_axis_name="s")

    n_chunks = per_w // chunk

    def body(table_hbm, idx_hbm, out_hbm, idx_v, rows0, rows1, sem0, sem1):
        wid = lax.axis_index("s") * info.num_cores + lax.axis_index("c")
        base = wid * per_w
        pltpu.sync_copy(idx_hbm.at[pl.ds(base, per_w)], idx_v)
        bufs = ((rows0, sem0), (rows1, sem1))

        def gather(i, buf):
            off = pl.multiple_of(i * chunk, chunk)
            return pltpu.make_async_copy(table_hbm.at[idx_v.at[pl.ds(off, chunk)]], buf[0], buf[1])

        def drain(i, buf):
            gather(i, buf).wait()
            pltpu.sync_copy(buf[0], out_hbm.at[pl.ds(base + pl.multiple_of(i * chunk, chunk), chunk)])

        gather(0, bufs[0]).start()

        @pl.loop(0, n_chunks // 2)
        def _(t):
            i = 2 * t
            gather(i + 1, bufs[1]).start()
            drain(i, bufs[0])

            @pl.when(i + 2 < n_chunks)
            def _():
                gather(i + 2, bufs[0]).start()

            drain(i + 1, bufs[1])

        if n_chunks % 2:
            drain(n_chunks - 1, bufs[0])

    return pl.kernel(
        body,
        out_type=jax.ShapeDtypeStruct((n_rows, d), table.dtype),
        mesh=mesh,
        scratch_types=[pltpu.VMEM((per_w,), jnp.int32), pltpu.VMEM((chunk, d), table.dtype),
                       pltpu.VMEM((chunk, d), table.dtype), pltpu.SemaphoreType.DMA, pltpu.SemaphoreType.DMA],
        name="sc_row_gather",
    )(table, idx)


def _route_plan(route, *, n_exp, ts, align):
    n_tok = route.shape[0]
    experts = route[:, :2].astype(jnp.int32)
    onehot = (experts[:, :, None] == jnp.arange(n_exp)[None, None, :]).astype(jnp.int32).sum(axis=1)
    blk = LANES if n_tok % LANES == 0 else 1
    oh = onehot.astype(F32).reshape(n_tok // blk, blk, n_exp)
    inside = jnp.einsum("ij,bje->bie", jnp.tril(jnp.ones((blk, blk), F32), -1), oh)
    blk_tot = oh.sum(axis=1)
    before = (inside + (jnp.cumsum(blk_tot, axis=0) - blk_tot)[:, None, :]).reshape(n_tok, n_exp)
    before = before.astype(jnp.int32)
    counts = onehot.sum(axis=0)
    gsz = (counts + ts - 1) // ts * ts
    gend = jnp.cumsum(gsz)
    gstart = gend - gsz
    pos = gstart[experts] + jnp.take_along_axis(before, experts, axis=1)
    n_slots = -(-(2 * n_tok + n_exp * ts) // align) * align
    n_tiles = n_slots // ts
    tile_start = jnp.arange(n_tiles, dtype=jnp.int32) * ts
    n_used = (gend[-1] // ts).astype(jnp.int32)
    tile_expert = jnp.minimum(jnp.searchsorted(gend, tile_start, side="right"), n_exp - 1).astype(jnp.int32)
    order = jnp.argsort(experts.reshape(-1), stable=True).astype(jnp.int32)
    slot_e = jnp.repeat(tile_expert, ts)
    rank = jnp.arange(n_slots, dtype=jnp.int32) - gstart[slot_e]
    cstart = jnp.cumsum(counts) - counts
    src = jnp.clip(cstart[slot_e] + rank, 0, 2 * n_tok - 1)
    tok = jnp.where(rank < counts[slot_e], order[src] // 2, 0).astype(jnp.int32)
    last_used = tile_expert[jnp.maximum(n_used - 1, 0)]
    tile_expert = jnp.where(jnp.arange(n_tiles) < n_used, tile_expert, last_used)
    return tok, pos.T.reshape(-1).astype(jnp.int32), tile_expert, n_used.reshape(1)


def _gffn_kernel(te_ref, nu_ref, h_ref, wg_ref, wu_ref, wd_ref, o_ref, a_ref):
    used = pl.program_id(0) < nu_ref[0]

    @pl.when(used)
    def _():
        h = _unpack_pairs(h_ref[...]).astype(BF16)
        o_ref[...] = _pack_pairs(_swiglu_rows(h, wg_ref.at[0], wu_ref.at[0], wd_ref.at[0], a_ref))

    @pl.when(jnp.logical_not(used))
    def _():
        o_ref[...] = jnp.zeros(o_ref.shape, o_ref.dtype)


def _grouped_ffn(hs, tile_expert, n_used, w_gate, w_up, w_down, *, layer, ts):
    n_slots, dp = hs.shape
    d, f = w_gate.shape[2:]
    once = pl.Buffered(1)
    grid_spec = pltpu.PrefetchScalarGridSpec(
        num_scalar_prefetch=2,
        grid=(n_slots // ts,),
        in_specs=[
            pl.BlockSpec((ts, dp), lambda j, te, nu: (j, 0)),
            pl.BlockSpec((1, 1, d, f), lambda j, te, nu: (layer, te[j], 0, 0), pipeline_mode=once),
            pl.BlockSpec((1, 1, d, f), lambda j, te, nu: (layer, te[j], 0, 0), pipeline_mode=once),
            pl.BlockSpec((1, 1, f, d), lambda j, te, nu: (layer, te[j], 0, 0), pipeline_mode=once),
        ],
        out_specs=pl.BlockSpec((ts, dp), lambda j, te, nu: (j, 0)),
        scratch_shapes=[pltpu.VMEM((ts, f), BF16)],
    )
    return pl.pallas_call(
        _gffn_kernel,
        grid_spec=grid_spec,
        out_shape=jax.ShapeDtypeStruct((n_slots, dp), PACKED),
        compiler_params=_cparams(("arbitrary",)),
        name="moe_grouped_ffn",
    )(tile_expert, n_used, hs, w_gate, w_up, w_down)


def _combine_kernel(y1_ref, y2_ref, r_ref, x_ref, mx_ref, mt_ref, fg_ref, o_ref, *, tm, ctx_len, row0, final):
    pos = (pl.program_id(1) + row0) * tm + lax.broadcasted_iota(jnp.int32, (tm, 1), 0)
    is_ctx = pos < ctx_len
    r = r_ref[0]
    f = r[:, 2:3] * _unpack_pairs(y1_ref[0, 0]) + r[:, 3:4] * _unpack_pairs(y2_ref[0, 0])
    xn = x_ref[0] + _mod(mx_ref, mt_ref, 5, is_ctx) * f
    o_ref[0] = _rms(xn) * fg_ref[...] if final else xn


def _combine(yg, route, s, mods, final_gain, *, ctx_len, tm, final):
    nb, lt, d = s.shape
    row0 = ctx_len // tm if final else 0
    rows = lt - row0 * tm

    def rspec(width):
        return pl.BlockSpec((1, tm, width), lambda b, j: (b, j + row0, 0))

    return pl.pallas_call(
        functools.partial(_combine_kernel, tm=tm, ctx_len=ctx_len, row0=row0, final=final),
        grid=(nb, rows // tm),
        in_specs=[pl.BlockSpec((1, 1, tm, yg.shape[-1]), lambda b, j: (0, b, j + row0, 0)),
                  pl.BlockSpec((1, 1, tm, yg.shape[-1]), lambda b, j: (1, b, j + row0, 0)),
                  rspec(LANES), rspec(d)] + _mod_specs(nb, d) + [_full_spec((1, d))],
        out_specs=_row_spec(tm, d),
        out_shape=jax.ShapeDtypeStruct((nb, rows, d), F32),
        compiler_params=_cparams(("parallel", "parallel")),
        name="moe_combine",
    )(yg, yg, route, s, mods, mods, final_gain.reshape(1, d))


def _moe(h2, route, s, mods, w_gate, w_up, w_down, final_gain, *, layer, ctx_len, n_exp, ts, tm, final):
    nb, lt, d = s.shape
    n_tok = nb * lt
    info = plsc.get_sparse_core_info()
    sc_rows = info.num_cores * info.num_subcores * SC_CHUNK
    tok, pos, tile_expert, n_used = _route_plan(route.reshape(n_tok, LANES), n_exp=n_exp, ts=ts,
                                                align=int(np.lcm(ts, sc_rows)))
    hs = _sc_gather(h2.reshape(n_tok, h2.shape[-1]), tok)
    ys = _grouped_ffn(hs, tile_expert, n_used, w_gate, w_up, w_down, layer=layer, ts=ts)
    yg = _sc_gather(ys, pos).reshape(2, nb, lt, ys.shape[-1])
    return _combine(yg, route, s, mods, final_gain, ctx_len=ctx_len, tm=tm, final=final)


def _final_kernel(x_ref, g_ref, o_ref):
    o_ref[0] = _rms(x_ref[0]) * g_ref[...]


def _final(s, gain, *, ctx_len, tm):
    nb, lt, d = s.shape
    seq = lt - ctx_len
    off = ctx_len // tm
    return pl.pallas_call(
        _final_kernel,
        grid=(nb, seq // tm),
        in_specs=[pl.BlockSpec((1, tm, d), lambda b, j: (b, j + off, 0)), _full_spec((1, d))],
        out_specs=_row_spec(tm, d),
        out_shape=jax.ShapeDtypeStruct((nb, seq, d), F32),
        compiler_params=_cparams(("parallel", "parallel")),
        name="final_norm",
    )(s, gain.reshape(1, d))


def _rope_tables(seq, ctx_len):
    rows = seq // GRID_W
    row = jnp.repeat(jnp.arange(rows), GRID_W).astype(F32)
    col = jnp.tile(jnp.arange(GRID_W), rows).astype(F32)

    def table(rot_dim, reps):
        axis_dim = rot_dim // 2
        inv = ROPE_THETA ** (-jnp.arange(0, axis_dim, 2, dtype=F32) / axis_dim)
        ang = jnp.concatenate([row[:, None] * inv, col[:, None] * inv], axis=-1)
        cos = jnp.concatenate([jnp.ones((ctx_len, axis_dim), F32), jnp.cos(ang)], axis=0)
        sin = jnp.concatenate([jnp.zeros((ctx_len, axis_dim), F32), jnp.sin(ang)], axis=0)
        return (jnp.tile(jnp.concatenate([cos, cos], axis=-1), (1, reps)),
                jnp.tile(jnp.concatenate([-sin, sin], axis=-1), (1, reps)))

    cos_c, sin_c = table(64, 2)
    cos_d, sin_d = table(32, 4)
    return cos_c, sin_c, cos_d, sin_d


def _placement(n_rows, n_cols, pairs):
    m = np.zeros((n_rows, n_cols), np.float32)
    for r, c in pairs:
        m[r, c] = 1.0
    return jnp.asarray(m, BF16)


def _cd_consts(q_gain, k_gain, cq_gain, ckv_gain, w_uq, w_ukv):
    hd, nh, nkv, d_nope, d_rope, d_v = 64, 8, 2, 64, 32, 64
    d_qk = d_nope + d_rope
    heads = np.arange(nh)
    b512 = jnp.asarray(np.kron(np.eye(nh), np.ones((hd, hd))), BF16)
    b128 = jnp.asarray(np.kron(np.eye(nkv), np.ones((hd, hd))), BF16)
    pct = _placement(nh * LANES, nh * hd, [(h * LANES + i, h * hd + i) for h in heads for i in range(hd)])
    nope_cols = (heads[:, None] * d_qk + np.arange(d_nope)[None, :]).reshape(-1)
    rope_cols = (heads[:, None] * d_qk + d_nope + np.arange(d_rope)[None, :]).reshape(-1)
    nope_rows = (heads[:, None] * LANES + np.arange(d_nope)[None, :]).reshape(-1)
    wqnt = jnp.zeros((nh * LANES, w_uq.shape[0]), F32).at[nope_rows].set(w_uq[:, nope_cols].T).astype(BF16)
    wqr = w_uq[:, rope_cols].astype(BF16)
    prt = _placement(nh * LANES, nh * d_rope,
                     [(h * LANES + d_nope + i, h * d_rope + i) for h in heads for i in range(d_rope)])
    kn_cols = (heads[:, None] * (d_nope + d_v) + np.arange(d_nope)[None, :]).reshape(-1)
    v_cols = (heads[:, None] * (d_nope + d_v) + d_nope + np.arange(d_v)[None, :]).reshape(-1)
    wkn = jnp.zeros((w_ukv.shape[0], nh * LANES), F32).at[:, nope_rows].set(w_ukv[:, kn_cols]).astype(BF16)
    pkr = _placement(LANES, nh * LANES, [(i, h * LANES + d_nope + i) for h in heads for i in range(d_rope)])
    wvt = w_ukv[:, v_cols].T.astype(BF16)
    pkc = _placement(nkv * hd, nkv * LANES, [(h * hd + i, h * LANES + i) for h in range(nkv) for i in range(hd)])
    eye = jnp.asarray(np.eye(nkv * hd), BF16)
    return [jnp.tile(q_gain, nh).reshape(1, -1), jnp.tile(k_gain, nkv).reshape(1, -1),
            cq_gain.reshape(1, -1), ckv_gain.reshape(1, -1), b512, b128,
            pct, wqnt, wqr, prt, wkn, pkr, wvt, pkc, eye]


def _largest_tile(n, candidates):
    for t in candidates:
        if n % t == 0:
            return t
    raise ValueError(f"no tile in {candidates} divides {n}")


def kernel(x, c, ctx, c_ctx, ada_w, ada_b, ab_w_in, a_v_gain, a_w_s, a_b_s, b_w_pool, b_scale, ab_w_out,
           ffn_w_gate, ffn_w_up, ffn_w_down, cd_w_in, c_q_gain, c_k_gain, d_cq_gain, d_ckv_gain, d_w_uq,
           d_w_ukv, cd_w_out, moe_router, moe_w_gate, moe_w_up, moe_w_down, final_gain):
    nb, seq, d = x.shape
    ctx_len = ctx.shape[1]
    lt = ctx_len + seq
    depth = ada_w.shape[0]
    n_exp = moe_router.shape[-1]
    chunk = a_w_s.shape[-1]
    tr = 256
    assert ctx_len % tr == 0 and seq % tr == 0 and seq % GRID_W == 0
    tm_ff = _largest_tile(lt, (768, 384, 256))
    assert ffn_w_gate.shape[-1] % FF_CHUNK == 0 and moe_w_gate.shape[-1] % FF_CHUNK == 0

    s = jnp.concatenate([ctx, x], axis=1)

    pad = -(nb + 1) % 8
    cvec = jnp.concatenate([c, c_ctx[None, :], jnp.zeros((pad, d), F32)], axis=0)
    mods_all = _ada_all(cvec, ada_w, ada_b).reshape(depth, nb + 1 + pad, 6, d)
    tabs = _rope_tables(seq, ctx_len)
    moe_wg, moe_wu, moe_wd = moe_w_gate, moe_w_up, moe_w_down.astype(BF16)

    for layer in range(depth):
        i = layer // 2
        mods = mods_all[layer]
        if layer % 2 == 0:
            bs_full = jnp.repeat(a_b_s[i].T, LANES, axis=1)
            s, h2 = _mixer(s, mods, ab_w_in[i].astype(BF16), a_v_gain[i].reshape(1, -1),
                           a_w_s[i].astype(BF16), bs_full, b_w_pool[i].astype(BF16), b_scale[i].reshape(1, -1),
                           ab_w_out[i].astype(BF16), ctx_len=ctx_len, tm=tr, chunk=chunk)
            s = _swiglu(h2, s, mods, ffn_w_gate[i][None].astype(BF16), ffn_w_up[i][None].astype(BF16),
                        ffn_w_down[i][None].astype(BF16), ctx_len=ctx_len, tm=tm_ff)
        else:
            w_in = jnp.pad(cd_w_in[i], ((0, 0), (0, 12 * LANES - cd_w_in.shape[-1]))).astype(BF16)
            consts = _cd_consts(c_q_gain[i], c_k_gain[i], d_cq_gain[i], d_ckv_gain[i], d_w_uq[i], d_w_ukv[i])
            qtc, qtd, kc, kd, vtc, vtd = _qkv(s, mods, w_in, tabs, consts, ctx_len=ctx_len, tm=tr)
            oc = _attention(qtc, kc, vtc, kv_slots=(0, 0, 0, 0, 1, 1, 1, 1), ctx_len=ctx_len, tq=tr,
                            name="attn_gqa")
            od = _attention(qtd, kd, vtd, kv_slots=tuple(range(8)), ctx_len=ctx_len, tq=tr, name="attn_mla")
            router = jnp.pad(moe_router[i], ((0, 0), (0, LANES - n_exp)))
            r_hi = router.astype(BF16)
            router = jnp.stack([r_hi, (router - r_hi.astype(F32)).astype(BF16)])
            s, h2, route = _oproj(oc, od, s, mods, cd_w_out[i].astype(BF16), router,
                                  ctx_len=ctx_len, tm=tr, n_exp=n_exp)
            s = _moe(h2, route, s, mods, moe_wg, moe_wu, moe_wd, final_gain, layer=i, ctx_len=ctx_len,
                     n_exp=n_exp, ts=512, tm=tr, final=layer == depth - 1)

    return s if depth % 2 == 0 else _final(s, final_gain, ctx_len=ctx_len, tm=tr)
```

```python
import functools

import numpy as np
import jax
import jax.numpy as jnp
from jax import lax
from jax.experimental import pallas as pl
from jax.experimental.pallas import tpu as pltpu
from jax.experimental.pallas import tpu_sc as plsc

F32 = jnp.float32
BF16 = jnp.bfloat16
EPS = 1e-6
ROPE_THETA = 10000.0
GRID_W = 64
LANES = 128
HALO = 8
POOL_WINDOWS = (2, 4, 8, 16)
NEG = -1e30
LOG2E = 1.4426950408889634
ATTN_UNROLL = 16
FF_CHUNK = 256
DEN_ROWS = 16
SC_CHUNK = 64
VMEM_LIMIT = 56 * 1024 * 1024


def _cparams(sem):
    return pltpu.CompilerParams(dimension_semantics=sem, vmem_limit_bytes=VMEM_LIMIT)


def _rms(x):
    return x * lax.rsqrt(jnp.mean(x * x, axis=-1, keepdims=True) + EPS)


def _is_ctx_rows(tm, ctx_len):
    pos = pl.program_id(1) * tm + lax.broadcasted_iota(jnp.int32, (tm, 1), 0)
    return pos < ctx_len


def _mod(mx_ref, mt_ref, idx, is_ctx):
    return jnp.where(is_ctx, mt_ref[0, idx:idx + 1, :], mx_ref[0, idx:idx + 1, :])


def _dot(a, b):
    return jnp.dot(a, b, preferred_element_type=F32)


def _dot_nt(a, b):
    return lax.dot_general(a, b, (((1,), (1,)), ((), ())), preferred_element_type=F32)


def _silu(x):
    return x * jax.nn.sigmoid(x)


PACKED = jnp.uint32


def _packed_width(d):
    return d // 2


def _pack_pairs(x):
    w = x.shape[1] // 2
    xb = x.astype(BF16).astype(F32)
    hi = pltpu.bitcast(xb[:, :w], jnp.uint32)
    lo = pltpu.bitcast(xb[:, w:], jnp.uint32)
    return hi | (lo >> 16)


def _unpack_pairs(p):
    hi = pltpu.bitcast(p & jnp.uint32(0xFFFF0000), F32)
    lo = pltpu.bitcast(p << 16, F32)
    return jnp.concatenate([hi, lo], axis=1)


def _ada_kernel(c_ref, w_ref, b_ref, o_ref):
    c = c_ref[...]
    o_ref[0] = jnp.dot(_silu(c), w_ref[0], preferred_element_type=F32,
                       precision=lax.Precision.HIGHEST) + b_ref[0]


def _ada_all(cvec, ada_w, ada_b):
    depth, d, n = ada_w.shape
    rows = cvec.shape[0]
    tn = 1536
    return pl.pallas_call(
        _ada_kernel,
        grid=(depth, n // tn),
        in_specs=[
            pl.BlockSpec((rows, d), lambda l, j: (0, 0)),
            pl.BlockSpec((1, d, tn), lambda l, j: (l, 0, j)),
            pl.BlockSpec((1, 1, tn), lambda l, j: (l, 0, j)),
        ],
        out_specs=pl.BlockSpec((1, rows, tn), lambda l, j: (l, 0, j)),
        out_shape=jax.ShapeDtypeStruct((depth, rows, n), F32),
        compiler_params=_cparams(("arbitrary", "arbitrary")),
        name="adaln",
    )(cvec, ada_w, ada_b.reshape(depth, 1, n))


def _row_spec(tm, width):
    return pl.BlockSpec((1, tm, width), lambda b, j, *_: (b, j, 0))


def _mod_specs(nb, d):
    return [pl.BlockSpec((1, 6, d), lambda b, j, *_: (b, 0, 0)),
            pl.BlockSpec((1, 6, d), lambda b, j, *_: (nb, 0, 0))]


def _full_spec(shape):
    nd = len(shape)
    return pl.BlockSpec(shape, lambda *_: (0,) * nd)


def _mixer_kernel(x_ref, xp_ref, xn_ref, mx_ref, mt_ref, win_ref, vg_ref, ws_ref, bs_ref, wp_ref,
                  psc_ref, wo_ref, xo_ref, h2_ref, *, tm, ctx_len, lt, chunk):
    is_ctx = _is_ctx_rows(tm, ctx_len)
    pos0 = pl.program_id(1) * tm
    aw = 4 * LANES
    sc1 = 1.0 + _mod(mx_ref, mt_ref, 1, is_ctx)
    sh1 = _mod(mx_ref, mt_ref, 0, is_ctx)
    z = _dot((_rms(x_ref[0]) * sc1 + sh1).astype(BF16), win_ref[...])
    halo_x = jnp.concatenate([xp_ref[0], xn_ref[0]], axis=0)
    halo_h = _rms(halo_x) * sc1[:2 * HALO] + sh1[:2 * HALO]
    halo_z = _dot(halo_h.astype(BF16), win_ref[:, 2 * aw:])
    za = z[:, :2 * aw]
    gl = 0.5 * za * (1.0 + lax.erf(za * (2.0 ** -0.5)))
    u = gl[:, :aw]
    v = gl[:, aw:]
    vn = (_rms(v) * vg_ref[...]).astype(BF16)
    ya_rows = []
    for c in range(tm // chunk):
        rs = slice(c * chunk, (c + 1) * chunk)
        cols = []
        for g in range(4):
            cs = slice(g * LANES, (g + 1) * LANES)
            sg = _dot(ws_ref[g], vn[rs, cs]) + bs_ref[:, cs]
            cols.append(u[rs, cs] * sg)
        ya_rows.append(jnp.concatenate(cols, axis=1))
    ya = jnp.concatenate(ya_rows, axis=0)

    in_ctx = pos0 < ctx_len
    seq_start = jnp.where(in_ctx, 0, ctx_len)
    seq_len = jnp.where(in_ctx, ctx_len, lt - ctx_len)
    p_in = pos0 - seq_start + lax.broadcasted_iota(jnp.int32, (tm, 1), 0)
    has_prev = jnp.logical_and(pos0 != 0, pos0 != ctx_len).astype(F32)
    has_next = jnp.logical_and(pos0 + tm != ctx_len, pos0 + tm != lt).astype(F32)
    zp = z[:, 2 * aw:]
    ext = jnp.concatenate([halo_z[:HALO] * has_prev, zp, halo_z[HALO:] * has_next], axis=0)
    n = tm + 2 * HALO
    yb_cols = []
    for g, w in enumerate(POOL_WINDOWS):
        cs = slice(g * LANES, (g + 1) * LANES)
        a = ext[:, cs]
        step = 1
        while step < w:
            a = a + pltpu.roll(a, n - step, 0)
            step *= 2
        shift = HALO - w // 2
        if shift:
            a = pltpu.roll(a, n - shift, 0)
        wsum = a[:tm]
        lo = jnp.clip(p_in - w // 2, 0, seq_len)
        hi = jnp.clip(p_in - w // 2 + w, 0, seq_len)
        pooled = wsum / (hi - lo).astype(F32) - zp[:, cs]
        yb_cols.append(_dot(pooled.astype(BF16), wp_ref[g]) * psc_ref[:, cs])
    y = jnp.concatenate([ya] + yb_cols, axis=1).astype(BF16)
    xn = x_ref[0] + _mod(mx_ref, mt_ref, 2, is_ctx) * _dot(y, wo_ref[...])
    xo_ref[0] = xn
    h2 = _rms(xn) * (1.0 + _mod(mx_ref, mt_ref, 4, is_ctx)) + _mod(mx_ref, mt_ref, 3, is_ctx)
    h2_ref[0] = h2.astype(BF16)


def _mixer(s, mods, w_in, v_gain, w_s, bs_full, w_pool, p_scale, w_out, *, ctx_len, tm, chunk):
    nb, lt, d = s.shape
    aw = 4 * LANES
    nh = lt // HALO
    th = tm // HALO
    pw = 4 * LANES
    halo_prev = pl.BlockSpec((1, HALO, d), lambda b, j: (b, jnp.maximum(j * th - 1, 0), 0))
    halo_next = pl.BlockSpec((1, HALO, d), lambda b, j: (b, jnp.minimum((j + 1) * th, nh - 1), 0))
    return pl.pallas_call(
        functools.partial(_mixer_kernel, tm=tm, ctx_len=ctx_len, lt=lt, chunk=chunk),
        grid=(nb, lt // tm),
        in_specs=[_row_spec(tm, d), halo_prev, halo_next] + _mod_specs(nb, d) + [
            _full_spec(w_in.shape), _full_spec((1, aw)), _full_spec(w_s.shape), _full_spec(bs_full.shape),
            _full_spec(w_pool.shape), _full_spec((1, pw)), _full_spec(w_out.shape)],
        out_specs=[_row_spec(tm, d), _row_spec(tm, d)],
        out_shape=[jax.ShapeDtypeStruct((nb, lt, d), F32), jax.ShapeDtypeStruct((nb, lt, d), BF16)],
        compiler_params=_cparams(("parallel", "parallel")),
        name="ab_mixer",
    )(s, s, s, mods, mods, w_in, v_gain, w_s, bs_full, w_pool, p_scale, w_out)


def _swiglu_rows(h, wg_ref, wu_ref, wd_ref, a_ref):
    for c in range(a_ref.shape[1] // FF_CHUNK):
        cs = slice(c * FF_CHUNK, (c + 1) * FF_CHUNK)
        g = _dot(h, wg_ref[0, :, cs].astype(BF16))
        a_ref[:, cs] = (_silu(g) * _dot(h, wu_ref[0, :, cs].astype(BF16))).astype(BF16)
    return _dot(a_ref[...], wd_ref[0])


def _swiglu_kernel(h_ref, x_ref, mx_ref, mt_ref, wg_ref, wu_ref, wd_ref, o_ref, a_ref, *, tm, ctx_len):
    is_ctx = _is_ctx_rows(tm, ctx_len)
    f = _swiglu_rows(h_ref[0], wg_ref, wu_ref, wd_ref, a_ref)
    o_ref[0] = x_ref[0] + _mod(mx_ref, mt_ref, 5, is_ctx) * f


def _swiglu(h2, s, mods, w_gate, w_up, w_down, *, ctx_len, tm):
    nb, lt, d = s.shape
    f = w_gate.shape[2]
    once = pl.Buffered(1)
    return pl.pallas_call(
        functools.partial(_swiglu_kernel, tm=tm, ctx_len=ctx_len),
        grid=(nb, lt // tm),
        in_specs=[_row_spec(tm, d), _row_spec(tm, d)] + _mod_specs(nb, d) + [
            pl.BlockSpec((1, d, f), lambda b, j: (0, 0, 0), pipeline_mode=once),
            pl.BlockSpec((1, d, f), lambda b, j: (0, 0, 0), pipeline_mode=once),
            pl.BlockSpec((1, f, d), lambda b, j: (0, 0, 0), pipeline_mode=once)],
        out_specs=_row_spec(tm, d),
        out_shape=jax.ShapeDtypeStruct((nb, lt, d), F32),
        scratch_shapes=[pltpu.VMEM((tm, f), BF16)],
        compiler_params=_cparams(("parallel", "parallel")),
        name="swiglu",
    )(h2, s, mods, mods, w_gate, w_up, w_down)


def _seg_sum(x2, b_ref):
    return _dot(x2.astype(BF16), b_ref[...])


def _rope_slabs(x, cos, sin, half):
    lane = lax.broadcasted_iota(jnp.int32, (x.shape[0], LANES), 1)
    first = (lane % (2 * half)) < half
    out = []
    for s in range(x.shape[1] // LANES):
        xs = x[:, s * LANES:(s + 1) * LANES]
        swapped = jnp.where(first, pltpu.roll(xs, LANES - half, 1), pltpu.roll(xs, half, 1))
        out.append(xs * cos + swapped * sin)
    return out[0] if len(out) == 1 else jnp.concatenate(out, axis=1)


def _store_heads(k_ref, k):
    for h in range(k_ref.shape[2]):
        k_ref[0, 0, h] = k[:, h * LANES:(h + 1) * LANES]


def _qkv_kernel(x_ref, mx_ref, mt_ref, win_ref, cc_ref, sc_ref, cd_ref, sd_ref,
                qg_ref, kg_ref, cqg_ref, ckvg_ref, b512_ref, b128_ref,
                pct_ref, wqnt_ref, wqr_ref, prt_ref, wkn_ref, pkr_ref, wvt_ref, pkc_ref, eye_ref,
                qtc_ref, qtd_ref, kc_ref, kd_ref, vtc_ref, vtd_ref,
                *, tm, ctx_len, head_dim, d_qk):
    is_ctx = _is_ctx_rows(tm, ctx_len)
    h = _rms(x_ref[0]) * (1.0 + _mod(mx_ref, mt_ref, 1, is_ctx)) + _mod(mx_ref, mt_ref, 0, is_ctx)
    z = _dot(h.astype(BF16), win_ref[...])
    cos_c, sin_c, cos_d, sin_d = cc_ref[...], sc_ref[...], cd_ref[...], sd_ref[...]
    o_kc, o_vc, o_cq, o_ckv, o_kr = 512, 640, 768, 1152, 1408

    qc = z[:, :o_kc]
    qc = qc * lax.rsqrt(_seg_sum(qc * qc, b512_ref) * (1.0 / head_dim) + EPS) * qg_ref[...]
    qc = _rope_slabs(qc, cos_c, sin_c, head_dim // 2) * (head_dim ** -0.5 * LOG2E)
    qtc_ref[0] = _dot_nt(pct_ref[...], qc.astype(BF16)).astype(BF16)

    kc = z[:, o_kc:o_vc]
    kc = kc * lax.rsqrt(_seg_sum(kc * kc, b128_ref) * (1.0 / head_dim) + EPS) * kg_ref[...]
    kc = _rope_slabs(kc, cos_c, sin_c, head_dim // 2)
    _store_heads(kc_ref, _dot(kc.astype(BF16), pkc_ref[...]).astype(BF16))
    vtc_ref[0, 0] = _dot_nt(eye_ref[...], z[:, o_vc:o_cq].astype(BF16)).astype(BF16)

    cq = (_rms(z[:, o_cq:o_ckv]) * cqg_ref[...]) * (d_qk ** -0.5 * LOG2E)
    cqb = cq.astype(BF16)
    q_rope = _rope_slabs(_dot(cqb, wqr_ref[...]), cos_d, sin_d, 16)
    qtd = _dot_nt(wqnt_ref[...], cqb) + _dot_nt(prt_ref[...], q_rope.astype(BF16))
    qtd_ref[0] = qtd.astype(BF16)

    ckv = (_rms(z[:, o_ckv:o_kr]) * ckvg_ref[...]).astype(BF16)
    kr = _rope_slabs(z[:, o_kr:], cos_d, sin_d, 16)
    _store_heads(kd_ref, (_dot(ckv, wkn_ref[...]) + _dot(kr.astype(BF16), pkr_ref[...])).astype(BF16))
    vtd_ref[0, 0] = _dot_nt(wvt_ref[...], ckv).astype(BF16)


def _qkv(s, mods, w_in, tabs, consts, *, ctx_len, tm):
    nb, lt, d = s.shape
    nch = lt // tm
    tab_spec = pl.BlockSpec((tm, LANES), lambda b, j: (j, 0))
    out_shapes = [
        jax.ShapeDtypeStruct((nb, 8 * LANES, lt), BF16),
        jax.ShapeDtypeStruct((nb, 8 * LANES, lt), BF16),
        jax.ShapeDtypeStruct((nb, nch, 2, tm, LANES), BF16),
        jax.ShapeDtypeStruct((nb, nch, 8, tm, LANES), BF16),
        jax.ShapeDtypeStruct((nb, nch, 2 * 64, tm), BF16),
        jax.ShapeDtypeStruct((nb, nch, 8 * 64, tm), BF16),
    ]
    out_specs = [
        pl.BlockSpec((1, 8 * LANES, tm), lambda b, j: (b, 0, j)),
        pl.BlockSpec((1, 8 * LANES, tm), lambda b, j: (b, 0, j)),
        pl.BlockSpec((1, 1, 2, tm, LANES), lambda b, j: (b, j, 0, 0, 0)),
        pl.BlockSpec((1, 1, 8, tm, LANES), lambda b, j: (b, j, 0, 0, 0)),
        pl.BlockSpec((1, 1, 2 * 64, tm), lambda b, j: (b, j, 0, 0)),
        pl.BlockSpec((1, 1, 8 * 64, tm), lambda b, j: (b, j, 0, 0)),
    ]
    return pl.pallas_call(
        functools.partial(_qkv_kernel, tm=tm, ctx_len=ctx_len, head_dim=64, d_qk=96),
        grid=(nb, nch),
        in_specs=[_row_spec(tm, d)] + _mod_specs(nb, d) + [_full_spec(w_in.shape)] + [tab_spec] * 4
        + [_full_spec(c.shape) for c in consts],
        out_specs=out_specs,
        out_shape=out_shapes,
        compiler_params=_cparams(("parallel", "parallel")),
        name="cd_qkv",
    )(s, mods, mods, w_in, *tabs, *consts)


def _attn_kernel(qt_ref, k_ref, vt_ref, o_ref, s0_ref, s1_ref, c0_ref, c1_ref, m_ref, acc_ref,
                 *, kv_slots, tq, n_ctx, n_all, dv):
    heads = len(kv_slots)

    def scores(j, g, kg, dst):
        s_dst, c_dst = dst
        st = _dot(k_ref[0, j, kg], qt_ref[0, g * LANES:(g + 1) * LANES, :])
        s_dst[g] = st
        c_dst[g] = jnp.max(st, axis=0, keepdims=True)

    def consume(j, src, g, kg):
        s_src, c_src = src
        m_old = m_ref[g]
        m_new = jnp.maximum(m_old, c_src[g])
        alpha = jnp.exp2(m_old - m_new)
        p = jnp.exp2((s_src[g] - m_new).astype(BF16))
        m_ref[g] = m_new
        vt = jnp.concatenate([vt_ref[0, j, kg * dv:(kg + 1) * dv, :], jnp.ones((DEN_ROWS, p.shape[0]), BF16)],
                             axis=0)
        acc_ref[g] = alpha * acc_ref[g] + _dot(vt, p)

    def step(j, j_next, cur, nxt):
        for g, kg in enumerate(kv_slots):
            if j_next is not None:
                scores(j_next, g, kg, nxt)
            consume(j, cur, g, kg)

    def run(n):
        bufs = ((s0_ref, c0_ref), (s1_ref, c1_ref))
        m_ref[...] = jnp.full(m_ref.shape, NEG, F32)
        acc_ref[...] = jnp.zeros(acc_ref.shape, F32)
        for g, kg in enumerate(kv_slots):
            scores(0, g, kg, bufs[0])
        n_main = (n - 1) // ATTN_UNROLL * ATTN_UNROLL

        def trip(i, c):
            for k in range(ATTN_UNROLL):
                j = ATTN_UNROLL * i + k
                step(j, j + 1, bufs[k % 2], bufs[(k + 1) % 2])
            return c

        lax.fori_loop(0, n_main // ATTN_UNROLL, trip, 0)
        for j in range(n_main, n):
            step(j, j + 1 if j + 1 < n else None, bufs[j % 2], bufs[(j + 1) % 2])
        ot = jnp.concatenate([acc_ref[g, :dv, :] / acc_ref[g, dv:dv + 1, :] for g in range(heads)],
                             axis=0)
        o_ref[0] = ot.T.astype(o_ref.dtype)

    is_ctx_tile = pl.program_id(1) * tq < n_ctx * k_ref.shape[3]

    @pl.when(is_ctx_tile)
    def _():
        run(n_ctx)

    @pl.when(jnp.logical_not(is_ctx_tile))
    def _():
        run(n_all)


def _attention(qt, k, vt, *, kv_slots, ctx_len, tq, name):
    nb, hq, lt = qt.shape
    heads = len(kv_slots)
    assert hq == heads * LANES
    nch, n_kv, tk = k.shape[1:4]
    dv = vt.shape[2] // n_kv
    once = pl.Buffered(1)
    return pl.pallas_call(
        functools.partial(_attn_kernel, kv_slots=kv_slots, tq=tq, n_ctx=ctx_len // tk, n_all=nch, dv=dv),
        grid=(nb, lt // tq),
        in_specs=[
            pl.BlockSpec((1, hq, tq), lambda b, q: (b, 0, q)),
            pl.BlockSpec((1, nch, n_kv, tk, LANES), lambda b, q: (b, 0, 0, 0, 0), pipeline_mode=once),
            pl.BlockSpec((1, nch, vt.shape[2], tk), lambda b, q: (b, 0, 0, 0), pipeline_mode=once),
        ],
        out_specs=pl.BlockSpec((1, tq, heads * dv), lambda b, q: (b, q, 0)),
        out_shape=jax.ShapeDtypeStruct((nb, lt, heads * dv), BF16),
        scratch_shapes=[pltpu.VMEM((heads, tk, tq), F32), pltpu.VMEM((heads, tk, tq), F32),
                        pltpu.VMEM((heads, 1, tq), F32), pltpu.VMEM((heads, 1, tq), F32),
                        pltpu.VMEM((heads, 1, tq), F32), pltpu.VMEM((heads, dv + DEN_ROWS, tq), F32)],
        compiler_params=_cparams(("parallel", "arbitrary")),
        name=name,
    )(qt, k, vt)


def _oproj_kernel(oc_ref, od_ref, x_ref, mx_ref, mt_ref, w_ref, r_ref, xo_ref, h2_ref, g_ref,
                  *, tm, ctx_len, n_exp):
    is_ctx = _is_ctx_rows(tm, ctx_len)
    o = jnp.concatenate([oc_ref[0], od_ref[0]], axis=-1)
    xn = x_ref[0] + _mod(mx_ref, mt_ref, 2, is_ctx) * _dot(o, w_ref[...])
    xo_ref[0] = xn
    h2 = _rms(xn) * (1.0 + _mod(mx_ref, mt_ref, 4, is_ctx)) + _mod(mx_ref, mt_ref, 3, is_ctx)
    h2_ref[0] = _pack_pairs(h2)
    h_hi = h2.astype(BF16)
    h_lo = (h2 - h_hi.astype(F32)).astype(BF16)
    logits = _dot(h_hi, r_ref[0]) + _dot(h_lo, r_ref[0]) + _dot(h_hi, r_ref[1])
    lane = lax.broadcasted_iota(jnp.int32, (tm, LANES), 1).astype(F32)
    lg = jnp.where(lane < n_exp, logits, NEG)
    m1 = jnp.max(lg, axis=-1, keepdims=True)
    i1 = jnp.min(jnp.where(lg == m1, lane, float(LANES)), axis=-1, keepdims=True)
    lg2 = jnp.where(lane == i1, NEG, lg)
    m2 = jnp.max(lg2, axis=-1, keepdims=True)
    i2 = jnp.min(jnp.where(lg2 == m2, lane, float(LANES)), axis=-1, keepdims=True)
    e2 = jnp.exp(m2 - m1)
    den = 1.0 + e2
    g_ref[0] = jnp.where(lane == 0.0, i1, jnp.where(lane == 1.0, i2, jnp.where(lane == 2.0, 1.0 / den, e2 / den)))


def _oproj(oc, od, s, mods, w_out, router, *, ctx_len, tm, n_exp):
    nb, lt, d = s.shape
    return pl.pallas_call(
        functools.partial(_oproj_kernel, tm=tm, ctx_len=ctx_len, n_exp=n_exp),
        grid=(nb, lt // tm),
        in_specs=[_row_spec(tm, oc.shape[2]), _row_spec(tm, od.shape[2]), _row_spec(tm, d)]
        + _mod_specs(nb, d) + [_full_spec(w_out.shape), _full_spec(router.shape)],
        out_specs=[_row_spec(tm, d), _row_spec(tm, _packed_width(d)), _row_spec(tm, LANES)],
        out_shape=[jax.ShapeDtypeStruct((nb, lt, d), F32), jax.ShapeDtypeStruct((nb, lt, _packed_width(d)), PACKED),
                   jax.ShapeDtypeStruct((nb, lt, LANES), F32)],
        compiler_params=_cparams(("parallel", "parallel")),
        name="cd_oproj_router",
    )(oc, od, s, mods, mods, w_out, router)


def _sc_gather(table, idx, *, chunk=SC_CHUNK):
    n_rows, d = idx.shape[0], table.shape[1]
    info = plsc.get_sparse_core_info()
    n_workers = info.num_cores * info.num_subcores
    per_w = n_rows // n_workers
    assert n_rows % (n_workers * chunk) == 0 and chunk % 8 == 0 and chunk <= LANES
    mesh = plsc.VectorSubcoreMesh(core_axis_name="c", subcore_axis_name="s")

    n_chunks = per_w // chunk

    def body(table_hbm, idx_hbm, out_hbm, idx_v, rows0, rows1, sem0, sem1):
        wid = lax.axis_index("s") * info.num_cores + lax.axis_index("c")
        base = wid * per_w
        pltpu.sync_copy(idx_hbm.at[pl.ds(base, per_w)], idx_v)
        bufs = ((rows0, sem0), (rows1, sem1))

        def gather(i, buf):
            off = pl.multiple_of(i * chunk, chunk)
            return pltpu.make_async_copy(table_hbm.at[idx_v.at[pl.ds(off, chunk)]], buf[0], buf[1])

        def drain(i, buf):
            gather(i, buf).wait()
            pltpu.sync_copy(buf[0], out_hbm.at[pl.ds(base + pl.multiple_of(i * chunk, chunk), chunk)])

        gather(0, bufs[0]).start()

        @pl.loop(0, n_chunks // 2)
        def _(t):
            i = 2 * t
            gather(i + 1, bufs[1]).start()
            drain(i, bufs[0])

            @pl.when(i + 2 < n_chunks)
            def _():
                gather(i + 2, bufs[0]).start()

            drain(i + 1, bufs[1])

        if n_chunks % 2:
            drain(n_chunks - 1, bufs[0])

    return pl.kernel(
        body,
        out_type=jax.ShapeDtypeStruct((n_rows, d), table.dtype),
        mesh=mesh,
        scratch_types=[pltpu.VMEM((per_w,), jnp.int32), pltpu.VMEM((chunk, d), table.dtype),
                       pltpu.VMEM((chunk, d), table.dtype), pltpu.SemaphoreType.DMA, pltpu.SemaphoreType.DMA],
        name="sc_row_gather",
    )(table, idx)


def _route_plan(route, *, n_exp, ts, align):
    n_tok = route.shape[0]
    experts = route[:, :2].astype(jnp.int32)
    onehot = (experts[:, :, None] == jnp.arange(n_exp)[None, None, :]).astype(jnp.int32).sum(axis=1)
    blk = LANES if n_tok % LANES == 0 else 1
    oh = onehot.astype(F32).reshape(n_tok // blk, blk, n_exp)
    inside = jnp.einsum("ij,bje->bie", jnp.tril(jnp.ones((blk, blk), F32), -1), oh)
    blk_tot = oh.sum(axis=1)
    before = (inside + (jnp.cumsum(blk_tot, axis=0) - blk_tot)[:, None, :]).reshape(n_tok, n_exp)
    before = before.astype(jnp.int32)
    counts = onehot.sum(axis=0)
    gsz = (counts + ts - 1) // ts * ts
    gend = jnp.cumsum(gsz)
    gstart = gend - gsz
    pos = gstart[experts] + jnp.take_along_axis(before, experts, axis=1)
    n_slots = -(-(2 * n_tok + n_exp * ts) // align) * align
    n_tiles = n_slots // ts
    tile_start = jnp.arange(n_tiles, dtype=jnp.int32) * ts
    n_used = (gend[-1] // ts).astype(jnp.int32)
    tile_expert = jnp.minimum(jnp.searchsorted(gend, tile_start, side="right"), n_exp - 1).astype(jnp.int32)
    order = jnp.argsort(experts.reshape(-1), stable=True).astype(jnp.int32)
    slot_e = jnp.repeat(tile_expert, ts)
    rank = jnp.arange(n_slots, dtype=jnp.int32) - gstart[slot_e]
    cstart = jnp.cumsum(counts) - counts
    src = jnp.clip(cstart[slot_e] + rank, 0, 2 * n_tok - 1)
    tok = jnp.where(rank < counts[slot_e], order[src] // 2, 0).astype(jnp.int32)
    last_used = tile_expert[jnp.maximum(n_used - 1, 0)]
    tile_expert = jnp.where(jnp.arange(n_tiles) < n_used, tile_expert, last_used)
    return tok, pos.T.reshape(-1).astype(jnp.int32), tile_expert, n_used.reshape(1)


def _gffn_kernel(te_ref, nu_ref, h_ref, wg_ref, wu_ref, wd_ref, o_ref, a_ref):
    used = pl.program_id(0) < nu_ref[0]

    @pl.when(used)
    def _():
        h = _unpack_pairs(h_ref[...]).astype(BF16)
        o_ref[...] = _pack_pairs(_swiglu_rows(h, wg_ref.at[0], wu_ref.at[0], wd_ref.at[0], a_ref))

    @pl.when(jnp.logical_not(used))
    def _():
        o_ref[...] = jnp.zeros(o_ref.shape, o_ref.dtype)


def _grouped_ffn(hs, tile_expert, n_used, w_gate, w_up, w_down, *, layer, ts):
    n_slots, dp = hs.shape
    d, f = w_gate.shape[2:]
    once = pl.Buffered(1)
    grid_spec = pltpu.PrefetchScalarGridSpec(
        num_scalar_prefetch=2,
        grid=(n_slots // ts,),
        in_specs=[
            pl.BlockSpec((ts, dp), lambda j, te, nu: (j, 0)),
            pl.BlockSpec((1, 1, d, f), lambda j, te, nu: (layer, te[j], 0, 0), pipeline_mode=once),
            pl.BlockSpec((1, 1, d, f), lambda j, te, nu: (layer, te[j], 0, 0), pipeline_mode=once),
            pl.BlockSpec((1, 1, f, d), lambda j, te, nu: (layer, te[j], 0, 0), pipeline_mode=once),
        ],
        out_specs=pl.BlockSpec((ts, dp), lambda j, te, nu: (j, 0)),
        scratch_shapes=[pltpu.VMEM((ts, f), BF16)],
    )
    return pl.pallas_call(
        _gffn_kernel,
        grid_spec=grid_spec,
        out_shape=jax.ShapeDtypeStruct((n_slots, dp), PACKED),
        compiler_params=_cparams(("arbitrary",)),
        name="moe_grouped_ffn",
    )(tile_expert, n_used, hs, w_gate, w_up, w_down)


def _combine_kernel(y1_ref, y2_ref, r_ref, x_ref, mx_ref, mt_ref, fg_ref, o_ref, *, tm, ctx_len, row0, final):
    pos = (pl.program_id(1) + row0) * tm + lax.broadcasted_iota(jnp.int32, (tm, 1), 0)
    is_ctx = pos < ctx_len
    r = r_ref[0]
    f = r[:, 2:3] * _unpack_pairs(y1_ref[0, 0]) + r[:, 3:4] * _unpack_pairs(y2_ref[0, 0])
    xn = x_ref[0] + _mod(mx_ref, mt_ref, 5, is_ctx) * f
    o_ref[0] = _rms(xn) * fg_ref[...] if final else xn


def _combine(yg, route, s, mods, final_gain, *, ctx_len, tm, final):
    nb, lt, d = s.shape
    row0 = ctx_len // tm if final else 0
    rows = lt - row0 * tm

    def rspec(width):
        return pl.BlockSpec((1, tm, width), lambda b, j: (b, j + row0, 0))

    return pl.pallas_call(
        functools.partial(_combine_kernel, tm=tm, ctx_len=ctx_len, row0=row0, final=final),
        grid=(nb, rows // tm),
        in_specs=[pl.BlockSpec((1, 1, tm, yg.shape[-1]), lambda b, j: (0, b, j + row0, 0)),
                  pl.BlockSpec((1, 1, tm, yg.shape[-1]), lambda b, j: (1, b, j + row0, 0)),
                  rspec(LANES), rspec(d)] + _mod_specs(nb, d) + [_full_spec((1, d))],
        out_specs=_row_spec(tm, d),
        out_shape=jax.ShapeDtypeStruct((nb, rows, d), F32),
        compiler_params=_cparams(("parallel", "parallel")),
        name="moe_combine",
    )(yg, yg, route, s, mods, mods, final_gain.reshape(1, d))


def _moe(h2, route, s, mods, w_gate, w_up, w_down, final_gain, *, layer, ctx_len, n_exp, ts, tm, final):
    nb, lt, d = s.shape
    n_tok = nb * lt
    info = plsc.get_sparse_core_info()
    sc_rows = info.num_cores * info.num_subcores * SC_CHUNK
    tok, pos, tile_expert, n_used = _route_plan(route.reshape(n_tok, LANES), n_exp=n_exp, ts=ts,
                                                align=int(np.lcm(ts, sc_rows)))
    hs = _sc_gather(h2.reshape(n_tok, h2.shape[-1]), tok)
    ys = _grouped_ffn(hs, tile_expert, n_used, w_gate, w_up, w_down, layer=layer, ts=ts)
    yg = _sc_gather(ys, pos).reshape(2, nb, lt, ys.shape[-1])
    return _combine(yg, route, s, mods, final_gain, ctx_len=ctx_len, tm=tm, final=final)


def _final_kernel(x_ref, g_ref, o_ref):
    o_ref[0] = _rms(x_ref[0]) * g_ref[...]


def _final(s, gain, *, ctx_len, tm):
    nb, lt, d = s.shape
    seq = lt - ctx_len
    off = ctx_len // tm
    return pl.pallas_call(
        _final_kernel,
        grid=(nb, seq // tm),
        in_specs=[pl.BlockSpec((1, tm, d), lambda b, j: (b, j + off, 0)), _full_spec((1, d))],
        out_specs=_row_spec(tm, d),
        out_shape=jax.ShapeDtypeStruct((nb, seq, d), F32),
        compiler_params=_cparams(("parallel", "parallel")),
        name="final_norm",
    )(s, gain.reshape(1, d))


def _rope_tables(seq, ctx_len):
    rows = seq // GRID_W
    row = jnp.repeat(jnp.arange(rows), GRID_W).astype(F32)
    col = jnp.tile(jnp.arange(GRID_W), rows).astype(F32)

    def table(rot_dim, reps):
        axis_dim = rot_dim // 2
        inv = ROPE_THETA ** (-jnp.arange(0, axis_dim, 2, dtype=F32) / axis_dim)
        ang = jnp.concatenate([row[:, None] * inv, col[:, None] * inv], axis=-1)
        cos = jnp.concatenate([jnp.ones((ctx_len, axis_dim), F32), jnp.cos(ang)], axis=0)
        sin = jnp.concatenate([jnp.zeros((ctx_len, axis_dim), F32), jnp.sin(ang)], axis=0)
        return (jnp.tile(jnp.concatenate([cos, cos], axis=-1), (1, reps)),
                jnp.tile(jnp.concatenate([-sin, sin], axis=-1), (1, reps)))

    cos_c, sin_c = table(64, 2)
    cos_d, sin_d = table(32, 4)
    return cos_c, sin_c, cos_d, sin_d


def _placement(n_rows, n_cols, pairs):
    m = np.zeros((n_rows, n_cols), np.float32)
    for r, c in pairs:
        m[r, c] = 1.0
    return jnp.asarray(m, BF16)


def _cd_consts(q_gain, k_gain, cq_gain, ckv_gain, w_uq, w_ukv):
    hd, nh, nkv, d_nope, d_rope, d_v = 64, 8, 2, 64, 32, 64
    d_qk = d_nope + d_rope
    heads = np.arange(nh)
    b512 = jnp.asarray(np.kron(np.eye(nh), np.ones((hd, hd))), BF16)
    b128 = jnp.asarray(np.kron(np.eye(nkv), np.ones((hd, hd))), BF16)
    pct = _placement(nh * LANES, nh * hd, [(h * LANES + i, h * hd + i) for h in heads for i in range(hd)])
    nope_cols = (heads[:, None] * d_qk + np.arange(d_nope)[None, :]).reshape(-1)
    rope_cols = (heads[:, None] * d_qk + d_nope + np.arange(d_rope)[None, :]).reshape(-1)
    nope_rows = (heads[:, None] * LANES + np.arange(d_nope)[None, :]).reshape(-1)
    wqnt = jnp.zeros((nh * LANES, w_uq.shape[0]), F32).at[nope_rows].set(w_uq[:, nope_cols].T).astype(BF16)
    wqr = w_uq[:, rope_cols].astype(BF16)
    prt = _placement(nh * LANES, nh * d_rope,
                     [(h * LANES + d_nope + i, h * d_rope + i) for h in heads for i in range(d_rope)])
    kn_cols = (heads[:, None] * (d_nope + d_v) + np.arange(d_nope)[None, :]).reshape(-1)
    v_cols = (heads[:, None] * (d_nope + d_v) + d_nope + np.arange(d_v)[None, :]).reshape(-1)
    wkn = jnp.zeros((w_ukv.shape[0], nh * LANES), F32).at[:, nope_rows].set(w_ukv[:, kn_cols]).astype(BF16)
    pkr = _placement(LANES, nh * LANES, [(i, h * LANES + d_nope + i) for h in heads for i in range(d_rope)])
    wvt = w_ukv[:, v_cols].T.astype(BF16)
    pkc = _placement(nkv * hd, nkv * LANES, [(h * hd + i, h * LANES + i) for h in range(nkv) for i in range(hd)])
    eye = jnp.asarray(np.eye(nkv * hd), BF16)
    return [jnp.tile(q_gain, nh).reshape(1, -1), jnp.tile(k_gain, nkv).reshape(1, -1),
            cq_gain.reshape(1, -1), ckv_gain.reshape(1, -1), b512, b128,
            pct, wqnt, wqr, prt, wkn, pkr, wvt, pkc, eye]


def _largest_tile(n, candidates):
    for t in candidates:
        if n % t == 0:
            return t
    raise ValueError(f"no tile in {candidates} divides {n}")


def kernel(x, c, ctx, c_ctx, ada_w, ada_b, ab_w_in, a_v_gain, a_w_s, a_b_s, b_w_pool, b_scale, ab_w_out,
           ffn_w_gate, ffn_w_up, ffn_w_down, cd_w_in, c_q_gain, c_k_gain, d_cq_gain, d_ckv_gain, d_w_uq,
           d_w_ukv, cd_w_out, moe_router, moe_w_gate, moe_w_up, moe_w_down, final_gain):
    nb, seq, d = x.shape
    ctx_len = ctx.shape[1]
    lt = ctx_len + seq
    depth = ada_w.shape[0]
    n_exp = moe_router.shape[-1]
    chunk = a_w_s.shape[-1]
    tr = 256
    assert ctx_len % tr == 0 and seq % tr == 0 and seq % GRID_W == 0
    tm_ff = _largest_tile(lt, (768, 384, 256))
    assert ffn_w_gate.shape[-1] % FF_CHUNK == 0 and moe_w_gate.shape[-1] % FF_CHUNK == 0

    s = jnp.concatenate([ctx, x], axis=1)

    pad = -(nb + 1) % 8
    cvec = jnp.concatenate([c, c_ctx[None, :], jnp.zeros((pad, d), F32)], axis=0)
    mods_all = _ada_all(cvec, ada_w, ada_b).reshape(depth, nb + 1 + pad, 6, d)
    tabs = _rope_tables(seq, ctx_len)
    moe_wg, moe_wu, moe_wd = moe_w_gate, moe_w_up, moe_w_down.astype(BF16)

    for layer in range(depth):
        i = layer // 2
        mods = mods_all[layer]
        if layer % 2 == 0:
            bs_full = jnp.repeat(a_b_s[i].T, LANES, axis=1)
            s, h2 = _mixer(s, mods, ab_w_in[i].astype(BF16), a_v_gain[i].reshape(1, -1),
                           a_w_s[i].astype(BF16), bs_full, b_w_pool[i].astype(BF16), b_scale[i].reshape(1, -1),
                           ab_w_out[i].astype(BF16), ctx_len=ctx_len, tm=tr, chunk=chunk)
            s = _swiglu(h2, s, mods, ffn_w_gate[i][None].astype(BF16), ffn_w_up[i][None].astype(BF16),
                        ffn_w_down[i][None].astype(BF16), ctx_len=ctx_len, tm=tm_ff)
        else:
            w_in = jnp.pad(cd_w_in[i], ((0, 0), (0, 12 * LANES - cd_w_in.shape[-1]))).astype(BF16)
            consts = _cd_consts(c_q_gain[i], c_k_gain[i], d_cq_gain[i], d_ckv_gain[i], d_w_uq[i], d_w_ukv[i])
            qtc, qtd, kc, kd, vtc, vtd = _qkv(s, mods, w_in, tabs, consts, ctx_len=ctx_len, tm=tr)
            oc = _attention(qtc, kc, vtc, kv_slots=(0, 0, 0, 0, 1, 1, 1, 1), ctx_len=ctx_len, tq=tr,
                            name="attn_gqa")
            od = _attention(qtd, kd, vtd, kv_slots=tuple(range(8)), ctx_len=ctx_len, tq=tr, name="attn_mla")
            router = jnp.pad(moe_router[i], ((0, 0), (0, LANES - n_exp)))
            r_hi = router.astype(BF16)
            router = jnp.stack([r_hi, (router - r_hi.astype(F32)).astype(BF16)])
            s, h2, route = _oproj(oc, od, s, mods, cd_w_out[i].astype(BF16), router,
                                  ctx_len=ctx_len, tm=tr, n_exp=n_exp)
            s = _moe(h2, route, s, mods, moe_wg, moe_wu, moe_wd, final_gain, layer=i, ctx_len=ctx_len,
                     n_exp=n_exp, ts=512, tm=tr, final=layer == depth - 1)

    return s if depth % 2 == 0 else _final(s, final_gain, ctx_len=ctx_len, tm=tr)
```

```python
import functools

import numpy as np
import jax
import jax.numpy as jnp
from jax import lax
from jax.experimental import pallas as pl
from jax.experimental.pallas import tpu as pltpu
from jax.experimental.pallas import tpu_sc as plsc

F32 = jnp.float32
BF16 = jnp.bfloat16
EPS = 1e-6
ROPE_THETA = 10000.0
GRID_W = 64
LANES = 128
HALO = 8
POOL_WINDOWS = (2, 4, 8, 16)
NEG = -1e30
LOG2E = 1.4426950408889634
ATTN_UNROLL = 16
FF_CHUNK = 256
DEN_ROWS = 16
SC_CHUNK = 64
VMEM_LIMIT = 56 * 1024 * 1024


def _cparams(sem):
    return pltpu.CompilerParams(dimension_semantics=sem, vmem_limit_bytes=VMEM_LIMIT)


def _rms(x):
    return x * lax.rsqrt(jnp.mean(x * x, axis=-1, keepdims=True) + EPS)


def _is_ctx_rows(tm, ctx_len):
    pos = pl.program_id(1) * tm + lax.broadcasted_iota(jnp.int32, (tm, 1), 0)
    return pos < ctx_len


def _mod(mx_ref, mt_ref, idx, is_ctx):
    return jnp.where(is_ctx, mt_ref[0, idx:idx + 1, :], mx_ref[0, idx:idx + 1, :])


def _dot(a, b):
    return jnp.dot(a, b, preferred_element_type=F32)


def _dot_nt(a, b):
    return lax.dot_general(a, b, (((1,), (1,)), ((), ())), preferred_element_type=F32)


def _silu(x):
    return x * jax.nn.sigmoid(x)


PACKED = jnp.uint32


def _packed_width(d):
    return d // 2


def _pack_pairs(x):
    w = x.shape[1] // 2
    xb = x.astype(BF16).astype(F32)
    hi = pltpu.bitcast(xb[:, :w], jnp.uint32)
    lo = pltpu.bitcast(xb[:, w:], jnp.uint32)
    return hi | (lo >> 16)


def _unpack_pairs(p):
    hi = pltpu.bitcast(p & jnp.uint32(0xFFFF0000), F32)
    lo = pltpu.bitcast(p << 16, F32)
    return jnp.concatenate([hi, lo], axis=1)


def _ada_kernel(c_ref, w_ref, b_ref, o_ref):
    c = c_ref[...]
    o_ref[0] = jnp.dot(_silu(c), w_ref[0], preferred_element_type=F32,
                       precision=lax.Precision.HIGHEST) + b_ref[0]


def _ada_all(cvec, ada_w, ada_b):
    depth, d, n = ada_w.shape
    rows = cvec.shape[0]
    tn = 1536
    return pl.pallas_call(
        _ada_kernel,
        grid=(depth, n // tn),
        in_specs=[
            pl.BlockSpec((rows, d), lambda l, j: (0, 0)),
            pl.BlockSpec((1, d, tn), lambda l, j: (l, 0, j)),
            pl.BlockSpec((1, 1, tn), lambda l, j: (l, 0, j)),
        ],
        out_specs=pl.BlockSpec((1, rows, tn), lambda l, j: (l, 0, j)),
        out_shape=jax.ShapeDtypeStruct((depth, rows, n), F32),
        compiler_params=_cparams(("arbitrary", "arbitrary")),
        name="adaln",
    )(cvec, ada_w, ada_b.reshape(depth, 1, n))


def _row_spec(tm, width):
    return pl.BlockSpec((1, tm, width), lambda b, j, *_: (b, j, 0))


def _mod_specs(nb, d):
    return [pl.BlockSpec((1, 6, d), lambda b, j, *_: (b, 0, 0)),
            pl.BlockSpec((1, 6, d), lambda b, j, *_: (nb, 0, 0))]


def _full_spec(shape):
    nd = len(shape)
    return pl.BlockSpec(shape, lambda *_: (0,) * nd)


def _mixer_kernel(x_ref, xp_ref, xn_ref, mx_ref, mt_ref, win_ref, vg_ref, ws_ref, bs_ref, wp_ref,
                  psc_ref, wo_ref, xo_ref, h2_ref, *, tm, ctx_len, lt, chunk):
    is_ctx = _is_ctx_rows(tm, ctx_len)
    pos0 = pl.program_id(1) * tm
    aw = 4 * LANES
    sc1 = 1.0 + _mod(mx_ref, mt_ref, 1, is_ctx)
    sh1 = _mod(mx_ref, mt_ref, 0, is_ctx)
    z = _dot((_rms(x_ref[0]) * sc1 + sh1).astype(BF16), win_ref[...])
    halo_x = jnp.concatenate([xp_ref[0], xn_ref[0]], axis=0)
    halo_h = _rms(halo_x) * sc1[:2 * HALO] + sh1[:2 * HALO]
    halo_z = _dot(halo_h.astype(BF16), win_ref[:, 2 * aw:])
    za = z[:, :2 * aw]
    gl = 0.5 * za * (1.0 + lax.erf(za * (2.0 ** -0.5)))
    u = gl[:, :aw]
    v = gl[:, aw:]
    vn = (_rms(v) * vg_ref[...]).astype(BF16)
    ya_rows = []
    for c in range(tm // chunk):
        rs = slice(c * chunk, (c + 1) * chunk)
        cols = []
        for g in range(4):
            cs = slice(g * LANES, (g + 1) * LANES)
            sg = _dot(ws_ref[g], vn[rs, cs]) + bs_ref[:, cs]
            cols.append(u[rs, cs] * sg)
        ya_rows.append(jnp.concatenate(cols, axis=1))
    ya = jnp.concatenate(ya_rows, axis=0)

    in_ctx = pos0 < ctx_len
    seq_start = jnp.where(in_ctx, 0, ctx_len)
    seq_len = jnp.where(in_ctx, ctx_len, lt - ctx_len)
    p_in = pos0 - seq_start + lax.broadcasted_iota(jnp.int32, (tm, 1), 0)
    has_prev = jnp.logical_and(pos0 != 0, pos0 != ctx_len).astype(F32)
    has_next = jnp.logical_and(pos0 + tm != ctx_len, pos0 + tm != lt).astype(F32)
    zp = z[:, 2 * aw:]
    ext = jnp.concatenate([halo_z[:HALO] * has_prev, zp, halo_z[HALO:] * has_next], axis=0)
    n = tm + 2 * HALO
    yb_cols = []
    for g, w in enumerate(POOL_WINDOWS):
        cs = slice(g * LANES, (g + 1) * LANES)
        a = ext[:, cs]
        step = 1
        while step < w:
            a = a + pltpu.roll(a, n - step, 0)
            step *= 2
        shift = HALO - w // 2
        if shift:
            a = pltpu.roll(a, n - shift, 0)
        wsum = a[:tm]
        lo = jnp.clip(p_in - w // 2, 0, seq_len)
        hi = jnp.clip(p_in - w // 2 + w, 0, seq_len)
        pooled = wsum / (hi - lo).astype(F32) - zp[:, cs]
        yb_cols.append(_dot(pooled.astype(BF16), wp_ref[g]) * psc_ref[:, cs])
    y = jnp.concatenate([ya] + yb_cols, axis=1).astype(BF16)
    xn = x_ref[0] + _mod(mx_ref, mt_ref, 2, is_ctx) * _dot(y, wo_ref[...])
    xo_ref[0] = xn
    h2 = _rms(xn) * (1.0 + _mod(mx_ref, mt_ref, 4, is_ctx)) + _mod(mx_ref, mt_ref, 3, is_ctx)
    h2_ref[0] = h2.astype(BF16)


def _mixer(s, mods, w_in, v_gain, w_s, bs_full, w_pool, p_scale, w_out, *, ctx_len, tm, chunk):
    nb, lt, d = s.shape
    aw = 4 * LANES
    nh = lt // HALO
    th = tm // HALO
    pw = 4 * LANES
    halo_prev = pl.BlockSpec((1, HALO, d), lambda b, j: (b, jnp.maximum(j * th - 1, 0), 0))
    halo_next = pl.BlockSpec((1, HALO, d), lambda b, j: (b, jnp.minimum((j + 1) * th, nh - 1), 0))
    return pl.pallas_call(
        functools.partial(_mixer_kernel, tm=tm, ctx_len=ctx_len, lt=lt, chunk=chunk),
        grid=(nb, lt // tm),
        in_specs=[_row_spec(tm, d), halo_prev, halo_next] + _mod_specs(nb, d) + [
            _full_spec(w_in.shape), _full_spec((1, aw)), _full_spec(w_s.shape), _full_spec(bs_full.shape),
            _full_spec(w_pool.shape), _full_spec((1, pw)), _full_spec(w_out.shape)],
        out_specs=[_row_spec(tm, d), _row_spec(tm, d)],
        out_shape=[jax.ShapeDtypeStruct((nb, lt, d), F32), jax.ShapeDtypeStruct((nb, lt, d), BF16)],
        compiler_params=_cparams(("parallel", "parallel")),
        name="ab_mixer",
    )(s, s, s, mods, mods, w_in, v_gain, w_s, bs_full, w_pool, p_scale, w_out)


def _swiglu_rows(h, wg_ref, wu_ref, wd_ref):
    out = None
    for c in range(wg_ref.shape[2] // FF_CHUNK):
        cs = slice(c * FF_CHUNK, (c + 1) * FF_CHUNK)
        g = _dot(h, wg_ref[0, :, cs].astype(BF16))
        a = (_silu(g) * _dot(h, wu_ref[0, :, cs].astype(BF16))).astype(BF16)
        part = _dot(a, wd_ref[0, cs, :].astype(BF16))
        out = part if out is None else out + part
    return out


def _swiglu_kernel(h_ref, x_ref, mx_ref, mt_ref, wg_ref, wu_ref, wd_ref, o_ref, *, tm, ctx_len):
    is_ctx = _is_ctx_rows(tm, ctx_len)
    f = _swiglu_rows(h_ref[0], wg_ref, wu_ref, wd_ref)
    o_ref[0] = x_ref[0] + _mod(mx_ref, mt_ref, 5, is_ctx) * f


def _swiglu(h2, s, mods, w_gate, w_up, w_down, *, ctx_len, tm):
    nb, lt, d = s.shape
    f = w_gate.shape[2]
    once = pl.Buffered(1)
    return pl.pallas_call(
        functools.partial(_swiglu_kernel, tm=tm, ctx_len=ctx_len),
        grid=(nb, lt // tm),
        in_specs=[_row_spec(tm, d), _row_spec(tm, d)] + _mod_specs(nb, d) + [
            pl.BlockSpec((1, d, f), lambda b, j: (0, 0, 0), pipeline_mode=once),
            pl.BlockSpec((1, d, f), lambda b, j: (0, 0, 0), pipeline_mode=once),
            pl.BlockSpec((1, f, d), lambda b, j: (0, 0, 0), pipeline_mode=once)],
        out_specs=_row_spec(tm, d),
        out_shape=jax.ShapeDtypeStruct((nb, lt, d), F32),
        compiler_params=_cparams(("parallel", "parallel")),
        name="swiglu",
    )(h2, s, mods, mods, w_gate, w_up, w_down)


def _seg_sum(x2, b_ref):
    return _dot(x2.astype(BF16), b_ref[...])


def _rope_slabs(x, cos, sin, half):
    lane = lax.broadcasted_iota(jnp.int32, (x.shape[0], LANES), 1)
    first = (lane % (2 * half)) < half
    out = []
    for s in range(x.shape[1] // LANES):
        xs = x[:, s * LANES:(s + 1) * LANES]
        swapped = jnp.where(first, pltpu.roll(xs, LANES - half, 1), pltpu.roll(xs, half, 1))
        out.append(xs * cos + swapped * sin)
    return out[0] if len(out) == 1 else jnp.concatenate(out, axis=1)


def _store_heads(k_ref, k):
    for h in range(k_ref.shape[2]):
        k_ref[0, 0, h] = k[:, h * LANES:(h + 1) * LANES]


def _qkv_kernel(x_ref, mx_ref, mt_ref, win_ref, cc_ref, sc_ref, cd_ref, sd_ref,
                qg_ref, kg_ref, cqg_ref, ckvg_ref, b512_ref, b128_ref,
                pct_ref, wqnt_ref, wqr_ref, prt_ref, wkn_ref, pkr_ref, wvt_ref, pkc_ref, eye_ref,
                qtc_ref, qtd_ref, kc_ref, kd_ref, vtc_ref, vtd_ref,
                *, tm, ctx_len, head_dim, d_qk):
    is_ctx = _is_ctx_rows(tm, ctx_len)
    h = _rms(x_ref[0]) * (1.0 + _mod(mx_ref, mt_ref, 1, is_ctx)) + _mod(mx_ref, mt_ref, 0, is_ctx)
    z = _dot(h.astype(BF16), win_ref[...])
    cos_c, sin_c, cos_d, sin_d = cc_ref[...], sc_ref[...], cd_ref[...], sd_ref[...]
    o_kc, o_vc, o_cq, o_ckv, o_kr = 512, 640, 768, 1152, 1408

    qc = z[:, :o_kc]
    qc = qc * lax.rsqrt(_seg_sum(qc * qc, b512_ref) * (1.0 / head_dim) + EPS) * qg_ref[...]
    qc = _rope_slabs(qc, cos_c, sin_c, head_dim // 2) * (head_dim ** -0.5 * LOG2E)
    qtc_ref[0] = _dot_nt(pct_ref[...], qc.astype(BF16)).astype(BF16)

    kc = z[:, o_kc:o_vc]
    kc = kc * lax.rsqrt(_seg_sum(kc * kc, b128_ref) * (1.0 / head_dim) + EPS) * kg_ref[...]
    kc = _rope_slabs(kc, cos_c, sin_c, head_dim // 2)
    _store_heads(kc_ref, _dot(kc.astype(BF16), pkc_ref[...]).astype(BF16))
    vtc_ref[0, 0] = _dot_nt(eye_ref[...], z[:, o_vc:o_cq].astype(BF16)).astype(BF16)

    cq = (_rms(z[:, o_cq:o_ckv]) * cqg_ref[...]) * (d_qk ** -0.5 * LOG2E)
    cqb = cq.astype(BF16)
    q_rope = _rope_slabs(_dot(cqb, wqr_ref[...]), cos_d, sin_d, 16)
    qtd = _dot_nt(wqnt_ref[...], cqb) + _dot_nt(prt_ref[...], q_rope.astype(BF16))
    qtd_ref[0] = qtd.astype(BF16)

    ckv = (_rms(z[:, o_ckv:o_kr]) * ckvg_ref[...]).astype(BF16)
    kr = _rope_slabs(z[:, o_kr:], cos_d, sin_d, 16)
    _store_heads(kd_ref, (_dot(ckv, wkn_ref[...]) + _dot(kr.astype(BF16), pkr_ref[...])).astype(BF16))
    vtd_ref[0, 0] = _dot_nt(wvt_ref[...], ckv).astype(BF16)


def _qkv(s, mods, w_in, tabs, consts, *, ctx_len, tm):
    nb, lt, d = s.shape
    nch = lt // tm
    tab_spec = pl.BlockSpec((tm, LANES), lambda b, j: (j, 0))
    out_shapes = [
        jax.ShapeDtypeStruct((nb, 8 * LANES, lt), BF16),
        jax.ShapeDtypeStruct((nb, 8 * LANES, lt), BF16),
        jax.ShapeDtypeStruct((nb, nch, 2, tm, LANES), BF16),
        jax.ShapeDtypeStruct((nb, nch, 8, tm, LANES), BF16),
        jax.ShapeDtypeStruct((nb, nch, 2 * 64, tm), BF16),
        jax.ShapeDtypeStruct((nb, nch, 8 * 64, tm), BF16),
    ]
    out_specs = [
        pl.BlockSpec((1, 8 * LANES, tm), lambda b, j: (b, 0, j)),
        pl.BlockSpec((1, 8 * LANES, tm), lambda b, j: (b, 0, j)),
        pl.BlockSpec((1, 1, 2, tm, LANES), lambda b, j: (b, j, 0, 0, 0)),
        pl.BlockSpec((1, 1, 8, tm, LANES), lambda b, j: (b, j, 0, 0, 0)),
        pl.BlockSpec((1, 1, 2 * 64, tm), lambda b, j: (b, j, 0, 0)),
        pl.BlockSpec((1, 1, 8 * 64, tm), lambda b, j: (b, j, 0, 0)),
    ]
    return pl.pallas_call(
        functools.partial(_qkv_kernel, tm=tm, ctx_len=ctx_len, head_dim=64, d_qk=96),
        grid=(nb, nch),
        in_specs=[_row_spec(tm, d)] + _mod_specs(nb, d) + [_full_spec(w_in.shape)] + [tab_spec] * 4
        + [_full_spec(c.shape) for c in consts],
        out_specs=out_specs,
        out_shape=out_shapes,
        compiler_params=_cparams(("parallel", "parallel")),
        name="cd_qkv",
    )(s, mods, mods, w_in, *tabs, *consts)


def _attn_kernel(qt_ref, k_ref, vt_ref, o_ref, s0_ref, s1_ref, c0_ref, c1_ref, m_ref, acc_ref,
                 *, kv_slots, tq, n_ctx, n_all, dv):
    heads = len(kv_slots)

    def scores(j, g, kg, dst):
        s_dst, c_dst = dst
        st = _dot(k_ref[0, j, kg], qt_ref[0, g * LANES:(g + 1) * LANES, :])
        s_dst[g] = st
        c_dst[g] = jnp.max(st, axis=0, keepdims=True)

    def consume(j, src, g, kg):
        s_src, c_src = src
        m_old = m_ref[g]
        m_new = jnp.maximum(m_old, c_src[g])
        alpha = jnp.exp2(m_old - m_new)
        p = jnp.exp2((s_src[g] - m_new).astype(BF16))
        m_ref[g] = m_new
        vt = jnp.concatenate([vt_ref[0, j, kg * dv:(kg + 1) * dv, :], jnp.ones((DEN_ROWS, p.shape[0]), BF16)],
                             axis=0)
        acc_ref[g] = alpha * acc_ref[g] + _dot(vt, p)

    def step(j, j_next, cur, nxt):
        for g, kg in enumerate(kv_slots):
            if j_next is not None:
                scores(j_next, g, kg, nxt)
            consume(j, cur, g, kg)

    def run(n):
        bufs = ((s0_ref, c0_ref), (s1_ref, c1_ref))
        m_ref[...] = jnp.full(m_ref.shape, NEG, F32)
        acc_ref[...] = jnp.zeros(acc_ref.shape, F32)
        for g, kg in enumerate(kv_slots):
            scores(0, g, kg, bufs[0])
        n_main = (n - 1) // ATTN_UNROLL * ATTN_UNROLL

        def trip(i, c):
            for k in range(ATTN_UNROLL):
                j = ATTN_UNROLL * i + k
                step(j, j + 1, bufs[k % 2], bufs[(k + 1) % 2])
            return c

        lax.fori_loop(0, n_main // ATTN_UNROLL, trip, 0)
        for j in range(n_main, n):
            step(j, j + 1 if j + 1 < n else None, bufs[j % 2], bufs[(j + 1) % 2])
        ot = jnp.concatenate([acc_ref[g, :dv, :] / acc_ref[g, dv:dv + 1, :] for g in range(heads)],
                             axis=0)
        o_ref[0] = ot.T.astype(o_ref.dtype)

    is_ctx_tile = pl.program_id(1) * tq < n_ctx * k_ref.shape[3]

    @pl.when(is_ctx_tile)
    def _():
        run(n_ctx)

    @pl.when(jnp.logical_not(is_ctx_tile))
    def _():
        run(n_all)


def _attention(qt, k, vt, *, kv_slots, ctx_len, tq, name):
    nb, hq, lt = qt.shape
    heads = len(kv_slots)
    assert hq == heads * LANES
    nch, n_kv, tk = k.shape[1:4]
    dv = vt.shape[2] // n_kv
    once = pl.Buffered(1)
    return pl.pallas_call(
        functools.partial(_attn_kernel, kv_slots=kv_slots, tq=tq, n_ctx=ctx_len // tk, n_all=nch, dv=dv),
        grid=(nb, lt // tq),
        in_specs=[
            pl.BlockSpec((1, hq, tq), lambda b, q: (b, 0, q)),
            pl.BlockSpec((1, nch, n_kv, tk, LANES), lambda b, q: (b, 0, 0, 0, 0), pipeline_mode=once),
            pl.BlockSpec((1, nch, vt.shape[2], tk), lambda b, q: (b, 0, 0, 0), pipeline_mode=once),
        ],
        out_specs=pl.BlockSpec((1, tq, heads * dv), lambda b, q: (b, q, 0)),
        out_shape=jax.ShapeDtypeStruct((nb, lt, heads * dv), BF16),
        scratch_shapes=[pltpu.VMEM((heads, tk, tq), F32), pltpu.VMEM((heads, tk, tq), F32),
                        pltpu.VMEM((heads, 1, tq), F32), pltpu.VMEM((heads, 1, tq), F32),
                        pltpu.VMEM((heads, 1, tq), F32), pltpu.VMEM((heads, dv + DEN_ROWS, tq), F32)],
        compiler_params=_cparams(("parallel", "arbitrary")),
        name=name,
    )(qt, k, vt)


def _oproj_kernel(oc_ref, od_ref, x_ref, mx_ref, mt_ref, w_ref, r_ref, xo_ref, h2_ref, g_ref,
                  *, tm, ctx_len, n_exp):
    is_ctx = _is_ctx_rows(tm, ctx_len)
    o = jnp.concatenate([oc_ref[0], od_ref[0]], axis=-1)
    xn = x_ref[0] + _mod(mx_ref, mt_ref, 2, is_ctx) * _dot(o, w_ref[...])
    xo_ref[0] = xn
    h2 = _rms(xn) * (1.0 + _mod(mx_ref, mt_ref, 4, is_ctx)) + _mod(mx_ref, mt_ref, 3, is_ctx)
    h2_ref[0] = _pack_pairs(h2)
    h_hi = h2.astype(BF16)
    h_lo = (h2 - h_hi.astype(F32)).astype(BF16)
    logits = _dot(h_hi, r_ref[0]) + _dot(h_lo, r_ref[0]) + _dot(h_hi, r_ref[1])
    lane = lax.broadcasted_iota(jnp.int32, (tm, LANES), 1).astype(F32)
    lg = jnp.where(lane < n_exp, logits, NEG)
    m1 = jnp.max(lg, axis=-1, keepdims=True)
    i1 = jnp.min(jnp.where(lg == m1, lane, float(LANES)), axis=-1, keepdims=True)
    lg2 = jnp.where(lane == i1, NEG, lg)
    m2 = jnp.max(lg2, axis=-1, keepdims=True)
    i2 = jnp.min(jnp.where(lg2 == m2, lane, float(LANES)), axis=-1, keepdims=True)
    e2 = jnp.exp(m2 - m1)
    den = 1.0 + e2
    g_ref[0] = jnp.where(lane == 0.0, i1, jnp.where(lane == 1.0, i2, jnp.where(lane == 2.0, 1.0 / den, e2 / den)))


def _oproj(oc, od, s, mods, w_out, router, *, ctx_len, tm, n_exp):
    nb, lt, d = s.shape
    return pl.pallas_call(
        functools.partial(_oproj_kernel, tm=tm, ctx_len=ctx_len, n_exp=n_exp),
        grid=(nb, lt // tm),
        in_specs=[_row_spec(tm, oc.shape[2]), _row_spec(tm, od.shape[2]), _row_spec(tm, d)]
        + _mod_specs(nb, d) + [_full_spec(w_out.shape), _full_spec(router.shape)],
        out_specs=[_row_spec(tm, d), _row_spec(tm, _packed_width(d)), _row_spec(tm, LANES)],
        out_shape=[jax.ShapeDtypeStruct((nb, lt, d), F32), jax.ShapeDtypeStruct((nb, lt, _packed_width(d)), PACKED),
                   jax.ShapeDtypeStruct((nb, lt, LANES), F32)],
        compiler_params=_cparams(("parallel", "parallel")),
        name="cd_oproj_router",
    )(oc, od, s, mods, mods, w_out, router)


def _sc_gather(table, idx, *, chunk=SC_CHUNK):
    n_rows, d = idx.shape[0], table.shape[1]
    info = plsc.get_sparse_core_info()
    n_workers = info.num_cores * info.num_subcores
    per_w = n_rows // n_workers
    assert n_rows % (n_workers * chunk) == 0 and chunk % 8 == 0 and chunk <= LANES
    mesh = plsc.VectorSubcoreMesh(core_axis_name="c", subcore_axis_name="s")

    n_chunks = per_w // chunk

    def body(table_hbm, idx_hbm, out_hbm, idx_v, rows0, rows1, sem0, sem1):
        wid = lax.axis_index("s") * info.num_cores + lax.axis_index("c")
        base = wid * per_w
        pltpu.sync_copy(idx_hbm.at[pl.ds(base, per_w)], idx_v)
        bufs = ((rows0, sem0), (rows1, sem1))

        def gather(i, buf):
            off = pl.multiple_of(i * chunk, chunk)
            return pltpu.make_async_copy(table_hbm.at[idx_v.at[pl.ds(off, chunk)]], buf[0], buf[1])

        def drain(i, buf):
            gather(i, buf).wait()
            pltpu.sync_copy(buf[0], out_hbm.at[pl.ds(base + pl.multiple_of(i * chunk, chunk), chunk)])

        gather(0, bufs[0]).start()

        @pl.loop(0, n_chunks // 2)
        def _(t):
            i = 2 * t
            gather(i + 1, bufs[1]).start()
            drain(i, bufs[0])

            @pl.when(i + 2 < n_chunks)
            def _():
                gather(i + 2, bufs[0]).start()

            drain(i + 1, bufs[1])

        if n_chunks % 2:
            drain(n_chunks - 1, bufs[0])

    return pl.kernel(
        body,
        out_type=jax.ShapeDtypeStruct((n_rows, d), table.dtype),
        mesh=mesh,
        scratch_types=[pltpu.VMEM((per_w,), jnp.int32), pltpu.VMEM((chunk, d), table.dtype),
                       pltpu.VMEM((chunk, d), table.dtype), pltpu.SemaphoreType.DMA, pltpu.SemaphoreType.DMA],
        name="sc_row_gather",
    )(table, idx)


def _route_plan(route, *, n_exp, ts, align):
    n_tok = route.shape[0]
    experts = route[:, :2].astype(jnp.int32)
    onehot = (experts[:, :, None] == jnp.arange(n_exp)[None, None, :]).astype(jnp.int32).sum(axis=1)
    blk = LANES if n_tok % LANES == 0 else 1
    oh = onehot.astype(F32).reshape(n_tok // blk, blk, n_exp)
    inside = jnp.einsum("ij,bje->bie", jnp.tril(jnp.ones((blk, blk), F32), -1), oh)
    blk_tot = oh.sum(axis=1)
    before = (inside + (jnp.cumsum(blk_tot, axis=0) - blk_tot)[:, None, :]).reshape(n_tok, n_exp)
    before = before.astype(jnp.int32)
    counts = onehot.sum(axis=0)
    gsz = (counts + ts - 1) // ts * ts
    gend = jnp.cumsum(gsz)
    gstart = gend - gsz
    pos = gstart[experts] + jnp.take_along_axis(before, experts, axis=1)
    n_slots = -(-(2 * n_tok + n_exp * ts) // align) * align
    n_tiles = n_slots // ts
    tile_start = jnp.arange(n_tiles, dtype=jnp.int32) * ts
    n_used = (gend[-1] // ts).astype(jnp.int32)
    tile_expert = jnp.minimum(jnp.searchsorted(gend, tile_start, side="right"), n_exp - 1).astype(jnp.int32)
    order = jnp.argsort(experts.reshape(-1), stable=True).astype(jnp.int32)
    slot_e = jnp.repeat(tile_expert, ts)
    rank = jnp.arange(n_slots, dtype=jnp.int32) - gstart[slot_e]
    cstart = jnp.cumsum(counts) - counts
    src = jnp.clip(cstart[slot_e] + rank, 0, 2 * n_tok - 1)
    tok = jnp.where(rank < counts[slot_e], order[src] // 2, 0).astype(jnp.int32)
    last_used = tile_expert[jnp.maximum(n_used - 1, 0)]
    tile_expert = jnp.where(jnp.arange(n_tiles) < n_used, tile_expert, last_used)
    return tok, pos.T.reshape(-1).astype(jnp.int32), tile_expert, n_used.reshape(1)


def _gffn_kernel(te_ref, nu_ref, h_ref, wg_ref, wu_ref, wd_ref, o_ref):
    used = pl.program_id(0) < nu_ref[0]

    @pl.when(used)
    def _():
        h = _unpack_pairs(h_ref[...]).astype(BF16)
        o_ref[...] = _pack_pairs(_swiglu_rows(h, wg_ref.at[0], wu_ref.at[0], wd_ref.at[0]))

    @pl.when(jnp.logical_not(used))
    def _():
        o_ref[...] = jnp.zeros(o_ref.shape, o_ref.dtype)


def _grouped_ffn(hs, tile_expert, n_used, w_gate, w_up, w_down, *, layer, ts):
    n_slots, dp = hs.shape
    d, f = w_gate.shape[2:]
    once = pl.Buffered(1)
    grid_spec = pltpu.PrefetchScalarGridSpec(
        num_scalar_prefetch=2,
        grid=(n_slots // ts,),
        in_specs=[
            pl.BlockSpec((ts, dp), lambda j, te, nu: (j, 0)),
            pl.BlockSpec((1, 1, d, f), lambda j, te, nu: (layer, te[j], 0, 0), pipeline_mode=once),
            pl.BlockSpec((1, 1, d, f), lambda j, te, nu: (layer, te[j], 0, 0), pipeline_mode=once),
            pl.BlockSpec((1, 1, f, d), lambda j, te, nu: (layer, te[j], 0, 0), pipeline_mode=once),
        ],
        out_specs=pl.BlockSpec((ts, dp), lambda j, te, nu: (j, 0)),
    )
    return pl.pallas_call(
        _gffn_kernel,
        grid_spec=grid_spec,
        out_shape=jax.ShapeDtypeStruct((n_slots, dp), PACKED),
        compiler_params=_cparams(("arbitrary",)),
        name="moe_grouped_ffn",
    )(tile_expert, n_used, hs, w_gate, w_up, w_down)


def _combine_kernel(y1_ref, y2_ref, r_ref, x_ref, mx_ref, mt_ref, fg_ref, o_ref, *, tm, ctx_len, row0, final):
    pos = (pl.program_id(1) + row0) * tm + lax.broadcasted_iota(jnp.int32, (tm, 1), 0)
    is_ctx = pos < ctx_len
    r = r_ref[0]
    f = r[:, 2:3] * _unpack_pairs(y1_ref[0, 0]) + r[:, 3:4] * _unpack_pairs(y2_ref[0, 0])
    xn = x_ref[0] + _mod(mx_ref, mt_ref, 5, is_ctx) * f
    o_ref[0] = _rms(xn) * fg_ref[...] if final else xn


def _combine(yg, route, s, mods, final_gain, *, ctx_len, tm, final):
    nb, lt, d = s.shape
    row0 = ctx_len // tm if final else 0
    rows = lt - row0 * tm

    def rspec(width):
        return pl.BlockSpec((1, tm, width), lambda b, j: (b, j + row0, 0))

    return pl.pallas_call(
        functools.partial(_combine_kernel, tm=tm, ctx_len=ctx_len, row0=row0, final=final),
        grid=(nb, rows // tm),
        in_specs=[pl.BlockSpec((1, 1, tm, yg.shape[-1]), lambda b, j: (0, b, j + row0, 0)),
                  pl.BlockSpec((1, 1, tm, yg.shape[-1]), lambda b, j: (1, b, j + row0, 0)),
                  rspec(LANES), rspec(d)] + _mod_specs(nb, d) + [_full_spec((1, d))],
        out_specs=_row_spec(tm, d),
        out_shape=jax.ShapeDtypeStruct((nb, rows, d), F32),
        compiler_params=_cparams(("parallel", "parallel")),
        name="moe_combine",
    )(yg, yg, route, s, mods, mods, final_gain.reshape(1, d))


def _moe(h2, route, s, mods, w_gate, w_up, w_down, final_gain, *, layer, ctx_len, n_exp, ts, tm, final):
    nb, lt, d = s.shape
    n_tok = nb * lt
    info = plsc.get_sparse_core_info()
    sc_rows = info.num_cores * info.num_subcores * SC_CHUNK
    tok, pos, tile_expert, n_used = _route_plan(route.reshape(n_tok, LANES), n_exp=n_exp, ts=ts,
                                                align=int(np.lcm(ts, sc_rows)))
    hs = _sc_gather(h2.reshape(n_tok, h2.shape[-1]), tok)
    ys = _grouped_ffn(hs, tile_expert, n_used, w_gate, w_up, w_down, layer=layer, ts=ts)
    yg = _sc_gather(ys, pos).reshape(2, nb, lt, ys.shape[-1])
    return _combine(yg, route, s, mods, final_gain, ctx_len=ctx_len, tm=tm, final=final)


def _final_kernel(x_ref, g_ref, o_ref):
    o_ref[0] = _rms(x_ref[0]) * g_ref[...]


def _final(s, gain, *, ctx_len, tm):
    nb, lt, d = s.shape
    seq = lt - ctx_len
    off = ctx_len // tm
    return pl.pallas_call(
        _final_kernel,
        grid=(nb, seq // tm),
        in_specs=[pl.BlockSpec((1, tm, d), lambda b, j: (b, j + off, 0)), _full_spec((1, d))],
        out_specs=_row_spec(tm, d),
        out_shape=jax.ShapeDtypeStruct((nb, seq, d), F32),
        compiler_params=_cparams(("parallel", "parallel")),
        name="final_norm",
    )(s, gain.reshape(1, d))


def _rope_tables(seq, ctx_len):
    rows = seq // GRID_W
    row = jnp.repeat(jnp.arange(rows), GRID_W).astype(F32)
    col = jnp.tile(jnp.arange(GRID_W), rows).astype(F32)

    def table(rot_dim, reps):
        axis_dim = rot_dim // 2
        inv = ROPE_THETA ** (-jnp.arange(0, axis_dim, 2, dtype=F32) / axis_dim)
        ang = jnp.concatenate([row[:, None] * inv, col[:, None] * inv], axis=-1)
        cos = jnp.concatenate([jnp.ones((ctx_len, axis_dim), F32), jnp.cos(ang)], axis=0)
        sin = jnp.concatenate([jnp.zeros((ctx_len, axis_dim), F32), jnp.sin(ang)], axis=0)
        return (jnp.tile(jnp.concatenate([cos, cos], axis=-1), (1, reps)),
                jnp.tile(jnp.concatenate([-sin, sin], axis=-1), (1, reps)))

    cos_c, sin_c = table(64, 2)
    cos_d, sin_d = table(32, 4)
    return cos_c, sin_c, cos_d, sin_d


def _placement(n_rows, n_cols, pairs):
    m = np.zeros((n_rows, n_cols), np.float32)
    for r, c in pairs:
        m[r, c] = 1.0
    return jnp.asarray(m, BF16)


def _cd_consts(q_gain, k_gain, cq_gain, ckv_gain, w_uq, w_ukv):
    hd, nh, nkv, d_nope, d_rope, d_v = 64, 8, 2, 64, 32, 64
    d_qk = d_nope + d_rope
    heads = np.arange(nh)
    b512 = jnp.asarray(np.kron(np.eye(nh), np.ones((hd, hd))), BF16)
    b128 = jnp.asarray(np.kron(np.eye(nkv), np.ones((hd, hd))), BF16)
    pct = _placement(nh * LANES, nh * hd, [(h * LANES + i, h * hd + i) for h in heads for i in range(hd)])
    nope_cols = (heads[:, None] * d_qk + np.arange(d_nope)[None, :]).reshape(-1)
    rope_cols = (heads[:, None] * d_qk + d_nope + np.arange(d_rope)[None, :]).reshape(-1)
    nope_rows = (heads[:, None] * LANES + np.arange(d_nope)[None, :]).reshape(-1)
    wqnt = jnp.zeros((nh * LANES, w_uq.shape[0]), F32).at[nope_rows].set(w_uq[:, nope_cols].T).astype(BF16)
    wqr = w_uq[:, rope_cols].astype(BF16)
    prt = _placement(nh * LANES, nh * d_rope,
                     [(h * LANES + d_nope + i, h * d_rope + i) for h in heads for i in range(d_rope)])
    kn_cols = (heads[:, None] * (d_nope + d_v) + np.arange(d_nope)[None, :]).reshape(-1)
    v_cols = (heads[:, None] * (d_nope + d_v) + d_nope + np.arange(d_v)[None, :]).reshape(-1)
    wkn = jnp.zeros((w_ukv.shape[0], nh * LANES), F32).at[:, nope_rows].set(w_ukv[:, kn_cols]).astype(BF16)
    pkr = _placement(LANES, nh * LANES, [(i, h * LANES + d_nope + i) for h in heads for i in range(d_rope)])
    wvt = w_ukv[:, v_cols].T.astype(BF16)
    pkc = _placement(nkv * hd, nkv * LANES, [(h * hd + i, h * LANES + i) for h in range(nkv) for i in range(hd)])
    eye = jnp.asarray(np.eye(nkv * hd), BF16)
    return [jnp.tile(q_gain, nh).reshape(1, -1), jnp.tile(k_gain, nkv).reshape(1, -1),
            cq_gain.reshape(1, -1), ckv_gain.reshape(1, -1), b512, b128,
            pct, wqnt, wqr, prt, wkn, pkr, wvt, pkc, eye]


def _largest_tile(n, candidates):
    for t in candidates:
        if n % t == 0:
            return t
    raise ValueError(f"no tile in {candidates} divides {n}")


def kernel(x, c, ctx, c_ctx, ada_w, ada_b, ab_w_in, a_v_gain, a_w_s, a_b_s, b_w_pool, b_scale, ab_w_out,
           ffn_w_gate, ffn_w_up, ffn_w_down, cd_w_in, c_q_gain, c_k_gain, d_cq_gain, d_ckv_gain, d_w_uq,
           d_w_ukv, cd_w_out, moe_router, moe_w_gate, moe_w_up, moe_w_down, final_gain):
    nb, seq, d = x.shape
    ctx_len = ctx.shape[1]
    lt = ctx_len + seq
    depth = ada_w.shape[0]
    n_exp = moe_router.shape[-1]
    chunk = a_w_s.shape[-1]
    tr = 256
    assert ctx_len % tr == 0 and seq % tr == 0 and seq % GRID_W == 0
    tm_ff = _largest_tile(lt, (768, 384, 256))
    assert ffn_w_gate.shape[-1] % FF_CHUNK == 0 and moe_w_gate.shape[-1] % FF_CHUNK == 0

    s = jnp.concatenate([ctx, x], axis=1)

    pad = -(nb + 1) % 8
    cvec = jnp.concatenate([c, c_ctx[None, :], jnp.zeros((pad, d), F32)], axis=0)
    mods_all = _ada_all(cvec, ada_w, ada_b).reshape(depth, nb + 1 + pad, 6, d)
    tabs = _rope_tables(seq, ctx_len)
    moe_wg, moe_wu, moe_wd = moe_w_gate, moe_w_up, moe_w_down

    for layer in range(depth):
        i = layer // 2
        mods = mods_all[layer]
        if layer % 2 == 0:
            bs_full = jnp.repeat(a_b_s[i].T, LANES, axis=1)
            s, h2 = _mixer(s, mods, ab_w_in[i].astype(BF16), a_v_gain[i].reshape(1, -1),
                           a_w_s[i].astype(BF16), bs_full, b_w_pool[i].astype(BF16), b_scale[i].reshape(1, -1),
                           ab_w_out[i].astype(BF16), ctx_len=ctx_len, tm=tr, chunk=chunk)
            s = _swiglu(h2, s, mods, ffn_w_gate[i][None].astype(BF16), ffn_w_up[i][None].astype(BF16),
                        ffn_w_down[i][None].astype(BF16), ctx_len=ctx_len, tm=tm_ff)
        else:
            w_in = jnp.pad(cd_w_in[i], ((0, 0), (0, 12 * LANES - cd_w_in.shape[-1]))).astype(BF16)
            consts = _cd_consts(c_q_gain[i], c_k_gain[i], d_cq_gain[i], d_ckv_gain[i], d_w_uq[i], d_w_ukv[i])
            qtc, qtd, kc, kd, vtc, vtd = _qkv(s, mods, w_in, tabs, consts, ctx_len=ctx_len, tm=tr)
            oc = _attention(qtc, kc, vtc, kv_slots=(0, 0, 0, 0, 1, 1, 1, 1), ctx_len=ctx_len, tq=tr,
                            name="attn_gqa")
            od = _attention(qtd, kd, vtd, kv_slots=tuple(range(8)), ctx_len=ctx_len, tq=tr, name="attn_mla")
            router = jnp.pad(moe_router[i], ((0, 0), (0, LANES - n_exp)))
            r_hi = router.astype(BF16)
            router = jnp.stack([r_hi, (router - r_hi.astype(F32)).astype(BF16)])
            s, h2, route = _oproj(oc, od, s, mods, cd_w_out[i].astype(BF16), router,
                                  ctx_len=ctx_len, tm=tr, n_exp=n_exp)
            s = _moe(h2, route, s, mods, moe_wg, moe_wu, moe_wd, final_gain, layer=i, ctx_len=ctx_len,
                     n_exp=n_exp, ts=512, tm=tr, final=layer == depth - 1)

    return s if depth % 2 == 0 else _final(s, final_gain, ctx_len=ctx_len, tm=tr)
```

```python
import functools

import numpy as np
import jax
import jax.numpy as jnp
from jax import lax
from jax.experimental import pallas as pl
from jax.experimental.pallas import tpu as pltpu
from jax.experimental.pallas import tpu_sc as plsc

F32 = jnp.float32
BF16 = jnp.bfloat16
EPS = 1e-6
ROPE_THETA = 10000.0
GRID_W = 64
LANES = 128
HALO = 8
POOL_WINDOWS = (2, 4, 8, 16)
NEG = -1e30
LOG2E = 1.4426950408889634
ATTN_UNROLL = 16
FF_CHUNK = 256
DEN_ROWS = 16
SC_CHUNK = 64
VMEM_LIMIT = 56 * 1024 * 1024


def _cparams(sem):
    return pltpu.CompilerParams(dimension_semantics=sem, vmem_limit_bytes=VMEM_LIMIT)


def _rms(x):
    return x * lax.rsqrt(jnp.mean(x * x, axis=-1, keepdims=True) + EPS)


def _is_ctx_rows(tm, ctx_len):
    pos = pl.program_id(1) * tm + lax.broadcasted_iota(jnp.int32, (tm, 1), 0)
    return pos < ctx_len


def _mod(mx_ref, mt_ref, idx, is_ctx):
    return jnp.where(is_ctx, mt_ref[0, idx:idx + 1, :], mx_ref[0, idx:idx + 1, :])


def _dot(a, b):
    return jnp.dot(a, b, preferred_element_type=F32)


def _dot_nt(a, b):
    return lax.dot_general(a, b, (((1,), (1,)), ((), ())), preferred_element_type=F32)


def _silu(x):
    return x * jax.nn.sigmoid(x)


PACKED = jnp.uint32


def _packed_width(d):
    return d // 2


def _pack_pairs(x):
    w = x.shape[1] // 2
    xb = x.astype(BF16).astype(F32)
    hi = pltpu.bitcast(xb[:, :w], jnp.uint32)
    lo = pltpu.bitcast(xb[:, w:], jnp.uint32)
    return hi | (lo >> 16)


def _unpack_pairs(p):
    hi = pltpu.bitcast(p & jnp.uint32(0xFFFF0000), F32)
    lo = pltpu.bitcast(p << 16, F32)
    return jnp.concatenate([hi, lo], axis=1)


def _ada_kernel(c_ref, w_ref, b_ref, o_ref):
    c = c_ref[...]
    o_ref[0] = jnp.dot(_silu(c), w_ref[0], preferred_element_type=F32,
                       precision=lax.Precision.HIGHEST) + b_ref[0]


def _ada_all(cvec, ada_w, ada_b):
    depth, d, n = ada_w.shape
    rows = cvec.shape[0]
    tn = 1536
    return pl.pallas_call(
        _ada_kernel,
        grid=(depth, n // tn),
        in_specs=[
            pl.BlockSpec((rows, d), lambda l, j: (0, 0)),
            pl.BlockSpec((1, d, tn), lambda l, j: (l, 0, j)),
            pl.BlockSpec((1, 1, tn), lambda l, j: (l, 0, j)),
        ],
        out_specs=pl.BlockSpec((1, rows, tn), lambda l, j: (l, 0, j)),
        out_shape=jax.ShapeDtypeStruct((depth, rows, n), F32),
        compiler_params=_cparams(("arbitrary", "arbitrary")),
        name="adaln",
    )(cvec, ada_w, ada_b.reshape(depth, 1, n))


def _row_spec(tm, width):
    return pl.BlockSpec((1, tm, width), lambda b, j, *_: (b, j, 0))


def _mod_specs(nb, d):
    return [pl.BlockSpec((1, 6, d), lambda b, j, *_: (b, 0, 0)),
            pl.BlockSpec((1, 6, d), lambda b, j, *_: (nb, 0, 0))]


def _full_spec(shape):
    nd = len(shape)
    return pl.BlockSpec(shape, lambda *_: (0,) * nd)


def _mixer_kernel(x_ref, xp_ref, xn_ref, mx_ref, mt_ref, win_ref, vg_ref, ws_ref, bs_ref, wp_ref,
                  psc_ref, wo_ref, xo_ref, h2_ref, *, tm, ctx_len, lt, chunk):
    is_ctx = _is_ctx_rows(tm, ctx_len)
    pos0 = pl.program_id(1) * tm
    aw = 4 * LANES
    sc1 = 1.0 + _mod(mx_ref, mt_ref, 1, is_ctx)
    sh1 = _mod(mx_ref, mt_ref, 0, is_ctx)
    z = _dot((_rms(x_ref[0]) * sc1 + sh1).astype(BF16), win_ref[...])
    halo_x = jnp.concatenate([xp_ref[0], xn_ref[0]], axis=0)
    halo_h = _rms(halo_x) * sc1[:2 * HALO] + sh1[:2 * HALO]
    halo_z = _dot(halo_h.astype(BF16), win_ref[:, 2 * aw:])
    za = z[:, :2 * aw]
    gl = 0.5 * za * (1.0 + lax.erf(za * (2.0 ** -0.5)))
    u = gl[:, :aw]
    v = gl[:, aw:]
    vn = (_rms(v) * vg_ref[...]).astype(BF16)
    ya_rows = []
    for c in range(tm // chunk):
        rs = slice(c * chunk, (c + 1) * chunk)
        cols = []
        for g in range(4):
            cs = slice(g * LANES, (g + 1) * LANES)
            sg = _dot(ws_ref[g], vn[rs, cs]) + bs_ref[:, cs]
            cols.append(u[rs, cs] * sg)
        ya_rows.append(jnp.concatenate(cols, axis=1))
    ya = jnp.concatenate(ya_rows, axis=0)

    in_ctx = pos0 < ctx_len
    seq_start = jnp.where(in_ctx, 0, ctx_len)
    seq_len = jnp.where(in_ctx, ctx_len, lt - ctx_len)
    p_in = pos0 - seq_start + lax.broadcasted_iota(jnp.int32, (tm, 1), 0)
    has_prev = jnp.logical_and(pos0 != 0, pos0 != ctx_len).astype(F32)
    has_next = jnp.logical_and(pos0 + tm != ctx_len, pos0 + tm != lt).astype(F32)
    zp = z[:, 2 * aw:]
    ext = jnp.concatenate([halo_z[:HALO] * has_prev, zp, halo_z[HALO:] * has_next], axis=0)
    n = tm + 2 * HALO
    yb_cols = []
    for g, w in enumerate(POOL_WINDOWS):
        cs = slice(g * LANES, (g + 1) * LANES)
        a = ext[:, cs]
        step = 1
        while step < w:
            a = a + pltpu.roll(a, n - step, 0)
            step *= 2
        shift = HALO - w // 2
        if shift:
            a = pltpu.roll(a, n - shift, 0)
        wsum = a[:tm]
        lo = jnp.clip(p_in - w // 2, 0, seq_len)
        hi = jnp.clip(p_in - w // 2 + w, 0, seq_len)
        pooled = wsum / (hi - lo).astype(F32) - zp[:, cs]
        yb_cols.append(_dot(pooled.astype(BF16), wp_ref[g]) * psc_ref[:, cs])
    y = jnp.concatenate([ya] + yb_cols, axis=1).astype(BF16)
    xn = x_ref[0] + _mod(mx_ref, mt_ref, 2, is_ctx) * _dot(y, wo_ref[...])
    xo_ref[0] = xn
    h2 = _rms(xn) * (1.0 + _mod(mx_ref, mt_ref, 4, is_ctx)) + _mod(mx_ref, mt_ref, 3, is_ctx)
    h2_ref[0] = h2.astype(BF16)


def _mixer(s, mods, w_in, v_gain, w_s, bs_full, w_pool, p_scale, w_out, *, ctx_len, tm, chunk):
    nb, lt, d = s.shape
    aw = 4 * LANES
    nh = lt // HALO
    th = tm // HALO
    pw = 4 * LANES
    halo_prev = pl.BlockSpec((1, HALO, d), lambda b, j: (b, jnp.maximum(j * th - 1, 0), 0))
    halo_next = pl.BlockSpec((1, HALO, d), lambda b, j: (b, jnp.minimum((j + 1) * th, nh - 1), 0))
    return pl.pallas_call(
        functools.partial(_mixer_kernel, tm=tm, ctx_len=ctx_len, lt=lt, chunk=chunk),
        grid=(nb, lt // tm),
        in_specs=[_row_spec(tm, d), halo_prev, halo_next] + _mod_specs(nb, d) + [
            _full_spec(w_in.shape), _full_spec((1, aw)), _full_spec(w_s.shape), _full_spec(bs_full.shape),
            _full_spec(w_pool.shape), _full_spec((1, pw)), _full_spec(w_out.shape)],
        out_specs=[_row_spec(tm, d), _row_spec(tm, d)],
        out_shape=[jax.ShapeDtypeStruct((nb, lt, d), F32), jax.ShapeDtypeStruct((nb, lt, d), BF16)],
        compiler_params=_cparams(("parallel", "parallel")),
        name="ab_mixer",
    )(s, s, s, mods, mods, w_in, v_gain, w_s, bs_full, w_pool, p_scale, w_out)


def _swiglu_rows(h, wg_ref, wu_ref, wd_ref):
    out = None
    for c in range(wg_ref.shape[2] // FF_CHUNK):
        cs = slice(c * FF_CHUNK, (c + 1) * FF_CHUNK)
        g = _dot(h, wg_ref[0, :, cs].astype(BF16))
        a = (_silu(g) * _dot(h, wu_ref[0, :, cs].astype(BF16))).astype(BF16)
        part = _dot(a, wd_ref[0, cs, :].astype(BF16))
        out = part if out is None else out + part
    return out


def _swiglu_kernel(h_ref, x_ref, mx_ref, mt_ref, wg_ref, wu_ref, wd_ref, o_ref, *, tm, ctx_len):
    is_ctx = _is_ctx_rows(tm, ctx_len)
    f = _swiglu_rows(h_ref[0], wg_ref, wu_ref, wd_ref)
    o_ref[0] = x_ref[0] + _mod(mx_ref, mt_ref, 5, is_ctx) * f


def _swiglu(h2, s, mods, w_gate, w_up, w_down, *, ctx_len, tm):
    nb, lt, d = s.shape
    f = w_gate.shape[2]
    once = pl.Buffered(1)
    return pl.pallas_call(
        functools.partial(_swiglu_kernel, tm=tm, ctx_len=ctx_len),
        grid=(nb, lt // tm),
        in_specs=[_row_spec(tm, d), _row_spec(tm, d)] + _mod_specs(nb, d) + [
            pl.BlockSpec((1, d, f), lambda b, j: (0, 0, 0), pipeline_mode=once),
            pl.BlockSpec((1, d, f), lambda b, j: (0, 0, 0), pipeline_mode=once),
            pl.BlockSpec((1, f, d), lambda b, j: (0, 0, 0), pipeline_mode=once)],
        out_specs=_row_spec(tm, d),
        out_shape=jax.ShapeDtypeStruct((nb, lt, d), F32),
        compiler_params=_cparams(("parallel", "parallel")),
        name="swiglu",
    )(h2, s, mods, mods, w_gate, w_up, w_down)


def _seg_sum(x2, b_ref):
    return _dot(x2.astype(BF16), b_ref[...])


def _rope_slabs(x, cos, sin, half):
    lane = lax.broadcasted_iota(jnp.int32, (x.shape[0], LANES), 1)
    first = (lane % (2 * half)) < half
    out = []
    for s in range(x.shape[1] // LANES):
        xs = x[:, s * LANES:(s + 1) * LANES]
        swapped = jnp.where(first, pltpu.roll(xs, LANES - half, 1), pltpu.roll(xs, half, 1))
        out.append(xs * cos + swapped * sin)
    return out[0] if len(out) == 1 else jnp.concatenate(out, axis=1)


def _store_heads(k_ref, k):
    for h in range(k_ref.shape[2]):
        k_ref[0, 0, h] = k[:, h * LANES:(h + 1) * LANES]


def _qkv_kernel(x_ref, mx_ref, mt_ref, win_ref, cc_ref, sc_ref, cd_ref, sd_ref,
                qg_ref, kg_ref, cqg_ref, ckvg_ref, b512_ref, b128_ref,
                pct_ref, wqnt_ref, wqr_ref, prt_ref, wkn_ref, pkr_ref, wvt_ref, pkc_ref, eye_ref,
                qtc_ref, qtd_ref, kc_ref, kd_ref, vtc_ref, vtd_ref,
                *, tm, ctx_len, head_dim, d_qk):
    is_ctx = _is_ctx_rows(tm, ctx_len)
    h = _rms(x_ref[0]) * (1.0 + _mod(mx_ref, mt_ref, 1, is_ctx)) + _mod(mx_ref, mt_ref, 0, is_ctx)
    z = _dot(h.astype(BF16), win_ref[...])
    cos_c, sin_c, cos_d, sin_d = cc_ref[...], sc_ref[...], cd_ref[...], sd_ref[...]
    o_kc, o_vc, o_cq, o_ckv, o_kr = 512, 640, 768, 1152, 1408

    qc = z[:, :o_kc]
    qc = qc * lax.rsqrt(_seg_sum(qc * qc, b512_ref) * (1.0 / head_dim) + EPS) * qg_ref[...]
    qc = _rope_slabs(qc, cos_c, sin_c, head_dim // 2) * (head_dim ** -0.5 * LOG2E)
    qtc_ref[0] = _dot_nt(pct_ref[...], qc.astype(BF16)).astype(BF16)

    kc = z[:, o_kc:o_vc]
    kc = kc * lax.rsqrt(_seg_sum(kc * kc, b128_ref) * (1.0 / head_dim) + EPS) * kg_ref[...]
    kc = _rope_slabs(kc, cos_c, sin_c, head_dim // 2)
    _store_heads(kc_ref, _dot(kc.astype(BF16), pkc_ref[...]).astype(BF16))
    vtc_ref[0, 0] = _dot_nt(eye_ref[...], z[:, o_vc:o_cq].astype(BF16)).astype(BF16)

    cq = (_rms(z[:, o_cq:o_ckv]) * cqg_ref[...]) * (d_qk ** -0.5 * LOG2E)
    cqb = cq.astype(BF16)
    q_rope = _rope_slabs(_dot(cqb, wqr_ref[...]), cos_d, sin_d, 16)
    qtd = _dot_nt(wqnt_ref[...], cqb) + _dot_nt(prt_ref[...], q_rope.astype(BF16))
    qtd_ref[0] = qtd.astype(BF16)

    ckv = (_rms(z[:, o_ckv:o_kr]) * ckvg_ref[...]).astype(BF16)
    kr = _rope_slabs(z[:, o_kr:], cos_d, sin_d, 16)
    _store_heads(kd_ref, (_dot(ckv, wkn_ref[...]) + _dot(kr.astype(BF16), pkr_ref[...])).astype(BF16))
    vtd_ref[0, 0] = _dot_nt(wvt_ref[...], ckv).astype(BF16)


def _qkv(s, mods, w_in, tabs, consts, *, ctx_len, tm):
    nb, lt, d = s.shape
    nch = lt // tm
    tab_spec = pl.BlockSpec((tm, LANES), lambda b, j: (j, 0))
    out_shapes = [
        jax.ShapeDtypeStruct((nb, 8 * LANES, lt), BF16),
        jax.ShapeDtypeStruct((nb, 8 * LANES, lt), BF16),
        jax.ShapeDtypeStruct((nb, nch, 2, tm, LANES), BF16),
        jax.ShapeDtypeStruct((nb, nch, 8, tm, LANES), BF16),
        jax.ShapeDtypeStruct((nb, nch, 2 * 64, tm), BF16),
        jax.ShapeDtypeStruct((nb, nch, 8 * 64, tm), BF16),
    ]
    out_specs = [
        pl.BlockSpec((1, 8 * LANES, tm), lambda b, j: (b, 0, j)),
        pl.BlockSpec((1, 8 * LANES, tm), lambda b, j: (b, 0, j)),
        pl.BlockSpec((1, 1, 2, tm, LANES), lambda b, j: (b, j, 0, 0, 0)),
        pl.BlockSpec((1, 1, 8, tm, LANES), lambda b, j: (b, j, 0, 0, 0)),
        pl.BlockSpec((1, 1, 2 * 64, tm), lambda b, j: (b, j, 0, 0)),
        pl.BlockSpec((1, 1, 8 * 64, tm), lambda b, j: (b, j, 0, 0)),
    ]
    return pl.pallas_call(
        functools.partial(_qkv_kernel, tm=tm, ctx_len=ctx_len, head_dim=64, d_qk=96),
        grid=(nb, nch),
        in_specs=[_row_spec(tm, d)] + _mod_specs(nb, d) + [_full_spec(w_in.shape)] + [tab_spec] * 4
        + [_full_spec(c.shape) for c in consts],
        out_specs=out_specs,
        out_shape=out_shapes,
        compiler_params=_cparams(("parallel", "parallel")),
        name="cd_qkv",
    )(s, mods, mods, w_in, *tabs, *consts)


def _attn_kernel(qt_ref, k_ref, vt_ref, o_ref, s0_ref, s1_ref, c0_ref, c1_ref, m_ref, acc_ref,
                 *, kv_slots, tq, n_ctx, n_all, dv):
    heads = len(kv_slots)

    def scores(j, g, kg, dst):
        s_dst, c_dst = dst
        st = _dot(k_ref[0, j, kg], qt_ref[0, g * LANES:(g + 1) * LANES, :])
        s_dst[g] = st
        c_dst[g] = jnp.max(st, axis=0, keepdims=True)

    def consume(j, src, g, kg):
        s_src, c_src = src
        m_old = m_ref[g]
        m_new = jnp.maximum(m_old, c_src[g])
        alpha = jnp.exp2(m_old - m_new)
        p = jnp.exp2((s_src[g] - m_new).astype(BF16))
        m_ref[g] = m_new
        vt = jnp.concatenate([vt_ref[0, j, kg * dv:(kg + 1) * dv, :], jnp.ones((DEN_ROWS, p.shape[0]), BF16)],
                             axis=0)
        acc_ref[g] = alpha * acc_ref[g] + _dot(vt, p)

    def step(j, j_next, cur, nxt):
        for g, kg in enumerate(kv_slots):
            if j_next is not None:
                scores(j_next, g, kg, nxt)
            consume(j, cur, g, kg)

    def run(n):
        bufs = ((s0_ref, c0_ref), (s1_ref, c1_ref))
        m_ref[...] = jnp.full(m_ref.shape, NEG, F32)
        acc_ref[...] = jnp.zeros(acc_ref.shape, F32)
        for g, kg in enumerate(kv_slots):
            scores(0, g, kg, bufs[0])
        n_main = (n - 1) // ATTN_UNROLL * ATTN_UNROLL

        def trip(i, c):
            for k in range(ATTN_UNROLL):
                j = ATTN_UNROLL * i + k
                step(j, j + 1, bufs[k % 2], bufs[(k + 1) % 2])
            return c

        lax.fori_loop(0, n_main // ATTN_UNROLL, trip, 0)
        for j in range(n_main, n):
            step(j, j + 1 if j + 1 < n else None, bufs[j % 2], bufs[(j + 1) % 2])
        ot = jnp.concatenate([acc_ref[g, :dv, :] / acc_ref[g, dv:dv + 1, :] for g in range(heads)],
                             axis=0)
        o_ref[0] = ot.T.astype(o_ref.dtype)

    is_ctx_tile = pl.program_id(2) * tq < n_ctx * k_ref.shape[3]

    @pl.when(is_ctx_tile)
    def _():
        run(n_ctx)

    @pl.when(jnp.logical_not(is_ctx_tile))
    def _():
        run(n_all)


def _attention(qt, k, vt, *, kv_slots, groups, ctx_len, tq, name):
    nb, hq, lt = qt.shape
    heads = len(kv_slots)
    assert hq == groups * heads * LANES
    nch, n_kv, tk = k.shape[1:4]
    dv = vt.shape[2] // n_kv
    kvb = n_kv // groups
    once = pl.Buffered(1)
    return pl.pallas_call(
        functools.partial(_attn_kernel, kv_slots=kv_slots, tq=tq, n_ctx=ctx_len // tk, n_all=nch, dv=dv),
        grid=(nb, groups, lt // tq),
        in_specs=[
            pl.BlockSpec((1, heads * LANES, tq), lambda b, p, q: (b, p, q)),
            pl.BlockSpec((1, nch, kvb, tk, LANES), lambda b, p, q: (b, 0, p, 0, 0), pipeline_mode=once),
            pl.BlockSpec((1, nch, kvb * dv, tk), lambda b, p, q: (b, 0, p, 0), pipeline_mode=once),
        ],
        out_specs=pl.BlockSpec((1, tq, heads * dv), lambda b, p, q: (b, q, p)),
        out_shape=jax.ShapeDtypeStruct((nb, lt, groups * heads * dv), BF16),
        scratch_shapes=[pltpu.VMEM((heads, tk, tq), F32), pltpu.VMEM((heads, tk, tq), F32),
                        pltpu.VMEM((heads, 1, tq), F32), pltpu.VMEM((heads, 1, tq), F32),
                        pltpu.VMEM((heads, 1, tq), F32), pltpu.VMEM((heads, dv + DEN_ROWS, tq), F32)],
        compiler_params=_cparams(("parallel", "parallel", "arbitrary")),
        name=name,
    )(qt, k, vt)


def _oproj_kernel(oc_ref, od_ref, x_ref, mx_ref, mt_ref, w_ref, r_ref, xo_ref, h2_ref, g_ref,
                  *, tm, ctx_len, n_exp):
    is_ctx = _is_ctx_rows(tm, ctx_len)
    o = jnp.concatenate([oc_ref[0], od_ref[0]], axis=-1)
    xn = x_ref[0] + _mod(mx_ref, mt_ref, 2, is_ctx) * _dot(o, w_ref[...])
    xo_ref[0] = xn
    h2 = _rms(xn) * (1.0 + _mod(mx_ref, mt_ref, 4, is_ctx)) + _mod(mx_ref, mt_ref, 3, is_ctx)
    h2_ref[0] = _pack_pairs(h2)
    h_hi = h2.astype(BF16)
    h_lo = (h2 - h_hi.astype(F32)).astype(BF16)
    logits = _dot(h_hi, r_ref[0]) + _dot(h_lo, r_ref[0]) + _dot(h_hi, r_ref[1])
    lane = lax.broadcasted_iota(jnp.int32, (tm, LANES), 1).astype(F32)
    lg = jnp.where(lane < n_exp, logits, NEG)
    m1 = jnp.max(lg, axis=-1, keepdims=True)
    i1 = jnp.min(jnp.where(lg == m1, lane, float(LANES)), axis=-1, keepdims=True)
    lg2 = jnp.where(lane == i1, NEG, lg)
    m2 = jnp.max(lg2, axis=-1, keepdims=True)
    i2 = jnp.min(jnp.where(lg2 == m2, lane, float(LANES)), axis=-1, keepdims=True)
    e2 = jnp.exp(m2 - m1)
    den = 1.0 + e2
    g_ref[0] = jnp.where(lane == 0.0, i1, jnp.where(lane == 1.0, i2, jnp.where(lane == 2.0, 1.0 / den, e2 / den)))


def _oproj(oc, od, s, mods, w_out, router, *, ctx_len, tm, n_exp):
    nb, lt, d = s.shape
    return pl.pallas_call(
        functools.partial(_oproj_kernel, tm=tm, ctx_len=ctx_len, n_exp=n_exp),
        grid=(nb, lt // tm),
        in_specs=[_row_spec(tm, oc.shape[2]), _row_spec(tm, od.shape[2]), _row_spec(tm, d)]
        + _mod_specs(nb, d) + [_full_spec(w_out.shape), _full_spec(router.shape)],
        out_specs=[_row_spec(tm, d), _row_spec(tm, _packed_width(d)), _row_spec(tm, LANES)],
        out_shape=[jax.ShapeDtypeStruct((nb, lt, d), F32), jax.ShapeDtypeStruct((nb, lt, _packed_width(d)), PACKED),
                   jax.ShapeDtypeStruct((nb, lt, LANES), F32)],
        compiler_params=_cparams(("parallel", "parallel")),
        name="cd_oproj_router",
    )(oc, od, s, mods, mods, w_out, router)


def _sc_gather(table, idx, *, chunk=SC_CHUNK):
    n_rows, d = idx.shape[0], table.shape[1]
    info = plsc.get_sparse_core_info()
    n_workers = info.num_cores * info.num_subcores
    per_w = n_rows // n_workers
    assert n_rows % (n_workers * chunk) == 0 and chunk % 8 == 0 and chunk <= LANES
    mesh = plsc.VectorSubcoreMesh(core_axis_name="c", subcore_axis_name="s")

    n_chunks = per_w // chunk

    def body(table_hbm, idx_hbm, out_hbm, idx_v, rows0, rows1, sem0, sem1):
        wid = lax.axis_index("s") * info.num_cores + lax.axis_index("c")
        base = wid * per_w
        pltpu.sync_copy(idx_hbm.at[pl.ds(base, per_w)], idx_v)
        bufs = ((rows0, sem0), (rows1, sem1))

        def gather(i, buf):
            off = pl.multiple_of(i * chunk, chunk)
            return pltpu.make_async_copy(table_hbm.at[idx_v.at[pl.ds(off, chunk)]], buf[0], buf[1])

        def drain(i, buf):
            gather(i, buf).wait()
            pltpu.sync_copy(buf[0], out_hbm.at[pl.ds(base + pl.multiple_of(i * chunk, chunk), chunk)])

        gather(0, bufs[0]).start()

        @pl.loop(0, n_chunks // 2)
        def _(t):
            i = 2 * t
            gather(i + 1, bufs[1]).start()
            drain(i, bufs[0])

            @pl.when(i + 2 < n_chunks)
            def _():
                gather(i + 2, bufs[0]).start()

            drain(i + 1, bufs[1])

        if n_chunks % 2:
            drain(n_chunks - 1, bufs[0])

    return pl.kernel(
        body,
        out_type=jax.ShapeDtypeStruct((n_rows, d), table.dtype),
        mesh=mesh,
        scratch_types=[pltpu.VMEM((per_w,), jnp.int32), pltpu.VMEM((chunk, d), table.dtype),
                       pltpu.VMEM((chunk, d), table.dtype), pltpu.SemaphoreType.DMA, pltpu.SemaphoreType.DMA],
        name="sc_row_gather",
    )(table, idx)


def _route_plan(route, *, n_exp, ts, align):
    n_tok = route.shape[0]
    experts = route[:, :2].astype(jnp.int32)
    onehot = (experts[:, :, None] == jnp.arange(n_exp)[None, None, :]).astype(jnp.int32).sum(axis=1)
    blk = LANES if n_tok % LANES == 0 else 1
    oh = onehot.astype(F32).reshape(n_tok // blk, blk, n_exp)
    inside = jnp.einsum("ij,bje->bie", jnp.tril(jnp.ones((blk, blk), F32), -1), oh)
    blk_tot = oh.sum(axis=1)
    before = (inside + (jnp.cumsum(blk_tot, axis=0) - blk_tot)[:, None, :]).reshape(n_tok, n_exp)
    before = before.astype(jnp.int32)
    counts = onehot.sum(axis=0)
    gsz = (counts + ts - 1) // ts * ts
    gend = jnp.cumsum(gsz)
    gstart = gend - gsz
    pos = gstart[experts] + jnp.take_along_axis(before, experts, axis=1)
    n_slots = -(-(2 * n_tok + n_exp * ts) // align) * align
    n_tiles = n_slots // ts
    tile_start = jnp.arange(n_tiles, dtype=jnp.int32) * ts
    n_used = (gend[-1] // ts).astype(jnp.int32)
    tile_expert = jnp.minimum(jnp.searchsorted(gend, tile_start, side="right"), n_exp - 1).astype(jnp.int32)
    order = jnp.argsort(experts.reshape(-1), stable=True).astype(jnp.int32)
    slot_e = jnp.repeat(tile_expert, ts)
    rank = jnp.arange(n_slots, dtype=jnp.int32) - gstart[slot_e]
    cstart = jnp.cumsum(counts) - counts
    src = jnp.clip(cstart[slot_e] + rank, 0, 2 * n_tok - 1)
    tok = jnp.where(rank < counts[slot_e], order[src] // 2, 0).astype(jnp.int32)
    last_used = tile_expert[jnp.maximum(n_used - 1, 0)]
    tile_expert = jnp.where(jnp.arange(n_tiles) < n_used, tile_expert, last_used)
    return tok, pos.T.reshape(-1).astype(jnp.int32), tile_expert, n_used.reshape(1)


def _gffn_kernel(te_ref, nu_ref, h_ref, wg_ref, wu_ref, wd_ref, o_ref):
    used = pl.program_id(0) < nu_ref[0]

    @pl.when(used)
    def _():
        h = _unpack_pairs(h_ref[...]).astype(BF16)
        o_ref[...] = _pack_pairs(_swiglu_rows(h, wg_ref.at[0], wu_ref.at[0], wd_ref.at[0]))

    @pl.when(jnp.logical_not(used))
    def _():
        o_ref[...] = jnp.zeros(o_ref.shape, o_ref.dtype)


def _grouped_ffn(hs, tile_expert, n_used, w_gate, w_up, w_down, *, layer, ts):
    n_slots, dp = hs.shape
    d, f = w_gate.shape[2:]
    once = pl.Buffered(1)
    grid_spec = pltpu.PrefetchScalarGridSpec(
        num_scalar_prefetch=2,
        grid=(n_slots // ts,),
        in_specs=[
            pl.BlockSpec((ts, dp), lambda j, te, nu: (j, 0)),
            pl.BlockSpec((1, 1, d, f), lambda j, te, nu: (layer, te[j], 0, 0), pipeline_mode=once),
            pl.BlockSpec((1, 1, d, f), lambda j, te, nu: (layer, te[j], 0, 0), pipeline_mode=once),
            pl.BlockSpec((1, 1, f, d), lambda j, te, nu: (layer, te[j], 0, 0), pipeline_mode=once),
        ],
        out_specs=pl.BlockSpec((ts, dp), lambda j, te, nu: (j, 0)),
    )
    return pl.pallas_call(
        _gffn_kernel,
        grid_spec=grid_spec,
        out_shape=jax.ShapeDtypeStruct((n_slots, dp), PACKED),
        compiler_params=_cparams(("arbitrary",)),
        name="moe_grouped_ffn",
    )(tile_expert, n_used, hs, w_gate, w_up, w_down)


def _combine_kernel(y1_ref, y2_ref, r_ref, x_ref, mx_ref, mt_ref, fg_ref, o_ref, *, tm, ctx_len, row0, final):
    pos = (pl.program_id(1) + row0) * tm + lax.broadcasted_iota(jnp.int32, (tm, 1), 0)
    is_ctx = pos < ctx_len
    r = r_ref[0]
    f = r[:, 2:3] * _unpack_pairs(y1_ref[0, 0]) + r[:, 3:4] * _unpack_pairs(y2_ref[0, 0])
    xn = x_ref[0] + _mod(mx_ref, mt_ref, 5, is_ctx) * f
    o_ref[0] = _rms(xn) * fg_ref[...] if final else xn


def _combine(yg, route, s, mods, final_gain, *, ctx_len, tm, final):
    nb, lt, d = s.shape
    row0 = ctx_len // tm if final else 0
    rows = lt - row0 * tm

    def rspec(width):
        return pl.BlockSpec((1, tm, width), lambda b, j: (b, j + row0, 0))

    return pl.pallas_call(
        functools.partial(_combine_kernel, tm=tm, ctx_len=ctx_len, row0=row0, final=final),
        grid=(nb, rows // tm),
        in_specs=[pl.BlockSpec((1, 1, tm, yg.shape[-1]), lambda b, j: (0, b, j + row0, 0)),
                  pl.BlockSpec((1, 1, tm, yg.shape[-1]), lambda b, j: (1, b, j + row0, 0)),
                  rspec(LANES), rspec(d)] + _mod_specs(nb, d) + [_full_spec((1, d))],
        out_specs=_row_spec(tm, d),
        out_shape=jax.ShapeDtypeStruct((nb, rows, d), F32),
        compiler_params=_cparams(("parallel", "parallel")),
        name="moe_combine",
    )(yg, yg, route, s, mods, mods, final_gain.reshape(1, d))


def _moe(h2, route, s, mods, w_gate, w_up, w_down, final_gain, *, layer, ctx_len, n_exp, ts, tm, final):
    nb, lt, d = s.shape
    n_tok = nb * lt
    info = plsc.get_sparse_core_info()
    sc_rows = info.num_cores * info.num_subcores * SC_CHUNK
    tok, pos, tile_expert, n_used = _route_plan(route.reshape(n_tok, LANES), n_exp=n_exp, ts=ts,
                                                align=int(np.lcm(ts, sc_rows)))
    hs = _sc_gather(h2.reshape(n_tok, h2.shape[-1]), tok)
    ys = _grouped_ffn(hs, tile_expert, n_used, w_gate, w_up, w_down, layer=layer, ts=ts)
    yg = _sc_gather(ys, pos).reshape(2, nb, lt, ys.shape[-1])
    return _combine(yg, route, s, mods, final_gain, ctx_len=ctx_len, tm=tm, final=final)


def _final_kernel(x_ref, g_ref, o_ref):
    o_ref[0] = _rms(x_ref[0]) * g_ref[...]


def _final(s, gain, *, ctx_len, tm):
    nb, lt, d = s.shape
    seq = lt - ctx_len
    off = ctx_len // tm
    return pl.pallas_call(
        _final_kernel,
        grid=(nb, seq // tm),
        in_specs=[pl.BlockSpec((1, tm, d), lambda b, j: (b, j + off, 0)), _full_spec((1, d))],
        out_specs=_row_spec(tm, d),
        out_shape=jax.ShapeDtypeStruct((nb, seq, d), F32),
        compiler_params=_cparams(("parallel", "parallel")),
        name="final_norm",
    )(s, gain.reshape(1, d))


def _rope_tables(seq, ctx_len):
    rows = seq // GRID_W
    row = jnp.repeat(jnp.arange(rows), GRID_W).astype(F32)
    col = jnp.tile(jnp.arange(GRID_W), rows).astype(F32)

    def table(rot_dim, reps):
        axis_dim = rot_dim // 2
        inv = ROPE_THETA ** (-jnp.arange(0, axis_dim, 2, dtype=F32) / axis_dim)
        ang = jnp.concatenate([row[:, None] * inv, col[:, None] * inv], axis=-1)
        cos = jnp.concatenate([jnp.ones((ctx_len, axis_dim), F32), jnp.cos(ang)], axis=0)
        sin = jnp.concatenate([jnp.zeros((ctx_len, axis_dim), F32), jnp.sin(ang)], axis=0)
        return (jnp.tile(jnp.concatenate([cos, cos], axis=-1), (1, reps)),
                jnp.tile(jnp.concatenate([-sin, sin], axis=-1), (1, reps)))

    cos_c, sin_c = table(64, 2)
    cos_d, sin_d = table(32, 4)
    return cos_c, sin_c, cos_d, sin_d


def _placement(n_rows, n_cols, pairs):
    m = np.zeros((n_rows, n_cols), np.float32)
    for r, c in pairs:
        m[r, c] = 1.0
    return jnp.asarray(m, BF16)


def _cd_consts(q_gain, k_gain, cq_gain, ckv_gain, w_uq, w_ukv):
    hd, nh, nkv, d_nope, d_rope, d_v = 64, 8, 2, 64, 32, 64
    d_qk = d_nope + d_rope
    heads = np.arange(nh)
    b512 = jnp.asarray(np.kron(np.eye(nh), np.ones((hd, hd))), BF16)
    b128 = jnp.asarray(np.kron(np.eye(nkv), np.ones((hd, hd))), BF16)
    pct = _placement(nh * LANES, nh * hd, [(h * LANES + i, h * hd + i) for h in heads for i in range(hd)])
    nope_cols = (heads[:, None] * d_qk + np.arange(d_nope)[None, :]).reshape(-1)
    rope_cols = (heads[:, None] * d_qk + d_nope + np.arange(d_rope)[None, :]).reshape(-1)
    nope_rows = (heads[:, None] * LANES + np.arange(d_nope)[None, :]).reshape(-1)
    wqnt = jnp.zeros((nh * LANES, w_uq.shape[0]), F32).at[nope_rows].set(w_uq[:, nope_cols].T).astype(BF16)
    wqr = w_uq[:, rope_cols].astype(BF16)
    prt = _placement(nh * LANES, nh * d_rope,
                     [(h * LANES + d_nope + i, h * d_rope + i) for h in heads for i in range(d_rope)])
    kn_cols = (heads[:, None] * (d_nope + d_v) + np.arange(d_nope)[None, :]).reshape(-1)
    v_cols = (heads[:, None] * (d_nope + d_v) + d_nope + np.arange(d_v)[None, :]).reshape(-1)
    wkn = jnp.zeros((w_ukv.shape[0], nh * LANES), F32).at[:, nope_rows].set(w_ukv[:, kn_cols]).astype(BF16)
    pkr = _placement(LANES, nh * LANES, [(i, h * LANES + d_nope + i) for h in heads for i in range(d_rope)])
    wvt = w_ukv[:, v_cols].T.astype(BF16)
    pkc = _placement(nkv * hd, nkv * LANES, [(h * hd + i, h * LANES + i) for h in range(nkv) for i in range(hd)])
    eye = jnp.asarray(np.eye(nkv * hd), BF16)
    return [jnp.tile(q_gain, nh).reshape(1, -1), jnp.tile(k_gain, nkv).reshape(1, -1),
            cq_gain.reshape(1, -1), ckv_gain.reshape(1, -1), b512, b128,
            pct, wqnt, wqr, prt, wkn, pkr, wvt, pkc, eye]


def _largest_tile(n, candidates):
    for t in candidates:
        if n % t == 0:
            return t
    raise ValueError(f"no tile in {candidates} divides {n}")


def kernel(x, c, ctx, c_ctx, ada_w, ada_b, ab_w_in, a_v_gain, a_w_s, a_b_s, b_w_pool, b_scale, ab_w_out,
           ffn_w_gate, ffn_w_up, ffn_w_down, cd_w_in, c_q_gain, c_k_gain, d_cq_gain, d_ckv_gain, d_w_uq,
           d_w_ukv, cd_w_out, moe_router, moe_w_gate, moe_w_up, moe_w_down, final_gain):
    nb, seq, d = x.shape
    ctx_len = ctx.shape[1]
    lt = ctx_len + seq
    depth = ada_w.shape[0]
    n_exp = moe_router.shape[-1]
    chunk = a_w_s.shape[-1]
    tr = 256
    assert ctx_len % tr == 0 and seq % tr == 0 and seq % GRID_W == 0
    tm_ff = _largest_tile(lt, (768, 384, 256))
    assert ffn_w_gate.shape[-1] % FF_CHUNK == 0 and moe_w_gate.shape[-1] % FF_CHUNK == 0

    s = jnp.concatenate([ctx, x], axis=1)

    pad = -(nb + 1) % 8
    cvec = jnp.concatenate([c, c_ctx[None, :], jnp.zeros((pad, d), F32)], axis=0)
    mods_all = _ada_all(cvec, ada_w, ada_b).reshape(depth, nb + 1 + pad, 6, d)
    tabs = _rope_tables(seq, ctx_len)
    moe_wg, moe_wu, moe_wd = moe_w_gate, moe_w_up, moe_w_down

    for layer in range(depth):
        i = layer // 2
        mods = mods_all[layer]
        if layer % 2 == 0:
            bs_full = jnp.repeat(a_b_s[i].T, LANES, axis=1)
            s, h2 = _mixer(s, mods, ab_w_in[i].astype(BF16), a_v_gain[i].reshape(1, -1),
                           a_w_s[i].astype(BF16), bs_full, b_w_pool[i].astype(BF16), b_scale[i].reshape(1, -1),
                           ab_w_out[i].astype(BF16), ctx_len=ctx_len, tm=tr, chunk=chunk)
            s = _swiglu(h2, s, mods, ffn_w_gate[i][None].astype(BF16), ffn_w_up[i][None].astype(BF16),
                        ffn_w_down[i][None].astype(BF16), ctx_len=ctx_len, tm=tm_ff)
        else:
            w_in = jnp.pad(cd_w_in[i], ((0, 0), (0, 12 * LANES - cd_w_in.shape[-1]))).astype(BF16)
            consts = _cd_consts(c_q_gain[i], c_k_gain[i], d_cq_gain[i], d_ckv_gain[i], d_w_uq[i], d_w_ukv[i])
            qtc, qtd, kc, kd, vtc, vtd = _qkv(s, mods, w_in, tabs, consts, ctx_len=ctx_len, tm=tr)
            oc = _attention(qtc, kc, vtc, kv_slots=(0, 0, 0, 0), groups=2, ctx_len=ctx_len, tq=tr,
                            name="attn_gqa")
            od = _attention(qtd, kd, vtd, kv_slots=tuple(range(8)), groups=1, ctx_len=ctx_len, tq=tr,
                            name="attn_mla")
            router = jnp.pad(moe_router[i], ((0, 0), (0, LANES - n_exp)))
            r_hi = router.astype(BF16)
            router = jnp.stack([r_hi, (router - r_hi.astype(F32)).astype(BF16)])
            s, h2, route = _oproj(oc, od, s, mods, cd_w_out[i].astype(BF16), router,
                                  ctx_len=ctx_len, tm=tr, n_exp=n_exp)
            s = _moe(h2, route, s, mods, moe_wg, moe_wu, moe_wd, final_gain, layer=i, ctx_len=ctx_len,
                     n_exp=n_exp, ts=512, tm=tr, final=layer == depth - 1)

    return s if depth % 2 == 0 else _final(s, final_gain, ctx_len=ctx_len, tm=tr)
```

```python
import functools

import numpy as np
import jax
import jax.numpy as jnp
from jax import lax
from jax.experimental import pallas as pl
from jax.experimental.pallas import tpu as pltpu
from jax.experimental.pallas import tpu_sc as plsc

F32 = jnp.float32
BF16 = jnp.bfloat16
EPS = 1e-6
ROPE_THETA = 10000.0
GRID_W = 64
LANES = 128
HALO = 8
POOL_WINDOWS = (2, 4, 8, 16)
NEG = -1e30
LOG2E = 1.4426950408889634
ATTN_UNROLL = 8
FF_CHUNK = 256
DEN_ROWS = 16
SC_CHUNK = 64
VMEM_LIMIT = 56 * 1024 * 1024


def _cparams(sem):
    return pltpu.CompilerParams(dimension_semantics=sem, vmem_limit_bytes=VMEM_LIMIT)


def _rms(x):
    return x * lax.rsqrt(jnp.mean(x * x, axis=-1, keepdims=True) + EPS)


def _is_ctx_rows(tm, ctx_len):
    pos = pl.program_id(1) * tm + lax.broadcasted_iota(jnp.int32, (tm, 1), 0)
    return pos < ctx_len


def _mod(mx_ref, mt_ref, idx, is_ctx):
    return jnp.where(is_ctx, mt_ref[0, idx:idx + 1, :], mx_ref[0, idx:idx + 1, :])


def _dot(a, b):
    return jnp.dot(a, b, preferred_element_type=F32)


def _dot_nt(a, b):
    return lax.dot_general(a, b, (((1,), (1,)), ((), ())), preferred_element_type=F32)


def _silu(x):
    return x * jax.nn.sigmoid(x)


PACKED = jnp.uint32


def _packed_width(d):
    return d // 2


def _pack_pairs(x):
    w = x.shape[1] // 2
    xb = x.astype(BF16).astype(F32)
    hi = pltpu.bitcast(xb[:, :w], jnp.uint32)
    lo = pltpu.bitcast(xb[:, w:], jnp.uint32)
    return hi | (lo >> 16)


def _unpack_pairs(p):
    hi = pltpu.bitcast(p & jnp.uint32(0xFFFF0000), F32)
    lo = pltpu.bitcast(p << 16, F32)
    return jnp.concatenate([hi, lo], axis=1)


def _ada_kernel(c_ref, w_ref, b_ref, o_ref):
    c = c_ref[...]
    o_ref[0] = jnp.dot(_silu(c), w_ref[0], preferred_element_type=F32,
                       precision=lax.Precision.HIGHEST) + b_ref[0]


def _ada_all(cvec, ada_w, ada_b):
    depth, d, n = ada_w.shape
    rows = cvec.shape[0]
    tn = 1536
    return pl.pallas_call(
        _ada_kernel,
        grid=(depth, n // tn),
        in_specs=[
            pl.BlockSpec((rows, d), lambda l, j: (0, 0)),
            pl.BlockSpec((1, d, tn), lambda l, j: (l, 0, j)),
            pl.BlockSpec((1, 1, tn), lambda l, j: (l, 0, j)),
        ],
        out_specs=pl.BlockSpec((1, rows, tn), lambda l, j: (l, 0, j)),
        out_shape=jax.ShapeDtypeStruct((depth, rows, n), F32),
        compiler_params=_cparams(("arbitrary", "arbitrary")),
        name="adaln",
    )(cvec, ada_w, ada_b.reshape(depth, 1, n))


def _row_spec(tm, width):
    return pl.BlockSpec((1, tm, width), lambda b, j, *_: (b, j, 0))


def _mod_specs(nb, d):
    return [pl.BlockSpec((1, 6, d), lambda b, j, *_: (b, 0, 0)),
            pl.BlockSpec((1, 6, d), lambda b, j, *_: (nb, 0, 0))]


def _full_spec(shape):
    nd = len(shape)
    return pl.BlockSpec(shape, lambda *_: (0,) * nd)


def _mixer_kernel(x_ref, xp_ref, xn_ref, mx_ref, mt_ref, win_ref, vg_ref, ws_ref, bs_ref, wp_ref,
                  psc_ref, wo_ref, xo_ref, h2_ref, *, tm, ctx_len, lt, chunk):
    is_ctx = _is_ctx_rows(tm, ctx_len)
    pos0 = pl.program_id(1) * tm
    aw = 4 * LANES
    sc1 = 1.0 + _mod(mx_ref, mt_ref, 1, is_ctx)
    sh1 = _mod(mx_ref, mt_ref, 0, is_ctx)
    z = _dot((_rms(x_ref[0]) * sc1 + sh1).astype(BF16), win_ref[...])
    halo_x = jnp.concatenate([xp_ref[0], xn_ref[0]], axis=0)
    halo_h = _rms(halo_x) * sc1[:2 * HALO] + sh1[:2 * HALO]
    halo_z = _dot(halo_h.astype(BF16), win_ref[:, 2 * aw:])
    za = z[:, :2 * aw]
    gl = 0.5 * za * (1.0 + lax.erf(za * (2.0 ** -0.5)))
    u = gl[:, :aw]
    v = gl[:, aw:]
    vn = (_rms(v) * vg_ref[...]).astype(BF16)
    ya_rows = []
    for c in range(tm // chunk):
        rs = slice(c * chunk, (c + 1) * chunk)
        cols = []
        for g in range(4):
            cs = slice(g * LANES, (g + 1) * LANES)
            sg = _dot(ws_ref[g], vn[rs, cs]) + bs_ref[:, cs]
            cols.append(u[rs, cs] * sg)
        ya_rows.append(jnp.concatenate(cols, axis=1))
    ya = jnp.concatenate(ya_rows, axis=0)

    in_ctx = pos0 < ctx_len
    seq_start = jnp.where(in_ctx, 0, ctx_len)
    seq_len = jnp.where(in_ctx, ctx_len, lt - ctx_len)
    p_in = pos0 - seq_start + lax.broadcasted_iota(jnp.int32, (tm, 1), 0)
    has_prev = jnp.logical_and(pos0 != 0, pos0 != ctx_len).astype(F32)
    has_next = jnp.logical_and(pos0 + tm != ctx_len, pos0 + tm != lt).astype(F32)
    zp = z[:, 2 * aw:]
    ext = jnp.concatenate([halo_z[:HALO] * has_prev, zp, halo_z[HALO:] * has_next], axis=0)
    n = tm + 2 * HALO
    yb_cols = []
    for g, w in enumerate(POOL_WINDOWS):
        cs = slice(g * LANES, (g + 1) * LANES)
        a = ext[:, cs]
        step = 1
        while step < w:
            a = a + pltpu.roll(a, n - step, 0)
            step *= 2
        shift = HALO - w // 2
        if shift:
            a = pltpu.roll(a, n - shift, 0)
        wsum = a[:tm]
        lo = jnp.clip(p_in - w // 2, 0, seq_len)
        hi = jnp.clip(p_in - w // 2 + w, 0, seq_len)
        pooled = wsum / (hi - lo).astype(F32) - zp[:, cs]
        yb_cols.append(_dot(pooled.astype(BF16), wp_ref[g]) * psc_ref[:, cs])
    y = jnp.concatenate([ya] + yb_cols, axis=1).astype(BF16)
    xn = x_ref[0] + _mod(mx_ref, mt_ref, 2, is_ctx) * _dot(y, wo_ref[...])
    xo_ref[0] = xn
    h2 = _rms(xn) * (1.0 + _mod(mx_ref, mt_ref, 4, is_ctx)) + _mod(mx_ref, mt_ref, 3, is_ctx)
    h2_ref[0] = h2.astype(BF16)


def _mixer(s, mods, w_in, v_gain, w_s, bs_full, w_pool, p_scale, w_out, *, ctx_len, tm, chunk):
    nb, lt, d = s.shape
    aw = 4 * LANES
    nh = lt // HALO
    th = tm // HALO
    pw = 4 * LANES
    halo_prev = pl.BlockSpec((1, HALO, d), lambda b, j: (b, jnp.maximum(j * th - 1, 0), 0))
    halo_next = pl.BlockSpec((1, HALO, d), lambda b, j: (b, jnp.minimum((j + 1) * th, nh - 1), 0))
    return pl.pallas_call(
        functools.partial(_mixer_kernel, tm=tm, ctx_len=ctx_len, lt=lt, chunk=chunk),
        grid=(nb, lt // tm),
        in_specs=[_row_spec(tm, d), halo_prev, halo_next] + _mod_specs(nb, d) + [
            _full_spec(w_in.shape), _full_spec((1, aw)), _full_spec(w_s.shape), _full_spec(bs_full.shape),
            _full_spec(w_pool.shape), _full_spec((1, pw)), _full_spec(w_out.shape)],
        out_specs=[_row_spec(tm, d), _row_spec(tm, d)],
        out_shape=[jax.ShapeDtypeStruct((nb, lt, d), F32), jax.ShapeDtypeStruct((nb, lt, d), BF16)],
        compiler_params=_cparams(("parallel", "parallel")),
        name="ab_mixer",
    )(s, s, s, mods, mods, w_in, v_gain, w_s, bs_full, w_pool, p_scale, w_out)


def _swiglu_rows(h, wg_ref, wu_ref, wd_ref):
    out = None
    for c in range(wg_ref.shape[2] // FF_CHUNK):
        cs = slice(c * FF_CHUNK, (c + 1) * FF_CHUNK)
        g = _dot(h, wg_ref[0, :, cs].astype(BF16))
        a = (_silu(g) * _dot(h, wu_ref[0, :, cs].astype(BF16))).astype(BF16)
        part = _dot(a, wd_ref[0, cs, :].astype(BF16))
        out = part if out is None else out + part
    return out


def _swiglu_kernel(h_ref, x_ref, mx_ref, mt_ref, wg_ref, wu_ref, wd_ref, o_ref, *, tm, ctx_len):
    is_ctx = _is_ctx_rows(tm, ctx_len)
    f = _swiglu_rows(h_ref[0], wg_ref, wu_ref, wd_ref)
    o_ref[0] = x_ref[0] + _mod(mx_ref, mt_ref, 5, is_ctx) * f


def _swiglu(h2, s, mods, w_gate, w_up, w_down, *, ctx_len, tm):
    nb, lt, d = s.shape
    f = w_gate.shape[2]
    once = pl.Buffered(1)
    return pl.pallas_call(
        functools.partial(_swiglu_kernel, tm=tm, ctx_len=ctx_len),
        grid=(nb, lt // tm),
        in_specs=[_row_spec(tm, d), _row_spec(tm, d)] + _mod_specs(nb, d) + [
            pl.BlockSpec((1, d, f), lambda b, j: (0, 0, 0), pipeline_mode=once),
            pl.BlockSpec((1, d, f), lambda b, j: (0, 0, 0), pipeline_mode=once),
            pl.BlockSpec((1, f, d), lambda b, j: (0, 0, 0), pipeline_mode=once)],
        out_specs=_row_spec(tm, d),
        out_shape=jax.ShapeDtypeStruct((nb, lt, d), F32),
        compiler_params=_cparams(("parallel", "parallel")),
        name="swiglu",
    )(h2, s, mods, mods, w_gate, w_up, w_down)


def _seg_sum(x2, b_ref):
    return _dot(x2.astype(BF16), b_ref[...])


def _rope_slabs(x, cos, sin, half):
    lane = lax.broadcasted_iota(jnp.int32, (x.shape[0], LANES), 1)
    first = (lane % (2 * half)) < half
    out = []
    for s in range(x.shape[1] // LANES):
        xs = x[:, s * LANES:(s + 1) * LANES]
        swapped = jnp.where(first, pltpu.roll(xs, LANES - half, 1), pltpu.roll(xs, half, 1))
        out.append(xs * cos + swapped * sin)
    return out[0] if len(out) == 1 else jnp.concatenate(out, axis=1)


def _store_heads(k_ref, k):
    for h in range(k_ref.shape[2]):
        k_ref[0, 0, h] = k[:, h * LANES:(h + 1) * LANES]


def _qkv_kernel(x_ref, mx_ref, mt_ref, win_ref, cc_ref, sc_ref, cd_ref, sd_ref,
                qg_ref, kg_ref, cqg_ref, ckvg_ref, b512_ref, b128_ref,
                pct_ref, wqnt_ref, wqr_ref, prt_ref, wkn_ref, pkr_ref, wvt_ref, pkc_ref, eye_ref,
                qtc_ref, qtd_ref, kc_ref, kd_ref, vtc_ref, vtd_ref,
                *, tm, ctx_len, head_dim, d_qk):
    is_ctx = _is_ctx_rows(tm, ctx_len)
    h = _rms(x_ref[0]) * (1.0 + _mod(mx_ref, mt_ref, 1, is_ctx)) + _mod(mx_ref, mt_ref, 0, is_ctx)
    z = _dot(h.astype(BF16), win_ref[...])
    cos_c, sin_c, cos_d, sin_d = cc_ref[...], sc_ref[...], cd_ref[...], sd_ref[...]
    o_kc, o_vc, o_cq, o_ckv, o_kr = 512, 640, 768, 1152, 1408

    qc = z[:, :o_kc]
    qc = qc * lax.rsqrt(_seg_sum(qc * qc, b512_ref) * (1.0 / head_dim) + EPS) * qg_ref[...]
    qc = _rope_slabs(qc, cos_c, sin_c, head_dim // 2) * (head_dim ** -0.5 * LOG2E)
    qtc_ref[0] = _dot_nt(pct_ref[...], qc.astype(BF16)).astype(BF16)

    kc = z[:, o_kc:o_vc]
    kc = kc * lax.rsqrt(_seg_sum(kc * kc, b128_ref) * (1.0 / head_dim) + EPS) * kg_ref[...]
    kc = _rope_slabs(kc, cos_c, sin_c, head_dim // 2)
    _store_heads(kc_ref, _dot(kc.astype(BF16), pkc_ref[...]).astype(BF16))
    vtc_ref[0, 0] = _dot_nt(eye_ref[...], z[:, o_vc:o_cq].astype(BF16)).astype(BF16)

    cq = (_rms(z[:, o_cq:o_ckv]) * cqg_ref[...]) * (d_qk ** -0.5 * LOG2E)
    cqb = cq.astype(BF16)
    q_rope = _rope_slabs(_dot(cqb, wqr_ref[...]), cos_d, sin_d, 16)
    qtd = _dot_nt(wqnt_ref[...], cqb) + _dot_nt(prt_ref[...], q_rope.astype(BF16))
    qtd_ref[0] = qtd.astype(BF16)

    ckv = (_rms(z[:, o_ckv:o_kr]) * ckvg_ref[...]).astype(BF16)
    kr = _rope_slabs(z[:, o_kr:], cos_d, sin_d, 16)
    _store_heads(kd_ref, (_dot(ckv, wkn_ref[...]) + _dot(kr.astype(BF16), pkr_ref[...])).astype(BF16))
    vtd_ref[0, 0] = _dot_nt(wvt_ref[...], ckv).astype(BF16)


def _qkv(s, mods, w_in, tabs, consts, *, ctx_len, tm):
    nb, lt, d = s.shape
    nch = lt // tm
    tab_spec = pl.BlockSpec((tm, LANES), lambda b, j: (j, 0))
    out_shapes = [
        jax.ShapeDtypeStruct((nb, 8 * LANES, lt), BF16),
        jax.ShapeDtypeStruct((nb, 8 * LANES, lt), BF16),
        jax.ShapeDtypeStruct((nb, nch, 2, tm, LANES), BF16),
        jax.ShapeDtypeStruct((nb, nch, 8, tm, LANES), BF16),
        jax.ShapeDtypeStruct((nb, nch, 2 * 64, tm), BF16),
        jax.ShapeDtypeStruct((nb, nch, 8 * 64, tm), BF16),
    ]
    out_specs = [
        pl.BlockSpec((1, 8 * LANES, tm), lambda b, j: (b, 0, j)),
        pl.BlockSpec((1, 8 * LANES, tm), lambda b, j: (b, 0, j)),
        pl.BlockSpec((1, 1, 2, tm, LANES), lambda b, j: (b, j, 0, 0, 0)),
        pl.BlockSpec((1, 1, 8, tm, LANES), lambda b, j: (b, j, 0, 0, 0)),
        pl.BlockSpec((1, 1, 2 * 64, tm), lambda b, j: (b, j, 0, 0)),
        pl.BlockSpec((1, 1, 8 * 64, tm), lambda b, j: (b, j, 0, 0)),
    ]
    return pl.pallas_call(
        functools.partial(_qkv_kernel, tm=tm, ctx_len=ctx_len, head_dim=64, d_qk=96),
        grid=(nb, nch),
        in_specs=[_row_spec(tm, d)] + _mod_specs(nb, d) + [_full_spec(w_in.shape)] + [tab_spec] * 4
        + [_full_spec(c.shape) for c in consts],
        out_specs=out_specs,
        out_shape=out_shapes,
        compiler_params=_cparams(("parallel", "parallel")),
        name="cd_qkv",
    )(s, mods, mods, w_in, *tabs, *consts)


def _attn_kernel(qt_ref, k_ref, vt_ref, o_ref, s0_ref, s1_ref, c0_ref, c1_ref, m_ref, acc_ref,
                 *, kv_slots, tq, n_ctx, n_all, dv):
    heads = len(kv_slots)

    def scores(j, g, kg, dst):
        s_dst, c_dst = dst
        st = _dot(k_ref[0, j, kg], qt_ref[0, g * LANES:(g + 1) * LANES, :])
        s_dst[g] = st
        c_dst[g] = jnp.max(st, axis=0, keepdims=True)

    def consume(j, src, g, kg):
        s_src, c_src = src
        m_old = m_ref[g]
        m_new = jnp.maximum(m_old, c_src[g])
        alpha = jnp.exp2(m_old - m_new)
        p = jnp.exp2((s_src[g] - m_new).astype(BF16))
        m_ref[g] = m_new
        vt = jnp.concatenate([vt_ref[0, j, kg * dv:(kg + 1) * dv, :], jnp.ones((DEN_ROWS, p.shape[0]), BF16)],
                             axis=0)
        acc_ref[g] = alpha * acc_ref[g] + _dot(vt, p)

    def step(j, j_next, cur, nxt):
        for g, kg in enumerate(kv_slots):
            if j_next is not None:
                scores(j_next, g, kg, nxt)
            consume(j, cur, g, kg)

    def run(n):
        bufs = ((s0_ref, c0_ref), (s1_ref, c1_ref))
        m_ref[...] = jnp.full(m_ref.shape, NEG, F32)
        acc_ref[...] = jnp.zeros(acc_ref.shape, F32)
        for g, kg in enumerate(kv_slots):
            scores(0, g, kg, bufs[0])
        n_main = (n - 1) // ATTN_UNROLL * ATTN_UNROLL

        def trip(i, c):
            for k in range(ATTN_UNROLL):
                j = ATTN_UNROLL * i + k
                step(j, j + 1, bufs[k % 2], bufs[(k + 1) % 2])
            return c

        lax.fori_loop(0, n_main // ATTN_UNROLL, trip, 0)
        for j in range(n_main, n):
            step(j, j + 1 if j + 1 < n else None, bufs[j % 2], bufs[(j + 1) % 2])
        ot = jnp.concatenate([acc_ref[g, :dv, :] / acc_ref[g, dv:dv + 1, :] for g in range(heads)],
                             axis=0)
        o_ref[0] = ot.T.astype(o_ref.dtype)

    is_ctx_tile = pl.program_id(1) * tq < n_ctx * k_ref.shape[3]

    @pl.when(is_ctx_tile)
    def _():
        run(n_ctx)

    @pl.when(jnp.logical_not(is_ctx_tile))
    def _():
        run(n_all)


def _attention(qt, k, vt, *, kv_slots, ctx_len, tq, name):
    nb, hq, lt = qt.shape
    heads = len(kv_slots)
    assert hq == heads * LANES
    nch, n_kv, tk = k.shape[1:4]
    dv = vt.shape[2] // n_kv
    once = pl.Buffered(1)
    return pl.pallas_call(
        functools.partial(_attn_kernel, kv_slots=kv_slots, tq=tq, n_ctx=ctx_len // tk, n_all=nch, dv=dv),
        grid=(nb, lt // tq),
        in_specs=[
            pl.BlockSpec((1, hq, tq), lambda b, q: (b, 0, q)),
            pl.BlockSpec((1, nch, n_kv, tk, LANES), lambda b, q: (b, 0, 0, 0, 0), pipeline_mode=once),
            pl.BlockSpec((1, nch, vt.shape[2], tk), lambda b, q: (b, 0, 0, 0), pipeline_mode=once),
        ],
        out_specs=pl.BlockSpec((1, tq, heads * dv), lambda b, q: (b, q, 0)),
        out_shape=jax.ShapeDtypeStruct((nb, lt, heads * dv), BF16),
        scratch_shapes=[pltpu.VMEM((heads, tk, tq), F32), pltpu.VMEM((heads, tk, tq), F32),
                        pltpu.VMEM((heads, 1, tq), F32), pltpu.VMEM((heads, 1, tq), F32),
                        pltpu.VMEM((heads, 1, tq), F32), pltpu.VMEM((heads, dv + DEN_ROWS, tq), F32)],
        compiler_params=_cparams(("parallel", "arbitrary")),
        name=name,
    )(qt, k, vt)


def _oproj_kernel(oc_ref, od_ref, x_ref, mx_ref, mt_ref, w_ref, r_ref, xo_ref, h2_ref, g_ref,
                  *, tm, ctx_len, n_exp):
    is_ctx = _is_ctx_rows(tm, ctx_len)
    o = jnp.concatenate([oc_ref[0], od_ref[0]], axis=-1)
    xn = x_ref[0] + _mod(mx_ref, mt_ref, 2, is_ctx) * _dot(o, w_ref[...])
    xo_ref[0] = xn
    h2 = _rms(xn) * (1.0 + _mod(mx_ref, mt_ref, 4, is_ctx)) + _mod(mx_ref, mt_ref, 3, is_ctx)
    h2_ref[0] = _pack_pairs(h2)
    h_hi = h2.astype(BF16)
    h_lo = (h2 - h_hi.astype(F32)).astype(BF16)
    logits = _dot(h_hi, r_ref[0]) + _dot(h_lo, r_ref[0]) + _dot(h_hi, r_ref[1])
    lane = lax.broadcasted_iota(jnp.int32, (tm, LANES), 1).astype(F32)
    lg = jnp.where(lane < n_exp, logits, NEG)
    m1 = jnp.max(lg, axis=-1, keepdims=True)
    i1 = jnp.min(jnp.where(lg == m1, lane, float(LANES)), axis=-1, keepdims=True)
    lg2 = jnp.where(lane == i1, NEG, lg)
    m2 = jnp.max(lg2, axis=-1, keepdims=True)
    i2 = jnp.min(jnp.where(lg2 == m2, lane, float(LANES)), axis=-1, keepdims=True)
    e2 = jnp.exp(m2 - m1)
    den = 1.0 + e2
    g_ref[0] = jnp.where(lane == 0.0, i1, jnp.where(lane == 1.0, i2, jnp.where(lane == 2.0, 1.0 / den, e2 / den)))


def _oproj(oc, od, s, mods, w_out, router, *, ctx_len, tm, n_exp):
    nb, lt, d = s.shape
    return pl.pallas_call(
        functools.partial(_oproj_kernel, tm=tm, ctx_len=ctx_len, n_exp=n_exp),
        grid=(nb, lt // tm),
        in_specs=[pl.BlockSpec((1, tm, d // 2), lambda b, j: (b, j, 0)),
                  pl.BlockSpec((1, tm, d // 2), lambda b, j: (b, j, 1)), _row_spec(tm, d)]
        + _mod_specs(nb, d) + [_full_spec(w_out.shape), _full_spec(router.shape)],
        out_specs=[_row_spec(tm, d), _row_spec(tm, _packed_width(d)), _row_spec(tm, LANES)],
        out_shape=[jax.ShapeDtypeStruct((nb, lt, d), F32), jax.ShapeDtypeStruct((nb, lt, _packed_width(d)), PACKED),
                   jax.ShapeDtypeStruct((nb, lt, LANES), F32)],
        compiler_params=_cparams(("parallel", "parallel")),
        name="cd_oproj_router",
    )(oc, od, s, mods, mods, w_out, router)


def _sc_gather(table, idx, *, chunk=SC_CHUNK):
    n_rows, d = idx.shape[0], table.shape[1]
    info = plsc.get_sparse_core_info()
    n_workers = info.num_cores * info.num_subcores
    per_w = n_rows // n_workers
    assert n_rows % (n_workers * chunk) == 0 and chunk % 8 == 0 and chunk <= LANES
    mesh = plsc.VectorSubcoreMesh(core_axis_name="c", subcore_axis_name="s")

    n_chunks = per_w // chunk

    def body(table_hbm, idx_hbm, out_hbm, idx_v, rows0, rows1, sem0, sem1):
        wid = lax.axis_index("s") * info.num_cores + lax.axis_index("c")
        base = wid * per_w
        pltpu.sync_copy(idx_hbm.at[pl.ds(base, per_w)], idx_v)
        bufs = ((rows0, sem0), (rows1, sem1))

        def gather(i, buf):
            off = pl.multiple_of(i * chunk, chunk)
            return pltpu.make_async_copy(table_hbm.at[idx_v.at[pl.ds(off, chunk)]], buf[0], buf[1])

        def drain(i, buf):
            gather(i, buf).wait()
            pltpu.sync_copy(buf[0], out_hbm.at[pl.ds(base + pl.multiple_of(i * chunk, chunk), chunk)])

        gather(0, bufs[0]).start()

        @pl.loop(0, n_chunks // 2)
        def _(t):
            i = 2 * t
            gather(i + 1, bufs[1]).start()
            drain(i, bufs[0])

            @pl.when(i + 2 < n_chunks)
            def _():
                gather(i + 2, bufs[0]).start()

            drain(i + 1, bufs[1])

        if n_chunks % 2:
            drain(n_chunks - 1, bufs[0])

    return pl.kernel(
        body,
        out_type=jax.ShapeDtypeStruct((n_rows, d), table.dtype),
        mesh=mesh,
        scratch_types=[pltpu.VMEM((per_w,), jnp.int32), pltpu.VMEM((chunk, d), table.dtype),
                       pltpu.VMEM((chunk, d), table.dtype), pltpu.SemaphoreType.DMA, pltpu.SemaphoreType.DMA],
        name="sc_row_gather",
    )(table, idx)


def _route_plan(route, *, n_exp, ts, align):
    n_tok = route.shape[0]
    experts = route[:, :2].astype(jnp.int32)
    onehot = (experts[:, :, None] == jnp.arange(n_exp)[None, None, :]).astype(jnp.int32).sum(axis=1)
    blk = LANES if n_tok % LANES == 0 else 1
    oh = onehot.astype(F32).reshape(n_tok // blk, blk, n_exp)
    inside = jnp.einsum("ij,bje->bie", jnp.tril(jnp.ones((blk, blk), F32), -1), oh)
    blk_tot = oh.sum(axis=1)
    before = (inside + (jnp.cumsum(blk_tot, axis=0) - blk_tot)[:, None, :]).reshape(n_tok, n_exp)
    before = before.astype(jnp.int32)
    counts = onehot.sum(axis=0)
    gsz = (counts + ts - 1) // ts * ts
    gend = jnp.cumsum(gsz)
    gstart = gend - gsz
    pos = gstart[experts] + jnp.take_along_axis(before, experts, axis=1)
    n_slots = -(-(2 * n_tok + n_exp * ts) // align) * align
    n_tiles = n_slots // ts
    tile_start = jnp.arange(n_tiles, dtype=jnp.int32) * ts
    n_used = (gend[-1] // ts).astype(jnp.int32)
    tile_expert = jnp.minimum(jnp.searchsorted(gend, tile_start, side="right"), n_exp - 1).astype(jnp.int32)
    order = jnp.argsort(experts.reshape(-1), stable=True).astype(jnp.int32)
    slot_e = jnp.repeat(tile_expert, ts)
    rank = jnp.arange(n_slots, dtype=jnp.int32) - gstart[slot_e]
    cstart = jnp.cumsum(counts) - counts
    src = jnp.clip(cstart[slot_e] + rank, 0, 2 * n_tok - 1)
    tok = jnp.where(rank < counts[slot_e], order[src] // 2, 0).astype(jnp.int32)
    last_used = tile_expert[jnp.maximum(n_used - 1, 0)]
    tile_expert = jnp.where(jnp.arange(n_tiles) < n_used, tile_expert, last_used)
    return tok, pos.T.reshape(-1).astype(jnp.int32), tile_expert, n_used.reshape(1)


def _gffn_kernel(te_ref, nu_ref, h_ref, wg_ref, wu_ref, wd_ref, o_ref):
    used = pl.program_id(0) < nu_ref[0]

    @pl.when(used)
    def _():
        h = _unpack_pairs(h_ref[...]).astype(BF16)
        o_ref[...] = _pack_pairs(_swiglu_rows(h, wg_ref.at[0], wu_ref.at[0], wd_ref.at[0]))

    @pl.when(jnp.logical_not(used))
    def _():
        o_ref[...] = jnp.zeros(o_ref.shape, o_ref.dtype)


def _grouped_ffn(hs, tile_expert, n_used, w_gate, w_up, w_down, *, layer, ts):
    n_slots, dp = hs.shape
    d, f = w_gate.shape[2:]
    once = pl.Buffered(1)
    grid_spec = pltpu.PrefetchScalarGridSpec(
        num_scalar_prefetch=2,
        grid=(n_slots // ts,),
        in_specs=[
            pl.BlockSpec((ts, dp), lambda j, te, nu: (j, 0)),
            pl.BlockSpec((1, 1, d, f), lambda j, te, nu: (layer, te[j], 0, 0), pipeline_mode=once),
            pl.BlockSpec((1, 1, d, f), lambda j, te, nu: (layer, te[j], 0, 0), pipeline_mode=once),
            pl.BlockSpec((1, 1, f, d), lambda j, te, nu: (layer, te[j], 0, 0), pipeline_mode=once),
        ],
        out_specs=pl.BlockSpec((ts, dp), lambda j, te, nu: (j, 0)),
    )
    return pl.pallas_call(
        _gffn_kernel,
        grid_spec=grid_spec,
        out_shape=jax.ShapeDtypeStruct((n_slots, dp), PACKED),
        compiler_params=_cparams(("arbitrary",)),
        name="moe_grouped_ffn",
    )(tile_expert, n_used, hs, w_gate, w_up, w_down)


def _combine_kernel(y1_ref, y2_ref, r_ref, x_ref, mx_ref, mt_ref, fg_ref, o_ref, *, tm, ctx_len, row0, final):
    pos = (pl.program_id(1) + row0) * tm + lax.broadcasted_iota(jnp.int32, (tm, 1), 0)
    is_ctx = pos < ctx_len
    r = r_ref[0]
    f = r[:, 2:3] * _unpack_pairs(y1_ref[0, 0]) + r[:, 3:4] * _unpack_pairs(y2_ref[0, 0])
    xn = x_ref[0] + _mod(mx_ref, mt_ref, 5, is_ctx) * f
    o_ref[0] = _rms(xn) * fg_ref[...] if final else xn


def _combine(yg, route, s, mods, final_gain, *, ctx_len, tm, final):
    nb, lt, d = s.shape
    row0 = ctx_len // tm if final else 0
    rows = lt - row0 * tm

    def rspec(width):
        return pl.BlockSpec((1, tm, width), lambda b, j: (b, j + row0, 0))

    return pl.pallas_call(
        functools.partial(_combine_kernel, tm=tm, ctx_len=ctx_len, row0=row0, final=final),
        grid=(nb, rows // tm),
        in_specs=[pl.BlockSpec((1, 1, tm, yg.shape[-1]), lambda b, j: (0, b, j + row0, 0)),
                  pl.BlockSpec((1, 1, tm, yg.shape[-1]), lambda b, j: (1, b, j + row0, 0)),
                  rspec(LANES), rspec(d)] + _mod_specs(nb, d) + [_full_spec((1, d))],
        out_specs=_row_spec(tm, d),
        out_shape=jax.ShapeDtypeStruct((nb, rows, d), F32),
        compiler_params=_cparams(("parallel", "parallel")),
        name="moe_combine",
    )(yg, yg, route, s, mods, mods, final_gain.reshape(1, d))


def _moe(h2, route, s, mods, w_gate, w_up, w_down, final_gain, *, layer, ctx_len, n_exp, ts, tm, final):
    nb, lt, d = s.shape
    n_tok = nb * lt
    info = plsc.get_sparse_core_info()
    sc_rows = info.num_cores * info.num_subcores * SC_CHUNK
    tok, pos, tile_expert, n_used = _route_plan(route.reshape(n_tok, LANES), n_exp=n_exp, ts=ts,
                                                align=int(np.lcm(ts, sc_rows)))
    hs = _sc_gather(h2.reshape(n_tok, h2.shape[-1]), tok)
    ys = _grouped_ffn(hs, tile_expert, n_used, w_gate, w_up, w_down, layer=layer, ts=ts)
    yg = _sc_gather(ys, pos).reshape(2, nb, lt, ys.shape[-1])
    return _combine(yg, route, s, mods, final_gain, ctx_len=ctx_len, tm=tm, final=final)


def _final_kernel(x_ref, g_ref, o_ref):
    o_ref[0] = _rms(x_ref[0]) * g_ref[...]


def _final(s, gain, *, ctx_len, tm):
    nb, lt, d = s.shape
    seq = lt - ctx_len
    off = ctx_len // tm
    return pl.pallas_call(
        _final_kernel,
        grid=(nb, seq // tm),
        in_specs=[pl.BlockSpec((1, tm, d), lambda b, j: (b, j + off, 0)), _full_spec((1, d))],
        out_specs=_row_spec(tm, d),
        out_shape=jax.ShapeDtypeStruct((nb, seq, d), F32),
        compiler_params=_cparams(("parallel", "parallel")),
        name="final_norm",
    )(s, gain.reshape(1, d))


def _rope_tables(seq, ctx_len):
    rows = seq // GRID_W
    row = jnp.repeat(jnp.arange(rows), GRID_W).astype(F32)
    col = jnp.tile(jnp.arange(GRID_W), rows).astype(F32)

    def table(rot_dim, reps):
        axis_dim = rot_dim // 2
        inv = ROPE_THETA ** (-jnp.arange(0, axis_dim, 2, dtype=F32) / axis_dim)
        ang = jnp.concatenate([row[:, None] * inv, col[:, None] * inv], axis=-1)
        cos = jnp.concatenate([jnp.ones((ctx_len, axis_dim), F32), jnp.cos(ang)], axis=0)
        sin = jnp.concatenate([jnp.zeros((ctx_len, axis_dim), F32), jnp.sin(ang)], axis=0)
        return (jnp.tile(jnp.concatenate([cos, cos], axis=-1), (1, reps)),
                jnp.tile(jnp.concatenate([-sin, sin], axis=-1), (1, reps)))

    cos_c, sin_c = table(64, 2)
    cos_d, sin_d = table(32, 4)
    return cos_c, sin_c, cos_d, sin_d


def _placement(n_rows, n_cols, pairs):
    m = np.zeros((n_rows, n_cols), np.float32)
    for r, c in pairs:
        m[r, c] = 1.0
    return jnp.asarray(m, BF16)


def _cd_consts(q_gain, k_gain, cq_gain, ckv_gain, w_uq, w_ukv):
    hd, nh, nkv, d_nope, d_rope, d_v = 64, 8, 2, 64, 32, 64
    d_qk = d_nope + d_rope
    heads = np.arange(nh)
    b512 = jnp.asarray(np.kron(np.eye(nh), np.ones((hd, hd))), BF16)
    b128 = jnp.asarray(np.kron(np.eye(nkv), np.ones((hd, hd))), BF16)
    pct = _placement(nh * LANES, nh * hd, [(h * LANES + i, h * hd + i) for h in heads for i in range(hd)])
    nope_cols = (heads[:, None] * d_qk + np.arange(d_nope)[None, :]).reshape(-1)
    rope_cols = (heads[:, None] * d_qk + d_nope + np.arange(d_rope)[None, :]).reshape(-1)
    nope_rows = (heads[:, None] * LANES + np.arange(d_nope)[None, :]).reshape(-1)
    wqnt = jnp.zeros((nh * LANES, w_uq.shape[0]), F32).at[nope_rows].set(w_uq[:, nope_cols].T).astype(BF16)
    wqr = w_uq[:, rope_cols].astype(BF16)
    prt = _placement(nh * LANES, nh * d_rope,
                     [(h * LANES + d_nope + i, h * d_rope + i) for h in heads for i in range(d_rope)])
    kn_cols = (heads[:, None] * (d_nope + d_v) + np.arange(d_nope)[None, :]).reshape(-1)
    v_cols = (heads[:, None] * (d_nope + d_v) + d_nope + np.arange(d_v)[None, :]).reshape(-1)
    wkn = jnp.zeros((w_ukv.shape[0], nh * LANES), F32).at[:, nope_rows].set(w_ukv[:, kn_cols]).astype(BF16)
    pkr = _placement(LANES, nh * LANES, [(i, h * LANES + d_nope + i) for h in heads for i in range(d_rope)])
    wvt = w_ukv[:, v_cols].T.astype(BF16)
    pkc = _placement(nkv * hd, nkv * LANES, [(h * hd + i, h * LANES + i) for h in range(nkv) for i in range(hd)])
    eye = jnp.asarray(np.eye(nkv * hd), BF16)
    return [jnp.tile(q_gain, nh).reshape(1, -1), jnp.tile(k_gain, nkv).reshape(1, -1),
            cq_gain.reshape(1, -1), ckv_gain.reshape(1, -1), b512, b128,
            pct, wqnt, wqr, prt, wkn, pkr, wvt, pkc, eye]


def _largest_tile(n, candidates):
    for t in candidates:
        if n % t == 0:
            return t
    raise ValueError(f"no tile in {candidates} divides {n}")


def kernel(x, c, ctx, c_ctx, ada_w, ada_b, ab_w_in, a_v_gain, a_w_s, a_b_s, b_w_pool, b_scale, ab_w_out,
           ffn_w_gate, ffn_w_up, ffn_w_down, cd_w_in, c_q_gain, c_k_gain, d_cq_gain, d_ckv_gain, d_w_uq,
           d_w_ukv, cd_w_out, moe_router, moe_w_gate, moe_w_up, moe_w_down, final_gain):
    nb, seq, d = x.shape
    ctx_len = ctx.shape[1]
    lt = ctx_len + seq
    depth = ada_w.shape[0]
    n_exp = moe_router.shape[-1]
    chunk = a_w_s.shape[-1]
    tr = 256
    assert ctx_len % tr == 0 and seq % tr == 0 and seq % GRID_W == 0
    tm_ff = _largest_tile(lt, (768, 384, 256))
    assert ffn_w_gate.shape[-1] % FF_CHUNK == 0 and moe_w_gate.shape[-1] % FF_CHUNK == 0

    s = jnp.concatenate([ctx, x], axis=1)

    pad = -(nb + 1) % 8
    cvec = jnp.concatenate([c, c_ctx[None, :], jnp.zeros((pad, d), F32)], axis=0)
    mods_all = _ada_all(cvec, ada_w, ada_b).reshape(depth, nb + 1 + pad, 6, d)
    tabs = _rope_tables(seq, ctx_len)
    moe_wg, moe_wu, moe_wd = moe_w_gate, moe_w_up, moe_w_down

    for layer in range(depth):
        i = layer // 2
        mods = mods_all[layer]
        if layer % 2 == 0:
            bs_full = jnp.repeat(a_b_s[i].T, LANES, axis=1)
            s, h2 = _mixer(s, mods, ab_w_in[i].astype(BF16), a_v_gain[i].reshape(1, -1),
                           a_w_s[i].astype(BF16), bs_full, b_w_pool[i].astype(BF16), b_scale[i].reshape(1, -1),
                           ab_w_out[i].astype(BF16), ctx_len=ctx_len, tm=tr, chunk=chunk)
            s = _swiglu(h2, s, mods, ffn_w_gate[i][None].astype(BF16), ffn_w_up[i][None].astype(BF16),
                        ffn_w_down[i][None].astype(BF16), ctx_len=ctx_len, tm=tm_ff)
        else:
            w_in = jnp.pad(cd_w_in[i], ((0, 0), (0, 12 * LANES - cd_w_in.shape[-1]))).astype(BF16)
            consts = _cd_consts(c_q_gain[i], c_k_gain[i], d_cq_gain[i], d_ckv_gain[i], d_w_uq[i], d_w_ukv[i])
            qtc, qtd, kc, kd, vtc, vtd = _qkv(s, mods, w_in, tabs, consts, ctx_len=ctx_len, tm=tr)
            oc = od = _attention(jnp.concatenate([qtc, qtd], axis=1), jnp.concatenate([kc, kd], axis=2),
                                 jnp.concatenate([vtc, vtd], axis=2),
                                 kv_slots=(0, 0, 0, 0, 1, 1, 1, 1) + tuple(range(2, 10)), ctx_len=ctx_len, tq=tr,
                                 name="attn_all")
            router = jnp.pad(moe_router[i], ((0, 0), (0, LANES - n_exp)))
            r_hi = router.astype(BF16)
            router = jnp.stack([r_hi, (router - r_hi.astype(F32)).astype(BF16)])
            s, h2, route = _oproj(oc, od, s, mods, cd_w_out[i].astype(BF16), router,
                                  ctx_len=ctx_len, tm=tr, n_exp=n_exp)
            s = _moe(h2, route, s, mods, moe_wg, moe_wu, moe_wd, final_gain, layer=i, ctx_len=ctx_len,
                     n_exp=n_exp, ts=512, tm=tr, final=layer == depth - 1)

    return s if depth % 2 == 0 else _final(s, final_gain, ctx_len=ctx_len, tm=tr)
```
